```python
import math
import jax, jax.numpy as jnp
from jax import lax
import numpy as np

D_MODEL = 1024
BATCH = 16
SEQ = 2048
DEPTH = 1

MEM_LEN = 256
POOL_WINDOWS = (2, 4, 8, 16)
N_POOL_GROUPS = 4
D_POOL = D_MODEL
POOL_GROUP = D_POOL // N_POOL_GROUPS
D_LRU = D_MODEL
N_LRU_HEADS = 4
LRU_BLOCK = D_LRU // N_LRU_HEADS
CONV_WIDTH = 4
RG_C = 8.0
N_XATTN_HEADS = 4
XATTN_HEAD_DIM = D_MODEL // N_XATTN_HEADS
N_EXPERTS = 32
TOP_K = 4
D_EXPERT = D_MODEL
SWIGLU_LIMIT = 7.0
SWIGLU_ALPHA = 1.702
EXPERT_BLOCK = 256
RMS_EPS = 1e-6
OFF_POOL = 0
OFF_LRU_X = OFF_POOL + D_POOL
OFF_LRU_Y = OFF_LRU_X + D_LRU
OFF_GATE_A = OFF_LRU_Y + D_LRU
OFF_GATE_B = OFF_GATE_A + D_MODEL
D_IN_TOTAL = OFF_GATE_B + D_MODEL

kernel_name = "hybrid_pool_rglru_xattn_moe"


def rms_norm(x, g):
    xf = x.astype(jnp.float32)
    y = xf * lax.rsqrt(jnp.mean(xf * xf, axis=-1, keepdims=True) + RMS_EPS)
    return (y * g.astype(jnp.float32)).astype(x.dtype)


def pool_mixer(u, w_group, scale):
    B_, S_, _ = u.shape
    uf = u.astype(jnp.float32)
    cs0 = jnp.pad(jnp.cumsum(uf, axis=1), ((0, 0), (1, 0), (0, 0)))
    t = jnp.arange(S_)
    outs = []
    for gi, w in enumerate(POOL_WINDOWS):
        lo, hi = gi * POOL_GROUP, (gi + 1) * POOL_GROUP
        c = cs0[:, :, lo:hi]
        lagged = jnp.pad(c[:, :S_ - w + 1], ((0, 0), (w - 1, 0), (0, 0)))
        cnt = jnp.minimum(t + 1, w).astype(jnp.float32)[None, :, None]
        outs.append((c[:, 1:] - lagged) / cnt - uf[:, :, lo:hi])
    p = jnp.stack(outs, axis=2).astype(u.dtype)
    p = jnp.einsum('bsgi,gij->bsgj', p, w_group).reshape(B_, S_, D_POOL)
    return p * scale


def causal_depthwise_conv(u, conv_w, conv_b):
    C = u.shape[-1]
    out = lax.conv_general_dilated(
        u, conv_w[:, None, :].astype(u.dtype), window_strides=(1,),
        padding=[(CONV_WIDTH - 1, 0)], dimension_numbers=('NWC', 'WIO', 'NWC'),
        feature_group_count=C)
    return out + conv_b


def rg_lru(xr, w_a, b_a, w_x, b_x, lam):
    B_, S_, C = xr.shape
    xh = xr.reshape(B_, S_, N_LRU_HEADS, LRU_BLOCK)
    r = jax.nn.sigmoid(jnp.einsum('bshi,hij->bshj', xh, w_a).reshape(B_, S_, C) + b_a).astype(jnp.float32)
    i = jax.nn.sigmoid(jnp.einsum('bshi,hij->bshj', xh, w_x).reshape(B_, S_, C) + b_x).astype(jnp.float32)
    log_a = RG_C * r * jax.nn.log_sigmoid(lam.astype(jnp.float32))
    a = jnp.exp(log_a)
    mult = jnp.sqrt(jnp.maximum(1.0 - jnp.exp(2.0 * log_a), 0.0))
    mult = jnp.where((jnp.arange(S_) == 0)[None, :, None], 1.0, mult)
    b = mult * i * xr.astype(jnp.float32)

    def combine(left, right):
        a1, b1 = left
        a2, b2 = right
        return a1 * a2, a2 * b1 + b2

    _, hs = lax.associative_scan(combine, (a, b), axis=1)
    return hs.astype(xr.dtype)


def hybrid_mixer(h, w_in, w_pool_group, pool_scale, w_pool_proj, conv_w, conv_b,
                 lru_w_a, lru_b_a, lru_w_x, lru_b_x, lru_lambda, w_lru_proj, w_mix_out):
    z = h @ w_in
    u_pool = z[..., OFF_POOL:OFF_LRU_X]
    u_lru = z[..., OFF_LRU_X:OFF_LRU_Y]
    u_gelu = z[..., OFF_LRU_Y:OFF_GATE_A]
    g_a = z[..., OFF_GATE_A:OFF_GATE_B]
    g_b = z[..., OFF_GATE_B:D_IN_TOTAL]
    y_a = pool_mixer(u_pool, w_pool_group, pool_scale) @ w_pool_proj
    xr = causal_depthwise_conv(u_lru, conv_w, conv_b)
    hr = rg_lru(xr, lru_w_a, lru_b_a, lru_w_x, lru_b_x, lru_lambda)
    y_b = (hr * jax.nn.gelu(u_gelu)) @ w_lru_proj
    m = jax.nn.sigmoid(g_a) * y_a + jax.nn.sigmoid(g_b) * y_b
    return m @ w_mix_out


def cross_attention(hq, memn, w_q, w_kv, w_o):
    B_, S_, _ = hq.shape
    M = memn.shape[1]
    q = (hq @ w_q).reshape(B_, S_, N_XATTN_HEADS, XATTN_HEAD_DIM)
    kv = (memn @ w_kv).reshape(B_, M, 2, N_XATTN_HEADS, XATTN_HEAD_DIM)
    k, v = kv[:, :, 0], kv[:, :, 1]
    s = jnp.einsum('bqhd,bkhd->bhqk', q, k).astype(jnp.float32) * (XATTN_HEAD_DIM ** -0.5)
    p = jax.nn.softmax(s, axis=-1).astype(v.dtype)
    o = jnp.einsum('bhqk,bkhd->bqhd', p, v).reshape(B_, S_, D_MODEL)
    return o @ w_o


def clamped_swiglu(hgu):
    g, u = hgu[..., :D_EXPERT], hgu[..., D_EXPERT:]
    g = jnp.minimum(g, SWIGLU_LIMIT)
    u = jnp.clip(u, -SWIGLU_LIMIT, SWIGLU_LIMIT)
    return (u + 1.0) * (g * jax.nn.sigmoid(SWIGLU_ALPHA * g))


def moe(xn, w_router, b_router, w_gate_up, b_gate_up, w_down, b_down):
    T = xn.shape[0]
    A = T * TOP_K
    logits = (xn @ w_router + b_router).astype(jnp.float32)
    top_val, top_idx = lax.top_k(logits, TOP_K)
    gates = jax.nn.softmax(top_val, axis=-1)
    flat_e = top_idx.reshape(-1).astype(jnp.int32)
    flat_tok = jnp.arange(A, dtype=jnp.int32) // TOP_K
    flat_g = gates.reshape(-1)
    order = jnp.argsort(flat_e)
    s_e, s_tok, s_g = flat_e[order], flat_tok[order], flat_g[order]
    counts = jnp.bincount(flat_e, length=N_EXPERTS).astype(jnp.int32)
    padded = (counts + EXPERT_BLOCK - 1) // EXPERT_BLOCK * EXPERT_BLOCK
    start = jnp.cumsum(counts) - counts
    pend = jnp.cumsum(padded)
    pstart = pend - padded
    dest = pstart[s_e] + (jnp.arange(A, dtype=jnp.int32) - start[s_e])
    n_blocks = -(-(A + N_EXPERTS * (EXPERT_BLOCK - 1)) // EXPERT_BLOCK)
    P = n_blocks * EXPERT_BLOCK
    tok_buf = jnp.zeros((P,), jnp.int32).at[dest].set(s_tok)
    gate_buf = jnp.zeros((P,), xn.dtype).at[dest].set(s_g.astype(xn.dtype))
    block_start = jnp.arange(n_blocks, dtype=jnp.int32) * EXPERT_BLOCK
    block_e = jnp.minimum(jnp.searchsorted(pend, block_start, side='right'), N_EXPERTS - 1)

    def expert_block(args):
        tok, e = args
        xb = xn[tok]
        hact = clamped_swiglu(xb @ w_gate_up[e] + b_gate_up[e])
        return hact @ w_down[e] + b_down[e]

    out = lax.map(expert_block, (tok_buf.reshape(n_blocks, EXPERT_BLOCK), block_e))
    out = out.reshape(P, D_MODEL) * gate_buf[:, None]
    return jnp.zeros_like(xn).at[tok_buf].add(out)


def setup_inputs(seed: int = 0) -> dict:
    key = jax.random.key(seed)
    ks = jax.random.split(key, 32)

    def nrm(k, shape, fan_in):
        return jax.random.normal(k, shape, jnp.float32) * (fan_in ** -0.5)

    def gain(k, shape):
        return 1.0 + 0.05 * jax.random.normal(k, shape, jnp.float32)

    def small(k, shape):
        return 0.01 * jax.random.normal(k, shape, jnp.float32)

    L = DEPTH
    a_init = jax.random.uniform(ks[11], (L, D_LRU), jnp.float32, minval=0.9, maxval=0.999)
    s_init = a_init ** (1.0 / RG_C)
    lru_lambda = jnp.log(s_init) - jnp.log1p(-s_init)
    return {
        "x": jax.random.normal(ks[0], (BATCH, SEQ, D_MODEL), jnp.float32),
        "mem": jax.random.normal(ks[1], (BATCH, MEM_LEN, D_MODEL), jnp.float32),
        "norm_mix": gain(ks[2], (L, D_MODEL)),
        "w_in": nrm(ks[3], (L, D_MODEL, D_IN_TOTAL), D_MODEL),
        "w_pool_group": nrm(ks[4], (L, N_POOL_GROUPS, POOL_GROUP, POOL_GROUP), POOL_GROUP),
        "pool_scale": gain(ks[5], (L, D_POOL)),
        "w_pool_proj": nrm(ks[6], (L, D_POOL, D_MODEL), D_POOL),
        "conv_w": nrm(ks[7], (L, CONV_WIDTH, D_LRU), CONV_WIDTH),
        "conv_b": small(ks[8], (L, D_LRU)),
        "lru_w_a": nrm(ks[9], (L, N_LRU_HEADS, LRU_BLOCK, LRU_BLOCK), LRU_BLOCK),
        "lru_b_a": small(ks[10], (L, D_LRU)),
        "lru_w_x": nrm(ks[12], (L, N_LRU_HEADS, LRU_BLOCK, LRU_BLOCK), LRU_BLOCK),
        "lru_b_x": small(ks[13], (L, D_LRU)),
        "lru_lambda": lru_lambda,
        "w_lru_proj": nrm(ks[14], (L, D_LRU, D_MODEL), D_LRU),
        "w_mix_out": nrm(ks[15], (L, D_MODEL, D_MODEL), D_MODEL),
        "norm_xattn": gain(ks[16], (L, D_MODEL)),
        "norm_mem": gain(ks[17], (L, D_MODEL)),
        "w_q": nrm(ks[18], (L, D_MODEL, D_MODEL), D_MODEL),
        "w_kv": nrm(ks[19], (L, D_MODEL, 2 * D_MODEL), D_MODEL),
        "w_o": nrm(ks[20], (L, D_MODEL, D_MODEL), D_MODEL),
        "norm_moe": gain(ks[21], (L, D_MODEL)),
        "w_router": nrm(ks[22], (L, D_MODEL, N_EXPERTS), D_MODEL),
        "b_router": small(ks[23], (L, N_EXPERTS)),
        "w_gate_up": nrm(ks[24], (L, N_EXPERTS, D_MODEL, 2 * D_EXPERT), D_MODEL),
        "b_gate_up": small(ks[25], (L, N_EXPERTS, 2 * D_EXPERT)),
        "w_down": nrm(ks[26], (L, N_EXPERTS, D_EXPERT, D_MODEL), D_EXPERT),
        "b_down": small(ks[27], (L, N_EXPERTS, D_MODEL)),
        "norm_final": gain(ks[28], (D_MODEL,)),
    }


def reference(x, mem, norm_mix, w_in, w_pool_group, pool_scale, w_pool_proj, conv_w, conv_b,
              lru_w_a, lru_b_a, lru_w_x, lru_b_x, lru_lambda, w_lru_proj, w_mix_out,
              norm_xattn, norm_mem, w_q, w_kv, w_o, norm_moe, w_router, b_router,
              w_gate_up, b_gate_up, w_down, b_down, norm_final):
    B_, S_, D_ = x.shape
    for l in range(DEPTH):
        x = x + hybrid_mixer(rms_norm(x, norm_mix[l]), w_in[l], w_pool_group[l], pool_scale[l],
                             w_pool_proj[l], conv_w[l], conv_b[l], lru_w_a[l], lru_b_a[l],
                             lru_w_x[l], lru_b_x[l], lru_lambda[l], w_lru_proj[l], w_mix_out[l])
        memn = rms_norm(mem, norm_mem[l])
        x = x + cross_attention(rms_norm(x, norm_xattn[l]), memn, w_q[l], w_kv[l], w_o[l])
        xn = rms_norm(x, norm_moe[l]).reshape(B_ * S_, D_)
        x = x + moe(xn, w_router[l], b_router[l], w_gate_up[l], b_gate_up[l],
                    w_down[l], b_down[l]).reshape(B_, S_, D_)
    return rms_norm(x, norm_final)
```

```python
import functools

import jax
import jax.numpy as jnp
from jax import lax
from jax.experimental import pallas as pl
from jax.experimental.pallas import tpu as pltpu

POOL_WINDOWS = (2, 4, 8, 16)
N_GROUPS = 4
CONV_WIDTH = 4
RG_C = 8.0
N_EXPERTS = 32
TOP_K = 4
SWIGLU_LIMIT = 7.0
SWIGLU_ALPHA = 1.702
RMS_EPS = 1e-6

MIX_STEPS = 32
KV_ROWS = 512
ATT_ROWS = 512
EXPERT_ROWS = 256
GATHER_ROWS = 512
COMBINE_ROWS = 256
VMEM_LIMIT = 52 * 1024 * 1024

BF16 = jnp.bfloat16
F32 = jnp.float32


def _const_spec(shape):
    nd = len(shape)
    return pl.BlockSpec(shape, lambda *_: (0,) * nd, pipeline_mode=pl.Buffered(1))


def _rms(x, g):
    return x * lax.rsqrt(jnp.mean(x * x, axis=-1, keepdims=True) + RMS_EPS) * g


def _dot(a, b):
    return jnp.dot(a, b, preferred_element_type=F32)


def _mixer_kernel(x_ref, nm_ref, win_ref, wpg_ref, psc_ref, wpp_ref, cw_ref, cb_ref,
                  wa_ref, ba_ref, wx_ref, bx_ref, lam_ref, wlp_ref, wmo_ref,
                  o_ref,
                  h_ref, up_ref, ul_ref, a_ref, b_ref, pm_ref, m_ref, mb_ref, hc_ref,
                  *, nb, ts):
    rows, d = x_ref.shape
    gw = d // N_GROUPS
    halo_p = (POOL_WINDOWS[-1]) * nb
    halo_c = (CONV_WIDTH - 1) * nb
    c = pl.program_id(0)

    @pl.when(c == 0)
    def _():
        up_ref[0:halo_p, :] = jnp.zeros((halo_p, d), F32)
        ul_ref[0:halo_c, :] = jnp.zeros((halo_c, d), F32)
        hc_ref[...] = jnp.zeros_like(hc_ref)

    h_ref[...] = _rms(x_ref[...], nm_ref[...]).astype(BF16)

    t_glob = c * ts + lax.broadcasted_iota(jnp.int32, (rows, 1), 0) // nb

    for g, w in enumerate(POOL_WINDOWS):
        cols = slice(g * gw, (g + 1) * gw)
        u = _dot(h_ref[...], win_ref[:, cols])
        up_ref[halo_p:halo_p + rows, cols] = u
        acc = u
        for j in range(1, w):
            acc = acc + up_ref[halo_p - j * nb:halo_p - j * nb + rows, cols]
        cnt = jnp.minimum(t_glob + 1, w).astype(F32)
        p = acc / cnt - u
        pg = _dot(p.astype(BF16), wpg_ref[g]) * psc_ref[:, cols]
        pm_ref[:, cols] = pg.astype(BF16)
    for g in range(N_GROUPS):
        cols = slice(g * gw, (g + 1) * gw)
        ya = _dot(pm_ref[...], wpp_ref[:, cols])
        ga = _dot(h_ref[...], win_ref[:, 3 * d + g * gw:3 * d + (g + 1) * gw])
        m_ref[:, cols] = jax.nn.sigmoid(ga) * ya

    for g in range(N_GROUPS):
        cols = slice(g * gw, (g + 1) * gw)
        ul_ref[halo_c:halo_c + rows, cols] = _dot(h_ref[...], win_ref[:, d + g * gw:d + (g + 1) * gw])
        xr = cb_ref[:, cols]
        for k in range(CONV_WIDTH):
            off = halo_c - (CONV_WIDTH - 1 - k) * nb
            xr = xr + ul_ref[off:off + rows, cols] * cw_ref[k:k + 1, cols]
        xrb = xr.astype(BF16)
        r = jax.nn.sigmoid(_dot(xrb, wa_ref[g]) + ba_ref[:, cols])
        i = jax.nn.sigmoid(_dot(xrb, wx_ref[g]) + bx_ref[:, cols])
        lam = lam_ref[:, cols]
        log_sig = jnp.minimum(lam, 0.0) - jnp.log(1.0 + jnp.exp(-jnp.abs(lam)))
        a = jnp.exp((RG_C * r) * log_sig)
        mult = jnp.sqrt(jnp.maximum(1.0 - a * a, 0.0))
        mult = jnp.where(t_glob == 0, 1.0, mult)
        a_ref[:, cols] = a
        b_ref[:, cols] = mult * i * xr

    def scan_step(t, hprev):
        sl = pl.ds(pl.multiple_of(t * nb, nb), nb)
        hn = a_ref[sl, :] * hprev + b_ref[sl, :]
        b_ref[sl, :] = hn
        return hn

    hc_ref[...] = lax.fori_loop(0, ts, scan_step, hc_ref[...], unroll=4)

    for g in range(N_GROUPS):
        cols = slice(g * gw, (g + 1) * gw)
        gl = jax.nn.gelu(_dot(h_ref[...], win_ref[:, 2 * d + g * gw:2 * d + (g + 1) * gw]),
                         approximate=True)
        pm_ref[:, cols] = (b_ref[:, cols] * gl).astype(BF16)
    for g in range(N_GROUPS):
        cols = slice(g * gw, (g + 1) * gw)
        yb = _dot(pm_ref[...], wlp_ref[:, cols])
        gb = _dot(h_ref[...], win_ref[:, 4 * d + g * gw:4 * d + (g + 1) * gw])
        mb_ref[:, cols] = (m_ref[:, cols] + jax.nn.sigmoid(gb) * yb).astype(BF16)
    for g in range(N_GROUPS):
        cols = slice(g * gw, (g + 1) * gw)
        o_ref[:, cols] = x_ref[:, cols] + _dot(mb_ref[...], wmo_ref[:, cols])

    up_ref[0:halo_p, :] = up_ref[rows:rows + halo_p, :]
    ul_ref[0:halo_c, :] = ul_ref[rows:rows + halo_c, :]


def _mixer(xt, nb, norm_mix, w_in, w_pool_group, pool_scale, w_pool_proj, conv_w, conv_b,
           lru_w_a, lru_b_a, lru_w_x, lru_b_x, lru_lambda, w_lru_proj, w_mix_out):
    n_rows, d = xt.shape
    ts = MIX_STEPS
    rows = ts * nb
    assert n_rows % rows == 0 and rows >= POOL_WINDOWS[-1] * nb and nb % 8 == 0
    row2 = lambda v: v.reshape(1, -1)
    args = (xt, row2(norm_mix), w_in.astype(BF16), w_pool_group.astype(BF16), row2(pool_scale),
            w_pool_proj.astype(BF16), conv_w, row2(conv_b), lru_w_a.astype(BF16), row2(lru_b_a),
            lru_w_x.astype(BF16), row2(lru_b_x), row2(lru_lambda), w_lru_proj.astype(BF16),
            w_mix_out.astype(BF16))
    in_specs = [pl.BlockSpec((rows, d), lambda c: (c, 0))] + [_const_spec(a.shape) for a in args[1:]]
    halo_p = POOL_WINDOWS[-1] * nb
    halo_c = (CONV_WIDTH - 1) * nb
    return pl.pallas_call(
        functools.partial(_mixer_kernel, nb=nb, ts=ts),
        grid=(n_rows // rows,),
        in_specs=in_specs,
        out_specs=pl.BlockSpec((rows, d), lambda c: (c, 0)),
        out_shape=jax.ShapeDtypeStruct((n_rows, d), F32),
        scratch_shapes=[
            pltpu.VMEM((rows, d), BF16),
            pltpu.VMEM((halo_p + rows, d), F32),
            pltpu.VMEM((halo_c + rows, d), F32),
            pltpu.VMEM((rows, d), F32),
            pltpu.VMEM((rows, d), F32),
            pltpu.VMEM((rows, d), BF16),
            pltpu.VMEM((rows, d), F32),
            pltpu.VMEM((rows, d), BF16),
            pltpu.VMEM((nb, d), F32),
        ],
        compiler_params=pltpu.CompilerParams(dimension_semantics=("arbitrary",),
                                             vmem_limit_bytes=VMEM_LIMIT),
        name="mixer",
    )(*args)


def _kv_kernel(m_ref, g_ref, w_ref, o_ref):
    o_ref[...] = _dot(_rms(m_ref[...], g_ref[...]).astype(BF16), w_ref[...]).astype(BF16)


def _kv_proj(mem2d, norm_mem, w_kv):
    n, d = mem2d.shape
    assert n % KV_ROWS == 0
    return pl.pallas_call(
        _kv_kernel,
        grid=(n // KV_ROWS,),
        in_specs=[pl.BlockSpec((KV_ROWS, d), lambda i: (i, 0)),
                  _const_spec((1, d)), _const_spec((d, 2 * d))],
        out_specs=pl.BlockSpec((KV_ROWS, 2 * d), lambda i: (i, 0)),
        out_shape=jax.ShapeDtypeStruct((n, 2 * d), BF16),
        compiler_params=pltpu.CompilerParams(dimension_semantics=("arbitrary",),
                                             vmem_limit_bytes=VMEM_LIMIT),
        name="kv_proj",
    )(mem2d, norm_mem.reshape(1, d), w_kv.astype(BF16))


def _attn_kernel(x_ref, kv_ref, gx_ref, wq_ref, wo_ref, gm_ref, wr_ref, br_ref,
                 x2_ref, xn_ref, lg_ref, o_scr):
    tq, d = x_ref.shape
    hd = d // N_GROUPS
    x = x_ref[...]
    q = _dot(_rms(x, gx_ref[...]).astype(BF16), wq_ref[...]).astype(BF16)
    for h in range(N_GROUPS):
        k = kv_ref[:, h * hd:(h + 1) * hd]
        v = kv_ref[:, d + h * hd:d + (h + 1) * hd]
        s = lax.dot_general(q[:, h * hd:(h + 1) * hd], k, (((1,), (1,)), ((), ())),
                            preferred_element_type=F32) * (hd ** -0.5)
        e = jnp.exp(s - jnp.max(s, axis=-1, keepdims=True))
        p = e / jnp.sum(e, axis=-1, keepdims=True)
        o_scr[:, h * hd:(h + 1) * hd] = _dot(p.astype(BF16), v).astype(BF16)
    x2 = x + _dot(o_scr[...], wo_ref[...])
    x2_ref[...] = x2
    xn = _rms(x2, gm_ref[...])
    xn_ref[...] = xn
    ne = br_ref.shape[-1]
    xh = xn.astype(BF16)
    xl = (xn - xh.astype(F32)).astype(BF16)
    ph = _dot(xh, wr_ref[...])
    pl_ = _dot(xl, wr_ref[:, 0:ne])
    lg_ref[...] = ph[:, 0:ne] + ph[:, ne:2 * ne] + pl_ + br_ref[...]


def _attention(x1_sbd, kv, nb, norm_xattn, w_q, w_o, norm_moe, w_router, b_router):
    s_len, bd = x1_sbd.shape
    d = bd // nb
    m = kv.shape[1]
    ne = w_router.shape[-1]
    nq = s_len // ATT_ROWS
    assert s_len % ATT_ROWS == 0
    wr_hi = w_router.astype(BF16)
    wr_lo = (w_router - wr_hi.astype(F32)).astype(BF16)
    wr2 = jnp.concatenate([wr_hi, wr_lo], axis=1)
    return pl.pallas_call(
        _attn_kernel,
        grid=(nb, nq),
        in_specs=[pl.BlockSpec((ATT_ROWS, d), lambda b, i: (i, b)),
                  pl.BlockSpec((None, m, 2 * d), lambda b, i: (b, 0, 0)),
                  _const_spec((1, d)), _const_spec((d, d)), _const_spec((d, d)),
                  _const_spec((1, d)), _const_spec((d, 2 * ne)), _const_spec((1, ne))],
        out_specs=[pl.BlockSpec((None, ATT_ROWS, d), lambda b, i: (b, i, 0)),
                   pl.BlockSpec((ATT_ROWS, d), lambda b, i: (b * nq + i, 0)),
                   pl.BlockSpec((ATT_ROWS, ne), lambda b, i: (b * nq + i, 0))],
        out_shape=[jax.ShapeDtypeStruct((nb, s_len, d), F32),
                   jax.ShapeDtypeStruct((nb * s_len, d), F32),
                   jax.ShapeDtypeStruct((nb * s_len, ne), F32)],
        scratch_shapes=[pltpu.VMEM((ATT_ROWS, d), BF16)],
        compiler_params=pltpu.CompilerParams(dimension_semantics=("arbitrary", "arbitrary"),
                                             vmem_limit_bytes=VMEM_LIMIT),
        name="attention",
    )(x1_sbd, kv, norm_xattn.reshape(1, d), w_q.astype(BF16), w_o.astype(BF16),
      norm_moe.reshape(1, d), wr2, b_router.reshape(1, ne))


def _gather_kernel(tok_ref, x_hbm, o_ref, idx_smem, buf, sem_idx, sem_rows):
    n = buf.shape[0]
    cp = pltpu.make_async_copy(tok_ref.at[0], idx_smem, sem_idx)
    cp.start()
    cp.wait()

    def issue(r, carry):
        pltpu.make_async_copy(x_hbm.at[pl.ds(idx_smem[0, r], 1)], buf.at[pl.ds(r, 1)], sem_rows).start()
        return carry

    lax.fori_loop(0, n, issue, 0, unroll=8)
    pltpu.make_async_copy(x_hbm.at[pl.ds(0, n)], buf, sem_rows).wait()
    o_ref[...] = buf[...].astype(o_ref.dtype)


def _gather_rows(src, idx, out_dtype):
    n = idx.shape[0]
    d = src.shape[1]
    g = GATHER_ROWS
    assert n % g == 0
    return pl.pallas_call(
        _gather_kernel,
        grid=(n // g,),
        in_specs=[pl.BlockSpec((1, 1, g), lambda i: (i, 0, 0)),
                  pl.BlockSpec(memory_space=pl.ANY)],
        out_specs=pl.BlockSpec((g, d), lambda i: (i, 0)),
        out_shape=jax.ShapeDtypeStruct((n, d), out_dtype),
        scratch_shapes=[pltpu.SMEM((1, g), jnp.int32), pltpu.VMEM((g, d), src.dtype),
                        pltpu.SemaphoreType.DMA, pltpu.SemaphoreType.DMA],
        compiler_params=pltpu.CompilerParams(dimension_semantics=("arbitrary",),
                                             vmem_limit_bytes=VMEM_LIMIT),
        name="gather_rows",
    )(idx.reshape(n // g, 1, g), src)


def _expert_kernel(be_ref, nu_ref, xs_ref, gate_ref, wgu_ref, bgu_ref, wdn_ref, bdn_ref,
                   y_ref, wgu_bf, wdn_bf):
    i = pl.program_id(0)
    de = wdn_ref.shape[0]
    prev = be_ref[jnp.maximum(i - 1, 0)]
    fresh = jnp.logical_or(i == 0, be_ref[i] != prev)
    active = i < nu_ref[0]

    @pl.when(jnp.logical_and(active, fresh))
    def _():
        wgu_bf[...] = wgu_ref[...].astype(BF16)
        wdn_bf[...] = wdn_ref[...].astype(BF16)

    @pl.when(active)
    def _():
        hgu = _dot(xs_ref[...], wgu_bf[...]) + bgu_ref[...]
        gl = jnp.minimum(hgu[:, :de], SWIGLU_LIMIT)
        up = jnp.clip(hgu[:, de:], -SWIGLU_LIMIT, SWIGLU_LIMIT)
        act = (up + 1.0) * (gl * jax.nn.sigmoid(SWIGLU_ALPHA * gl))
        y = _dot(act.astype(BF16), wdn_bf[...]) + bdn_ref[...]
        y_ref[...] = y * gate_ref[...]

    @pl.when(jnp.logical_not(active))
    def _():
        y_ref[...] = jnp.zeros_like(y_ref)


def _experts(xs, gate_col, block_e, n_used, w_gate_up, b_gate_up, w_down, b_down):
    p, d = xs.shape
    ne, _, de2 = w_gate_up.shape
    de = de2 // 2
    bm = EXPERT_ROWS
    nblk = p // bm
    grid_spec = pltpu.PrefetchScalarGridSpec(
        num_scalar_prefetch=2,
        grid=(nblk,),
        in_specs=[
            pl.BlockSpec((bm, d), lambda i, be, nu: (i, 0)),
            pl.BlockSpec((bm, 1), lambda i, be, nu: (i, 0)),
            pl.BlockSpec((None, d, de2), lambda i, be, nu: (be[i], 0, 0)),
            pl.BlockSpec((None, 1, de2), lambda i, be, nu: (be[i], 0, 0)),
            pl.BlockSpec((None, de, d), lambda i, be, nu: (be[i], 0, 0)),
            pl.BlockSpec((None, 1, d), lambda i, be, nu: (be[i], 0, 0)),
        ],
        out_specs=pl.BlockSpec((bm, d), lambda i, be, nu: (i, 0)),
        scratch_shapes=[pltpu.VMEM((d, de2), BF16), pltpu.VMEM((de, d), BF16)],
    )
    return pl.pallas_call(
        _expert_kernel,
        grid_spec=grid_spec,
        out_shape=jax.ShapeDtypeStruct((p, d), F32),
        compiler_params=pltpu.CompilerParams(dimension_semantics=("arbitrary",),
                                             vmem_limit_bytes=VMEM_LIMIT),
        name="experts",
    )(block_e, n_used, xs, gate_col, w_gate_up, b_gate_up.reshape(ne, 1, de2),
      w_down, b_down.reshape(ne, 1, d))


def _combine_kernel(dest_ref, x2_ref, gf_ref, y_hbm, o_ref, idx_smem, buf, sem_idx, sem_rows):
    kk, n, d = buf.shape
    cp = pltpu.make_async_copy(dest_ref.at[0], idx_smem, sem_idx)
    cp.start()
    cp.wait()

    def issue(r, carry):
        for k in range(kk):
            pltpu.make_async_copy(y_hbm.at[pl.ds(idx_smem[k, r], 1)], buf.at[k, pl.ds(r, 1)],
                                  sem_rows).start()
        return carry

    lax.fori_loop(0, n, issue, 0, unroll=4)
    for k in range(kk):
        pltpu.make_async_copy(y_hbm.at[pl.ds(0, n)], buf.at[k], sem_rows).wait()
    acc = x2_ref[...]
    for k in range(kk):
        acc = acc + buf[k]
    o_ref[...] = _rms(acc, gf_ref[...])


def _combine(dest, x2, y, norm_final):
    t, d = x2.shape
    n = COMBINE_ROWS
    kk = dest.shape[1]
    assert t % n == 0
    dest_blk = dest.reshape(t // n, n, kk).transpose(0, 2, 1)
    return pl.pallas_call(
        _combine_kernel,
        grid=(t // n,),
        in_specs=[pl.BlockSpec((1, kk, n), lambda i: (i, 0, 0)),
                  pl.BlockSpec((n, d), lambda i: (i, 0)),
                  _const_spec((1, d)),
                  pl.BlockSpec(memory_space=pl.ANY)],
        out_specs=pl.BlockSpec((n, d), lambda i: (i, 0)),
        out_shape=jax.ShapeDtypeStruct((t, d), F32),
        scratch_shapes=[pltpu.SMEM((kk, n), jnp.int32), pltpu.VMEM((kk, n, d), F32),
                        pltpu.SemaphoreType.DMA, pltpu.SemaphoreType.DMA],
        compiler_params=pltpu.CompilerParams(dimension_semantics=("arbitrary",),
                                             vmem_limit_bytes=VMEM_LIMIT),
        name="combine",
    )(dest_blk, x2, norm_final.reshape(1, d), y)


def _route(logits):
    t, ne = logits.shape
    bm = EXPERT_ROWS
    top_val, top_idx = lax.top_k(logits, TOP_K)
    gates = jax.nn.softmax(top_val, axis=-1)
    onehot = jnp.sum((top_idx[:, :, None] == jnp.arange(ne, dtype=top_idx.dtype)).astype(jnp.int32), axis=1)
    csum = jnp.cumsum(onehot, axis=0)
    rank = jnp.take_along_axis(csum - onehot, top_idx, axis=1)
    counts = csum[-1]
    padded = (counts + bm - 1) // bm * bm
    pend = jnp.cumsum(padded)
    pstart = pend - padded
    dest = (pstart[top_idx] + rank).astype(jnp.int32)
    a = t * TOP_K
    nblk = -(-(a + ne * (bm - 1)) // bm)
    nblk = -(-nblk * bm // GATHER_ROWS) * GATHER_ROWS // bm
    p = nblk * bm
    flat_dest = dest.reshape(-1)
    tok_buf = jnp.zeros((p,), jnp.int32).at[flat_dest].set(
        jnp.arange(a, dtype=jnp.int32) // TOP_K, unique_indices=True)
    gate_buf = jnp.zeros((p,), F32).at[flat_dest].set(gates.reshape(-1), unique_indices=True)
    block_start = jnp.arange(nblk, dtype=jnp.int32) * bm
    block_e = jnp.minimum(jnp.searchsorted(pend, block_start, side='right'), ne - 1).astype(jnp.int32)
    n_used = (pend[-1] // bm).astype(jnp.int32).reshape(1)
    return dest, tok_buf, gate_buf, block_e, n_used


def kernel(x, mem, norm_mix, w_in, w_pool_group, pool_scale, w_pool_proj, conv_w, conv_b, lru_w_a, lru_b_a, lru_w_x, lru_b_x, lru_lambda, w_lru_proj, w_mix_out, norm_xattn, norm_mem, w_q, w_kv, w_o, norm_moe, w_router, b_router, w_gate_up, b_gate_up, w_down, b_down, norm_final):
    nb, s_len, d = x.shape
    m_len = mem.shape[1]
    assert norm_mix.shape[0] == 1, "single-layer stack"
    l = 0
    xt = jnp.transpose(x, (1, 0, 2)).reshape(s_len * nb, d)
    x1 = _mixer(xt, nb, norm_mix[l], w_in[l], w_pool_group[l], pool_scale[l], w_pool_proj[l],
                conv_w[l], conv_b[l], lru_w_a[l], lru_b_a[l], lru_w_x[l], lru_b_x[l],
                lru_lambda[l], w_lru_proj[l], w_mix_out[l])
    kv = _kv_proj(mem.reshape(nb * m_len, d), norm_mem[l], w_kv[l]).reshape(nb, m_len, 2 * d)
    x2, xn, logits = _attention(x1.reshape(s_len, nb * d), kv, nb, norm_xattn[l], w_q[l], w_o[l],
                                norm_moe[l], w_router[l], b_router[l])
    dest, tok_buf, gate_buf, block_e, n_used = _route(logits)
    xs = _gather_rows(xn, tok_buf, BF16)
    y = _experts(xs, gate_buf.reshape(-1, 1), block_e, n_used, w_gate_up[l], b_gate_up[l],
                 w_down[l], b_down[l])
    out = _combine(dest, x2.reshape(nb * s_len, d), y, norm_final)
    return out.reshape(nb, s_len, d)
```

```python
import functools

import jax
import jax.numpy as jnp
from jax import lax
from jax.experimental import pallas as pl
from jax.experimental.pallas import tpu as pltpu

POOL_WINDOWS = (2, 4, 8, 16)
N_GROUPS = 4
CONV_WIDTH = 4
RG_C = 8.0
N_EXPERTS = 32
TOP_K = 4
SWIGLU_LIMIT = 7.0
SWIGLU_ALPHA = 1.702
RMS_EPS = 1e-6

MIX_STEPS = 32
KV_ROWS = 512
ATT_ROWS = 512
ROUTE_ROWS = 256
EXPERT_ROWS = 256
RUN_ALIGN = 8
VMEM_LIMIT = 52 * 1024 * 1024

BF16 = jnp.bfloat16
F32 = jnp.float32
I32 = jnp.int32


def _const_spec(shape):
    nd = len(shape)
    return pl.BlockSpec(shape, lambda *_: (0,) * nd, pipeline_mode=pl.Buffered(1))


def _rms(x, g):
    return x * lax.rsqrt(jnp.mean(x * x, axis=-1, keepdims=True) + RMS_EPS) * g


def _dot(a, b):
    return jnp.dot(a, b, preferred_element_type=F32)


def _dot_nt(a, b):
    return lax.dot_general(a, b, (((1,), (1,)), ((), ())), preferred_element_type=F32)


def _mixer_kernel(x_ref, nm_ref, win_ref, wpg_ref, psc_ref, wpp_ref, cw_ref, cb_ref,
                  wa_ref, ba_ref, wx_ref, bx_ref, lam_ref, wlp_ref, wmo_ref,
                  o_ref,
                  h_ref, up_ref, ul_ref, a_ref, b_ref, pm_ref, m_ref, mb_ref, hc_ref,
                  *, nb, ts):
    rows, d = x_ref.shape
    gw = d // N_GROUPS
    halo_p = (POOL_WINDOWS[-1]) * nb
    halo_c = (CONV_WIDTH - 1) * nb
    c = pl.program_id(0)

    @pl.when(c == 0)
    def _():
        up_ref[0:halo_p, :] = jnp.zeros((halo_p, d), F32)
        ul_ref[0:halo_c, :] = jnp.zeros((halo_c, d), F32)
        hc_ref[...] = jnp.zeros_like(hc_ref)

    h_ref[...] = _rms(x_ref[...], nm_ref[...]).astype(BF16)

    t_glob = c * ts + lax.broadcasted_iota(I32, (rows, 1), 0) // nb

    for g, w in enumerate(POOL_WINDOWS):
        cols = slice(g * gw, (g + 1) * gw)
        u = _dot(h_ref[...], win_ref[:, cols])
        up_ref[halo_p:halo_p + rows, cols] = u
        acc = u
        for j in range(1, w):
            acc = acc + up_ref[halo_p - j * nb:halo_p - j * nb + rows, cols]
        cnt = jnp.minimum(t_glob + 1, w).astype(F32)
        p = acc / cnt - u
        pg = _dot(p.astype(BF16), wpg_ref[g]) * psc_ref[:, cols]
        pm_ref[:, cols] = pg.astype(BF16)
    for g in range(N_GROUPS):
        cols = slice(g * gw, (g + 1) * gw)
        ya = _dot(pm_ref[...], wpp_ref[:, cols])
        ga = _dot(h_ref[...], win_ref[:, 3 * d + g * gw:3 * d + (g + 1) * gw])
        m_ref[:, cols] = jax.nn.sigmoid(ga) * ya

    for g in range(N_GROUPS):
        cols = slice(g * gw, (g + 1) * gw)
        ul_ref[halo_c:halo_c + rows, cols] = _dot(h_ref[...], win_ref[:, d + g * gw:d + (g + 1) * gw])
        xr = cb_ref[:, cols]
        for k in range(CONV_WIDTH):
            off = halo_c - (CONV_WIDTH - 1 - k) * nb
            xr = xr + ul_ref[off:off + rows, cols] * cw_ref[k:k + 1, cols]
        xrb = xr.astype(BF16)
        r = jax.nn.sigmoid(_dot(xrb, wa_ref[g]) + ba_ref[:, cols])
        i = jax.nn.sigmoid(_dot(xrb, wx_ref[g]) + bx_ref[:, cols])
        lam = lam_ref[:, cols]
        log_sig = jnp.minimum(lam, 0.0) - jnp.log(1.0 + jnp.exp(-jnp.abs(lam)))
        a = jnp.exp((RG_C * r) * log_sig)
        mult = jnp.sqrt(jnp.maximum(1.0 - a * a, 0.0))
        mult = jnp.where(t_glob == 0, 1.0, mult)
        a_ref[:, cols] = a
        b_ref[:, cols] = mult * i * xr

    def scan_step(t, hprev):
        sl = pl.ds(pl.multiple_of(t * nb, nb), nb)
        hn = a_ref[sl, :] * hprev + b_ref[sl, :]
        b_ref[sl, :] = hn
        return hn

    hc_ref[...] = lax.fori_loop(0, ts, scan_step, hc_ref[...], unroll=4)

    for g in range(N_GROUPS):
        cols = slice(g * gw, (g + 1) * gw)
        gl = jax.nn.gelu(_dot(h_ref[...], win_ref[:, 2 * d + g * gw:2 * d + (g + 1) * gw]),
                         approximate=True)
        pm_ref[:, cols] = (b_ref[:, cols] * gl).astype(BF16)
    for g in range(N_GROUPS):
        cols = slice(g * gw, (g + 1) * gw)
        yb = _dot(pm_ref[...], wlp_ref[:, cols])
        gb = _dot(h_ref[...], win_ref[:, 4 * d + g * gw:4 * d + (g + 1) * gw])
        mb_ref[:, cols] = (m_ref[:, cols] + jax.nn.sigmoid(gb) * yb).astype(BF16)
    for g in range(N_GROUPS):
        cols = slice(g * gw, (g + 1) * gw)
        o_ref[:, cols] = x_ref[:, cols] + _dot(mb_ref[...], wmo_ref[:, cols])

    up_ref[0:halo_p, :] = up_ref[rows:rows + halo_p, :]
    ul_ref[0:halo_c, :] = ul_ref[rows:rows + halo_c, :]


def _mixer(xt, nb, norm_mix, w_in, w_pool_group, pool_scale, w_pool_proj, conv_w, conv_b,
           lru_w_a, lru_b_a, lru_w_x, lru_b_x, lru_lambda, w_lru_proj, w_mix_out):
    n_rows, d = xt.shape
    ts = MIX_STEPS
    rows = ts * nb
    assert n_rows % rows == 0 and rows >= POOL_WINDOWS[-1] * nb and nb % 8 == 0
    row2 = lambda v: v.reshape(1, -1)
    args = (xt, row2(norm_mix), w_in.astype(BF16), w_pool_group.astype(BF16), row2(pool_scale),
            w_pool_proj.astype(BF16), conv_w, row2(conv_b), lru_w_a.astype(BF16), row2(lru_b_a),
            lru_w_x.astype(BF16), row2(lru_b_x), row2(lru_lambda), w_lru_proj.astype(BF16),
            w_mix_out.astype(BF16))
    in_specs = [pl.BlockSpec((rows, d), lambda c: (c, 0))] + [_const_spec(a.shape) for a in args[1:]]
    halo_p = POOL_WINDOWS[-1] * nb
    halo_c = (CONV_WIDTH - 1) * nb
    return pl.pallas_call(
        functools.partial(_mixer_kernel, nb=nb, ts=ts),
        grid=(n_rows // rows,),
        in_specs=in_specs,
        out_specs=pl.BlockSpec((rows, d), lambda c: (c, 0)),
        out_shape=jax.ShapeDtypeStruct((n_rows, d), F32),
        scratch_shapes=[
            pltpu.VMEM((rows, d), BF16),
            pltpu.VMEM((halo_p + rows, d), F32),
            pltpu.VMEM((halo_c + rows, d), F32),
            pltpu.VMEM((rows, d), F32),
            pltpu.VMEM((rows, d), F32),
            pltpu.VMEM((rows, d), BF16),
            pltpu.VMEM((rows, d), F32),
            pltpu.VMEM((rows, d), BF16),
            pltpu.VMEM((nb, d), F32),
        ],
        compiler_params=pltpu.CompilerParams(dimension_semantics=("arbitrary",),
                                             vmem_limit_bytes=VMEM_LIMIT),
        name="mixer",
    )(*args)


def _kv_kernel(m_ref, g_ref, w_ref, o_ref):
    o_ref[...] = _dot(_rms(m_ref[...], g_ref[...]).astype(BF16), w_ref[...]).astype(BF16)


def _kv_proj(mem2d, norm_mem, w_kv):
    n, d = mem2d.shape
    assert n % KV_ROWS == 0
    return pl.pallas_call(
        _kv_kernel,
        grid=(n // KV_ROWS,),
        in_specs=[pl.BlockSpec((KV_ROWS, d), lambda i: (i, 0)),
                  _const_spec((1, d)), _const_spec((d, 2 * d))],
        out_specs=pl.BlockSpec((KV_ROWS, 2 * d), lambda i: (i, 0)),
        out_shape=jax.ShapeDtypeStruct((n, 2 * d), BF16),
        compiler_params=pltpu.CompilerParams(dimension_semantics=("arbitrary",),
                                             vmem_limit_bytes=VMEM_LIMIT),
        name="kv_proj",
    )(mem2d, norm_mem.reshape(1, d), w_kv.astype(BF16))


def _attn_kernel(x_ref, kv_ref, gx_ref, wq_ref, wo_ref, gm_ref, wr_ref, br_ref,
                 x2_ref, xn_ref, idx_ref, gate_ref, lrank_ref, cnt_ref, o_scr, xh_scr, xl_scr):
    tq, d = x_ref.shape
    hd = d // N_GROUPS
    ne = br_ref.shape[0]
    tb = ROUTE_ROWS
    x = x_ref[...]
    q = _dot(_rms(x, gx_ref[...]).astype(BF16), wq_ref[...]).astype(BF16)
    for h in range(N_GROUPS):
        k = kv_ref[:, h * hd:(h + 1) * hd]
        v = kv_ref[:, d + h * hd:d + (h + 1) * hd]
        s = _dot_nt(q[:, h * hd:(h + 1) * hd], k) * (hd ** -0.5)
        e = jnp.exp(s - jnp.max(s, axis=-1, keepdims=True))
        p = e / jnp.sum(e, axis=-1, keepdims=True)
        o_scr[:, h * hd:(h + 1) * hd] = _dot(p.astype(BF16), v).astype(BF16)
    x2 = x + _dot(o_scr[...], wo_ref[...])
    x2_ref[...] = x2
    xn = _rms(x2, gm_ref[...])
    xh = xn.astype(BF16)
    xn_ref[...] = xh
    xh_scr[...] = xh
    xl_scr[...] = (xn - xh.astype(F32)).astype(BF16)

    iota_f = lax.broadcasted_iota(I32, (ne, tb), 0).astype(F32)
    before = (lax.broadcasted_iota(I32, (tb, tb), 0) < lax.broadcasted_iota(I32, (tb, tb), 1)
              ).astype(BF16)
    for sb in range(tq // tb):
        rows = slice(sb * tb, (sb + 1) * tb)
        ph = _dot_nt(wr_ref[...], xh_scr[rows, :])
        pl_ = _dot_nt(wr_ref[0:ne, :], xl_scr[rows, :])
        work = ph[0:ne] + ph[ne:2 * ne] + pl_ + br_ref[...]
        vals, idxs, sels = [], [], []
        for _ in range(TOP_K):
            m = jnp.max(work, axis=0, keepdims=True)
            idx = jnp.min(jnp.where(work == m, iota_f, float(ne)), axis=0, keepdims=True)
            sel = iota_f == idx
            vals.append(m)
            idxs.append(idx.astype(I32))
            sels.append(sel)
            work = jnp.where(sel, -jnp.inf, work)
        ex = [jnp.exp(v - vals[0]) for v in vals]
        den = ex[0] + ex[1] + ex[2] + ex[3]
        onehot = jnp.zeros((ne, tb), F32)
        for sel in sels:
            onehot = onehot + sel.astype(F32)
        oh16 = onehot.astype(BF16)
        prefix = _dot(oh16, before)
        lr = [jnp.sum(jnp.where(sel, prefix, 0.0), axis=0, keepdims=True).astype(I32) for sel in sels]
        idx_ref[:, rows] = jnp.concatenate(idxs, axis=0)
        gate_ref[:, rows] = jnp.concatenate([e_ / den for e_ in ex], axis=0)
        lrank_ref[:, rows] = jnp.concatenate(lr, axis=0)
        cnt_ref[sb] = _dot_nt(jnp.ones((1, tb), BF16), oh16).astype(I32)


def _attention(x1_sbd, kv, nb, norm_xattn, w_q, w_o, norm_moe, w_router, b_router):
    s_len, bd = x1_sbd.shape
    d = bd // nb
    m = kv.shape[1]
    ne = w_router.shape[-1]
    tq, tb = ATT_ROWS, ROUTE_ROWS
    nq = s_len // tq
    t = nb * s_len
    assert s_len % tq == 0 and tq % tb == 0
    wr_hi = w_router.astype(BF16)
    wr_lo = (w_router - wr_hi.astype(F32)).astype(BF16)
    wr2t = jnp.concatenate([wr_hi, wr_lo], axis=1).T
    tok = lambda b, i: (0, b * nq + i)
    return pl.pallas_call(
        _attn_kernel,
        grid=(nb, nq),
        in_specs=[pl.BlockSpec((tq, d), lambda b, i: (i, b)),
                  pl.BlockSpec((None, m, 2 * d), lambda b, i: (b, 0, 0)),
                  _const_spec((1, d)), _const_spec((d, d)), _const_spec((d, d)),
                  _const_spec((1, d)), _const_spec((2 * ne, d)), _const_spec((ne, 1))],
        out_specs=[pl.BlockSpec((None, tq, d), lambda b, i: (b, i, 0)),
                   pl.BlockSpec((tq, d), lambda b, i: (b * nq + i, 0)),
                   pl.BlockSpec((TOP_K, tq), tok),
                   pl.BlockSpec((TOP_K, tq), tok),
                   pl.BlockSpec((TOP_K, tq), tok),
                   pl.BlockSpec((tq // tb, 1, ne), lambda b, i: (b * nq + i, 0, 0))],
        out_shape=[jax.ShapeDtypeStruct((nb, s_len, d), F32),
                   jax.ShapeDtypeStruct((t, d), BF16),
                   jax.ShapeDtypeStruct((TOP_K, t), I32),
                   jax.ShapeDtypeStruct((TOP_K, t), F32),
                   jax.ShapeDtypeStruct((TOP_K, t), I32),
                   jax.ShapeDtypeStruct((t // tb, 1, ne), I32)],
        scratch_shapes=[pltpu.VMEM((tq, d), BF16), pltpu.VMEM((tq, d), BF16), pltpu.VMEM((tq, d), BF16)],
        compiler_params=pltpu.CompilerParams(dimension_semantics=("arbitrary", "arbitrary"),
                                             vmem_limit_bytes=VMEM_LIMIT),
        name="attention",
    )(x1_sbd, kv, norm_xattn.reshape(1, d), w_q.astype(BF16), w_o.astype(BF16),
      norm_moe.reshape(1, d), wr2t, b_router.reshape(ne, 1))


def _pow2_chunks(limit):
    sizes = []
    c = RUN_ALIGN
    while c <= limit:
        sizes.append(c)
        c *= 2
    return sizes[::-1]


def _for_each_chunk(n, limit, fn):
    for size in _pow2_chunks(limit):
        @pl.when((n & size) != 0)
        def _(size=size):
            fn(pl.multiple_of(n & ~(2 * size - 1), RUN_ALIGN), size)


def _wait_rows(n, limit, src, dst, sem):
    _for_each_chunk(n, limit, lambda off, size: pltpu.make_async_copy(
        src.at[pl.ds(0, size)], dst.at[pl.ds(0, size)], sem).wait())


def _dispatch_kernel(n8_ref, loc_ref, run_ref, tot_ref, tailn_ref, tails_ref,
                     xn_ref, pos_ref, xs_hbm, buf, zbuf, sems, zsem):
    b = pl.program_id(0)
    nblk = pl.num_programs(0)
    tb = xn_ref.shape[0]
    r_loc = buf.shape[1]
    ne = tailn_ref.shape[0] - 1
    slot = b % 2

    zrows = zbuf.shape[0]

    def zero_rest(i, carry):
        dst = pl.multiple_of(tails_ref[ne] + i * zrows, zrows)
        pltpu.make_async_copy(zbuf, xs_hbm.at[pl.ds(dst, zrows)], zsem).start()
        return carry

    def wait_rest(i, carry):
        pltpu.make_async_copy(zbuf, xs_hbm.at[pl.ds(0, zrows)], zsem).wait()
        return carry

    @pl.when(b == 0)
    def _():
        zbuf[...] = jnp.zeros_like(zbuf)
        for e in range(ne):
            _for_each_chunk(tailn_ref[e], zrows, lambda off, size, e=e: pltpu.make_async_copy(
                zbuf.at[pl.ds(0, size)],
                xs_hbm.at[pl.ds(pl.multiple_of(tails_ref[e] + off, RUN_ALIGN), size)], zsem).start())
        lax.fori_loop(0, tailn_ref[ne], zero_rest, 0)

    @pl.when(b >= 2)
    def _():
        _wait_rows(tot_ref[b - 2], r_loc, buf.at[slot], xs_hbm, sems.at[slot])

    iota_r = lax.broadcasted_iota(I32, (r_loc, tb), 0)
    hit = iota_r == pos_ref[0:1, :]
    for k in range(1, TOP_K):
        hit = jnp.logical_or(hit, iota_r == pos_ref[k:k + 1, :])
    buf[slot] = _dot(hit.astype(BF16), xn_ref[...])

    for e in range(ne):
        n = n8_ref[b * ne + e]
        src0 = loc_ref[b * ne + e]
        dst0 = run_ref[b * ne + e]
        _for_each_chunk(n, tb, lambda off, size, src0=src0, dst0=dst0: pltpu.make_async_copy(
            buf.at[slot, pl.ds(pl.multiple_of(src0 + off, RUN_ALIGN), size)],
            xs_hbm.at[pl.ds(pl.multiple_of(dst0 + off, RUN_ALIGN), size)], sems.at[slot]).start())

    @pl.when(b == nblk - 1)
    def _():
        @pl.when(b >= 1)
        def _():
            _wait_rows(tot_ref[b - 1], r_loc, buf.at[1 - slot], xs_hbm, sems.at[1 - slot])
        _wait_rows(tot_ref[b], r_loc, buf.at[slot], xs_hbm, sems.at[slot])
        for e in range(ne):
            _wait_rows(tailn_ref[e], zrows, zbuf, xs_hbm, zsem)
        lax.fori_loop(0, tailn_ref[ne], wait_rest, 0)


def _dispatch(xn, pos_t, tables, p_rows, r_loc):
    t, d = xn.shape
    tb = ROUTE_ROWS
    grid_spec = pltpu.PrefetchScalarGridSpec(
        num_scalar_prefetch=6,
        grid=(t // tb,),
        in_specs=[pl.BlockSpec((tb, d), lambda b, *_: (b, 0)),
                  pl.BlockSpec((TOP_K, tb), lambda b, *_: (0, b))],
        out_specs=pl.BlockSpec(memory_space=pl.ANY),
        scratch_shapes=[pltpu.VMEM((2, r_loc, d), F32), pltpu.VMEM((EXPERT_ROWS // 2, d), F32),
                        pltpu.SemaphoreType.DMA((2,)), pltpu.SemaphoreType.DMA],
    )
    return pl.pallas_call(
        _dispatch_kernel,
        grid_spec=grid_spec,
        out_shape=jax.ShapeDtypeStruct((p_rows, d), F32),
        compiler_params=pltpu.CompilerParams(dimension_semantics=("arbitrary",),
                                             vmem_limit_bytes=VMEM_LIMIT),
        name="dispatch",
    )(*tables, xn, pos_t)


def _expert_kernel(be_ref, nu_ref, xs_ref, wgu_ref, bgu_ref, wdn_ref, bdn_ref,
                   y_ref, wgu_bf, wdn_bf):
    i = pl.program_id(0)
    de = wdn_ref.shape[0]
    prev = be_ref[jnp.maximum(i - 1, 0)]
    fresh = jnp.logical_or(i == 0, be_ref[i] != prev)
    active = i < nu_ref[0]

    @pl.when(jnp.logical_and(active, fresh))
    def _():
        wgu_bf[...] = wgu_ref[...].astype(BF16)
        wdn_bf[...] = wdn_ref[...].astype(BF16)

    @pl.when(active)
    def _():
        hgu = _dot(xs_ref[...].astype(BF16), wgu_bf[...]) + bgu_ref[...]
        gl = jnp.minimum(hgu[:, :de], SWIGLU_LIMIT)
        up = jnp.clip(hgu[:, de:], -SWIGLU_LIMIT, SWIGLU_LIMIT)
        act = (up + 1.0) * (gl * jax.nn.sigmoid(SWIGLU_ALPHA * gl))
        y_ref[...] = _dot(act.astype(BF16), wdn_bf[...]) + bdn_ref[...]

    @pl.when(jnp.logical_not(active))
    def _():
        y_ref[...] = jnp.zeros_like(y_ref)


def _experts(xs, block_e, n_used, w_gate_up, b_gate_up, w_down, b_down):
    p, d = xs.shape
    ne, _, de2 = w_gate_up.shape
    de = de2 // 2
    bm = EXPERT_ROWS
    row_blk = lambda i, be, nu: (jnp.minimum(i, nu[0] - 1), 0)
    grid_spec = pltpu.PrefetchScalarGridSpec(
        num_scalar_prefetch=2,
        grid=(p // bm,),
        in_specs=[
            pl.BlockSpec((bm, d), row_blk),
            pl.BlockSpec((None, d, de2), lambda i, be, nu: (be[i], 0, 0)),
            pl.BlockSpec((None, 1, de2), lambda i, be, nu: (be[i], 0, 0)),
            pl.BlockSpec((None, de, d), lambda i, be, nu: (be[i], 0, 0)),
            pl.BlockSpec((None, 1, d), lambda i, be, nu: (be[i], 0, 0)),
        ],
        out_specs=pl.BlockSpec((bm, d), lambda i, be, nu: (i, 0)),
        scratch_shapes=[pltpu.VMEM((d, de2), BF16), pltpu.VMEM((de, d), BF16)],
    )
    return pl.pallas_call(
        _expert_kernel,
        grid_spec=grid_spec,
        out_shape=jax.ShapeDtypeStruct((p, d), F32),
        compiler_params=pltpu.CompilerParams(dimension_semantics=("arbitrary",),
                                             vmem_limit_bytes=VMEM_LIMIT),
        name="experts",
    )(block_e, n_used, xs, w_gate_up, b_gate_up.reshape(ne, 1, de2), w_down, b_down.reshape(ne, 1, d))


def _combine_kernel(n8_ref, loc_ref, run_ref, tot_ref,
                    pos_ref, gate_ref, x2_ref, gf_ref, y_hbm, o_ref, buf, sems):
    b = pl.program_id(0)
    nblk = pl.num_programs(0)
    tb = x2_ref.shape[0]
    r_loc = buf.shape[1]
    ne = n8_ref.shape[0] // tot_ref.shape[0]
    slot = b % 2

    def fetch(blk, s):
        for e in range(ne):
            n = n8_ref[blk * ne + e]
            dst0 = loc_ref[blk * ne + e]
            src0 = run_ref[blk * ne + e]
            _for_each_chunk(n, tb, lambda off, size, src0=src0, dst0=dst0: pltpu.make_async_copy(
                y_hbm.at[pl.ds(pl.multiple_of(src0 + off, RUN_ALIGN), size)],
                buf.at[s, pl.ds(pl.multiple_of(dst0 + off, RUN_ALIGN), size)], sems.at[s]).start())

    @pl.when(b == 0)
    def _():
        buf[...] = jnp.zeros_like(buf)
        fetch(0, 0)

    @pl.when(b + 1 < nblk)
    def _():
        fetch(b + 1, 1 - slot)

    _wait_rows(tot_ref[b], r_loc, y_hbm, buf.at[slot], sems.at[slot])

    iota_c = lax.broadcasted_iota(I32, (tb, r_loc), 1)
    w = jnp.zeros((tb, r_loc), F32)
    for k in range(TOP_K):
        w = w + jnp.where(iota_c == pos_ref[:, k:k + 1], gate_ref[:, k:k + 1], 0.0)
    w_hi = w.astype(BF16)
    w_lo = (w - w_hi.astype(F32)).astype(BF16)
    y = buf[slot].astype(BF16)
    o_ref[...] = _rms(x2_ref[...] + _dot(w_hi, y) + _dot(w_lo, y), gf_ref[...])


def _combine(pos, gates, x2, y, norm_final, tables, r_loc):
    t, d = x2.shape
    tb = ROUTE_ROWS
    grid_spec = pltpu.PrefetchScalarGridSpec(
        num_scalar_prefetch=4,
        grid=(t // tb,),
        in_specs=[pl.BlockSpec((tb, TOP_K), lambda b, *_: (b, 0)),
                  pl.BlockSpec((tb, TOP_K), lambda b, *_: (b, 0)),
                  pl.BlockSpec((tb, d), lambda b, *_: (b, 0)),
                  pl.BlockSpec((1, d), lambda b, *_: (0, 0)),
                  pl.BlockSpec(memory_space=pl.ANY)],
        out_specs=pl.BlockSpec((tb, d), lambda b, *_: (b, 0)),
        scratch_shapes=[pltpu.VMEM((2, r_loc, d), F32), pltpu.SemaphoreType.DMA((2,))],
    )
    return pl.pallas_call(
        _combine_kernel,
        grid_spec=grid_spec,
        out_shape=jax.ShapeDtypeStruct((t, d), F32),
        compiler_params=pltpu.CompilerParams(dimension_semantics=("arbitrary",),
                                             vmem_limit_bytes=VMEM_LIMIT),
        name="combine",
    )(*tables, pos, gates, x2, norm_final.reshape(1, d), y)


def _layout(cnt, idx_t, lrank_t):
    nblk, ne = cnt.shape
    tb, bm = ROUTE_ROWS, EXPERT_ROWS
    n8 = (cnt + RUN_ALIGN - 1) // RUN_ALIGN * RUN_ALIGN
    loc = jnp.cumsum(n8, axis=1) - n8
    tot = jnp.sum(n8, axis=1)
    size = jnp.sum(n8, axis=0)
    padded = (size + bm - 1) // bm * bm
    pend = jnp.cumsum(padded)
    pstart = pend - padded
    run = pstart[None, :] + jnp.cumsum(n8, axis=0) - n8
    p_blocks = -(-(nblk * tb * TOP_K + nblk * ne * (RUN_ALIGN - 1) + ne * (bm - RUN_ALIGN)) // bm)
    n_used = (pend[-1] // bm).astype(I32)
    blk_start = jnp.minimum(jnp.arange(p_blocks, dtype=I32), n_used - 1) * bm
    block_e = jnp.minimum(jnp.searchsorted(pend, blk_start, side='right'), ne - 1).astype(I32)
    blk_of_tok = jnp.arange(idx_t.shape[1], dtype=I32) // tb
    pos_t = loc.reshape(-1)[blk_of_tok[None, :] * ne + idx_t] + lrank_t
    flat = lambda a: a.reshape(-1).astype(I32)
    tables = (flat(n8), flat(loc), flat(run), flat(tot))
    half = bm // 2
    tails = (flat(jnp.concatenate([padded - size, (p_blocks * bm - pend[-1:]) // half])),
             flat(jnp.concatenate([pstart + size, pend[-1:]])))
    r_loc = -(-(tb * TOP_K + ne * (RUN_ALIGN - 1)) // bm) * bm
    return tables, tails, pos_t.astype(I32), block_e, n_used.reshape(1), p_blocks * bm, r_loc


def kernel(x, mem, norm_mix, w_in, w_pool_group, pool_scale, w_pool_proj, conv_w, conv_b, lru_w_a, lru_b_a, lru_w_x, lru_b_x, lru_lambda, w_lru_proj, w_mix_out, norm_xattn, norm_mem, w_q, w_kv, w_o, norm_moe, w_router, b_router, w_gate_up, b_gate_up, w_down, b_down, norm_final):
    nb, s_len, d = x.shape
    m_len = mem.shape[1]
    assert norm_mix.shape[0] == 1, "single-layer stack"
    l = 0
    xt = jnp.transpose(x, (1, 0, 2)).reshape(s_len * nb, d)
    x1 = _mixer(xt, nb, norm_mix[l], w_in[l], w_pool_group[l], pool_scale[l], w_pool_proj[l],
                conv_w[l], conv_b[l], lru_w_a[l], lru_b_a[l], lru_w_x[l], lru_b_x[l],
                lru_lambda[l], w_lru_proj[l], w_mix_out[l])
    kv = _kv_proj(mem.reshape(nb * m_len, d), norm_mem[l], w_kv[l]).reshape(nb, m_len, 2 * d)
    x2, xn, idx_t, gate_t, lrank_t, cnt = _attention(
        x1.reshape(s_len, nb * d), kv, nb, norm_xattn[l], w_q[l], w_o[l],
        norm_moe[l], w_router[l], b_router[l])
    tables, tails, pos_t, block_e, n_used, p_rows, r_loc = _layout(
        cnt.reshape(cnt.shape[0], -1), idx_t, lrank_t)
    xs = _dispatch(xn, pos_t, tables + tails, p_rows, r_loc)
    y = _experts(xs, block_e, n_used, w_gate_up[l], b_gate_up[l], w_down[l], b_down[l])
    out = _combine(pos_t.T, gate_t.T, x2.reshape(nb * s_len, d), y, norm_final, tables, r_loc)
    return out.reshape(nb, s_len, d)
```

```python
import functools

import jax
import jax.numpy as jnp
from jax import lax
from jax.experimental import pallas as pl
from jax.experimental.pallas import tpu as pltpu

POOL_WINDOWS = (2, 4, 8, 16)
N_GROUPS = 4
CONV_WIDTH = 4
RG_C = 8.0
N_EXPERTS = 32
TOP_K = 4
SWIGLU_LIMIT = 7.0
SWIGLU_ALPHA = 1.702
RMS_EPS = 1e-6

MIX_STEPS = 32
KV_ROWS = 512
ATT_ROWS = 512
ROUTE_ROWS = 256
EXPERT_ROWS = 256
RUN_ALIGN = 8
VMEM_LIMIT = 52 * 1024 * 1024

BF16 = jnp.bfloat16
F32 = jnp.float32
I32 = jnp.int32


def _const_spec(shape):
    nd = len(shape)
    return pl.BlockSpec(shape, lambda *_: (0,) * nd, pipeline_mode=pl.Buffered(1))


def _rms(x, g):
    return x * lax.rsqrt(jnp.mean(x * x, axis=-1, keepdims=True) + RMS_EPS) * g


def _dot(a, b):
    return jnp.dot(a, b, preferred_element_type=F32)


def _dot_nt(a, b):
    return lax.dot_general(a, b, (((1,), (1,)), ((), ())), preferred_element_type=F32)


def _mixer_kernel(x_ref, nm_ref, win_ref, wpg_ref, psc_ref, wpp_ref, cw_ref, cb_ref,
                  wa_ref, ba_ref, wx_ref, bx_ref, lam_ref, wlp_ref, wmo_ref,
                  o_ref,
                  h_ref, up_ref, ul_ref, a_ref, b_ref, pm_ref, m_ref, mb_ref, hc_ref,
                  *, nb, ts):
    rows, d = x_ref.shape
    gw = d // N_GROUPS
    halo_p = (POOL_WINDOWS[-1]) * nb
    halo_c = (CONV_WIDTH - 1) * nb
    c = pl.program_id(0)

    @pl.when(c == 0)
    def _():
        up_ref[0:halo_p, :] = jnp.zeros((halo_p, d), F32)
        ul_ref[0:halo_c, :] = jnp.zeros((halo_c, d), F32)
        hc_ref[...] = jnp.zeros_like(hc_ref)

    h_ref[...] = _rms(x_ref[...], nm_ref[...]).astype(BF16)

    t_glob = c * ts + lax.broadcasted_iota(I32, (rows, 1), 0) // nb

    for g, w in enumerate(POOL_WINDOWS):
        cols = slice(g * gw, (g + 1) * gw)
        u = _dot(h_ref[...], win_ref[:, cols])
        up_ref[halo_p:halo_p + rows, cols] = u
        acc = u
        for j in range(1, w):
            acc = acc + up_ref[halo_p - j * nb:halo_p - j * nb + rows, cols]
        cnt = jnp.minimum(t_glob + 1, w).astype(F32)
        p = acc / cnt - u
        pg = _dot(p.astype(BF16), wpg_ref[g]) * psc_ref[:, cols]
        pm_ref[:, cols] = pg.astype(BF16)
    for g in range(N_GROUPS):
        cols = slice(g * gw, (g + 1) * gw)
        ya = _dot(pm_ref[...], wpp_ref[:, cols])
        ga = _dot(h_ref[...], win_ref[:, 3 * d + g * gw:3 * d + (g + 1) * gw])
        m_ref[:, cols] = jax.nn.sigmoid(ga) * ya

    for g in range(N_GROUPS):
        cols = slice(g * gw, (g + 1) * gw)
        ul_ref[halo_c:halo_c + rows, cols] = _dot(h_ref[...], win_ref[:, d + g * gw:d + (g + 1) * gw])
        xr = cb_ref[:, cols]
        for k in range(CONV_WIDTH):
            off = halo_c - (CONV_WIDTH - 1 - k) * nb
            xr = xr + ul_ref[off:off + rows, cols] * cw_ref[k:k + 1, cols]
        xrb = xr.astype(BF16)
        r = jax.nn.sigmoid(_dot(xrb, wa_ref[g]) + ba_ref[:, cols])
        i = jax.nn.sigmoid(_dot(xrb, wx_ref[g]) + bx_ref[:, cols])
        lam = lam_ref[:, cols]
        log_sig = jnp.minimum(lam, 0.0) - jnp.log(1.0 + jnp.exp(-jnp.abs(lam)))
        a = jnp.exp((RG_C * r) * log_sig)
        mult = jnp.sqrt(jnp.maximum(1.0 - a * a, 0.0))
        mult = jnp.where(t_glob == 0, 1.0, mult)
        a_ref[:, cols] = a
        b_ref[:, cols] = mult * i * xr

    def scan_step(t, hprev):
        sl = pl.ds(pl.multiple_of(t * nb, nb), nb)
        hn = a_ref[sl, :] * hprev + b_ref[sl, :]
        b_ref[sl, :] = hn
        return hn

    hc_ref[...] = lax.fori_loop(0, ts, scan_step, hc_ref[...], unroll=4)

    for g in range(N_GROUPS):
        cols = slice(g * gw, (g + 1) * gw)
        gl = jax.nn.gelu(_dot(h_ref[...], win_ref[:, 2 * d + g * gw:2 * d + (g + 1) * gw]),
                         approximate=True)
        pm_ref[:, cols] = (b_ref[:, cols] * gl).astype(BF16)
    for g in range(N_GROUPS):
        cols = slice(g * gw, (g + 1) * gw)
        yb = _dot(pm_ref[...], wlp_ref[:, cols])
        gb = _dot(h_ref[...], win_ref[:, 4 * d + g * gw:4 * d + (g + 1) * gw])
        mb_ref[:, cols] = (m_ref[:, cols] + jax.nn.sigmoid(gb) * yb).astype(BF16)
    for g in range(N_GROUPS):
        cols = slice(g * gw, (g + 1) * gw)
        o_ref[:, cols] = x_ref[:, cols] + _dot(mb_ref[...], wmo_ref[:, cols])

    up_ref[0:halo_p, :] = up_ref[rows:rows + halo_p, :]
    ul_ref[0:halo_c, :] = ul_ref[rows:rows + halo_c, :]


def _mixer(xt, nb, norm_mix, w_in, w_pool_group, pool_scale, w_pool_proj, conv_w, conv_b,
           lru_w_a, lru_b_a, lru_w_x, lru_b_x, lru_lambda, w_lru_proj, w_mix_out):
    n_rows, d = xt.shape
    ts = MIX_STEPS
    rows = ts * nb
    assert n_rows % rows == 0 and rows >= POOL_WINDOWS[-1] * nb and nb % 8 == 0
    row2 = lambda v: v.reshape(1, -1)
    args = (xt, row2(norm_mix), w_in.astype(BF16), w_pool_group.astype(BF16), row2(pool_scale),
            w_pool_proj.astype(BF16), conv_w, row2(conv_b), lru_w_a.astype(BF16), row2(lru_b_a),
            lru_w_x.astype(BF16), row2(lru_b_x), row2(lru_lambda), w_lru_proj.astype(BF16),
            w_mix_out.astype(BF16))
    in_specs = [pl.BlockSpec((rows, d), lambda c: (c, 0))] + [_const_spec(a.shape) for a in args[1:]]
    halo_p = POOL_WINDOWS[-1] * nb
    halo_c = (CONV_WIDTH - 1) * nb
    return pl.pallas_call(
        functools.partial(_mixer_kernel, nb=nb, ts=ts),
        grid=(n_rows // rows,),
        in_specs=in_specs,
        out_specs=pl.BlockSpec((rows, d), lambda c: (c, 0)),
        out_shape=jax.ShapeDtypeStruct((n_rows, d), F32),
        scratch_shapes=[
            pltpu.VMEM((rows, d), BF16),
            pltpu.VMEM((halo_p + rows, d), F32),
            pltpu.VMEM((halo_c + rows, d), F32),
            pltpu.VMEM((rows, d), F32),
            pltpu.VMEM((rows, d), F32),
            pltpu.VMEM((rows, d), BF16),
            pltpu.VMEM((rows, d), F32),
            pltpu.VMEM((rows, d), BF16),
            pltpu.VMEM((nb, d), F32),
        ],
        compiler_params=pltpu.CompilerParams(dimension_semantics=("arbitrary",),
                                             vmem_limit_bytes=VMEM_LIMIT),
        name="mixer",
    )(*args)


def _kv_kernel(m_ref, g_ref, w_ref, o_ref):
    o_ref[...] = _dot(_rms(m_ref[...], g_ref[...]).astype(BF16), w_ref[...]).astype(BF16)


def _kv_proj(mem2d, norm_mem, w_kv):
    n, d = mem2d.shape
    assert n % KV_ROWS == 0
    return pl.pallas_call(
        _kv_kernel,
        grid=(n // KV_ROWS,),
        in_specs=[pl.BlockSpec((KV_ROWS, d), lambda i: (i, 0)),
                  _const_spec((1, d)), _const_spec((d, 2 * d))],
        out_specs=pl.BlockSpec((KV_ROWS, 2 * d), lambda i: (i, 0)),
        out_shape=jax.ShapeDtypeStruct((n, 2 * d), BF16),
        compiler_params=pltpu.CompilerParams(dimension_semantics=("arbitrary",),
                                             vmem_limit_bytes=VMEM_LIMIT),
        name="kv_proj",
    )(mem2d, norm_mem.reshape(1, d), w_kv.astype(BF16))


def _attn_kernel(x_ref, kv_ref, gx_ref, wq_ref, wo_ref, gm_ref, wr_ref, br_ref,
                 x2_ref, xn_ref, idx_ref, gate_ref, lrank_ref, cnt_ref, o_scr, xh_scr, xl_scr):
    tq, d = x_ref.shape
    hd = d // N_GROUPS
    ne = br_ref.shape[0]
    tb = ROUTE_ROWS
    x = x_ref[...]
    q = _dot(_rms(x, gx_ref[...]).astype(BF16), wq_ref[...]).astype(BF16)
    for h in range(N_GROUPS):
        k = kv_ref[:, h * hd:(h + 1) * hd]
        v = kv_ref[:, d + h * hd:d + (h + 1) * hd]
        s = _dot_nt(q[:, h * hd:(h + 1) * hd], k) * (hd ** -0.5)
        e = jnp.exp(s - jnp.max(s, axis=-1, keepdims=True))
        p = e / jnp.sum(e, axis=-1, keepdims=True)
        o_scr[:, h * hd:(h + 1) * hd] = _dot(p.astype(BF16), v).astype(BF16)
    x2 = x + _dot(o_scr[...], wo_ref[...])
    x2_ref[...] = x2
    xn = _rms(x2, gm_ref[...])
    xh = xn.astype(BF16)
    xn_ref[...] = xh
    xh_scr[...] = xh
    xl_scr[...] = (xn - xh.astype(F32)).astype(BF16)

    iota_f = lax.broadcasted_iota(I32, (ne, tb), 0).astype(F32)
    before = (lax.broadcasted_iota(I32, (tb, tb), 0) < lax.broadcasted_iota(I32, (tb, tb), 1)
              ).astype(BF16)
    for sb in range(tq // tb):
        rows = slice(sb * tb, (sb + 1) * tb)
        ph = _dot_nt(wr_ref[...], xh_scr[rows, :])
        pl_ = _dot_nt(wr_ref[0:ne, :], xl_scr[rows, :])
        work = ph[0:ne] + ph[ne:2 * ne] + pl_ + br_ref[...]
        vals, idxs, sels = [], [], []
        for _ in range(TOP_K):
            m = jnp.max(work, axis=0, keepdims=True)
            idx = jnp.min(jnp.where(work == m, iota_f, float(ne)), axis=0, keepdims=True)
            sel = iota_f == idx
            vals.append(m)
            idxs.append(idx.astype(I32))
            sels.append(sel)
            work = jnp.where(sel, -jnp.inf, work)
        ex = [jnp.exp(v - vals[0]) for v in vals]
        den = ex[0] + ex[1] + ex[2] + ex[3]
        onehot = jnp.zeros((ne, tb), F32)
        for sel in sels:
            onehot = onehot + sel.astype(F32)
        oh16 = onehot.astype(BF16)
        prefix = _dot(oh16, before)
        lr = [jnp.sum(jnp.where(sel, prefix, 0.0), axis=0, keepdims=True).astype(I32) for sel in sels]
        idx_ref[:, rows] = jnp.concatenate(idxs, axis=0)
        gate_ref[:, rows] = jnp.concatenate([e_ / den for e_ in ex], axis=0)
        lrank_ref[:, rows] = jnp.concatenate(lr, axis=0)
        cnt_ref[sb] = _dot_nt(jnp.ones((1, tb), BF16), oh16).astype(I32)


def _attention(x1_sbd, kv, nb, norm_xattn, w_q, w_o, norm_moe, w_router, b_router):
    s_len, bd = x1_sbd.shape
    d = bd // nb
    m = kv.shape[1]
    ne = w_router.shape[-1]
    tq, tb = ATT_ROWS, ROUTE_ROWS
    nq = s_len // tq
    t = nb * s_len
    assert s_len % tq == 0 and tq % tb == 0
    wr_hi = w_router.astype(BF16)
    wr_lo = (w_router - wr_hi.astype(F32)).astype(BF16)
    wr2t = jnp.concatenate([wr_hi, wr_lo], axis=1).T
    tok = lambda b, i: (0, b * nq + i)
    return pl.pallas_call(
        _attn_kernel,
        grid=(nb, nq),
        in_specs=[pl.BlockSpec((tq, d), lambda b, i: (i, b)),
                  pl.BlockSpec((None, m, 2 * d), lambda b, i: (b, 0, 0)),
                  _const_spec((1, d)), _const_spec((d, d)), _const_spec((d, d)),
                  _const_spec((1, d)), _const_spec((2 * ne, d)), _const_spec((ne, 1))],
        out_specs=[pl.BlockSpec((None, tq, d), lambda b, i: (b, i, 0)),
                   pl.BlockSpec((tq, d), lambda b, i: (b * nq + i, 0)),
                   pl.BlockSpec((TOP_K, tq), tok),
                   pl.BlockSpec((TOP_K, tq), tok),
                   pl.BlockSpec((TOP_K, tq), tok),
                   pl.BlockSpec((tq // tb, 1, ne), lambda b, i: (b * nq + i, 0, 0))],
        out_shape=[jax.ShapeDtypeStruct((nb, s_len, d), F32),
                   jax.ShapeDtypeStruct((t, d), BF16),
                   jax.ShapeDtypeStruct((TOP_K, t), I32),
                   jax.ShapeDtypeStruct((TOP_K, t), F32),
                   jax.ShapeDtypeStruct((TOP_K, t), I32),
                   jax.ShapeDtypeStruct((t // tb, 1, ne), I32)],
        scratch_shapes=[pltpu.VMEM((tq, d), BF16), pltpu.VMEM((tq, d), BF16), pltpu.VMEM((tq, d), BF16)],
        compiler_params=pltpu.CompilerParams(dimension_semantics=("arbitrary", "arbitrary"),
                                             vmem_limit_bytes=VMEM_LIMIT),
        name="attention",
    )(x1_sbd, kv, norm_xattn.reshape(1, d), w_q.astype(BF16), w_o.astype(BF16),
      norm_moe.reshape(1, d), wr2t, b_router.reshape(ne, 1))


def _pow2_chunks(limit):
    sizes = []
    c = RUN_ALIGN
    while c <= limit:
        sizes.append(c)
        c *= 2
    return sizes[::-1]


def _for_each_chunk(n, limit, fn):
    for size in _pow2_chunks(limit):
        @pl.when((n & size) != 0)
        def _(size=size):
            fn(pl.multiple_of(n & ~(2 * size - 1), RUN_ALIGN), size)


def _wait_rows(n, limit, src, dst, sem):
    _for_each_chunk(n, limit, lambda off, size: pltpu.make_async_copy(
        src.at[pl.ds(0, size)], dst.at[pl.ds(0, size)], sem).wait())


def _dispatch_kernel(n8_ref, loc_ref, run_ref, tot_ref, tailn_ref, tails_ref,
                     xn_ref, idx_ref, lrank_ref, locv_ref, xs_hbm, pos_ref, buf, zbuf, sems, zsem):
    b = pl.program_id(0)
    nblk = pl.num_programs(0)
    tb = xn_ref.shape[0]
    r_loc = buf.shape[1]
    ne = tailn_ref.shape[0] - 1
    slot = b % 2

    zrows = zbuf.shape[0]

    def zero_rest(i, carry):
        dst = pl.multiple_of(tails_ref[ne] + i * zrows, zrows)
        pltpu.make_async_copy(zbuf, xs_hbm.at[pl.ds(dst, zrows)], zsem).start()
        return carry

    def wait_rest(i, carry):
        pltpu.make_async_copy(zbuf, xs_hbm.at[pl.ds(0, zrows)], zsem).wait()
        return carry

    @pl.when(b == 0)
    def _():
        zbuf[...] = jnp.zeros_like(zbuf)
        for e in range(ne):
            _for_each_chunk(tailn_ref[e], zrows, lambda off, size, e=e: pltpu.make_async_copy(
                zbuf.at[pl.ds(0, size)],
                xs_hbm.at[pl.ds(pl.multiple_of(tails_ref[e] + off, RUN_ALIGN), size)], zsem).start())
        lax.fori_loop(0, tailn_ref[ne], zero_rest, 0)

    @pl.when(b >= 2)
    def _():
        _wait_rows(tot_ref[b - 2], r_loc, buf.at[slot], xs_hbm, sems.at[slot])

    iota_e = lax.broadcasted_iota(I32, (ne, tb), 0)
    loc_col = locv_ref[...].astype(F32)
    for k in range(TOP_K):
        run0 = jnp.sum(jnp.where(iota_e == idx_ref[k:k + 1, :], loc_col, 0.0), axis=0, keepdims=True)
        pos_ref[k:k + 1, :] = run0.astype(I32) + lrank_ref[k:k + 1, :]

    iota_r = lax.broadcasted_iota(I32, (r_loc, tb), 0)
    hit = iota_r == pos_ref[0:1, :]
    for k in range(1, TOP_K):
        hit = jnp.logical_or(hit, iota_r == pos_ref[k:k + 1, :])
    buf[slot] = _dot(hit.astype(BF16), xn_ref[...])

    for e in range(ne):
        n = n8_ref[b * ne + e]
        src0 = loc_ref[b * ne + e]
        dst0 = run_ref[b * ne + e]
        _for_each_chunk(n, tb, lambda off, size, src0=src0, dst0=dst0: pltpu.make_async_copy(
            buf.at[slot, pl.ds(pl.multiple_of(src0 + off, RUN_ALIGN), size)],
            xs_hbm.at[pl.ds(pl.multiple_of(dst0 + off, RUN_ALIGN), size)], sems.at[slot]).start())

    @pl.when(b == nblk - 1)
    def _():
        @pl.when(b >= 1)
        def _():
            _wait_rows(tot_ref[b - 1], r_loc, buf.at[1 - slot], xs_hbm, sems.at[1 - slot])
        _wait_rows(tot_ref[b], r_loc, buf.at[slot], xs_hbm, sems.at[slot])
        for e in range(ne):
            _wait_rows(tailn_ref[e], zrows, zbuf, xs_hbm, zsem)
        lax.fori_loop(0, tailn_ref[ne], wait_rest, 0)


def _dispatch(xn, idx_t, lrank_t, loc, tables, p_rows, r_loc):
    t, d = xn.shape
    tb = ROUTE_ROWS
    nblk, ne = loc.shape
    tok = pl.BlockSpec((TOP_K, tb), lambda b, *_: (0, b))
    grid_spec = pltpu.PrefetchScalarGridSpec(
        num_scalar_prefetch=6,
        grid=(t // tb,),
        in_specs=[pl.BlockSpec((tb, d), lambda b, *_: (b, 0)), tok, tok,
                  pl.BlockSpec((None, ne, 1), lambda b, *_: (b, 0, 0))],
        out_specs=[pl.BlockSpec(memory_space=pl.ANY), tok],
        scratch_shapes=[pltpu.VMEM((2, r_loc, d), F32), pltpu.VMEM((EXPERT_ROWS // 2, d), F32),
                        pltpu.SemaphoreType.DMA((2,)), pltpu.SemaphoreType.DMA],
    )
    return pl.pallas_call(
        _dispatch_kernel,
        grid_spec=grid_spec,
        out_shape=[jax.ShapeDtypeStruct((p_rows, d), F32), jax.ShapeDtypeStruct((TOP_K, t), I32)],
        compiler_params=pltpu.CompilerParams(dimension_semantics=("arbitrary",),
                                             vmem_limit_bytes=VMEM_LIMIT),
        name="dispatch",
    )(*tables, xn, idx_t, lrank_t, loc.reshape(nblk, ne, 1))


def _expert_kernel(be_ref, nu_ref, xs_ref, wgu_ref, bgu_ref, wdn_ref, bdn_ref,
                   y_ref, wgu_bf, wdn_bf):
    i = pl.program_id(0)
    de = wdn_ref.shape[0]
    prev = be_ref[jnp.maximum(i - 1, 0)]
    fresh = jnp.logical_or(i == 0, be_ref[i] != prev)
    active = i < nu_ref[0]

    @pl.when(jnp.logical_and(active, fresh))
    def _():
        wgu_bf[...] = wgu_ref[...].astype(BF16)
        wdn_bf[...] = wdn_ref[...].astype(BF16)

    @pl.when(active)
    def _():
        hgu = _dot(xs_ref[...].astype(BF16), wgu_bf[...]) + bgu_ref[...]
        gl = jnp.minimum(hgu[:, :de], SWIGLU_LIMIT)
        up = jnp.clip(hgu[:, de:], -SWIGLU_LIMIT, SWIGLU_LIMIT)
        act = (up + 1.0) * (gl * jax.nn.sigmoid(SWIGLU_ALPHA * gl))
        y_ref[...] = _dot(act.astype(BF16), wdn_bf[...]) + bdn_ref[...]

    @pl.when(jnp.logical_not(active))
    def _():
        y_ref[...] = jnp.zeros_like(y_ref)


def _experts(xs, block_e, n_used, w_gate_up, b_gate_up, w_down, b_down):
    p, d = xs.shape
    ne, _, de2 = w_gate_up.shape
    de = de2 // 2
    bm = EXPERT_ROWS
    row_blk = lambda i, be, nu: (jnp.minimum(i, nu[0] - 1), 0)
    grid_spec = pltpu.PrefetchScalarGridSpec(
        num_scalar_prefetch=2,
        grid=(p // bm,),
        in_specs=[
            pl.BlockSpec((bm, d), row_blk),
            pl.BlockSpec((None, d, de2), lambda i, be, nu: (be[i], 0, 0)),
            pl.BlockSpec((None, 1, de2), lambda i, be, nu: (be[i], 0, 0)),
            pl.BlockSpec((None, de, d), lambda i, be, nu: (be[i], 0, 0)),
            pl.BlockSpec((None, 1, d), lambda i, be, nu: (be[i], 0, 0)),
        ],
        out_specs=pl.BlockSpec((bm, d), lambda i, be, nu: (i, 0)),
        scratch_shapes=[pltpu.VMEM((d, de2), BF16), pltpu.VMEM((de, d), BF16)],
    )
    return pl.pallas_call(
        _expert_kernel,
        grid_spec=grid_spec,
        out_shape=jax.ShapeDtypeStruct((p, d), F32),
        compiler_params=pltpu.CompilerParams(dimension_semantics=("arbitrary",),
                                             vmem_limit_bytes=VMEM_LIMIT),
        name="experts",
    )(block_e, n_used, xs, w_gate_up, b_gate_up.reshape(ne, 1, de2), w_down, b_down.reshape(ne, 1, d))


def _combine_kernel(n8_ref, loc_ref, run_ref, tot_ref,
                    pos_ref, gate_ref, x2_ref, gf_ref, y_hbm, o_ref, buf, sems):
    b = pl.program_id(0)
    nblk = pl.num_programs(0)
    tb = x2_ref.shape[0]
    r_loc = buf.shape[1]
    ne = n8_ref.shape[0] // tot_ref.shape[0]
    slot = b % 2

    def fetch(blk, s):
        for e in range(ne):
            n = n8_ref[blk * ne + e]
            dst0 = loc_ref[blk * ne + e]
            src0 = run_ref[blk * ne + e]
            _for_each_chunk(n, tb, lambda off, size, src0=src0, dst0=dst0: pltpu.make_async_copy(
                y_hbm.at[pl.ds(pl.multiple_of(src0 + off, RUN_ALIGN), size)],
                buf.at[s, pl.ds(pl.multiple_of(dst0 + off, RUN_ALIGN), size)], sems.at[s]).start())

    @pl.when(b == 0)
    def _():
        buf[...] = jnp.zeros_like(buf)
        fetch(0, 0)

    @pl.when(b + 1 < nblk)
    def _():
        fetch(b + 1, 1 - slot)

    _wait_rows(tot_ref[b], r_loc, y_hbm, buf.at[slot], sems.at[slot])

    iota_c = lax.broadcasted_iota(I32, (tb, r_loc), 1)
    w = jnp.zeros((tb, r_loc), F32)
    for k in range(TOP_K):
        w = w + jnp.where(iota_c == pos_ref[:, k:k + 1], gate_ref[:, k:k + 1], 0.0)
    w_hi = w.astype(BF16)
    w_lo = (w - w_hi.astype(F32)).astype(BF16)
    y = buf[slot].astype(BF16)
    o_ref[...] = _rms(x2_ref[...] + _dot(w_hi, y) + _dot(w_lo, y), gf_ref[...])


def _combine(pos, gates, x2, y, norm_final, tables, r_loc):
    t, d = x2.shape
    tb = ROUTE_ROWS
    grid_spec = pltpu.PrefetchScalarGridSpec(
        num_scalar_prefetch=4,
        grid=(t // tb,),
        in_specs=[pl.BlockSpec((tb, TOP_K), lambda b, *_: (b, 0)),
                  pl.BlockSpec((tb, TOP_K), lambda b, *_: (b, 0)),
                  pl.BlockSpec((tb, d), lambda b, *_: (b, 0)),
                  pl.BlockSpec((1, d), lambda b, *_: (0, 0)),
                  pl.BlockSpec(memory_space=pl.ANY)],
        out_specs=pl.BlockSpec((tb, d), lambda b, *_: (b, 0)),
        scratch_shapes=[pltpu.VMEM((2, r_loc, d), F32), pltpu.SemaphoreType.DMA((2,))],
    )
    return pl.pallas_call(
        _combine_kernel,
        grid_spec=grid_spec,
        out_shape=jax.ShapeDtypeStruct((t, d), F32),
        compiler_params=pltpu.CompilerParams(dimension_semantics=("arbitrary",),
                                             vmem_limit_bytes=VMEM_LIMIT),
        name="combine",
    )(*tables, pos, gates, x2, norm_final.reshape(1, d), y)


def _excl_cumsum(a, axis):
    n = a.shape[axis]
    a = jnp.moveaxis(a, axis, -1)
    earlier = jnp.arange(n)[None, :] < jnp.arange(n)[:, None]
    out = jnp.sum(jnp.where(earlier, a[..., None, :], 0), axis=-1)
    return jnp.moveaxis(out, -1, axis)


def _layout(cnt):
    nblk, ne = cnt.shape
    tb, bm = ROUTE_ROWS, EXPERT_ROWS
    n8 = (cnt + RUN_ALIGN - 1) // RUN_ALIGN * RUN_ALIGN
    loc = _excl_cumsum(n8, 1)
    tot = jnp.sum(n8, axis=1)
    size = jnp.sum(n8, axis=0)
    padded = (size + bm - 1) // bm * bm
    pstart = _excl_cumsum(padded, 0)
    pend = pstart + padded
    run = pstart[None, :] + _excl_cumsum(n8, 0)
    p_blocks = -(-(nblk * tb * TOP_K + nblk * ne * (RUN_ALIGN - 1) + ne * (bm - RUN_ALIGN)) // bm)
    n_used = (pend[-1] // bm).astype(I32)
    blk_start = jnp.minimum(jnp.arange(p_blocks, dtype=I32), n_used - 1) * bm
    block_e = jnp.minimum(jnp.sum((blk_start[:, None] >= pend[None, :]).astype(I32), axis=1), ne - 1)
    flat = lambda a: a.reshape(-1).astype(I32)
    tables = (flat(n8), flat(loc), flat(run), flat(tot))
    half = bm // 2
    tails = (flat(jnp.concatenate([padded - size, (p_blocks * bm - pend[-1:]) // half])),
             flat(jnp.concatenate([pstart + size, pend[-1:]])))
    r_loc = -(-(tb * TOP_K + ne * (RUN_ALIGN - 1)) // bm) * bm
    return tables, tails, loc.astype(I32), block_e.astype(I32), n_used.reshape(1), p_blocks * bm, r_loc


def kernel(x, mem, norm_mix, w_in, w_pool_group, pool_scale, w_pool_proj, conv_w, conv_b, lru_w_a, lru_b_a, lru_w_x, lru_b_x, lru_lambda, w_lru_proj, w_mix_out, norm_xattn, norm_mem, w_q, w_kv, w_o, norm_moe, w_router, b_router, w_gate_up, b_gate_up, w_down, b_down, norm_final):
    nb, s_len, d = x.shape
    m_len = mem.shape[1]
    assert norm_mix.shape[0] == 1, "single-layer stack"
    l = 0
    xt = jnp.transpose(x, (1, 0, 2)).reshape(s_len * nb, d)
    x1 = _mixer(xt, nb, norm_mix[l], w_in[l], w_pool_group[l], pool_scale[l], w_pool_proj[l],
                conv_w[l], conv_b[l], lru_w_a[l], lru_b_a[l], lru_w_x[l], lru_b_x[l],
                lru_lambda[l], w_lru_proj[l], w_mix_out[l])
    kv = _kv_proj(mem.reshape(nb * m_len, d), norm_mem[l], w_kv[l]).reshape(nb, m_len, 2 * d)
    x2, xn, idx_t, gate_t, lrank_t, cnt = _attention(
        x1.reshape(s_len, nb * d), kv, nb, norm_xattn[l], w_q[l], w_o[l],
        norm_moe[l], w_router[l], b_router[l])
    tables, tails, loc, block_e, n_used, p_rows, r_loc = _layout(cnt.reshape(cnt.shape[0], -1))
    xs, pos_t = _dispatch(xn, idx_t, lrank_t, loc, tables + tails, p_rows, r_loc)
    y = _experts(xs, block_e, n_used, w_gate_up[l], b_gate_up[l], w_down[l], b_down[l])
    out = _combine(pos_t.T, gate_t.T, x2.reshape(nb * s_len, d), y, norm_final, tables, r_loc)
    return out.reshape(nb, s_len, d)
```

```python
import functools

import jax
import jax.numpy as jnp
from jax import lax
from jax.experimental import pallas as pl
from jax.experimental.pallas import tpu as pltpu

POOL_WINDOWS = (2, 4, 8, 16)
N_GROUPS = 4
CONV_WIDTH = 4
RG_C = 8.0
N_EXPERTS = 32
TOP_K = 4
SWIGLU_LIMIT = 7.0
SWIGLU_ALPHA = 1.702
RMS_EPS = 1e-6

MIX_STEPS = 32
KV_ROWS = 512
ATT_ROWS = 512
ROUTE_ROWS = 256
EXPERT_ROWS = 512
RUN_ALIGN = 8
VMEM_LIMIT = 52 * 1024 * 1024

BF16 = jnp.bfloat16
F32 = jnp.float32
I32 = jnp.int32


def _const_spec(shape):
    nd = len(shape)
    return pl.BlockSpec(shape, lambda *_: (0,) * nd, pipeline_mode=pl.Buffered(1))


def _rms(x, g):
    return x * lax.rsqrt(jnp.mean(x * x, axis=-1, keepdims=True) + RMS_EPS) * g


def _dot(a, b):
    return jnp.dot(a, b, preferred_element_type=F32)


def _dot_nt(a, b):
    return lax.dot_general(a, b, (((1,), (1,)), ((), ())), preferred_element_type=F32)


def _mixer_kernel(x_ref, nm_ref, win_ref, wpg_ref, psc_ref, wpp_ref, cw_ref, cb_ref,
                  wa_ref, ba_ref, wx_ref, bx_ref, lam_ref, wlp_ref, wmo_ref,
                  o_ref,
                  h_ref, up_ref, ul_ref, a_ref, b_ref, pm_ref, m_ref, mb_ref, hc_ref,
                  *, nb, ts):
    rows, d = x_ref.shape
    gw = d // N_GROUPS
    halo_p = (POOL_WINDOWS[-1]) * nb
    halo_c = (CONV_WIDTH - 1) * nb
    c = pl.program_id(0)

    @pl.when(c == 0)
    def _():
        up_ref[0:halo_p, :] = jnp.zeros((halo_p, d), F32)
        ul_ref[0:halo_c, :] = jnp.zeros((halo_c, d), F32)
        hc_ref[...] = jnp.zeros_like(hc_ref)

    h_ref[...] = _rms(x_ref[...], nm_ref[...]).astype(BF16)

    t_glob = c * ts + lax.broadcasted_iota(I32, (rows, 1), 0) // nb

    for g, w in enumerate(POOL_WINDOWS):
        cols = slice(g * gw, (g + 1) * gw)
        u = _dot(h_ref[...], win_ref[:, cols])
        up_ref[halo_p:halo_p + rows, cols] = u
        acc = u
        for j in range(1, w):
            acc = acc + up_ref[halo_p - j * nb:halo_p - j * nb + rows, cols]
        cnt = jnp.minimum(t_glob + 1, w).astype(F32)
        p = acc / cnt - u
        pg = _dot(p.astype(BF16), wpg_ref[g]) * psc_ref[:, cols]
        pm_ref[:, cols] = pg.astype(BF16)
    for g in range(N_GROUPS):
        cols = slice(g * gw, (g + 1) * gw)
        ya = _dot(pm_ref[...], wpp_ref[:, cols])
        ga = _dot(h_ref[...], win_ref[:, 3 * d + g * gw:3 * d + (g + 1) * gw])
        m_ref[:, cols] = jax.nn.sigmoid(ga) * ya

    for g in range(N_GROUPS):
        cols = slice(g * gw, (g + 1) * gw)
        ul_ref[halo_c:halo_c + rows, cols] = _dot(h_ref[...], win_ref[:, d + g * gw:d + (g + 1) * gw])
        xr = cb_ref[:, cols]
        for k in range(CONV_WIDTH):
            off = halo_c - (CONV_WIDTH - 1 - k) * nb
            xr = xr + ul_ref[off:off + rows, cols] * cw_ref[k:k + 1, cols]
        xrb = xr.astype(BF16)
        r = jax.nn.sigmoid(_dot(xrb, wa_ref[g]) + ba_ref[:, cols])
        i = jax.nn.sigmoid(_dot(xrb, wx_ref[g]) + bx_ref[:, cols])
        lam = lam_ref[:, cols]
        log_sig = jnp.minimum(lam, 0.0) - jnp.log(1.0 + jnp.exp(-jnp.abs(lam)))
        a = jnp.exp((RG_C * r) * log_sig)
        mult = jnp.sqrt(jnp.maximum(1.0 - a * a, 0.0))
        mult = jnp.where(t_glob == 0, 1.0, mult)
        a_ref[:, cols] = a
        b_ref[:, cols] = mult * i * xr

    def scan_step(t, hprev):
        sl = pl.ds(pl.multiple_of(t * nb, nb), nb)
        hn = a_ref[sl, :] * hprev + b_ref[sl, :]
        b_ref[sl, :] = hn
        return hn

    hc_ref[...] = lax.fori_loop(0, ts, scan_step, hc_ref[...], unroll=4)

    for g in range(N_GROUPS):
        cols = slice(g * gw, (g + 1) * gw)
        gl = jax.nn.gelu(_dot(h_ref[...], win_ref[:, 2 * d + g * gw:2 * d + (g + 1) * gw]),
                         approximate=True)
        pm_ref[:, cols] = (b_ref[:, cols] * gl).astype(BF16)
    for g in range(N_GROUPS):
        cols = slice(g * gw, (g + 1) * gw)
        yb = _dot(pm_ref[...], wlp_ref[:, cols])
        gb = _dot(h_ref[...], win_ref[:, 4 * d + g * gw:4 * d + (g + 1) * gw])
        mb_ref[:, cols] = (m_ref[:, cols] + jax.nn.sigmoid(gb) * yb).astype(BF16)
    for g in range(N_GROUPS):
        cols = slice(g * gw, (g + 1) * gw)
        o_ref[:, cols] = x_ref[:, cols] + _dot(mb_ref[...], wmo_ref[:, cols])

    up_ref[0:halo_p, :] = up_ref[rows:rows + halo_p, :]
    ul_ref[0:halo_c, :] = ul_ref[rows:rows + halo_c, :]


def _mixer(xt, nb, norm_mix, w_in, w_pool_group, pool_scale, w_pool_proj, conv_w, conv_b,
           lru_w_a, lru_b_a, lru_w_x, lru_b_x, lru_lambda, w_lru_proj, w_mix_out):
    n_rows, d = xt.shape
    ts = MIX_STEPS
    rows = ts * nb
    assert n_rows % rows == 0 and rows >= POOL_WINDOWS[-1] * nb and nb % 8 == 0
    row2 = lambda v: v.reshape(1, -1)
    args = (xt, row2(norm_mix), w_in.astype(BF16), w_pool_group.astype(BF16), row2(pool_scale),
            w_pool_proj.astype(BF16), conv_w, row2(conv_b), lru_w_a.astype(BF16), row2(lru_b_a),
            lru_w_x.astype(BF16), row2(lru_b_x), row2(lru_lambda), w_lru_proj.astype(BF16),
            w_mix_out.astype(BF16))
    in_specs = [pl.BlockSpec((rows, d), lambda c: (c, 0))] + [_const_spec(a.shape) for a in args[1:]]
    halo_p = POOL_WINDOWS[-1] * nb
    halo_c = (CONV_WIDTH - 1) * nb
    return pl.pallas_call(
        functools.partial(_mixer_kernel, nb=nb, ts=ts),
        grid=(n_rows // rows,),
        in_specs=in_specs,
        out_specs=pl.BlockSpec((rows, d), lambda c: (c, 0)),
        out_shape=jax.ShapeDtypeStruct((n_rows, d), F32),
        scratch_shapes=[
            pltpu.VMEM((rows, d), BF16),
            pltpu.VMEM((halo_p + rows, d), F32),
            pltpu.VMEM((halo_c + rows, d), F32),
            pltpu.VMEM((rows, d), F32),
            pltpu.VMEM((rows, d), F32),
            pltpu.VMEM((rows, d), BF16),
            pltpu.VMEM((rows, d), F32),
            pltpu.VMEM((rows, d), BF16),
            pltpu.VMEM((nb, d), F32),
        ],
        compiler_params=pltpu.CompilerParams(dimension_semantics=("arbitrary",),
                                             vmem_limit_bytes=VMEM_LIMIT),
        name="mixer",
    )(*args)


def _kv_kernel(m_ref, g_ref, w_ref, o_ref):
    o_ref[...] = _dot(_rms(m_ref[...], g_ref[...]).astype(BF16), w_ref[...]).astype(BF16)


def _kv_proj(mem2d, norm_mem, w_kv):
    n, d = mem2d.shape
    assert n % KV_ROWS == 0
    return pl.pallas_call(
        _kv_kernel,
        grid=(n // KV_ROWS,),
        in_specs=[pl.BlockSpec((KV_ROWS, d), lambda i: (i, 0)),
                  _const_spec((1, d)), _const_spec((d, 2 * d))],
        out_specs=pl.BlockSpec((KV_ROWS, 2 * d), lambda i: (i, 0)),
        out_shape=jax.ShapeDtypeStruct((n, 2 * d), BF16),
        compiler_params=pltpu.CompilerParams(dimension_semantics=("arbitrary",),
                                             vmem_limit_bytes=VMEM_LIMIT),
        name="kv_proj",
    )(mem2d, norm_mem.reshape(1, d), w_kv.astype(BF16))


def _attn_kernel(x_ref, kv_ref, gx_ref, wq_ref, wo_ref, gm_ref, wr_ref, br_ref,
                 x2_ref, xn_ref, idx_ref, gate_ref, lrank_ref, cnt_ref, o_scr, xh_scr, xl_scr):
    tq, d = x_ref.shape
    hd = d // N_GROUPS
    ne = br_ref.shape[0]
    tb = ROUTE_ROWS
    x = x_ref[...]
    q = _dot(_rms(x, gx_ref[...]).astype(BF16), wq_ref[...]).astype(BF16)
    for h in range(N_GROUPS):
        k = kv_ref[:, h * hd:(h + 1) * hd]
        v = kv_ref[:, d + h * hd:d + (h + 1) * hd]
        s = _dot_nt(q[:, h * hd:(h + 1) * hd], k) * (hd ** -0.5)
        e = jnp.exp(s - jnp.max(s, axis=-1, keepdims=True))
        p = e / jnp.sum(e, axis=-1, keepdims=True)
        o_scr[:, h * hd:(h + 1) * hd] = _dot(p.astype(BF16), v).astype(BF16)
    x2 = x + _dot(o_scr[...], wo_ref[...])
    x2_ref[...] = x2
    xn = _rms(x2, gm_ref[...])
    xh = xn.astype(BF16)
    xn_ref[...] = xh
    xh_scr[...] = xh
    xl_scr[...] = (xn - xh.astype(F32)).astype(BF16)

    iota_f = lax.broadcasted_iota(I32, (ne, tb), 0).astype(F32)
    before = (lax.broadcasted_iota(I32, (tb, tb), 0) < lax.broadcasted_iota(I32, (tb, tb), 1)
              ).astype(BF16)
    for sb in range(tq // tb):
        rows = slice(sb * tb, (sb + 1) * tb)
        ph = _dot_nt(wr_ref[...], xh_scr[rows, :])
        pl_ = _dot_nt(wr_ref[0:ne, :], xl_scr[rows, :])
        work = ph[0:ne] + ph[ne:2 * ne] + pl_ + br_ref[...]
        vals, idxs, sels = [], [], []
        for _ in range(TOP_K):
            m = jnp.max(work, axis=0, keepdims=True)
            idx = jnp.min(jnp.where(work == m, iota_f, float(ne)), axis=0, keepdims=True)
            sel = iota_f == idx
            vals.append(m)
            idxs.append(idx.astype(I32))
            sels.append(sel)
            work = jnp.where(sel, -jnp.inf, work)
        ex = [jnp.exp(v - vals[0]) for v in vals]
        den = ex[0] + ex[1] + ex[2] + ex[3]
        onehot = jnp.zeros((ne, tb), F32)
        for sel in sels:
            onehot = onehot + sel.astype(F32)
        oh16 = onehot.astype(BF16)
        prefix = _dot(oh16, before)
        lr = [jnp.sum(jnp.where(sel, prefix, 0.0), axis=0, keepdims=True).astype(I32) for sel in sels]
        idx_ref[:, rows] = jnp.concatenate(idxs, axis=0)
        gate_ref[:, rows] = jnp.concatenate([e_ / den for e_ in ex], axis=0)
        lrank_ref[:, rows] = jnp.concatenate(lr, axis=0)
        cnt_ref[sb] = _dot_nt(jnp.ones((1, tb), BF16), oh16).astype(I32)


def _attention(x1_sbd, kv, nb, norm_xattn, w_q, w_o, norm_moe, w_router, b_router):
    s_len, bd = x1_sbd.shape
    d = bd // nb
    m = kv.shape[1]
    ne = w_router.shape[-1]
    tq, tb = ATT_ROWS, ROUTE_ROWS
    nq = s_len // tq
    t = nb * s_len
    assert s_len % tq == 0 and tq % tb == 0
    wr_hi = w_router.astype(BF16)
    wr_lo = (w_router - wr_hi.astype(F32)).astype(BF16)
    wr2t = jnp.concatenate([wr_hi, wr_lo], axis=1).T
    tok = lambda b, i: (0, b * nq + i)
    return pl.pallas_call(
        _attn_kernel,
        grid=(nb, nq),
        in_specs=[pl.BlockSpec((tq, d), lambda b, i: (i, b)),
                  pl.BlockSpec((None, m, 2 * d), lambda b, i: (b, 0, 0)),
                  _const_spec((1, d)), _const_spec((d, d)), _const_spec((d, d)),
                  _const_spec((1, d)), _const_spec((2 * ne, d)), _const_spec((ne, 1))],
        out_specs=[pl.BlockSpec((None, tq, d), lambda b, i: (b, i, 0)),
                   pl.BlockSpec((tq, d), lambda b, i: (b * nq + i, 0)),
                   pl.BlockSpec((TOP_K, tq), tok),
                   pl.BlockSpec((TOP_K, tq), tok),
                   pl.BlockSpec((TOP_K, tq), tok),
                   pl.BlockSpec((tq // tb, 1, ne), lambda b, i: (b * nq + i, 0, 0))],
        out_shape=[jax.ShapeDtypeStruct((nb, s_len, d), F32),
                   jax.ShapeDtypeStruct((t, d), BF16),
                   jax.ShapeDtypeStruct((TOP_K, t), I32),
                   jax.ShapeDtypeStruct((TOP_K, t), F32),
                   jax.ShapeDtypeStruct((TOP_K, t), I32),
                   jax.ShapeDtypeStruct((t // tb, 1, ne), I32)],
        scratch_shapes=[pltpu.VMEM((tq, d), BF16), pltpu.VMEM((tq, d), BF16), pltpu.VMEM((tq, d), BF16)],
        compiler_params=pltpu.CompilerParams(dimension_semantics=("arbitrary", "arbitrary"),
                                             vmem_limit_bytes=VMEM_LIMIT),
        name="attention",
    )(x1_sbd, kv, norm_xattn.reshape(1, d), w_q.astype(BF16), w_o.astype(BF16),
      norm_moe.reshape(1, d), wr2t, b_router.reshape(ne, 1))


def _pow2_chunks(limit):
    sizes = []
    c = RUN_ALIGN
    while c <= limit:
        sizes.append(c)
        c *= 2
    return sizes[::-1]


def _for_each_chunk(n, limit, fn):
    for size in _pow2_chunks(limit):
        @pl.when((n & size) != 0)
        def _(size=size):
            fn(pl.multiple_of(n & ~(2 * size - 1), RUN_ALIGN), size)


def _wait_rows(n, limit, src, dst, sem):
    _for_each_chunk(n, limit, lambda off, size: pltpu.make_async_copy(
        src.at[pl.ds(0, size)], dst.at[pl.ds(0, size)], sem).wait())


def _dispatch_kernel(n8_ref, loc_ref, run_ref, tot_ref, tailn_ref, tails_ref,
                     xn_ref, idx_ref, lrank_ref, locv_ref, xs_hbm, pos_ref, buf, zbuf, sems, zsem):
    b = pl.program_id(0)
    nblk = pl.num_programs(0)
    tb = xn_ref.shape[0]
    r_loc = buf.shape[1]
    ne = tailn_ref.shape[0] - 1
    slot = b % 2

    zrows = zbuf.shape[0]

    def zero_rest(i, carry):
        dst = pl.multiple_of(tails_ref[ne] + i * zrows, zrows)
        pltpu.make_async_copy(zbuf, xs_hbm.at[pl.ds(dst, zrows)], zsem).start()
        return carry

    def wait_rest(i, carry):
        pltpu.make_async_copy(zbuf, xs_hbm.at[pl.ds(0, zrows)], zsem).wait()
        return carry

    @pl.when(b == 0)
    def _():
        zbuf[...] = jnp.zeros_like(zbuf)
        for e in range(ne):
            _for_each_chunk(tailn_ref[e], zrows, lambda off, size, e=e: pltpu.make_async_copy(
                zbuf.at[pl.ds(0, size)],
                xs_hbm.at[pl.ds(pl.multiple_of(tails_ref[e] + off, RUN_ALIGN), size)], zsem).start())
        lax.fori_loop(0, tailn_ref[ne], zero_rest, 0)

    @pl.when(b >= 2)
    def _():
        _wait_rows(tot_ref[b - 2], r_loc, buf.at[slot], xs_hbm, sems.at[slot])

    iota_e = lax.broadcasted_iota(I32, (ne, tb), 0)
    loc_col = locv_ref[...].astype(F32)
    for k in range(TOP_K):
        run0 = jnp.sum(jnp.where(iota_e == idx_ref[k:k + 1, :], loc_col, 0.0), axis=0, keepdims=True)
        pos_ref[k:k + 1, :] = run0.astype(I32) + lrank_ref[k:k + 1, :]

    iota_r = lax.broadcasted_iota(I32, (r_loc, tb), 0)
    hit = iota_r == pos_ref[0:1, :]
    for k in range(1, TOP_K):
        hit = jnp.logical_or(hit, iota_r == pos_ref[k:k + 1, :])
    buf[slot] = _dot(hit.astype(BF16), xn_ref[...])

    for e in range(ne):
        n = n8_ref[b * ne + e]
        src0 = loc_ref[b * ne + e]
        dst0 = run_ref[b * ne + e]
        _for_each_chunk(n, tb, lambda off, size, src0=src0, dst0=dst0: pltpu.make_async_copy(
            buf.at[slot, pl.ds(pl.multiple_of(src0 + off, RUN_ALIGN), size)],
            xs_hbm.at[pl.ds(pl.multiple_of(dst0 + off, RUN_ALIGN), size)], sems.at[slot]).start())

    @pl.when(b == nblk - 1)
    def _():
        @pl.when(b >= 1)
        def _():
            _wait_rows(tot_ref[b - 1], r_loc, buf.at[1 - slot], xs_hbm, sems.at[1 - slot])
        _wait_rows(tot_ref[b], r_loc, buf.at[slot], xs_hbm, sems.at[slot])
        for e in range(ne):
            _wait_rows(tailn_ref[e], zrows, zbuf, xs_hbm, zsem)
        lax.fori_loop(0, tailn_ref[ne], wait_rest, 0)


def _dispatch(xn, idx_t, lrank_t, loc, tables, p_rows, r_loc):
    t, d = xn.shape
    tb = ROUTE_ROWS
    nblk, ne = loc.shape
    tok = pl.BlockSpec((TOP_K, tb), lambda b, *_: (0, b))
    grid_spec = pltpu.PrefetchScalarGridSpec(
        num_scalar_prefetch=6,
        grid=(t // tb,),
        in_specs=[pl.BlockSpec((tb, d), lambda b, *_: (b, 0)), tok, tok,
                  pl.BlockSpec((None, ne, 1), lambda b, *_: (b, 0, 0))],
        out_specs=[pl.BlockSpec(memory_space=pl.ANY), tok],
        scratch_shapes=[pltpu.VMEM((2, r_loc, d), F32), pltpu.VMEM((EXPERT_ROWS // 2, d), F32),
                        pltpu.SemaphoreType.DMA((2,)), pltpu.SemaphoreType.DMA],
    )
    return pl.pallas_call(
        _dispatch_kernel,
        grid_spec=grid_spec,
        out_shape=[jax.ShapeDtypeStruct((p_rows, d), F32), jax.ShapeDtypeStruct((TOP_K, t), I32)],
        compiler_params=pltpu.CompilerParams(dimension_semantics=("arbitrary",),
                                             vmem_limit_bytes=VMEM_LIMIT),
        name="dispatch",
    )(*tables, xn, idx_t, lrank_t, loc.reshape(nblk, ne, 1))


def _expert_kernel(be_ref, nu_ref, xs_ref, wgu_ref, bgu_ref, wdn_ref, bdn_ref,
                   y_ref, wgu_bf, wdn_bf):
    i = pl.program_id(0)
    de = wdn_ref.shape[0]
    prev = be_ref[jnp.maximum(i - 1, 0)]
    fresh = jnp.logical_or(i == 0, be_ref[i] != prev)
    active = i < nu_ref[0]

    @pl.when(jnp.logical_and(active, fresh))
    def _():
        wgu_bf[...] = wgu_ref[...].astype(BF16)
        wdn_bf[...] = wdn_ref[...].astype(BF16)

    @pl.when(active)
    def _():
        hgu = _dot(xs_ref[...].astype(BF16), wgu_bf[...]) + bgu_ref[...]
        gl = jnp.minimum(hgu[:, :de], SWIGLU_LIMIT)
        up = jnp.clip(hgu[:, de:], -SWIGLU_LIMIT, SWIGLU_LIMIT)
        act = (up + 1.0) * (gl * jax.nn.sigmoid(SWIGLU_ALPHA * gl))
        y_ref[...] = _dot(act.astype(BF16), wdn_bf[...]) + bdn_ref[...]

    @pl.when(jnp.logical_not(active))
    def _():
        y_ref[...] = jnp.zeros_like(y_ref)


def _experts(xs, block_e, n_used, w_gate_up, b_gate_up, w_down, b_down):
    p, d = xs.shape
    ne, _, de2 = w_gate_up.shape
    de = de2 // 2
    bm = EXPERT_ROWS
    row_blk = lambda i, be, nu: (jnp.minimum(i, nu[0] - 1), 0)
    grid_spec = pltpu.PrefetchScalarGridSpec(
        num_scalar_prefetch=2,
        grid=(p // bm,),
        in_specs=[
            pl.BlockSpec((bm, d), row_blk),
            pl.BlockSpec((None, d, de2), lambda i, be, nu: (be[i], 0, 0)),
            pl.BlockSpec((None, 1, de2), lambda i, be, nu: (be[i], 0, 0)),
            pl.BlockSpec((None, de, d), lambda i, be, nu: (be[i], 0, 0)),
            pl.BlockSpec((None, 1, d), lambda i, be, nu: (be[i], 0, 0)),
        ],
        out_specs=pl.BlockSpec((bm, d), lambda i, be, nu: (i, 0)),
        scratch_shapes=[pltpu.VMEM((d, de2), BF16), pltpu.VMEM((de, d), BF16)],
    )
    return pl.pallas_call(
        _expert_kernel,
        grid_spec=grid_spec,
        out_shape=jax.ShapeDtypeStruct((p, d), F32),
        compiler_params=pltpu.CompilerParams(dimension_semantics=("arbitrary",),
                                             vmem_limit_bytes=VMEM_LIMIT),
        name="experts",
    )(block_e, n_used, xs, w_gate_up, b_gate_up.reshape(ne, 1, de2), w_down, b_down.reshape(ne, 1, d))


def _combine_kernel(n8_ref, loc_ref, run_ref, tot_ref,
                    pos_ref, gate_ref, x2_ref, gf_ref, y_hbm, o_ref, buf, sems):
    b = pl.program_id(0)
    nblk = pl.num_programs(0)
    tb = x2_ref.shape[0]
    r_loc = buf.shape[1]
    ne = n8_ref.shape[0] // tot_ref.shape[0]
    slot = b % 2

    def fetch(blk, s):
        for e in range(ne):
            n = n8_ref[blk * ne + e]
            dst0 = loc_ref[blk * ne + e]
            src0 = run_ref[blk * ne + e]
            _for_each_chunk(n, tb, lambda off, size, src0=src0, dst0=dst0: pltpu.make_async_copy(
                y_hbm.at[pl.ds(pl.multiple_of(src0 + off, RUN_ALIGN), size)],
                buf.at[s, pl.ds(pl.multiple_of(dst0 + off, RUN_ALIGN), size)], sems.at[s]).start())

    @pl.when(b == 0)
    def _():
        buf[...] = jnp.zeros_like(buf)
        fetch(0, 0)

    @pl.when(b + 1 < nblk)
    def _():
        fetch(b + 1, 1 - slot)

    _wait_rows(tot_ref[b], r_loc, y_hbm, buf.at[slot], sems.at[slot])

    iota_c = lax.broadcasted_iota(I32, (tb, r_loc), 1)
    w = jnp.zeros((tb, r_loc), F32)
    for k in range(TOP_K):
        w = w + jnp.where(iota_c == pos_ref[:, k:k + 1], gate_ref[:, k:k + 1], 0.0)
    w_hi = w.astype(BF16)
    w_lo = (w - w_hi.astype(F32)).astype(BF16)
    y = buf[slot].astype(BF16)
    o_ref[...] = _rms(x2_ref[...] + _dot(w_hi, y) + _dot(w_lo, y), gf_ref[...])


def _combine(pos, gates, x2, y, norm_final, tables, r_loc):
    t, d = x2.shape
    tb = ROUTE_ROWS
    grid_spec = pltpu.PrefetchScalarGridSpec(
        num_scalar_prefetch=4,
        grid=(t // tb,),
        in_specs=[pl.BlockSpec((tb, TOP_K), lambda b, *_: (b, 0)),
                  pl.BlockSpec((tb, TOP_K), lambda b, *_: (b, 0)),
                  pl.BlockSpec((tb, d), lambda b, *_: (b, 0)),
                  pl.BlockSpec((1, d), lambda b, *_: (0, 0)),
                  pl.BlockSpec(memory_space=pl.ANY)],
        out_specs=pl.BlockSpec((tb, d), lambda b, *_: (b, 0)),
        scratch_shapes=[pltpu.VMEM((2, r_loc, d), F32), pltpu.SemaphoreType.DMA((2,))],
    )
    return pl.pallas_call(
        _combine_kernel,
        grid_spec=grid_spec,
        out_shape=jax.ShapeDtypeStruct((t, d), F32),
        compiler_params=pltpu.CompilerParams(dimension_semantics=("arbitrary",),
                                             vmem_limit_bytes=VMEM_LIMIT),
        name="combine",
    )(*tables, pos, gates, x2, norm_final.reshape(1, d), y)


def _excl_cumsum(a, axis):
    n = a.shape[axis]
    a = jnp.moveaxis(a, axis, -1)
    earlier = jnp.arange(n)[None, :] < jnp.arange(n)[:, None]
    out = jnp.sum(jnp.where(earlier, a[..., None, :], 0), axis=-1)
    return jnp.moveaxis(out, -1, axis)


def _layout(cnt):
    nblk, ne = cnt.shape
    tb, bm = ROUTE_ROWS, EXPERT_ROWS
    n8 = (cnt + RUN_ALIGN - 1) // RUN_ALIGN * RUN_ALIGN
    loc = _excl_cumsum(n8, 1)
    tot = jnp.sum(n8, axis=1)
    size = jnp.sum(n8, axis=0)
    padded = (size + bm - 1) // bm * bm
    pstart = _excl_cumsum(padded, 0)
    pend = pstart + padded
    run = pstart[None, :] + _excl_cumsum(n8, 0)
    p_blocks = -(-(nblk * tb * TOP_K + nblk * ne * (RUN_ALIGN - 1) + ne * (bm - RUN_ALIGN)) // bm)
    n_used = (pend[-1] // bm).astype(I32)
    blk_start = jnp.minimum(jnp.arange(p_blocks, dtype=I32), n_used - 1) * bm
    block_e = jnp.minimum(jnp.sum((blk_start[:, None] >= pend[None, :]).astype(I32), axis=1), ne - 1)
    flat = lambda a: a.reshape(-1).astype(I32)
    tables = (flat(n8), flat(loc), flat(run), flat(tot))
    half = bm // 2
    tails = (flat(jnp.concatenate([padded - size, (p_blocks * bm - pend[-1:]) // half])),
             flat(jnp.concatenate([pstart + size, pend[-1:]])))
    r_loc = -(-(tb * TOP_K + ne * (RUN_ALIGN - 1)) // 256) * 256
    return tables, tails, loc.astype(I32), block_e.astype(I32), n_used.reshape(1), p_blocks * bm, r_loc


def kernel(x, mem, norm_mix, w_in, w_pool_group, pool_scale, w_pool_proj, conv_w, conv_b, lru_w_a, lru_b_a, lru_w_x, lru_b_x, lru_lambda, w_lru_proj, w_mix_out, norm_xattn, norm_mem, w_q, w_kv, w_o, norm_moe, w_router, b_router, w_gate_up, b_gate_up, w_down, b_down, norm_final):
    nb, s_len, d = x.shape
    m_len = mem.shape[1]
    assert norm_mix.shape[0] == 1, "single-layer stack"
    l = 0
    xt = jnp.transpose(x, (1, 0, 2)).reshape(s_len * nb, d)
    x1 = _mixer(xt, nb, norm_mix[l], w_in[l], w_pool_group[l], pool_scale[l], w_pool_proj[l],
                conv_w[l], conv_b[l], lru_w_a[l], lru_b_a[l], lru_w_x[l], lru_b_x[l],
                lru_lambda[l], w_lru_proj[l], w_mix_out[l])
    kv = _kv_proj(mem.reshape(nb * m_len, d), norm_mem[l], w_kv[l]).reshape(nb, m_len, 2 * d)
    x2, xn, idx_t, gate_t, lrank_t, cnt = _attention(
        x1.reshape(s_len, nb * d), kv, nb, norm_xattn[l], w_q[l], w_o[l],
        norm_moe[l], w_router[l], b_router[l])
    tables, tails, loc, block_e, n_used, p_rows, r_loc = _layout(cnt.reshape(cnt.shape[0], -1))
    xs, pos_t = _dispatch(xn, idx_t, lrank_t, loc, tables + tails, p_rows, r_loc)
    y = _experts(xs, block_e, n_used, w_gate_up[l], b_gate_up[l], w_down[l], b_down[l])
    out = _combine(pos_t.T, gate_t.T, x2.reshape(nb * s_len, d), y, norm_final, tables, r_loc)
    return out.reshape(nb, s_len, d)
```

```python
import functools

import jax
import jax.numpy as jnp
from jax import lax
from jax.experimental import pallas as pl
from jax.experimental.pallas import tpu as pltpu

POOL_WINDOWS = (2, 4, 8, 16)
N_GROUPS = 4
CONV_WIDTH = 4
RG_C = 8.0
N_EXPERTS = 32
TOP_K = 4
SWIGLU_LIMIT = 7.0
SWIGLU_ALPHA = 1.702
RMS_EPS = 1e-6

MIX_STEPS = 32
KV_ROWS = 512
ATT_ROWS = 512
ROUTE_ROWS = 256
EXPERT_ROWS = 512
RUN_ALIGN = 8
VMEM_LIMIT = 52 * 1024 * 1024

BF16 = jnp.bfloat16
F32 = jnp.float32
I32 = jnp.int32


def _const_spec(shape):
    nd = len(shape)
    return pl.BlockSpec(shape, lambda *_: (0,) * nd, pipeline_mode=pl.Buffered(1))


def _rms(x, g):
    return x * lax.rsqrt(jnp.mean(x * x, axis=-1, keepdims=True) + RMS_EPS) * g


def _dot(a, b):
    return jnp.dot(a, b, preferred_element_type=F32)


def _dot_nt(a, b):
    return lax.dot_general(a, b, (((1,), (1,)), ((), ())), preferred_element_type=F32)


def _mixer_kernel(x_hbm, nm_ref, win_ref, wpg_ref, psc_ref, wpp_ref, cw_ref, cb_ref,
                  wa_ref, ba_ref, wx_ref, bx_ref, lam_ref, wlp_ref, wmo_ref,
                  o_ref,
                  xbuf, xsem,
                  h_ref, up_ref, ul_ref, a_ref, b_ref, pm_ref, m_ref, mb_ref, t_ref, gl_ref, gb_ref, hc_ref,
                  *, nb, ts):
    rows, d = o_ref.shape
    gw = d // N_GROUPS
    halo_p = (POOL_WINDOWS[-1]) * nb
    halo_c = (CONV_WIDTH - 1) * nb
    c = pl.program_id(0)

    @pl.when(c == 0)
    def _():
        up_ref[0:halo_p, :] = jnp.zeros((halo_p, d), F32)
        ul_ref[0:halo_c, :] = jnp.zeros((halo_c, d), F32)
        hc_ref[...] = jnp.zeros_like(hc_ref)

    def x_copy(step, t, slot):
        return pltpu.make_async_copy(x_hbm.at[:, step * ts + t, :], xbuf.at[slot, t], xsem.at[slot])

    def fetch(step, slot):
        lax.fori_loop(0, ts, lambda t, carry: (x_copy(step, t, slot).start(), carry)[1], 0)

    slot = c % 2

    @pl.when(c == 0)
    def _():
        fetch(0, 0)

    @pl.when(c + 1 < pl.num_programs(0))
    def _():
        fetch(c + 1, 1 - slot)

    lax.fori_loop(0, ts, lambda t, carry: (x_copy(c, t, slot).wait(), carry)[1], 0)

    h_ref[...] = _rms(xbuf[slot].reshape(rows, d), nm_ref[...]).astype(BF16)

    t_glob = c * ts + lax.broadcasted_iota(I32, (rows, 1), 0) // nb

    up_ref[halo_p:halo_p + rows, :] = _dot(h_ref[...], win_ref[:, 0:d])
    ul_ref[halo_c:halo_c + rows, :] = _dot(h_ref[...], win_ref[:, d:2 * d])

    for g, w in enumerate(POOL_WINDOWS):
        cols = slice(g * gw, (g + 1) * gw)
        u = up_ref[halo_p:halo_p + rows, cols]
        acc = u
        for j in range(1, w):
            acc = acc + up_ref[halo_p - j * nb:halo_p - j * nb + rows, cols]
        cnt = jnp.minimum(t_glob + 1, w).astype(F32)
        p = acc / cnt - u
        pg = _dot(p.astype(BF16), wpg_ref[g]) * psc_ref[:, cols]
        pm_ref[:, cols] = pg.astype(BF16)
    t_ref[...] = _dot(h_ref[...], win_ref[:, 3 * d:4 * d])
    m_ref[...] = _dot(pm_ref[...], wpp_ref[...])

    for g in range(N_GROUPS):
        cols = slice(g * gw, (g + 1) * gw)
        xr = cb_ref[:, cols]
        for k in range(CONV_WIDTH):
            off = halo_c - (CONV_WIDTH - 1 - k) * nb
            xr = xr + ul_ref[off:off + rows, cols] * cw_ref[k:k + 1, cols]
        xrb = xr.astype(BF16)
        r = jax.nn.sigmoid(_dot(xrb, wa_ref[g]) + ba_ref[:, cols])
        i = jax.nn.sigmoid(_dot(xrb, wx_ref[g]) + bx_ref[:, cols])
        gl_ref[:, cols] = _dot(h_ref[...], win_ref[:, 2 * d + g * gw:2 * d + (g + 1) * gw])
        gb_ref[:, cols] = _dot(h_ref[...], win_ref[:, 4 * d + g * gw:4 * d + (g + 1) * gw])
        lam = lam_ref[:, cols]
        log_sig = jnp.minimum(lam, 0.0) - jnp.log(1.0 + jnp.exp(-jnp.abs(lam)))
        a = jnp.exp((RG_C * r) * log_sig)
        mult = jnp.sqrt(jnp.maximum(1.0 - a * a, 0.0))
        mult = jnp.where(t_glob == 0, 1.0, mult)
        a_ref[:, cols] = a
        b_ref[:, cols] = mult * i * xr
    m_ref[...] = jax.nn.sigmoid(t_ref[...]) * m_ref[...]

    def scan_step(t, hprev):
        sl = pl.ds(pl.multiple_of(t * nb, nb), nb)
        hn = a_ref[sl, :] * hprev + b_ref[sl, :]
        b_ref[sl, :] = hn
        return hn

    hc_ref[...] = lax.fori_loop(0, ts, scan_step, hc_ref[...], unroll=4)

    half = rows // 2
    for hs in (slice(0, half), slice(half, rows)):
        pm_ref[hs, :] = (b_ref[hs, :] * jax.nn.gelu(gl_ref[hs, :], approximate=True)).astype(BF16)
    for hs in (slice(0, half), slice(half, rows)):
        yb = _dot(pm_ref[hs, :], wlp_ref[...])
        mb_ref[hs, :] = (m_ref[hs, :] + jax.nn.sigmoid(gb_ref[hs, :]) * yb).astype(BF16)
    for i, hs in enumerate((slice(0, half), slice(half, rows))):
        x_half = xbuf[slot, i * (ts // 2):(i + 1) * (ts // 2)].reshape(half, d)
        o_ref[hs, :] = x_half + _dot(mb_ref[hs, :], wmo_ref[...])

    up_ref[0:halo_p, :] = up_ref[rows:rows + halo_p, :]
    ul_ref[0:halo_c, :] = ul_ref[rows:rows + halo_c, :]


def _mixer(x, norm_mix, w_in, w_pool_group, pool_scale, w_pool_proj, conv_w, conv_b,
           lru_w_a, lru_b_a, lru_w_x, lru_b_x, lru_lambda, w_lru_proj, w_mix_out):
    nb, s_len, d = x.shape
    ts = MIX_STEPS
    rows = ts * nb
    n_rows = s_len * nb
    assert s_len % ts == 0 and ts % 2 == 0 and ts >= POOL_WINDOWS[-1] and nb % 8 == 0
    row2 = lambda v: v.reshape(1, -1)
    args = (x, row2(norm_mix), w_in.astype(BF16), w_pool_group.astype(BF16), row2(pool_scale),
            w_pool_proj.astype(BF16), conv_w, row2(conv_b), lru_w_a.astype(BF16), row2(lru_b_a),
            lru_w_x.astype(BF16), row2(lru_b_x), row2(lru_lambda), w_lru_proj.astype(BF16),
            w_mix_out.astype(BF16))
    in_specs = [pl.BlockSpec(memory_space=pl.ANY)] + [_const_spec(a.shape) for a in args[1:]]
    halo_p = POOL_WINDOWS[-1] * nb
    halo_c = (CONV_WIDTH - 1) * nb
    return pl.pallas_call(
        functools.partial(_mixer_kernel, nb=nb, ts=ts),
        grid=(n_rows // rows,),
        in_specs=in_specs,
        out_specs=pl.BlockSpec((rows, d), lambda c: (c, 0)),
        out_shape=jax.ShapeDtypeStruct((n_rows, d), F32),
        scratch_shapes=[
            pltpu.VMEM((2, ts, nb, d), F32),
            pltpu.SemaphoreType.DMA((2,)),
            pltpu.VMEM((rows, d), BF16),
            pltpu.VMEM((halo_p + rows, d), F32),
            pltpu.VMEM((halo_c + rows, d), F32),
            pltpu.VMEM((rows, d), F32),
            pltpu.VMEM((rows, d), F32),
            pltpu.VMEM((rows, d), BF16),
            pltpu.VMEM((rows, d), F32),
            pltpu.VMEM((rows, d), BF16),
            pltpu.VMEM((rows, d), F32),
            pltpu.VMEM((rows, d), F32),
            pltpu.VMEM((rows, d), F32),
            pltpu.VMEM((nb, d), F32),
        ],
        compiler_params=pltpu.CompilerParams(dimension_semantics=("arbitrary",),
                                             vmem_limit_bytes=VMEM_LIMIT),
        name="mixer",
    )(*args)


def _kv_kernel(m_ref, g_ref, w_ref, o_ref):
    o_ref[...] = _dot(_rms(m_ref[...], g_ref[...]).astype(BF16), w_ref[...]).astype(BF16)


def _kv_proj(mem2d, norm_mem, w_kv):
    n, d = mem2d.shape
    assert n % KV_ROWS == 0
    return pl.pallas_call(
        _kv_kernel,
        grid=(n // KV_ROWS,),
        in_specs=[pl.BlockSpec((KV_ROWS, d), lambda i: (i, 0)),
                  _const_spec((1, d)), _const_spec((d, 2 * d))],
        out_specs=pl.BlockSpec((KV_ROWS, 2 * d), lambda i: (i, 0)),
        out_shape=jax.ShapeDtypeStruct((n, 2 * d), BF16),
        compiler_params=pltpu.CompilerParams(dimension_semantics=("arbitrary",),
                                             vmem_limit_bytes=VMEM_LIMIT),
        name="kv_proj",
    )(mem2d, norm_mem.reshape(1, d), w_kv.astype(BF16))


def _attn_kernel(x_ref, kv_ref, gx_ref, wq_ref, wo_ref, gm_ref, wr_ref, br_ref,
                 x2_ref, xn_ref, idx_ref, gate_ref, lrank_ref, cnt_ref, o_scr, xh_scr, xl_scr):
    tq, d = x_ref.shape
    hd = d // N_GROUPS
    ne = br_ref.shape[0]
    tb = ROUTE_ROWS
    x = x_ref[...]
    q = _dot(_rms(x, gx_ref[...]).astype(BF16), wq_ref[...]).astype(BF16)
    for h in range(N_GROUPS):
        k = kv_ref[:, h * hd:(h + 1) * hd]
        v = kv_ref[:, d + h * hd:d + (h + 1) * hd]
        s = _dot_nt(q[:, h * hd:(h + 1) * hd], k) * (hd ** -0.5)
        e = jnp.exp(s - jnp.max(s, axis=-1, keepdims=True))
        p = e / jnp.sum(e, axis=-1, keepdims=True)
        o_scr[:, h * hd:(h + 1) * hd] = _dot(p.astype(BF16), v).astype(BF16)
    x2 = x + _dot(o_scr[...], wo_ref[...])
    x2_ref[...] = x2
    xn = _rms(x2, gm_ref[...])
    xh = xn.astype(BF16)
    xn_ref[...] = xh
    xh_scr[...] = xh
    xl_scr[...] = (xn - xh.astype(F32)).astype(BF16)

    iota_f = lax.broadcasted_iota(I32, (ne, tb), 0).astype(F32)
    before = (lax.broadcasted_iota(I32, (tb, tb), 0) < lax.broadcasted_iota(I32, (tb, tb), 1)
              ).astype(BF16)
    for sb in range(tq // tb):
        rows = slice(sb * tb, (sb + 1) * tb)
        ph = _dot_nt(wr_ref[...], xh_scr[rows, :])
        pl_ = _dot_nt(wr_ref[0:ne, :], xl_scr[rows, :])
        work = ph[0:ne] + ph[ne:2 * ne] + pl_ + br_ref[...]
        vals, idxs, sels = [], [], []
        for _ in range(TOP_K):
            m = jnp.max(work, axis=0, keepdims=True)
            idx = jnp.min(jnp.where(work == m, iota_f, float(ne)), axis=0, keepdims=True)
            sel = iota_f == idx
            vals.append(m)
            idxs.append(idx.astype(I32))
            sels.append(sel)
            work = jnp.where(sel, -jnp.inf, work)
        ex = [jnp.exp(v - vals[0]) for v in vals]
        den = ex[0] + ex[1] + ex[2] + ex[3]
        onehot = jnp.zeros((ne, tb), F32)
        for sel in sels:
            onehot = onehot + sel.astype(F32)
        oh16 = onehot.astype(BF16)
        prefix = _dot(oh16, before)
        lr = [jnp.sum(jnp.where(sel, prefix, 0.0), axis=0, keepdims=True).astype(I32) for sel in sels]
        idx_ref[:, rows] = jnp.concatenate(idxs, axis=0)
        gate_ref[:, rows] = jnp.concatenate([e_ / den for e_ in ex], axis=0)
        lrank_ref[:, rows] = jnp.concatenate(lr, axis=0)
        cnt_ref[sb] = _dot_nt(jnp.ones((1, tb), BF16), oh16).astype(I32)


def _attention(x1_sbd, kv, nb, norm_xattn, w_q, w_o, norm_moe, w_router, b_router):
    s_len, bd = x1_sbd.shape
    d = bd // nb
    m = kv.shape[1]
    ne = w_router.shape[-1]
    tq, tb = ATT_ROWS, ROUTE_ROWS
    nq = s_len // tq
    t = nb * s_len
    assert s_len % tq == 0 and tq % tb == 0
    wr_hi = w_router.astype(BF16)
    wr_lo = (w_router - wr_hi.astype(F32)).astype(BF16)
    wr2t = jnp.concatenate([wr_hi, wr_lo], axis=1).T
    tok = lambda b, i: (0, b * nq + i)
    return pl.pallas_call(
        _attn_kernel,
        grid=(nb, nq),
        in_specs=[pl.BlockSpec((tq, d), lambda b, i: (i, b)),
                  pl.BlockSpec((None, m, 2 * d), lambda b, i: (b, 0, 0)),
                  _const_spec((1, d)), _const_spec((d, d)), _const_spec((d, d)),
                  _const_spec((1, d)), _const_spec((2 * ne, d)), _const_spec((ne, 1))],
        out_specs=[pl.BlockSpec((None, tq, d), lambda b, i: (b, i, 0)),
                   pl.BlockSpec((tq, d), lambda b, i: (b * nq + i, 0)),
                   pl.BlockSpec((TOP_K, tq), tok),
                   pl.BlockSpec((TOP_K, tq), tok),
                   pl.BlockSpec((TOP_K, tq), tok),
                   pl.BlockSpec((tq // tb, 1, ne), lambda b, i: (b * nq + i, 0, 0))],
        out_shape=[jax.ShapeDtypeStruct((nb, s_len, d), F32),
                   jax.ShapeDtypeStruct((t, d), BF16),
                   jax.ShapeDtypeStruct((TOP_K, t), I32),
                   jax.ShapeDtypeStruct((TOP_K, t), F32),
                   jax.ShapeDtypeStruct((TOP_K, t), I32),
                   jax.ShapeDtypeStruct((t // tb, 1, ne), I32)],
        scratch_shapes=[pltpu.VMEM((tq, d), BF16), pltpu.VMEM((tq, d), BF16), pltpu.VMEM((tq, d), BF16)],
        compiler_params=pltpu.CompilerParams(dimension_semantics=("arbitrary", "arbitrary"),
                                             vmem_limit_bytes=VMEM_LIMIT),
        name="attention",
    )(x1_sbd, kv, norm_xattn.reshape(1, d), w_q.astype(BF16), w_o.astype(BF16),
      norm_moe.reshape(1, d), wr2t, b_router.reshape(ne, 1))


def _pow2_chunks(limit):
    sizes = []
    c = RUN_ALIGN
    while c <= limit:
        sizes.append(c)
        c *= 2
    return sizes[::-1]


def _for_each_chunk(n, limit, fn):
    for size in _pow2_chunks(limit):
        @pl.when((n & size) != 0)
        def _(size=size):
            fn(pl.multiple_of(n & ~(2 * size - 1), RUN_ALIGN), size)


def _wait_rows(n, limit, src, dst, sem):
    _for_each_chunk(n, limit, lambda off, size: pltpu.make_async_copy(
        src.at[pl.ds(0, size)], dst.at[pl.ds(0, size)], sem).wait())


def _dispatch_kernel(n8_ref, loc_ref, run_ref, tot_ref, tailn_ref, tails_ref,
                     xn_ref, idx_ref, lrank_ref, locv_ref, xs_hbm, pos_ref, buf, zbuf, sems, zsem):
    b = pl.program_id(0)
    nblk = pl.num_programs(0)
    tb = xn_ref.shape[0]
    r_loc = buf.shape[1]
    ne = tailn_ref.shape[0] - 1
    slot = b % 2

    zrows = zbuf.shape[0]

    def zero_rest(i, carry):
        dst = pl.multiple_of(tails_ref[ne] + i * zrows, zrows)
        pltpu.make_async_copy(zbuf, xs_hbm.at[pl.ds(dst, zrows)], zsem).start()
        return carry

    def wait_rest(i, carry):
        pltpu.make_async_copy(zbuf, xs_hbm.at[pl.ds(0, zrows)], zsem).wait()
        return carry

    @pl.when(b == 0)
    def _():
        zbuf[...] = jnp.zeros_like(zbuf)
        for e in range(ne):
            _for_each_chunk(tailn_ref[e], zrows, lambda off, size, e=e: pltpu.make_async_copy(
                zbuf.at[pl.ds(0, size)],
                xs_hbm.at[pl.ds(pl.multiple_of(tails_ref[e] + off, RUN_ALIGN), size)], zsem).start())
        lax.fori_loop(0, tailn_ref[ne], zero_rest, 0)

    @pl.when(b >= 2)
    def _():
        _wait_rows(tot_ref[b - 2], r_loc, buf.at[slot], xs_hbm, sems.at[slot])

    iota_e = lax.broadcasted_iota(I32, (ne, tb), 0)
    loc_col = locv_ref[...].astype(F32)
    for k in range(TOP_K):
        run0 = jnp.sum(jnp.where(iota_e == idx_ref[k:k + 1, :], loc_col, 0.0), axis=0, keepdims=True)
        pos_ref[k:k + 1, :] = run0.astype(I32) + lrank_ref[k:k + 1, :]

    iota_r = lax.broadcasted_iota(I32, (r_loc, tb), 0)
    hit = iota_r == pos_ref[0:1, :]
    for k in range(1, TOP_K):
        hit = jnp.logical_or(hit, iota_r == pos_ref[k:k + 1, :])
    buf[slot] = _dot(hit.astype(BF16), xn_ref[...])

    for e in range(ne):
        n = n8_ref[b * ne + e]
        src0 = loc_ref[b * ne + e]
        dst0 = run_ref[b * ne + e]
        _for_each_chunk(n, tb, lambda off, size, src0=src0, dst0=dst0: pltpu.make_async_copy(
            buf.at[slot, pl.ds(pl.multiple_of(src0 + off, RUN_ALIGN), size)],
            xs_hbm.at[pl.ds(pl.multiple_of(dst0 + off, RUN_ALIGN), size)], sems.at[slot]).start())

    @pl.when(b == nblk - 1)
    def _():
        @pl.when(b >= 1)
        def _():
            _wait_rows(tot_ref[b - 1], r_loc, buf.at[1 - slot], xs_hbm, sems.at[1 - slot])
        _wait_rows(tot_ref[b], r_loc, buf.at[slot], xs_hbm, sems.at[slot])
        for e in range(ne):
            _wait_rows(tailn_ref[e], zrows, zbuf, xs_hbm, zsem)
        lax.fori_loop(0, tailn_ref[ne], wait_rest, 0)


def _dispatch(xn, idx_t, lrank_t, loc, tables, p_rows, r_loc):
    t, d = xn.shape
    tb = ROUTE_ROWS
    nblk, ne = loc.shape
    tok = pl.BlockSpec((TOP_K, tb), lambda b, *_: (0, b))
    grid_spec = pltpu.PrefetchScalarGridSpec(
        num_scalar_prefetch=6,
        grid=(t // tb,),
        in_specs=[pl.BlockSpec((tb, d), lambda b, *_: (b, 0)), tok, tok,
                  pl.BlockSpec((None, ne, 1), lambda b, *_: (b, 0, 0))],
        out_specs=[pl.BlockSpec(memory_space=pl.ANY), tok],
        scratch_shapes=[pltpu.VMEM((2, r_loc, d), F32), pltpu.VMEM((EXPERT_ROWS // 2, d), F32),
                        pltpu.SemaphoreType.DMA((2,)), pltpu.SemaphoreType.DMA],
    )
    return pl.pallas_call(
        _dispatch_kernel,
        grid_spec=grid_spec,
        out_shape=[jax.ShapeDtypeStruct((p_rows, d), F32), jax.ShapeDtypeStruct((TOP_K, t), I32)],
        compiler_params=pltpu.CompilerParams(dimension_semantics=("arbitrary",),
                                             vmem_limit_bytes=VMEM_LIMIT),
        name="dispatch",
    )(*tables, xn, idx_t, lrank_t, loc.reshape(nblk, ne, 1))


def _expert_kernel(be_ref, nu_ref, xs_ref, wgu_ref, bgu_ref, wdn_ref, bdn_ref,
                   y_ref, wgu_bf, wdn_bf):
    i = pl.program_id(0)
    de = wdn_ref.shape[0]
    prev = be_ref[jnp.maximum(i - 1, 0)]
    fresh = jnp.logical_or(i == 0, be_ref[i] != prev)
    active = i < nu_ref[0]

    @pl.when(jnp.logical_and(active, fresh))
    def _():
        wgu_bf[...] = wgu_ref[...].astype(BF16)
        wdn_bf[...] = wdn_ref[...].astype(BF16)

    @pl.when(active)
    def _():
        hgu = _dot(xs_ref[...].astype(BF16), wgu_bf[...]) + bgu_ref[...]
        gl = jnp.minimum(hgu[:, :de], SWIGLU_LIMIT)
        up = jnp.clip(hgu[:, de:], -SWIGLU_LIMIT, SWIGLU_LIMIT)
        act = (up + 1.0) * (gl * jax.nn.sigmoid(SWIGLU_ALPHA * gl))
        y_ref[...] = _dot(act.astype(BF16), wdn_bf[...]) + bdn_ref[...]

    @pl.when(jnp.logical_not(active))
    def _():
        y_ref[...] = jnp.zeros_like(y_ref)


def _experts(xs, block_e, n_used, w_gate_up, b_gate_up, w_down, b_down):
    p, d = xs.shape
    ne, _, de2 = w_gate_up.shape
    de = de2 // 2
    bm = EXPERT_ROWS
    row_blk = lambda i, be, nu: (jnp.minimum(i, nu[0] - 1), 0)
    grid_spec = pltpu.PrefetchScalarGridSpec(
        num_scalar_prefetch=2,
        grid=(p // bm,),
        in_specs=[
            pl.BlockSpec((bm, d), row_blk),
            pl.BlockSpec((None, d, de2), lambda i, be, nu: (be[i], 0, 0)),
            pl.BlockSpec((None, 1, de2), lambda i, be, nu: (be[i], 0, 0)),
            pl.BlockSpec((None, de, d), lambda i, be, nu: (be[i], 0, 0)),
            pl.BlockSpec((None, 1, d), lambda i, be, nu: (be[i], 0, 0)),
        ],
        out_specs=pl.BlockSpec((bm, d), lambda i, be, nu: (i, 0)),
        scratch_shapes=[pltpu.VMEM((d, de2), BF16), pltpu.VMEM((de, d), BF16)],
    )
    return pl.pallas_call(
        _expert_kernel,
        grid_spec=grid_spec,
        out_shape=jax.ShapeDtypeStruct((p, d), F32),
        compiler_params=pltpu.CompilerParams(dimension_semantics=("arbitrary",),
                                             vmem_limit_bytes=VMEM_LIMIT),
        name="experts",
    )(block_e, n_used, xs, w_gate_up, b_gate_up.reshape(ne, 1, de2), w_down, b_down.reshape(ne, 1, d))


def _combine_kernel(n8_ref, loc_ref, run_ref, tot_ref,
                    pos_ref, gate_ref, x2_ref, gf_ref, y_hbm, o_ref, buf, sems):
    b = pl.program_id(0)
    nblk = pl.num_programs(0)
    tb = x2_ref.shape[0]
    r_loc = buf.shape[1]
    ne = n8_ref.shape[0] // tot_ref.shape[0]
    slot = b % 2

    def fetch(blk, s):
        for e in range(ne):
            n = n8_ref[blk * ne + e]
            dst0 = loc_ref[blk * ne + e]
            src0 = run_ref[blk * ne + e]
            _for_each_chunk(n, tb, lambda off, size, src0=src0, dst0=dst0: pltpu.make_async_copy(
                y_hbm.at[pl.ds(pl.multiple_of(src0 + off, RUN_ALIGN), size)],
                buf.at[s, pl.ds(pl.multiple_of(dst0 + off, RUN_ALIGN), size)], sems.at[s]).start())

    @pl.when(b == 0)
    def _():
        buf[...] = jnp.zeros_like(buf)
        fetch(0, 0)

    @pl.when(b + 1 < nblk)
    def _():
        fetch(b + 1, 1 - slot)

    _wait_rows(tot_ref[b], r_loc, y_hbm, buf.at[slot], sems.at[slot])

    iota_c = lax.broadcasted_iota(I32, (tb, r_loc), 1)
    w = jnp.zeros((tb, r_loc), F32)
    for k in range(TOP_K):
        w = w + jnp.where(iota_c == pos_ref[:, k:k + 1], gate_ref[:, k:k + 1], 0.0)
    w_hi = w.astype(BF16)
    w_lo = (w - w_hi.astype(F32)).astype(BF16)
    y = buf[slot].astype(BF16)
    o_ref[...] = _rms(x2_ref[...] + _dot(w_hi, y) + _dot(w_lo, y), gf_ref[...])


def _combine(pos, gates, x2, y, norm_final, tables, r_loc):
    t, d = x2.shape
    tb = ROUTE_ROWS
    grid_spec = pltpu.PrefetchScalarGridSpec(
        num_scalar_prefetch=4,
        grid=(t // tb,),
        in_specs=[pl.BlockSpec((tb, TOP_K), lambda b, *_: (b, 0)),
                  pl.BlockSpec((tb, TOP_K), lambda b, *_: (b, 0)),
                  pl.BlockSpec((tb, d), lambda b, *_: (b, 0)),
                  pl.BlockSpec((1, d), lambda b, *_: (0, 0)),
                  pl.BlockSpec(memory_space=pl.ANY)],
        out_specs=pl.BlockSpec((tb, d), lambda b, *_: (b, 0)),
        scratch_shapes=[pltpu.VMEM((2, r_loc, d), F32), pltpu.SemaphoreType.DMA((2,))],
    )
    return pl.pallas_call(
        _combine_kernel,
        grid_spec=grid_spec,
        out_shape=jax.ShapeDtypeStruct((t, d), F32),
        compiler_params=pltpu.CompilerParams(dimension_semantics=("arbitrary",),
                                             vmem_limit_bytes=VMEM_LIMIT),
        name="combine",
    )(*tables, pos, gates, x2, norm_final.reshape(1, d), y)


def _excl_cumsum(a, axis):
    n = a.shape[axis]
    a = jnp.moveaxis(a, axis, -1)
    earlier = jnp.arange(n)[None, :] < jnp.arange(n)[:, None]
    out = jnp.sum(jnp.where(earlier, a[..., None, :], 0), axis=-1)
    return jnp.moveaxis(out, -1, axis)


def _layout(cnt):
    nblk, ne = cnt.shape
    tb, bm = ROUTE_ROWS, EXPERT_ROWS
    n8 = (cnt + RUN_ALIGN - 1) // RUN_ALIGN * RUN_ALIGN
    loc = _excl_cumsum(n8, 1)
    tot = jnp.sum(n8, axis=1)
    size = jnp.sum(n8, axis=0)
    padded = (size + bm - 1) // bm * bm
    pstart = _excl_cumsum(padded, 0)
    pend = pstart + padded
    run = pstart[None, :] + _excl_cumsum(n8, 0)
    p_blocks = -(-(nblk * tb * TOP_K + nblk * ne * (RUN_ALIGN - 1) + ne * (bm - RUN_ALIGN)) // bm)
    n_used = (pend[-1] // bm).astype(I32)
    blk_start = jnp.minimum(jnp.arange(p_blocks, dtype=I32), n_used - 1) * bm
    block_e = jnp.minimum(jnp.sum((blk_start[:, None] >= pend[None, :]).astype(I32), axis=1), ne - 1)
    flat = lambda a: a.reshape(-1).astype(I32)
    tables = (flat(n8), flat(loc), flat(run), flat(tot))
    half = bm // 2
    tails = (flat(jnp.concatenate([padded - size, (p_blocks * bm - pend[-1:]) // half])),
             flat(jnp.concatenate([pstart + size, pend[-1:]])))
    r_loc = -(-(tb * TOP_K + ne * (RUN_ALIGN - 1)) // 256) * 256
    return tables, tails, loc.astype(I32), block_e.astype(I32), n_used.reshape(1), p_blocks * bm, r_loc


def kernel(x, mem, norm_mix, w_in, w_pool_group, pool_scale, w_pool_proj, conv_w, conv_b, lru_w_a, lru_b_a, lru_w_x, lru_b_x, lru_lambda, w_lru_proj, w_mix_out, norm_xattn, norm_mem, w_q, w_kv, w_o, norm_moe, w_router, b_router, w_gate_up, b_gate_up, w_down, b_down, norm_final):
    nb, s_len, d = x.shape
    m_len = mem.shape[1]
    assert norm_mix.shape[0] == 1, "single-layer stack"
    l = 0
    x1 = _mixer(x, norm_mix[l], w_in[l], w_pool_group[l], pool_scale[l], w_pool_proj[l],
                conv_w[l], conv_b[l], lru_w_a[l], lru_b_a[l], lru_w_x[l], lru_b_x[l],
                lru_lambda[l], w_lru_proj[l], w_mix_out[l])
    kv = _kv_proj(mem.reshape(nb * m_len, d), norm_mem[l], w_kv[l]).reshape(nb, m_len, 2 * d)
    x2, xn, idx_t, gate_t, lrank_t, cnt = _attention(
        x1.reshape(s_len, nb * d), kv, nb, norm_xattn[l], w_q[l], w_o[l],
        norm_moe[l], w_router[l], b_router[l])
    tables, tails, loc, block_e, n_used, p_rows, r_loc = _layout(cnt.reshape(cnt.shape[0], -1))
    xs, pos_t = _dispatch(xn, idx_t, lrank_t, loc, tables + tails, p_rows, r_loc)
    y = _experts(xs, block_e, n_used, w_gate_up[l], b_gate_up[l], w_down[l], b_down[l])
    out = _combine(pos_t.T, gate_t.T, x2.reshape(nb * s_len, d), y, norm_final, tables, r_loc)
    return out.reshape(nb, s_len, d)
```

```python
import functools

import jax
import jax.numpy as jnp
from jax import lax
from jax.experimental import pallas as pl
from jax.experimental.pallas import tpu as pltpu

POOL_WINDOWS = (2, 4, 8, 16)
N_GROUPS = 4
CONV_WIDTH = 4
RG_C = 8.0
N_EXPERTS = 32
TOP_K = 4
SWIGLU_LIMIT = 7.0
SWIGLU_ALPHA = 1.702
RMS_EPS = 1e-6

MIX_STEPS = 32
KV_ROWS = 512
ATT_ROWS = 512
ROUTE_ROWS = 256
EXPERT_ROWS = 512
RUN_ALIGN = 8
VMEM_LIMIT = 52 * 1024 * 1024

BF16 = jnp.bfloat16
F32 = jnp.float32
I32 = jnp.int32


def _const_spec(shape):
    nd = len(shape)
    return pl.BlockSpec(shape, lambda *_: (0,) * nd, pipeline_mode=pl.Buffered(1))


def _rms(x, g):
    return x * lax.rsqrt(jnp.mean(x * x, axis=-1, keepdims=True) + RMS_EPS) * g


def _dot(a, b):
    return jnp.dot(a, b, preferred_element_type=F32)


def _dot_nt(a, b):
    return lax.dot_general(a, b, (((1,), (1,)), ((), ())), preferred_element_type=F32)


def _mixer_kernel(x_hbm, nm_ref, win_ref, wpg_ref, psc_ref, wpp_ref, cw_ref, cb_ref,
                  wa_ref, ba_ref, wx_ref, bx_ref, lam_ref, wlp_ref, wmo_ref,
                  o_ref,
                  xbuf, xsem,
                  h_ref, up_ref, ul_ref, a_ref, b_ref, pm_ref, m_ref, mb_ref, t_ref, gl_ref, gb_ref, hc_ref,
                  *, nb, ts):
    rows, d = o_ref.shape
    gw = d // N_GROUPS
    halo_p = (POOL_WINDOWS[-1]) * nb
    halo_c = (CONV_WIDTH - 1) * nb
    c = pl.program_id(0)

    @pl.when(c == 0)
    def _():
        up_ref[0:halo_p, :] = jnp.zeros((halo_p, d), F32)
        ul_ref[0:halo_c, :] = jnp.zeros((halo_c, d), F32)
        hc_ref[...] = jnp.zeros_like(hc_ref)

    def x_copy(step, t, slot):
        return pltpu.make_async_copy(x_hbm.at[:, step * ts + t, :], xbuf.at[slot, t], xsem.at[slot])

    def fetch(step, slot):
        lax.fori_loop(0, ts, lambda t, carry: (x_copy(step, t, slot).start(), carry)[1], 0)

    slot = c % 2

    @pl.when(c == 0)
    def _():
        fetch(0, 0)

    @pl.when(c + 1 < pl.num_programs(0))
    def _():
        fetch(c + 1, 1 - slot)

    lax.fori_loop(0, ts, lambda t, carry: (x_copy(c, t, slot).wait(), carry)[1], 0)

    h_ref[...] = _rms(xbuf[slot].reshape(rows, d), nm_ref[...]).astype(BF16)

    t_glob = c * ts + lax.broadcasted_iota(I32, (rows, 1), 0) // nb

    up_ref[halo_p:halo_p + rows, :] = _dot(h_ref[...], win_ref[:, 0:d])
    ul_ref[halo_c:halo_c + rows, :] = _dot(h_ref[...], win_ref[:, d:2 * d])

    for g, w in enumerate(POOL_WINDOWS):
        cols = slice(g * gw, (g + 1) * gw)
        u = up_ref[halo_p:halo_p + rows, cols]
        acc = u
        for j in range(1, w):
            acc = acc + up_ref[halo_p - j * nb:halo_p - j * nb + rows, cols]
        cnt = jnp.minimum(t_glob + 1, w).astype(F32)
        p = acc / cnt - u
        pg = _dot(p.astype(BF16), wpg_ref[g]) * psc_ref[:, cols]
        pm_ref[:, cols] = pg.astype(BF16)
    t_ref[...] = _dot(h_ref[...], win_ref[:, 3 * d:4 * d])
    m_ref[...] = _dot(pm_ref[...], wpp_ref[...])

    for g in range(N_GROUPS):
        cols = slice(g * gw, (g + 1) * gw)
        xr = cb_ref[:, cols]
        for k in range(CONV_WIDTH):
            off = halo_c - (CONV_WIDTH - 1 - k) * nb
            xr = xr + ul_ref[off:off + rows, cols] * cw_ref[k:k + 1, cols]
        xrb = xr.astype(BF16)
        r = jax.nn.sigmoid(_dot(xrb, wa_ref[g]) + ba_ref[:, cols])
        i = jax.nn.sigmoid(_dot(xrb, wx_ref[g]) + bx_ref[:, cols])
        gl_ref[:, cols] = _dot(h_ref[...], win_ref[:, 2 * d + g * gw:2 * d + (g + 1) * gw])
        gb_ref[:, cols] = _dot(h_ref[...], win_ref[:, 4 * d + g * gw:4 * d + (g + 1) * gw])
        lam = lam_ref[:, cols]
        log_sig = jnp.minimum(lam, 0.0) - jnp.log(1.0 + jnp.exp(-jnp.abs(lam)))
        a = jnp.exp((RG_C * r) * log_sig)
        mult = jnp.sqrt(jnp.maximum(1.0 - a * a, 0.0))
        mult = jnp.where(t_glob == 0, 1.0, mult)
        a_ref[:, cols] = a
        b_ref[:, cols] = mult * i * xr
    m_ref[...] = jax.nn.sigmoid(t_ref[...]) * m_ref[...]

    def scan_step(t, hprev):
        sl = pl.ds(pl.multiple_of(t * nb, nb), nb)
        hn = a_ref[sl, :] * hprev + b_ref[sl, :]
        b_ref[sl, :] = hn
        return hn

    hc_ref[...] = lax.fori_loop(0, ts, scan_step, hc_ref[...], unroll=4)

    half = rows // 2
    for hs in (slice(0, half), slice(half, rows)):
        pm_ref[hs, :] = (b_ref[hs, :] * jax.nn.gelu(gl_ref[hs, :], approximate=True)).astype(BF16)
    for hs in (slice(0, half), slice(half, rows)):
        yb = _dot(pm_ref[hs, :], wlp_ref[...])
        mb_ref[hs, :] = (m_ref[hs, :] + jax.nn.sigmoid(gb_ref[hs, :]) * yb).astype(BF16)
    for i, hs in enumerate((slice(0, half), slice(half, rows))):
        x_half = xbuf[slot, i * (ts // 2):(i + 1) * (ts // 2)].reshape(half, d)
        o_ref[hs, :] = x_half + _dot(mb_ref[hs, :], wmo_ref[...])

    up_ref[0:halo_p, :] = up_ref[rows:rows + halo_p, :]
    ul_ref[0:halo_c, :] = ul_ref[rows:rows + halo_c, :]


def _mixer(x, norm_mix, w_in, w_pool_group, pool_scale, w_pool_proj, conv_w, conv_b,
           lru_w_a, lru_b_a, lru_w_x, lru_b_x, lru_lambda, w_lru_proj, w_mix_out):
    nb, s_len, d = x.shape
    ts = MIX_STEPS
    rows = ts * nb
    n_rows = s_len * nb
    assert s_len % ts == 0 and ts % 2 == 0 and ts >= POOL_WINDOWS[-1] and nb % 8 == 0
    row2 = lambda v: v.reshape(1, -1)
    args = (x, row2(norm_mix), w_in.astype(BF16), w_pool_group.astype(BF16), row2(pool_scale),
            w_pool_proj.astype(BF16), conv_w, row2(conv_b), lru_w_a.astype(BF16), row2(lru_b_a),
            lru_w_x.astype(BF16), row2(lru_b_x), row2(lru_lambda), w_lru_proj.astype(BF16),
            w_mix_out.astype(BF16))
    in_specs = [pl.BlockSpec(memory_space=pl.ANY)] + [_const_spec(a.shape) for a in args[1:]]
    halo_p = POOL_WINDOWS[-1] * nb
    halo_c = (CONV_WIDTH - 1) * nb
    return pl.pallas_call(
        functools.partial(_mixer_kernel, nb=nb, ts=ts),
        grid=(n_rows // rows,),
        in_specs=in_specs,
        out_specs=pl.BlockSpec((rows, d), lambda c: (c, 0)),
        out_shape=jax.ShapeDtypeStruct((n_rows, d), F32),
        scratch_shapes=[
            pltpu.VMEM((2, ts, nb, d), F32),
            pltpu.SemaphoreType.DMA((2,)),
            pltpu.VMEM((rows, d), BF16),
            pltpu.VMEM((halo_p + rows, d), F32),
            pltpu.VMEM((halo_c + rows, d), F32),
            pltpu.VMEM((rows, d), F32),
            pltpu.VMEM((rows, d), F32),
            pltpu.VMEM((rows, d), BF16),
            pltpu.VMEM((rows, d), F32),
            pltpu.VMEM((rows, d), BF16),
            pltpu.VMEM((rows, d), F32),
            pltpu.VMEM((rows, d), F32),
            pltpu.VMEM((rows, d), F32),
            pltpu.VMEM((nb, d), F32),
        ],
        compiler_params=pltpu.CompilerParams(dimension_semantics=("arbitrary",),
                                             vmem_limit_bytes=VMEM_LIMIT),
        name="mixer",
    )(*args)


def _kv_kernel(m_ref, g_ref, w_ref, o_ref):
    o_ref[...] = _dot(_rms(m_ref[...], g_ref[...]).astype(BF16), w_ref[...]).astype(BF16)


def _kv_proj(mem2d, norm_mem, w_kv):
    n, d = mem2d.shape
    assert n % KV_ROWS == 0
    return pl.pallas_call(
        _kv_kernel,
        grid=(n // KV_ROWS,),
        in_specs=[pl.BlockSpec((KV_ROWS, d), lambda i: (i, 0)),
                  _const_spec((1, d)), _const_spec((d, 2 * d))],
        out_specs=pl.BlockSpec((KV_ROWS, 2 * d), lambda i: (i, 0)),
        out_shape=jax.ShapeDtypeStruct((n, 2 * d), BF16),
        compiler_params=pltpu.CompilerParams(dimension_semantics=("arbitrary",),
                                             vmem_limit_bytes=VMEM_LIMIT),
        name="kv_proj",
    )(mem2d, norm_mem.reshape(1, d), w_kv.astype(BF16))


def _attn_kernel(x_hbm, kv_ref, gx_ref, wq_ref, wo_ref, gm_ref, wr_ref, br_ref,
                 x2_ref, xn_ref, idx_ref, gate_ref, lrank_ref, cnt_ref,
                 xbuf, xsem, o_scr, xh_scr, xl_scr):
    tq, d = x2_ref.shape
    hd = d // N_GROUPS
    ne = br_ref.shape[0]
    tb = ROUTE_ROWS

    b, i = pl.program_id(0), pl.program_id(1)
    nq = pl.num_programs(1)
    step = b * nq + i
    slot = step % 2

    def x_copy(bb, ii, s):
        return pltpu.make_async_copy(x_hbm.at[pl.ds(ii * tq, tq), bb, :], xbuf.at[s], xsem.at[s])

    @pl.when(step == 0)
    def _():
        x_copy(0, 0, 0).start()

    @pl.when(step + 1 < pl.num_programs(0) * nq)
    def _():
        wrap = i + 1 == nq
        x_copy(jnp.where(wrap, b + 1, b), jnp.where(wrap, 0, i + 1), 1 - slot).start()

    x_copy(b, i, slot).wait()
    x = xbuf[slot]
    q = _dot(_rms(x, gx_ref[...]).astype(BF16), wq_ref[...]).astype(BF16)
    for h in range(N_GROUPS):
        k = kv_ref[:, h * hd:(h + 1) * hd]
        v = kv_ref[:, d + h * hd:d + (h + 1) * hd]
        s = _dot_nt(q[:, h * hd:(h + 1) * hd], k) * (hd ** -0.5)
        e = jnp.exp(s - jnp.max(s, axis=-1, keepdims=True))
        p = e / jnp.sum(e, axis=-1, keepdims=True)
        o_scr[:, h * hd:(h + 1) * hd] = _dot(p.astype(BF16), v).astype(BF16)
    x2 = x + _dot(o_scr[...], wo_ref[...])
    x2_ref[...] = x2
    xn = _rms(x2, gm_ref[...])
    xh = xn.astype(BF16)
    xn_ref[...] = xh
    xh_scr[...] = xh
    xl_scr[...] = (xn - xh.astype(F32)).astype(BF16)

    iota_f = lax.broadcasted_iota(I32, (ne, tb), 0).astype(F32)
    before = (lax.broadcasted_iota(I32, (tb, tb), 0) < lax.broadcasted_iota(I32, (tb, tb), 1)
              ).astype(BF16)
    for sb in range(tq // tb):
        rows = slice(sb * tb, (sb + 1) * tb)
        ph = _dot_nt(wr_ref[...], xh_scr[rows, :])
        pl_ = _dot_nt(wr_ref[0:ne, :], xl_scr[rows, :])
        work = ph[0:ne] + ph[ne:2 * ne] + pl_ + br_ref[...]
        vals, idxs, sels = [], [], []
        for _ in range(TOP_K):
            m = jnp.max(work, axis=0, keepdims=True)
            idx = jnp.min(jnp.where(work == m, iota_f, float(ne)), axis=0, keepdims=True)
            sel = iota_f == idx
            vals.append(m)
            idxs.append(idx.astype(I32))
            sels.append(sel)
            work = jnp.where(sel, -jnp.inf, work)
        ex = [jnp.exp(v - vals[0]) for v in vals]
        den = ex[0] + ex[1] + ex[2] + ex[3]
        onehot = jnp.zeros((ne, tb), F32)
        for sel in sels:
            onehot = onehot + sel.astype(F32)
        oh16 = onehot.astype(BF16)
        prefix = _dot(oh16, before)
        lr = [jnp.sum(jnp.where(sel, prefix, 0.0), axis=0, keepdims=True).astype(I32) for sel in sels]
        idx_ref[:, rows] = jnp.concatenate(idxs, axis=0)
        gate_ref[:, rows] = jnp.concatenate([e_ / den for e_ in ex], axis=0)
        lrank_ref[:, rows] = jnp.concatenate(lr, axis=0)
        cnt_ref[sb] = _dot_nt(jnp.ones((1, tb), BF16), oh16).astype(I32)


def _attention(x1, kv, norm_xattn, w_q, w_o, norm_moe, w_router, b_router):
    s_len, nb, d = x1.shape
    m = kv.shape[1]
    ne = w_router.shape[-1]
    tq, tb = ATT_ROWS, ROUTE_ROWS
    nq = s_len // tq
    t = nb * s_len
    assert s_len % tq == 0 and tq % tb == 0
    wr_hi = w_router.astype(BF16)
    wr_lo = (w_router - wr_hi.astype(F32)).astype(BF16)
    wr2t = jnp.concatenate([wr_hi, wr_lo], axis=1).T
    tok = lambda b, i: (0, b * nq + i)
    return pl.pallas_call(
        _attn_kernel,
        grid=(nb, nq),
        in_specs=[pl.BlockSpec(memory_space=pl.ANY),
                  pl.BlockSpec((None, m, 2 * d), lambda b, i: (b, 0, 0)),
                  _const_spec((1, d)), _const_spec((d, d)), _const_spec((d, d)),
                  _const_spec((1, d)), _const_spec((2 * ne, d)), _const_spec((ne, 1))],
        out_specs=[pl.BlockSpec((None, tq, d), lambda b, i: (b, i, 0)),
                   pl.BlockSpec((tq, d), lambda b, i: (b * nq + i, 0)),
                   pl.BlockSpec((TOP_K, tq), tok),
                   pl.BlockSpec((TOP_K, tq), tok),
                   pl.BlockSpec((TOP_K, tq), tok),
                   pl.BlockSpec((tq // tb, 1, ne), lambda b, i: (b * nq + i, 0, 0))],
        out_shape=[jax.ShapeDtypeStruct((nb, s_len, d), F32),
                   jax.ShapeDtypeStruct((t, d), BF16),
                   jax.ShapeDtypeStruct((TOP_K, t), I32),
                   jax.ShapeDtypeStruct((TOP_K, t), F32),
                   jax.ShapeDtypeStruct((TOP_K, t), I32),
                   jax.ShapeDtypeStruct((t // tb, 1, ne), I32)],
        scratch_shapes=[pltpu.VMEM((2, tq, d), F32), pltpu.SemaphoreType.DMA((2,)),
                        pltpu.VMEM((tq, d), BF16), pltpu.VMEM((tq, d), BF16), pltpu.VMEM((tq, d), BF16)],
        compiler_params=pltpu.CompilerParams(dimension_semantics=("arbitrary", "arbitrary"),
                                             vmem_limit_bytes=VMEM_LIMIT),
        name="attention",
    )(x1, kv, norm_xattn.reshape(1, d), w_q.astype(BF16), w_o.astype(BF16),
      norm_moe.reshape(1, d), wr2t, b_router.reshape(ne, 1))


def _pow2_chunks(limit):
    sizes = []
    c = RUN_ALIGN
    while c <= limit:
        sizes.append(c)
        c *= 2
    return sizes[::-1]


def _for_each_chunk(n, limit, fn):
    for size in _pow2_chunks(limit):
        @pl.when((n & size) != 0)
        def _(size=size):
            fn(pl.multiple_of(n & ~(2 * size - 1), RUN_ALIGN), size)


def _wait_rows(n, limit, src, dst, sem):
    _for_each_chunk(n, limit, lambda off, size: pltpu.make_async_copy(
        src.at[pl.ds(0, size)], dst.at[pl.ds(0, size)], sem).wait())


def _dispatch_kernel(n8_ref, loc_ref, run_ref, tot_ref, tailn_ref, tails_ref,
                     xn_ref, idx_ref, lrank_ref, locv_ref, xs_hbm, pos_ref, buf, zbuf, sems, zsem):
    b = pl.program_id(0)
    nblk = pl.num_programs(0)
    tb = xn_ref.shape[0]
    r_loc = buf.shape[1]
    ne = tailn_ref.shape[0] - 1
    slot = b % 2

    zrows = zbuf.shape[0]

    def zero_rest(i, carry):
        dst = pl.multiple_of(tails_ref[ne] + i * zrows, zrows)
        pltpu.make_async_copy(zbuf, xs_hbm.at[pl.ds(dst, zrows)], zsem).start()
        return carry

    def wait_rest(i, carry):
        pltpu.make_async_copy(zbuf, xs_hbm.at[pl.ds(0, zrows)], zsem).wait()
        return carry

    @pl.when(b == 0)
    def _():
        zbuf[...] = jnp.zeros_like(zbuf)
        for e in range(ne):
            _for_each_chunk(tailn_ref[e], zrows, lambda off, size, e=e: pltpu.make_async_copy(
                zbuf.at[pl.ds(0, size)],
                xs_hbm.at[pl.ds(pl.multiple_of(tails_ref[e] + off, RUN_ALIGN), size)], zsem).start())
        lax.fori_loop(0, tailn_ref[ne], zero_rest, 0)

    @pl.when(b >= 2)
    def _():
        _wait_rows(tot_ref[b - 2], r_loc, buf.at[slot], xs_hbm, sems.at[slot])

    iota_e = lax.broadcasted_iota(I32, (ne, tb), 0)
    loc_col = locv_ref[...].astype(F32)
    for k in range(TOP_K):
        run0 = jnp.sum(jnp.where(iota_e == idx_ref[k:k + 1, :], loc_col, 0.0), axis=0, keepdims=True)
        pos_ref[k:k + 1, :] = run0.astype(I32) + lrank_ref[k:k + 1, :]

    iota_r = lax.broadcasted_iota(I32, (r_loc, tb), 0)
    hit = iota_r == pos_ref[0:1, :]
    for k in range(1, TOP_K):
        hit = jnp.logical_or(hit, iota_r == pos_ref[k:k + 1, :])
    buf[slot] = _dot(hit.astype(BF16), xn_ref[...])

    for e in range(ne):
        n = n8_ref[b * ne + e]
        src0 = loc_ref[b * ne + e]
        dst0 = run_ref[b * ne + e]
        _for_each_chunk(n, tb, lambda off, size, src0=src0, dst0=dst0: pltpu.make_async_copy(
            buf.at[slot, pl.ds(pl.multiple_of(src0 + off, RUN_ALIGN), size)],
            xs_hbm.at[pl.ds(pl.multiple_of(dst0 + off, RUN_ALIGN), size)], sems.at[slot]).start())

    @pl.when(b == nblk - 1)
    def _():
        @pl.when(b >= 1)
        def _():
            _wait_rows(tot_ref[b - 1], r_loc, buf.at[1 - slot], xs_hbm, sems.at[1 - slot])
        _wait_rows(tot_ref[b], r_loc, buf.at[slot], xs_hbm, sems.at[slot])
        for e in range(ne):
            _wait_rows(tailn_ref[e], zrows, zbuf, xs_hbm, zsem)
        lax.fori_loop(0, tailn_ref[ne], wait_rest, 0)


def _dispatch(xn, idx_t, lrank_t, loc, tables, p_rows, r_loc):
    t, d = xn.shape
    tb = ROUTE_ROWS
    nblk, ne = loc.shape
    tok = pl.BlockSpec((TOP_K, tb), lambda b, *_: (0, b))
    grid_spec = pltpu.PrefetchScalarGridSpec(
        num_scalar_prefetch=6,
        grid=(t // tb,),
        in_specs=[pl.BlockSpec((tb, d), lambda b, *_: (b, 0)), tok, tok,
                  pl.BlockSpec((None, ne, 1), lambda b, *_: (b, 0, 0))],
        out_specs=[pl.BlockSpec(memory_space=pl.ANY), tok],
        scratch_shapes=[pltpu.VMEM((2, r_loc, d), F32), pltpu.VMEM((EXPERT_ROWS // 2, d), F32),
                        pltpu.SemaphoreType.DMA((2,)), pltpu.SemaphoreType.DMA],
    )
    return pl.pallas_call(
        _dispatch_kernel,
        grid_spec=grid_spec,
        out_shape=[jax.ShapeDtypeStruct((p_rows, d), F32), jax.ShapeDtypeStruct((TOP_K, t), I32)],
        compiler_params=pltpu.CompilerParams(dimension_semantics=("arbitrary",),
                                             vmem_limit_bytes=VMEM_LIMIT),
        name="dispatch",
    )(*tables, xn, idx_t, lrank_t, loc.reshape(nblk, ne, 1))


def _expert_kernel(be_ref, nu_ref, xs_ref, wgu_ref, bgu_ref, wdn_ref, bdn_ref,
                   y_ref, wgu_bf, wdn_bf):
    i = pl.program_id(0)
    de = wdn_ref.shape[0]
    prev = be_ref[jnp.maximum(i - 1, 0)]
    fresh = jnp.logical_or(i == 0, be_ref[i] != prev)
    active = i < nu_ref[0]

    @pl.when(jnp.logical_and(active, fresh))
    def _():
        wgu_bf[...] = wgu_ref[...].astype(BF16)
        wdn_bf[...] = wdn_ref[...].astype(BF16)

    @pl.when(active)
    def _():
        hgu = _dot(xs_ref[...].astype(BF16), wgu_bf[...]) + bgu_ref[...]
        gl = jnp.minimum(hgu[:, :de], SWIGLU_LIMIT)
        up = jnp.clip(hgu[:, de:], -SWIGLU_LIMIT, SWIGLU_LIMIT)
        act = (up + 1.0) * (gl * jax.nn.sigmoid(SWIGLU_ALPHA * gl))
        y_ref[...] = _dot(act.astype(BF16), wdn_bf[...]) + bdn_ref[...]

    @pl.when(jnp.logical_not(active))
    def _():
        y_ref[...] = jnp.zeros_like(y_ref)


def _experts(xs, block_e, n_used, w_gate_up, b_gate_up, w_down, b_down):
    p, d = xs.shape
    ne, _, de2 = w_gate_up.shape
    de = de2 // 2
    bm = EXPERT_ROWS
    row_blk = lambda i, be, nu: (jnp.minimum(i, nu[0] - 1), 0)
    grid_spec = pltpu.PrefetchScalarGridSpec(
        num_scalar_prefetch=2,
        grid=(p // bm,),
        in_specs=[
            pl.BlockSpec((bm, d), row_blk),
            pl.BlockSpec((None, d, de2), lambda i, be, nu: (be[i], 0, 0)),
            pl.BlockSpec((None, 1, de2), lambda i, be, nu: (be[i], 0, 0)),
            pl.BlockSpec((None, de, d), lambda i, be, nu: (be[i], 0, 0)),
            pl.BlockSpec((None, 1, d), lambda i, be, nu: (be[i], 0, 0)),
        ],
        out_specs=pl.BlockSpec((bm, d), lambda i, be, nu: (i, 0)),
        scratch_shapes=[pltpu.VMEM((d, de2), BF16), pltpu.VMEM((de, d), BF16)],
    )
    return pl.pallas_call(
        _expert_kernel,
        grid_spec=grid_spec,
        out_shape=jax.ShapeDtypeStruct((p, d), F32),
        compiler_params=pltpu.CompilerParams(dimension_semantics=("arbitrary",),
                                             vmem_limit_bytes=VMEM_LIMIT),
        name="experts",
    )(block_e, n_used, xs, w_gate_up, b_gate_up.reshape(ne, 1, de2), w_down, b_down.reshape(ne, 1, d))


def _combine_kernel(n8_ref, loc_ref, run_ref, tot_ref,
                    pos_ref, gate_ref, x2_ref, gf_ref, y_hbm, o_ref, buf, sems):
    b = pl.program_id(0)
    nblk = pl.num_programs(0)
    tb = x2_ref.shape[0]
    r_loc = buf.shape[1]
    ne = n8_ref.shape[0] // tot_ref.shape[0]
    slot = b % 2

    def fetch(blk, s):
        for e in range(ne):
            n = n8_ref[blk * ne + e]
            dst0 = loc_ref[blk * ne + e]
            src0 = run_ref[blk * ne + e]
            _for_each_chunk(n, tb, lambda off, size, src0=src0, dst0=dst0: pltpu.make_async_copy(
                y_hbm.at[pl.ds(pl.multiple_of(src0 + off, RUN_ALIGN), size)],
                buf.at[s, pl.ds(pl.multiple_of(dst0 + off, RUN_ALIGN), size)], sems.at[s]).start())

    @pl.when(b == 0)
    def _():
        buf[...] = jnp.zeros_like(buf)
        fetch(0, 0)

    @pl.when(b + 1 < nblk)
    def _():
        fetch(b + 1, 1 - slot)

    _wait_rows(tot_ref[b], r_loc, y_hbm, buf.at[slot], sems.at[slot])

    iota_c = lax.broadcasted_iota(I32, (tb, r_loc), 1)
    w = jnp.zeros((tb, r_loc), F32)
    for k in range(TOP_K):
        w = w + jnp.where(iota_c == pos_ref[:, k:k + 1], gate_ref[:, k:k + 1], 0.0)
    w_hi = w.astype(BF16)
    w_lo = (w - w_hi.astype(F32)).astype(BF16)
    y = buf[slot].astype(BF16)
    o_ref[...] = _rms(x2_ref[...] + _dot(w_hi, y) + _dot(w_lo, y), gf_ref[...])


def _combine(pos, gates, x2, y, norm_final, tables, r_loc):
    t, d = x2.shape
    tb = ROUTE_ROWS
    grid_spec = pltpu.PrefetchScalarGridSpec(
        num_scalar_prefetch=4,
        grid=(t // tb,),
        in_specs=[pl.BlockSpec((tb, TOP_K), lambda b, *_: (b, 0)),
                  pl.BlockSpec((tb, TOP_K), lambda b, *_: (b, 0)),
                  pl.BlockSpec((tb, d), lambda b, *_: (b, 0)),
                  pl.BlockSpec((1, d), lambda b, *_: (0, 0)),
                  pl.BlockSpec(memory_space=pl.ANY)],
        out_specs=pl.BlockSpec((tb, d), lambda b, *_: (b, 0)),
        scratch_shapes=[pltpu.VMEM((2, r_loc, d), F32), pltpu.SemaphoreType.DMA((2,))],
    )
    return pl.pallas_call(
        _combine_kernel,
        grid_spec=grid_spec,
        out_shape=jax.ShapeDtypeStruct((t, d), F32),
        compiler_params=pltpu.CompilerParams(dimension_semantics=("arbitrary",),
                                             vmem_limit_bytes=VMEM_LIMIT),
        name="combine",
    )(*tables, pos, gates, x2, norm_final.reshape(1, d), y)


def _excl_cumsum(a, axis):
    n = a.shape[axis]
    a = jnp.moveaxis(a, axis, -1)
    earlier = jnp.arange(n)[None, :] < jnp.arange(n)[:, None]
    out = jnp.sum(jnp.where(earlier, a[..., None, :], 0), axis=-1)
    return jnp.moveaxis(out, -1, axis)


def _layout(cnt):
    nblk, ne = cnt.shape
    tb, bm = ROUTE_ROWS, EXPERT_ROWS
    n8 = (cnt + RUN_ALIGN - 1) // RUN_ALIGN * RUN_ALIGN
    loc = _excl_cumsum(n8, 1)
    tot = jnp.sum(n8, axis=1)
    size = jnp.sum(n8, axis=0)
    padded = (size + bm - 1) // bm * bm
    pstart = _excl_cumsum(padded, 0)
    pend = pstart + padded
    run = pstart[None, :] + _excl_cumsum(n8, 0)
    p_blocks = -(-(nblk * tb * TOP_K + nblk * ne * (RUN_ALIGN - 1) + ne * (bm - RUN_ALIGN)) // bm)
    n_used = (pend[-1] // bm).astype(I32)
    blk_start = jnp.minimum(jnp.arange(p_blocks, dtype=I32), n_used - 1) * bm
    block_e = jnp.minimum(jnp.sum((blk_start[:, None] >= pend[None, :]).astype(I32), axis=1), ne - 1)
    flat = lambda a: a.reshape(-1).astype(I32)
    tables = (flat(n8), flat(loc), flat(run), flat(tot))
    half = bm // 2
    tails = (flat(jnp.concatenate([padded - size, (p_blocks * bm - pend[-1:]) // half])),
             flat(jnp.concatenate([pstart + size, pend[-1:]])))
    r_loc = -(-(tb * TOP_K + ne * (RUN_ALIGN - 1)) // 256) * 256
    return tables, tails, loc.astype(I32), block_e.astype(I32), n_used.reshape(1), p_blocks * bm, r_loc


def kernel(x, mem, norm_mix, w_in, w_pool_group, pool_scale, w_pool_proj, conv_w, conv_b, lru_w_a, lru_b_a, lru_w_x, lru_b_x, lru_lambda, w_lru_proj, w_mix_out, norm_xattn, norm_mem, w_q, w_kv, w_o, norm_moe, w_router, b_router, w_gate_up, b_gate_up, w_down, b_down, norm_final):
    nb, s_len, d = x.shape
    m_len = mem.shape[1]
    assert norm_mix.shape[0] == 1, "single-layer stack"
    l = 0
    x1 = _mixer(x, norm_mix[l], w_in[l], w_pool_group[l], pool_scale[l], w_pool_proj[l],
                conv_w[l], conv_b[l], lru_w_a[l], lru_b_a[l], lru_w_x[l], lru_b_x[l],
                lru_lambda[l], w_lru_proj[l], w_mix_out[l])
    kv = _kv_proj(mem.reshape(nb * m_len, d), norm_mem[l], w_kv[l]).reshape(nb, m_len, 2 * d)
    x2, xn, idx_t, gate_t, lrank_t, cnt = _attention(
        x1.reshape(s_len, nb, d), kv, norm_xattn[l], w_q[l], w_o[l],
        norm_moe[l], w_router[l], b_router[l])
    tables, tails, loc, block_e, n_used, p_rows, r_loc = _layout(cnt.reshape(cnt.shape[0], -1))
    xs, pos_t = _dispatch(xn, idx_t, lrank_t, loc, tables + tails, p_rows, r_loc)
    y = _experts(xs, block_e, n_used, w_gate_up[l], b_gate_up[l], w_down[l], b_down[l])
    out = _combine(pos_t.T, gate_t.T, x2.reshape(nb * s_len, d), y, norm_final, tables, r_loc)
    return out.reshape(nb, s_len, d)
```

```python
import functools

import jax
import jax.numpy as jnp
from jax import lax
from jax.experimental import pallas as pl
from jax.experimental.pallas import tpu as pltpu

POOL_WINDOWS = (2, 4, 8, 16)
N_GROUPS = 4
CONV_WIDTH = 4
RG_C = 8.0
N_EXPERTS = 32
TOP_K = 4
SWIGLU_LIMIT = 7.0
SWIGLU_ALPHA = 1.702
RMS_EPS = 1e-6

MIX_STEPS = 32
KV_ROWS = 512
ATT_ROWS = 512
ROUTE_ROWS = 256
EXPERT_ROWS = 512
RUN_ALIGN = 8
VMEM_LIMIT = 52 * 1024 * 1024

BF16 = jnp.bfloat16
F32 = jnp.float32
I32 = jnp.int32


def _const_spec(shape):
    nd = len(shape)
    return pl.BlockSpec(shape, lambda *_: (0,) * nd, pipeline_mode=pl.Buffered(1))


def _rms(x, g):
    return x * lax.rsqrt(jnp.mean(x * x, axis=-1, keepdims=True) + RMS_EPS) * g


def _dot(a, b):
    return jnp.dot(a, b, preferred_element_type=F32)


def _dot_nt(a, b):
    return lax.dot_general(a, b, (((1,), (1,)), ((), ())), preferred_element_type=F32)


def _mixer_kernel(x_hbm, nm_ref, win_ref, wpg_ref, psc_ref, wpp_ref, cw_ref, cb_ref,
                  wa_ref, ba_ref, wx_ref, bx_ref, lam_ref, wlp_ref, wmo_ref,
                  o_ref,
                  xbuf, xsem,
                  h_ref, up_ref, ul_ref, a_ref, b_ref, pm_ref, m_ref, mb_ref, t_ref, gl_ref, gb_ref, hc_ref,
                  *, nb, ts):
    rows, d = o_ref.shape
    gw = d // N_GROUPS
    halo_p = (POOL_WINDOWS[-1]) * nb
    halo_c = (CONV_WIDTH - 1) * nb
    c = pl.program_id(0)

    @pl.when(c == 0)
    def _():
        up_ref[0:halo_p, :] = jnp.zeros((halo_p, d), F32)
        ul_ref[0:halo_c, :] = jnp.zeros((halo_c, d), F32)
        hc_ref[...] = jnp.zeros_like(hc_ref)

    def x_copy(step, t, slot):
        return pltpu.make_async_copy(x_hbm.at[:, step * ts + t, :], xbuf.at[slot, t], xsem.at[slot])

    def fetch(step, slot):
        lax.fori_loop(0, ts, lambda t, carry: (x_copy(step, t, slot).start(), carry)[1], 0)

    slot = c % 2

    @pl.when(c == 0)
    def _():
        fetch(0, 0)

    @pl.when(c + 1 < pl.num_programs(0))
    def _():
        fetch(c + 1, 1 - slot)

    lax.fori_loop(0, ts, lambda t, carry: (x_copy(c, t, slot).wait(), carry)[1], 0)

    h_ref[...] = _rms(xbuf[slot].reshape(rows, d), nm_ref[...]).astype(BF16)

    t_glob = c * ts + lax.broadcasted_iota(I32, (rows, 1), 0) // nb

    up_ref[halo_p:halo_p + rows, :] = _dot(h_ref[...], win_ref[:, 0:d])
    ul_ref[halo_c:halo_c + rows, :] = _dot(h_ref[...], win_ref[:, d:2 * d])

    for g, w in enumerate(POOL_WINDOWS):
        cols = slice(g * gw, (g + 1) * gw)
        u = up_ref[halo_p:halo_p + rows, cols]
        acc = u
        for j in range(1, w):
            acc = acc + up_ref[halo_p - j * nb:halo_p - j * nb + rows, cols]
        cnt = jnp.minimum(t_glob + 1, w).astype(F32)
        p = acc / cnt - u
        pg = _dot(p.astype(BF16), wpg_ref[g]) * psc_ref[:, cols]
        pm_ref[:, cols] = pg.astype(BF16)
    t_ref[...] = _dot(h_ref[...], win_ref[:, 3 * d:4 * d])
    m_ref[...] = _dot(pm_ref[...], wpp_ref[...])

    for g in range(N_GROUPS):
        cols = slice(g * gw, (g + 1) * gw)
        xr = cb_ref[:, cols]
        for k in range(CONV_WIDTH):
            off = halo_c - (CONV_WIDTH - 1 - k) * nb
            xr = xr + ul_ref[off:off + rows, cols] * cw_ref[k:k + 1, cols]
        xrb = xr.astype(BF16)
        r = jax.nn.sigmoid(_dot(xrb, wa_ref[g]) + ba_ref[:, cols])
        i = jax.nn.sigmoid(_dot(xrb, wx_ref[g]) + bx_ref[:, cols])
        gl_ref[:, cols] = _dot(h_ref[...], win_ref[:, 2 * d + g * gw:2 * d + (g + 1) * gw])
        gb_ref[:, cols] = _dot(h_ref[...], win_ref[:, 4 * d + g * gw:4 * d + (g + 1) * gw])
        lam = lam_ref[:, cols]
        log_sig = jnp.minimum(lam, 0.0) - jnp.log(1.0 + jnp.exp(-jnp.abs(lam)))
        a = jnp.exp((RG_C * r) * log_sig)
        mult = jnp.sqrt(jnp.maximum(1.0 - a * a, 0.0))
        mult = jnp.where(t_glob == 0, 1.0, mult)
        a_ref[:, cols] = a
        b_ref[:, cols] = mult * i * xr
    m_ref[...] = jax.nn.sigmoid(t_ref[...]) * m_ref[...]

    def scan_step(t, hprev):
        sl = pl.ds(pl.multiple_of(t * nb, nb), nb)
        hn = a_ref[sl, :] * hprev + b_ref[sl, :]
        b_ref[sl, :] = hn
        return hn

    hc_ref[...] = lax.fori_loop(0, ts, scan_step, hc_ref[...], unroll=4)

    half = rows // 2
    for hs in (slice(0, half), slice(half, rows)):
        pm_ref[hs, :] = (b_ref[hs, :] * jax.nn.gelu(gl_ref[hs, :], approximate=True)).astype(BF16)
    for hs in (slice(0, half), slice(half, rows)):
        yb = _dot(pm_ref[hs, :], wlp_ref[...])
        mb_ref[hs, :] = (m_ref[hs, :] + jax.nn.sigmoid(gb_ref[hs, :]) * yb).astype(BF16)
    for i, hs in enumerate((slice(0, half), slice(half, rows))):
        x_half = xbuf[slot, i * (ts // 2):(i + 1) * (ts // 2)].reshape(half, d)
        o_ref[hs, :] = x_half + _dot(mb_ref[hs, :], wmo_ref[...])

    up_ref[0:halo_p, :] = up_ref[rows:rows + halo_p, :]
    ul_ref[0:halo_c, :] = ul_ref[rows:rows + halo_c, :]


def _mixer(x, norm_mix, w_in, w_pool_group, pool_scale, w_pool_proj, conv_w, conv_b,
           lru_w_a, lru_b_a, lru_w_x, lru_b_x, lru_lambda, w_lru_proj, w_mix_out):
    nb, s_len, d = x.shape
    ts = MIX_STEPS
    rows = ts * nb
    n_rows = s_len * nb
    assert s_len % ts == 0 and ts % 2 == 0 and ts >= POOL_WINDOWS[-1] and nb % 8 == 0
    row2 = lambda v: v.reshape(1, -1)
    args = (x, row2(norm_mix), w_in.astype(BF16), w_pool_group.astype(BF16), row2(pool_scale),
            w_pool_proj.astype(BF16), conv_w, row2(conv_b), lru_w_a.astype(BF16), row2(lru_b_a),
            lru_w_x.astype(BF16), row2(lru_b_x), row2(lru_lambda), w_lru_proj.astype(BF16),
            w_mix_out.astype(BF16))
    in_specs = [pl.BlockSpec(memory_space=pl.ANY)] + [_const_spec(a.shape) for a in args[1:]]
    halo_p = POOL_WINDOWS[-1] * nb
    halo_c = (CONV_WIDTH - 1) * nb
    return pl.pallas_call(
        functools.partial(_mixer_kernel, nb=nb, ts=ts),
        grid=(n_rows // rows,),
        in_specs=in_specs,
        out_specs=pl.BlockSpec((rows, d), lambda c: (c, 0)),
        out_shape=jax.ShapeDtypeStruct((n_rows, d), F32),
        scratch_shapes=[
            pltpu.VMEM((2, ts, nb, d), F32),
            pltpu.SemaphoreType.DMA((2,)),
            pltpu.VMEM((rows, d), BF16),
            pltpu.VMEM((halo_p + rows, d), F32),
            pltpu.VMEM((halo_c + rows, d), F32),
            pltpu.VMEM((rows, d), F32),
            pltpu.VMEM((rows, d), F32),
            pltpu.VMEM((rows, d), BF16),
            pltpu.VMEM((rows, d), F32),
            pltpu.VMEM((rows, d), BF16),
            pltpu.VMEM((rows, d), F32),
            pltpu.VMEM((rows, d), F32),
            pltpu.VMEM((rows, d), F32),
            pltpu.VMEM((nb, d), F32),
        ],
        compiler_params=pltpu.CompilerParams(dimension_semantics=("arbitrary",),
                                             vmem_limit_bytes=VMEM_LIMIT),
        name="mixer",
    )(*args)


def _kv_kernel(m_ref, g_ref, w_ref, o_ref):
    o_ref[...] = _dot(_rms(m_ref[...], g_ref[...]).astype(BF16), w_ref[...]).astype(BF16)


def _kv_proj(mem2d, norm_mem, w_kv):
    n, d = mem2d.shape
    assert n % KV_ROWS == 0
    return pl.pallas_call(
        _kv_kernel,
        grid=(n // KV_ROWS,),
        in_specs=[pl.BlockSpec((KV_ROWS, d), lambda i: (i, 0)),
                  _const_spec((1, d)), _const_spec((d, 2 * d))],
        out_specs=pl.BlockSpec((KV_ROWS, 2 * d), lambda i: (i, 0)),
        out_shape=jax.ShapeDtypeStruct((n, 2 * d), BF16),
        compiler_params=pltpu.CompilerParams(dimension_semantics=("arbitrary",),
                                             vmem_limit_bytes=VMEM_LIMIT),
        name="kv_proj",
    )(mem2d, norm_mem.reshape(1, d), w_kv.astype(BF16))


def _attn_kernel(x_hbm, kv_ref, gx_ref, wq_ref, wo_ref, gm_ref, wr_ref, br_ref,
                 x2_ref, xn_ref, idx_ref, gate_ref, lrank_ref, cnt_ref,
                 xbuf, xsem, o_scr, xh_scr, xl_scr):
    tq, d = x2_ref.shape
    hd = d // N_GROUPS
    ne = br_ref.shape[0]
    tb = ROUTE_ROWS

    b, i = pl.program_id(0), pl.program_id(1)
    nq = pl.num_programs(1)
    step = b * nq + i
    slot = step % 2

    def x_copy(bb, ii, s):
        return pltpu.make_async_copy(x_hbm.at[pl.ds(ii * tq, tq), bb, :], xbuf.at[s], xsem.at[s])

    @pl.when(step == 0)
    def _():
        x_copy(0, 0, 0).start()

    @pl.when(step + 1 < pl.num_programs(0) * nq)
    def _():
        wrap = i + 1 == nq
        x_copy(jnp.where(wrap, b + 1, b), jnp.where(wrap, 0, i + 1), 1 - slot).start()

    x_copy(b, i, slot).wait()
    x = xbuf[slot]
    q = _dot(_rms(x, gx_ref[...]).astype(BF16), wq_ref[...]).astype(BF16)
    for h in range(N_GROUPS):
        k = kv_ref[:, h * hd:(h + 1) * hd]
        v = kv_ref[:, d + h * hd:d + (h + 1) * hd]
        s = _dot_nt(q[:, h * hd:(h + 1) * hd], k) * (hd ** -0.5)
        e = jnp.exp(s - jnp.max(s, axis=-1, keepdims=True))
        p = e / jnp.sum(e, axis=-1, keepdims=True)
        o_scr[:, h * hd:(h + 1) * hd] = _dot(p.astype(BF16), v).astype(BF16)
    x2 = x + _dot(o_scr[...], wo_ref[...])
    x2_ref[...] = x2
    xn = _rms(x2, gm_ref[...])
    xh = xn.astype(BF16)
    xn_ref[...] = xh
    xh_scr[...] = xh
    xl_scr[...] = (xn - xh.astype(F32)).astype(BF16)

    iota_f = lax.broadcasted_iota(I32, (ne, tb), 0).astype(F32)
    before = (lax.broadcasted_iota(I32, (tb, tb), 0) < lax.broadcasted_iota(I32, (tb, tb), 1)
              ).astype(BF16)
    for sb in range(tq // tb):
        rows = slice(sb * tb, (sb + 1) * tb)
        ph = _dot_nt(wr_ref[...], xh_scr[rows, :])
        pl_ = _dot_nt(wr_ref[0:ne, :], xl_scr[rows, :])
        work = ph[0:ne] + ph[ne:2 * ne] + pl_ + br_ref[...]
        vals, idxs, sels = [], [], []
        for _ in range(TOP_K):
            m = jnp.max(work, axis=0, keepdims=True)
            idx = jnp.min(jnp.where(work == m, iota_f, float(ne)), axis=0, keepdims=True)
            sel = iota_f == idx
            vals.append(m)
            idxs.append(idx.astype(I32))
            sels.append(sel)
            work = jnp.where(sel, -jnp.inf, work)
        ex = [jnp.exp(v - vals[0]) for v in vals]
        den = ex[0] + ex[1] + ex[2] + ex[3]
        onehot = jnp.zeros((ne, tb), F32)
        for sel in sels:
            onehot = onehot + sel.astype(F32)
        oh16 = onehot.astype(BF16)
        prefix = _dot(oh16, before)
        lr = [jnp.sum(jnp.where(sel, prefix, 0.0), axis=0, keepdims=True).astype(I32) for sel in sels]
        idx_ref[:, rows] = jnp.concatenate(idxs, axis=0)
        gate_ref[:, rows] = jnp.concatenate([e_ / den for e_ in ex], axis=0)
        lrank_ref[:, rows] = jnp.concatenate(lr, axis=0)
        cnt_ref[sb] = _dot_nt(jnp.ones((1, tb), BF16), oh16).astype(I32)


def _attention(x1, kv, norm_xattn, w_q, w_o, norm_moe, w_router, b_router):
    s_len, nb, d = x1.shape
    m = kv.shape[1]
    ne = w_router.shape[-1]
    tq, tb = ATT_ROWS, ROUTE_ROWS
    nq = s_len // tq
    t = nb * s_len
    assert s_len % tq == 0 and tq % tb == 0
    wr_hi = w_router.astype(BF16)
    wr_lo = (w_router - wr_hi.astype(F32)).astype(BF16)
    wr2t = jnp.concatenate([wr_hi, wr_lo], axis=1).T
    tok = lambda b, i: (0, b * nq + i)
    return pl.pallas_call(
        _attn_kernel,
        grid=(nb, nq),
        in_specs=[pl.BlockSpec(memory_space=pl.ANY),
                  pl.BlockSpec((None, m, 2 * d), lambda b, i: (b, 0, 0)),
                  _const_spec((1, d)), _const_spec((d, d)), _const_spec((d, d)),
                  _const_spec((1, d)), _const_spec((2 * ne, d)), _const_spec((ne, 1))],
        out_specs=[pl.BlockSpec((None, tq, d), lambda b, i: (b, i, 0)),
                   pl.BlockSpec((tq, d), lambda b, i: (b * nq + i, 0)),
                   pl.BlockSpec((TOP_K, tq), tok),
                   pl.BlockSpec((TOP_K, tq), tok),
                   pl.BlockSpec((TOP_K, tq), tok),
                   pl.BlockSpec((tq // tb, 1, ne), lambda b, i: (b * nq + i, 0, 0))],
        out_shape=[jax.ShapeDtypeStruct((nb, s_len, d), F32),
                   jax.ShapeDtypeStruct((t, d), BF16),
                   jax.ShapeDtypeStruct((TOP_K, t), I32),
                   jax.ShapeDtypeStruct((TOP_K, t), F32),
                   jax.ShapeDtypeStruct((TOP_K, t), I32),
                   jax.ShapeDtypeStruct((t // tb, 1, ne), I32)],
        scratch_shapes=[pltpu.VMEM((2, tq, d), F32), pltpu.SemaphoreType.DMA((2,)),
                        pltpu.VMEM((tq, d), BF16), pltpu.VMEM((tq, d), BF16), pltpu.VMEM((tq, d), BF16)],
        compiler_params=pltpu.CompilerParams(dimension_semantics=("arbitrary", "arbitrary"),
                                             vmem_limit_bytes=VMEM_LIMIT),
        name="attention",
    )(x1, kv, norm_xattn.reshape(1, d), w_q.astype(BF16), w_o.astype(BF16),
      norm_moe.reshape(1, d), wr2t, b_router.reshape(ne, 1))


def _pow2_chunks(limit):
    sizes = []
    c = RUN_ALIGN
    while c <= limit:
        sizes.append(c)
        c *= 2
    return sizes[::-1]


def _for_each_chunk(n, limit, fn):
    for size in _pow2_chunks(limit):
        @pl.when((n & size) != 0)
        def _(size=size):
            fn(pl.multiple_of(n & ~(2 * size - 1), RUN_ALIGN), size)


def _wait_rows(n, limit, src, dst, sem):
    _for_each_chunk(n, limit, lambda off, size: pltpu.make_async_copy(
        src.at[pl.ds(0, size)], dst.at[pl.ds(0, size)], sem).wait())


def _dispatch_kernel(n8_ref, loc_ref, run_ref, tot_ref, tailn_ref, tails_ref,
                     xn_ref, idx_ref, lrank_ref, locv_ref, xs_hbm, pos_ref, buf, zbuf, sems, zsem):
    b = pl.program_id(0)
    nblk = pl.num_programs(0)
    tb = xn_ref.shape[0]
    r_loc = buf.shape[1]
    ne = tailn_ref.shape[0] - 1
    slot = b % 2

    zrows = zbuf.shape[0]

    def zero_rest(i, carry):
        dst = pl.multiple_of(tails_ref[ne] + i * zrows, zrows)
        pltpu.make_async_copy(zbuf, xs_hbm.at[pl.ds(dst, zrows)], zsem).start()
        return carry

    def wait_rest(i, carry):
        pltpu.make_async_copy(zbuf, xs_hbm.at[pl.ds(0, zrows)], zsem).wait()
        return carry

    @pl.when(b == 0)
    def _():
        zbuf[...] = jnp.zeros_like(zbuf)
        for e in range(ne):
            _for_each_chunk(tailn_ref[e], zrows, lambda off, size, e=e: pltpu.make_async_copy(
                zbuf.at[pl.ds(0, size)],
                xs_hbm.at[pl.ds(pl.multiple_of(tails_ref[e] + off, RUN_ALIGN), size)], zsem).start())
        lax.fori_loop(0, tailn_ref[ne], zero_rest, 0)

    @pl.when(b >= 2)
    def _():
        _wait_rows(tot_ref[b - 2], r_loc, buf.at[slot], xs_hbm, sems.at[slot])

    iota_e = lax.broadcasted_iota(I32, (ne, tb), 0)
    loc_col = locv_ref[...].astype(F32)
    for k in range(TOP_K):
        run0 = jnp.sum(jnp.where(iota_e == idx_ref[k:k + 1, :], loc_col, 0.0), axis=0, keepdims=True)
        pos_ref[k:k + 1, :] = run0.astype(I32) + lrank_ref[k:k + 1, :]

    iota_r = lax.broadcasted_iota(I32, (r_loc, tb), 0)
    hit = iota_r == pos_ref[0:1, :]
    for k in range(1, TOP_K):
        hit = jnp.logical_or(hit, iota_r == pos_ref[k:k + 1, :])
    buf[slot] = _dot(hit.astype(BF16), xn_ref[...])

    for e in range(ne):
        n = n8_ref[b * ne + e]
        src0 = loc_ref[b * ne + e]
        dst0 = run_ref[b * ne + e]
        _for_each_chunk(n, tb, lambda off, size, src0=src0, dst0=dst0: pltpu.make_async_copy(
            buf.at[slot, pl.ds(pl.multiple_of(src0 + off, RUN_ALIGN), size)],
            xs_hbm.at[pl.ds(pl.multiple_of(dst0 + off, RUN_ALIGN), size)], sems.at[slot]).start())

    @pl.when(b == nblk - 1)
    def _():
        @pl.when(b >= 1)
        def _():
            _wait_rows(tot_ref[b - 1], r_loc, buf.at[1 - slot], xs_hbm, sems.at[1 - slot])
        _wait_rows(tot_ref[b], r_loc, buf.at[slot], xs_hbm, sems.at[slot])
        for e in range(ne):
            _wait_rows(tailn_ref[e], zrows, zbuf, xs_hbm, zsem)
        lax.fori_loop(0, tailn_ref[ne], wait_rest, 0)


def _dispatch(xn, idx_t, lrank_t, loc, tables, p_rows, r_loc):
    t, d = xn.shape
    tb = ROUTE_ROWS
    nblk, ne = loc.shape
    tok = pl.BlockSpec((TOP_K, tb), lambda b, *_: (0, b))
    grid_spec = pltpu.PrefetchScalarGridSpec(
        num_scalar_prefetch=6,
        grid=(t // tb,),
        in_specs=[pl.BlockSpec((tb, d), lambda b, *_: (b, 0)), tok, tok,
                  pl.BlockSpec((None, ne, 1), lambda b, *_: (b, 0, 0))],
        out_specs=[pl.BlockSpec(memory_space=pl.ANY), tok],
        scratch_shapes=[pltpu.VMEM((2, r_loc, d), F32), pltpu.VMEM((EXPERT_ROWS // 2, d), F32),
                        pltpu.SemaphoreType.DMA((2,)), pltpu.SemaphoreType.DMA],
    )
    return pl.pallas_call(
        _dispatch_kernel,
        grid_spec=grid_spec,
        out_shape=[jax.ShapeDtypeStruct((p_rows, d), F32), jax.ShapeDtypeStruct((TOP_K, t), I32)],
        compiler_params=pltpu.CompilerParams(dimension_semantics=("arbitrary",),
                                             vmem_limit_bytes=VMEM_LIMIT),
        name="dispatch",
    )(*tables, xn, idx_t, lrank_t, loc.reshape(nblk, ne, 1))


def _expert_kernel(blk0_ref, nblk_ref, nu_ref, xs_hbm, wgu_ref, bgu_ref, wdn_ref, bdn_ref, y_hbm,
                   xbuf, ybuf, zbuf, in_sem, out_sem, zsem, wgu_bf, wdn_bf):
    e = pl.program_id(0)
    bm = xbuf.shape[1]
    zrows = zbuf.shape[0]
    de = wdn_ref.shape[0]
    n_used = nu_ref[0]

    def x_copy(g, slot):
        return pltpu.make_async_copy(xs_hbm.at[pl.ds(pl.multiple_of(g * bm, bm), bm)], xbuf.at[slot],
                                     in_sem.at[slot])

    def y_copy(g, slot):
        return pltpu.make_async_copy(ybuf.at[slot], y_hbm.at[pl.ds(pl.multiple_of(g * bm, bm), bm)],
                                     out_sem.at[slot])

    @pl.when(jnp.logical_and(e == 0, n_used > 0))
    def _():
        x_copy(0, 0).start()

    wgu_bf[...] = wgu_ref[...].astype(BF16)
    wdn_bf[...] = wdn_ref[...].astype(BF16)

    def block(j, carry):
        g = blk0_ref[e] + j
        slot = g % 2

        @pl.when(g + 1 < n_used)
        def _():
            x_copy(g + 1, 1 - slot).start()

        x_copy(g, slot).wait()

        @pl.when(g >= 2)
        def _():
            y_copy(g - 2, slot).wait()

        hgu = _dot(xbuf[slot].astype(BF16), wgu_bf[...]) + bgu_ref[...]
        gl = jnp.minimum(hgu[:, :de], SWIGLU_LIMIT)
        up = jnp.clip(hgu[:, de:], -SWIGLU_LIMIT, SWIGLU_LIMIT)
        act = (up + 1.0) * (gl * jax.nn.sigmoid(SWIGLU_ALPHA * gl))
        ybuf[slot] = _dot(act.astype(BF16), wdn_bf[...]) + bdn_ref[...]
        y_copy(g, slot).start()
        return carry

    lax.fori_loop(0, nblk_ref[e], block, 0)

    @pl.when(e == pl.num_programs(0) - 1)
    def _():
        for back in (2, 1):
            @pl.when(n_used >= back)
            def _(back=back):
                y_copy(n_used - back, (n_used - back) % 2).wait()
        n_zero = (y_hbm.shape[0] - n_used * bm) // zrows
        zbuf[...] = jnp.zeros_like(zbuf)

        def z_copy(i):
            return pltpu.make_async_copy(
                zbuf, y_hbm.at[pl.ds(pl.multiple_of(n_used * bm + i * zrows, zrows), zrows)], zsem)

        lax.fori_loop(0, n_zero, lambda i, c: (z_copy(i).start(), c)[1], 0)
        lax.fori_loop(0, n_zero, lambda i, c: (z_copy(i).wait(), c)[1], 0)


def _experts(xs, blk0, nblk, n_used, w_gate_up, b_gate_up, w_down, b_down):
    p, d = xs.shape
    ne, _, de2 = w_gate_up.shape
    de = de2 // 2
    bm = EXPERT_ROWS
    per_expert = lambda shape: pl.BlockSpec((None,) + shape, lambda e, *_: (e, 0, 0))
    grid_spec = pltpu.PrefetchScalarGridSpec(
        num_scalar_prefetch=3,
        grid=(ne,),
        in_specs=[pl.BlockSpec(memory_space=pl.ANY),
                  per_expert((d, de2)), per_expert((1, de2)), per_expert((de, d)), per_expert((1, d))],
        out_specs=pl.BlockSpec(memory_space=pl.ANY),
        scratch_shapes=[pltpu.VMEM((2, bm, d), F32), pltpu.VMEM((2, bm, d), F32),
                        pltpu.VMEM((bm // 4, d), F32),
                        pltpu.SemaphoreType.DMA((2,)), pltpu.SemaphoreType.DMA((2,)), pltpu.SemaphoreType.DMA,
                        pltpu.VMEM((d, de2), BF16), pltpu.VMEM((de, d), BF16)],
    )
    return pl.pallas_call(
        _expert_kernel,
        grid_spec=grid_spec,
        out_shape=jax.ShapeDtypeStruct((p, d), F32),
        compiler_params=pltpu.CompilerParams(dimension_semantics=("arbitrary",),
                                             vmem_limit_bytes=VMEM_LIMIT),
        name="experts",
    )(blk0, nblk, n_used, xs, w_gate_up, b_gate_up.reshape(ne, 1, de2), w_down, b_down.reshape(ne, 1, d))


def _combine_kernel(n8_ref, loc_ref, run_ref, tot_ref,
                    pos_ref, gate_ref, x2_ref, gf_ref, y_hbm, o_ref, buf, sems):
    b = pl.program_id(0)
    nblk = pl.num_programs(0)
    tb = x2_ref.shape[0]
    r_loc = buf.shape[1]
    ne = n8_ref.shape[0] // tot_ref.shape[0]
    slot = b % 2

    def fetch(blk, s):
        for e in range(ne):
            n = n8_ref[blk * ne + e]
            dst0 = loc_ref[blk * ne + e]
            src0 = run_ref[blk * ne + e]
            _for_each_chunk(n, tb, lambda off, size, src0=src0, dst0=dst0: pltpu.make_async_copy(
                y_hbm.at[pl.ds(pl.multiple_of(src0 + off, RUN_ALIGN), size)],
                buf.at[s, pl.ds(pl.multiple_of(dst0 + off, RUN_ALIGN), size)], sems.at[s]).start())

    @pl.when(b == 0)
    def _():
        buf[...] = jnp.zeros_like(buf)
        fetch(0, 0)

    @pl.when(b + 1 < nblk)
    def _():
        fetch(b + 1, 1 - slot)

    _wait_rows(tot_ref[b], r_loc, y_hbm, buf.at[slot], sems.at[slot])

    iota_c = lax.broadcasted_iota(I32, (tb, r_loc), 1)
    w = jnp.zeros((tb, r_loc), F32)
    for k in range(TOP_K):
        w = w + jnp.where(iota_c == pos_ref[:, k:k + 1], gate_ref[:, k:k + 1], 0.0)
    w_hi = w.astype(BF16)
    w_lo = (w - w_hi.astype(F32)).astype(BF16)
    y = buf[slot].astype(BF16)
    o_ref[...] = _rms(x2_ref[...] + _dot(w_hi, y) + _dot(w_lo, y), gf_ref[...])


def _combine(pos, gates, x2, y, norm_final, tables, r_loc):
    t, d = x2.shape
    tb = ROUTE_ROWS
    grid_spec = pltpu.PrefetchScalarGridSpec(
        num_scalar_prefetch=4,
        grid=(t // tb,),
        in_specs=[pl.BlockSpec((tb, TOP_K), lambda b, *_: (b, 0)),
                  pl.BlockSpec((tb, TOP_K), lambda b, *_: (b, 0)),
                  pl.BlockSpec((tb, d), lambda b, *_: (b, 0)),
                  pl.BlockSpec((1, d), lambda b, *_: (0, 0)),
                  pl.BlockSpec(memory_space=pl.ANY)],
        out_specs=pl.BlockSpec((tb, d), lambda b, *_: (b, 0)),
        scratch_shapes=[pltpu.VMEM((2, r_loc, d), F32), pltpu.SemaphoreType.DMA((2,))],
    )
    return pl.pallas_call(
        _combine_kernel,
        grid_spec=grid_spec,
        out_shape=jax.ShapeDtypeStruct((t, d), F32),
        compiler_params=pltpu.CompilerParams(dimension_semantics=("arbitrary",),
                                             vmem_limit_bytes=VMEM_LIMIT),
        name="combine",
    )(*tables, pos, gates, x2, norm_final.reshape(1, d), y)


def _excl_cumsum(a, axis):
    n = a.shape[axis]
    a = jnp.moveaxis(a, axis, -1)
    earlier = jnp.arange(n)[None, :] < jnp.arange(n)[:, None]
    out = jnp.sum(jnp.where(earlier, a[..., None, :], 0), axis=-1)
    return jnp.moveaxis(out, -1, axis)


def _layout(cnt):
    nblk, ne = cnt.shape
    tb, bm = ROUTE_ROWS, EXPERT_ROWS
    n8 = (cnt + RUN_ALIGN - 1) // RUN_ALIGN * RUN_ALIGN
    loc = _excl_cumsum(n8, 1)
    tot = jnp.sum(n8, axis=1)
    size = jnp.sum(n8, axis=0)
    padded = (size + bm - 1) // bm * bm
    pstart = _excl_cumsum(padded, 0)
    pend = pstart + padded
    run = pstart[None, :] + _excl_cumsum(n8, 0)
    p_blocks = -(-(nblk * tb * TOP_K + nblk * ne * (RUN_ALIGN - 1) + ne * (bm - RUN_ALIGN)) // bm)
    n_used = (pend[-1] // bm).astype(I32)
    flat = lambda a: a.reshape(-1).astype(I32)
    tables = (flat(n8), flat(loc), flat(run), flat(tot))
    half = bm // 2
    tails = (flat(jnp.concatenate([padded - size, (p_blocks * bm - pend[-1:]) // half])),
             flat(jnp.concatenate([pstart + size, pend[-1:]])))
    r_loc = -(-(tb * TOP_K + ne * (RUN_ALIGN - 1)) // 256) * 256
    blocks = ((pstart // bm).astype(I32), (padded // bm).astype(I32), n_used.reshape(1))
    return tables, tails, loc.astype(I32), blocks, p_blocks * bm, r_loc


def kernel(x, mem, norm_mix, w_in, w_pool_group, pool_scale, w_pool_proj, conv_w, conv_b, lru_w_a, lru_b_a, lru_w_x, lru_b_x, lru_lambda, w_lru_proj, w_mix_out, norm_xattn, norm_mem, w_q, w_kv, w_o, norm_moe, w_router, b_router, w_gate_up, b_gate_up, w_down, b_down, norm_final):
    nb, s_len, d = x.shape
    m_len = mem.shape[1]
    assert norm_mix.shape[0] == 1, "single-layer stack"
    l = 0
    x1 = _mixer(x, norm_mix[l], w_in[l], w_pool_group[l], pool_scale[l], w_pool_proj[l],
                conv_w[l], conv_b[l], lru_w_a[l], lru_b_a[l], lru_w_x[l], lru_b_x[l],
                lru_lambda[l], w_lru_proj[l], w_mix_out[l])
    kv = _kv_proj(mem.reshape(nb * m_len, d), norm_mem[l], w_kv[l]).reshape(nb, m_len, 2 * d)
    x2, xn, idx_t, gate_t, lrank_t, cnt = _attention(
        x1.reshape(s_len, nb, d), kv, norm_xattn[l], w_q[l], w_o[l],
        norm_moe[l], w_router[l], b_router[l])
    tables, tails, loc, blocks, p_rows, r_loc = _layout(cnt.reshape(cnt.shape[0], -1))
    xs, pos_t = _dispatch(xn, idx_t, lrank_t, loc, tables + tails, p_rows, r_loc)
    y = _experts(xs, *blocks, w_gate_up[l], b_gate_up[l], w_down[l], b_down[l])
    out = _combine(pos_t.T, gate_t.T, x2.reshape(nb * s_len, d), y, norm_final, tables, r_loc)
    return out.reshape(nb, s_len, d)
```

```python
import functools

import jax
import jax.numpy as jnp
from jax import lax
from jax.experimental import pallas as pl
from jax.experimental.pallas import tpu as pltpu

POOL_WINDOWS = (2, 4, 8, 16)
N_GROUPS = 4
CONV_WIDTH = 4
RG_C = 8.0
N_EXPERTS = 32
TOP_K = 4
SWIGLU_LIMIT = 7.0
SWIGLU_ALPHA = 1.702
RMS_EPS = 1e-6

MIX_STEPS = 32
KV_ROWS = 512
ATT_ROWS = 512
ROUTE_ROWS = 256
EXPERT_ROWS = 512
RUN_ALIGN = 8
VMEM_LIMIT = 52 * 1024 * 1024

BF16 = jnp.bfloat16
F32 = jnp.float32
I32 = jnp.int32


def _const_spec(shape):
    nd = len(shape)
    return pl.BlockSpec(shape, lambda *_: (0,) * nd, pipeline_mode=pl.Buffered(1))


def _rms(x, g):
    return x * lax.rsqrt(jnp.mean(x * x, axis=-1, keepdims=True) + RMS_EPS) * g


def _dot(a, b):
    return jnp.dot(a, b, preferred_element_type=F32)


def _dot_nt(a, b):
    return lax.dot_general(a, b, (((1,), (1,)), ((), ())), preferred_element_type=F32)


def _mixer_kernel(x_hbm, nm_ref, win_ref, wpg_ref, psc_ref, wpp_ref, cw_ref, cb_ref,
                  wa_ref, ba_ref, wx_ref, bx_ref, lam_ref, wlp_ref, wmo_ref,
                  o_ref,
                  xbuf, xsem,
                  h_ref, up_ref, ul_ref, a_ref, b_ref, pm_ref, m_ref, mb_ref, t_ref, gl_ref, gb_ref, hc_ref,
                  *, nb, ts):
    rows, d = o_ref.shape
    gw = d // N_GROUPS
    halo_p = (POOL_WINDOWS[-1]) * nb
    halo_c = (CONV_WIDTH - 1) * nb
    c = pl.program_id(0)

    @pl.when(c == 0)
    def _():
        up_ref[0:halo_p, :] = jnp.zeros((halo_p, d), F32)
        ul_ref[0:halo_c, :] = jnp.zeros((halo_c, d), F32)
        hc_ref[...] = jnp.zeros_like(hc_ref)

    def x_copy(step, t, slot):
        return pltpu.make_async_copy(x_hbm.at[:, step * ts + t, :], xbuf.at[slot, t], xsem.at[slot])

    def fetch(step, slot):
        lax.fori_loop(0, ts, lambda t, carry: (x_copy(step, t, slot).start(), carry)[1], 0)

    slot = c % 2

    @pl.when(c == 0)
    def _():
        fetch(0, 0)

    @pl.when(c + 1 < pl.num_programs(0))
    def _():
        fetch(c + 1, 1 - slot)

    lax.fori_loop(0, ts, lambda t, carry: (x_copy(c, t, slot).wait(), carry)[1], 0)

    h_ref[...] = _rms(xbuf[slot].reshape(rows, d), nm_ref[...]).astype(BF16)

    t_glob = c * ts + lax.broadcasted_iota(I32, (rows, 1), 0) // nb

    up_ref[halo_p:halo_p + rows, :] = _dot(h_ref[...], win_ref[:, 0:d])
    ul_ref[halo_c:halo_c + rows, :] = _dot(h_ref[...], win_ref[:, d:2 * d])

    for g, w in enumerate(POOL_WINDOWS):
        cols = slice(g * gw, (g + 1) * gw)
        u = up_ref[halo_p:halo_p + rows, cols]
        acc = u
        for j in range(1, w):
            acc = acc + up_ref[halo_p - j * nb:halo_p - j * nb + rows, cols]
        cnt = jnp.minimum(t_glob + 1, w).astype(F32)
        p = acc / cnt - u
        pg = _dot(p.astype(BF16), wpg_ref[g]) * psc_ref[:, cols]
        pm_ref[:, cols] = pg.astype(BF16)
    t_ref[...] = _dot(h_ref[...], win_ref[:, 3 * d:4 * d])
    m_ref[...] = _dot(pm_ref[...], wpp_ref[...])

    for g in range(N_GROUPS):
        cols = slice(g * gw, (g + 1) * gw)
        xr = cb_ref[:, cols]
        for k in range(CONV_WIDTH):
            off = halo_c - (CONV_WIDTH - 1 - k) * nb
            xr = xr + ul_ref[off:off + rows, cols] * cw_ref[k:k + 1, cols]
        xrb = xr.astype(BF16)
        r = jax.nn.sigmoid(_dot(xrb, wa_ref[g]) + ba_ref[:, cols])
        i = jax.nn.sigmoid(_dot(xrb, wx_ref[g]) + bx_ref[:, cols])
        gl_ref[:, cols] = _dot(h_ref[...], win_ref[:, 2 * d + g * gw:2 * d + (g + 1) * gw])
        gb_ref[:, cols] = _dot(h_ref[...], win_ref[:, 4 * d + g * gw:4 * d + (g + 1) * gw])
        lam = lam_ref[:, cols]
        log_sig = jnp.minimum(lam, 0.0) - jnp.log(1.0 + jnp.exp(-jnp.abs(lam)))
        a = jnp.exp((RG_C * r) * log_sig)
        mult = jnp.sqrt(jnp.maximum(1.0 - a * a, 0.0))
        mult = jnp.where(t_glob == 0, 1.0, mult)
        a_ref[:, cols] = a
        b_ref[:, cols] = mult * i * xr
    m_ref[...] = jax.nn.sigmoid(t_ref[...]) * m_ref[...]

    def scan_step(t, hprev):
        sl = pl.ds(pl.multiple_of(t * nb, nb), nb)
        hn = a_ref[sl, :] * hprev + b_ref[sl, :]
        b_ref[sl, :] = hn
        return hn

    hc_ref[...] = lax.fori_loop(0, ts, scan_step, hc_ref[...], unroll=4)

    half = rows // 2
    for hs in (slice(0, half), slice(half, rows)):
        pm_ref[hs, :] = (b_ref[hs, :] * jax.nn.gelu(gl_ref[hs, :], approximate=True)).astype(BF16)
    for hs in (slice(0, half), slice(half, rows)):
        yb = _dot(pm_ref[hs, :], wlp_ref[...])
        mb_ref[hs, :] = (m_ref[hs, :] + jax.nn.sigmoid(gb_ref[hs, :]) * yb).astype(BF16)
    for i, hs in enumerate((slice(0, half), slice(half, rows))):
        x_half = xbuf[slot, i * (ts // 2):(i + 1) * (ts // 2)].reshape(half, d)
        o_ref[hs, :] = x_half + _dot(mb_ref[hs, :], wmo_ref[...])

    up_ref[0:halo_p, :] = up_ref[rows:rows + halo_p, :]
    ul_ref[0:halo_c, :] = ul_ref[rows:rows + halo_c, :]


def _mixer(x, norm_mix, w_in, w_pool_group, pool_scale, w_pool_proj, conv_w, conv_b,
           lru_w_a, lru_b_a, lru_w_x, lru_b_x, lru_lambda, w_lru_proj, w_mix_out):
    nb, s_len, d = x.shape
    ts = MIX_STEPS
    rows = ts * nb
    n_rows = s_len * nb
    assert s_len % ts == 0 and ts % 2 == 0 and ts >= POOL_WINDOWS[-1] and nb % 8 == 0
    row2 = lambda v: v.reshape(1, -1)
    args = (x, row2(norm_mix), w_in.astype(BF16), w_pool_group.astype(BF16), row2(pool_scale),
            w_pool_proj.astype(BF16), conv_w, row2(conv_b), lru_w_a.astype(BF16), row2(lru_b_a),
            lru_w_x.astype(BF16), row2(lru_b_x), row2(lru_lambda), w_lru_proj.astype(BF16),
            w_mix_out.astype(BF16))
    in_specs = [pl.BlockSpec(memory_space=pl.ANY)] + [_const_spec(a.shape) for a in args[1:]]
    halo_p = POOL_WINDOWS[-1] * nb
    halo_c = (CONV_WIDTH - 1) * nb
    return pl.pallas_call(
        functools.partial(_mixer_kernel, nb=nb, ts=ts),
        grid=(n_rows // rows,),
        in_specs=in_specs,
        out_specs=pl.BlockSpec((rows, d), lambda c: (c, 0)),
        out_shape=jax.ShapeDtypeStruct((n_rows, d), F32),
        scratch_shapes=[
            pltpu.VMEM((2, ts, nb, d), F32),
            pltpu.SemaphoreType.DMA((2,)),
            pltpu.VMEM((rows, d), BF16),
            pltpu.VMEM((halo_p + rows, d), F32),
            pltpu.VMEM((halo_c + rows, d), F32),
            pltpu.VMEM((rows, d), F32),
            pltpu.VMEM((rows, d), F32),
            pltpu.VMEM((rows, d), BF16),
            pltpu.VMEM((rows, d), F32),
            pltpu.VMEM((rows, d), BF16),
            pltpu.VMEM((rows, d), F32),
            pltpu.VMEM((rows, d), F32),
            pltpu.VMEM((rows, d), F32),
            pltpu.VMEM((nb, d), F32),
        ],
        compiler_params=pltpu.CompilerParams(dimension_semantics=("arbitrary",),
                                             vmem_limit_bytes=VMEM_LIMIT),
        name="mixer",
    )(*args)


def _kv_kernel(m_ref, g_ref, w_ref, o_ref):
    o_ref[...] = _dot(_rms(m_ref[...], g_ref[...]).astype(BF16), w_ref[...]).astype(BF16)


def _kv_proj(mem2d, norm_mem, w_kv):
    n, d = mem2d.shape
    assert n % KV_ROWS == 0
    return pl.pallas_call(
        _kv_kernel,
        grid=(n // KV_ROWS,),
        in_specs=[pl.BlockSpec((KV_ROWS, d), lambda i: (i, 0)),
                  _const_spec((1, d)), _const_spec((d, 2 * d))],
        out_specs=pl.BlockSpec((KV_ROWS, 2 * d), lambda i: (i, 0)),
        out_shape=jax.ShapeDtypeStruct((n, 2 * d), BF16),
        compiler_params=pltpu.CompilerParams(dimension_semantics=("arbitrary",),
                                             vmem_limit_bytes=VMEM_LIMIT),
        name="kv_proj",
    )(mem2d, norm_mem.reshape(1, d), w_kv.astype(BF16))


def _attn_kernel(x_hbm, kv_ref, gx_ref, wq_ref, wo_ref, gm_ref, wr_ref, br_ref,
                 x2_ref, xn_ref, idx_ref, gate_ref, lrank_ref, cnt_ref,
                 xbuf, xsem, o_scr, xh_scr, xl_scr):
    tq, d = x2_ref.shape
    hd = d // N_GROUPS
    ne = br_ref.shape[0]
    tb = ROUTE_ROWS

    b, i = pl.program_id(0), pl.program_id(1)
    nq = pl.num_programs(1)
    step = b * nq + i
    slot = step % 2

    def x_copy(bb, ii, s):
        return pltpu.make_async_copy(x_hbm.at[pl.ds(ii * tq, tq), bb, :], xbuf.at[s], xsem.at[s])

    @pl.when(step == 0)
    def _():
        x_copy(0, 0, 0).start()

    @pl.when(step + 1 < pl.num_programs(0) * nq)
    def _():
        wrap = i + 1 == nq
        x_copy(jnp.where(wrap, b + 1, b), jnp.where(wrap, 0, i + 1), 1 - slot).start()

    x_copy(b, i, slot).wait()
    x = xbuf[slot]
    q = _dot(_rms(x, gx_ref[...]).astype(BF16), wq_ref[...]).astype(BF16)
    for h in range(N_GROUPS):
        k = kv_ref[:, h * hd:(h + 1) * hd]
        v = kv_ref[:, d + h * hd:d + (h + 1) * hd]
        s = _dot_nt(q[:, h * hd:(h + 1) * hd], k) * (hd ** -0.5)
        e = jnp.exp(s - jnp.max(s, axis=-1, keepdims=True))
        p = e / jnp.sum(e, axis=-1, keepdims=True)
        o_scr[:, h * hd:(h + 1) * hd] = _dot(p.astype(BF16), v).astype(BF16)
    x2 = x + _dot(o_scr[...], wo_ref[...])
    x2_ref[...] = x2
    xn = _rms(x2, gm_ref[...])
    xh = xn.astype(BF16)
    xn_ref[...] = xh
    xh_scr[...] = xh
    xl_scr[...] = (xn - xh.astype(F32)).astype(BF16)

    iota_f = lax.broadcasted_iota(I32, (ne, tb), 0).astype(F32)
    before = (lax.broadcasted_iota(I32, (tb, tb), 0) < lax.broadcasted_iota(I32, (tb, tb), 1)
              ).astype(BF16)
    for sb in range(tq // tb):
        rows = slice(sb * tb, (sb + 1) * tb)
        ph = _dot_nt(wr_ref[...], xh_scr[rows, :])
        pl_ = _dot_nt(wr_ref[0:ne, :], xl_scr[rows, :])
        work = ph[0:ne] + ph[ne:2 * ne] + pl_ + br_ref[...]
        vals, idxs, sels = [], [], []
        for _ in range(TOP_K):
            m = jnp.max(work, axis=0, keepdims=True)
            idx = jnp.min(jnp.where(work == m, iota_f, float(ne)), axis=0, keepdims=True)
            sel = iota_f == idx
            vals.append(m)
            idxs.append(idx.astype(I32))
            sels.append(sel)
            work = jnp.where(sel, -jnp.inf, work)
        ex = [jnp.exp(v - vals[0]) for v in vals]
        den = ex[0] + ex[1] + ex[2] + ex[3]
        onehot = jnp.zeros((ne, tb), F32)
        for sel in sels:
            onehot = onehot + sel.astype(F32)
        oh16 = onehot.astype(BF16)
        prefix = _dot(oh16, before)
        lr = [jnp.sum(jnp.where(sel, prefix, 0.0), axis=0, keepdims=True).astype(I32) for sel in sels]
        idx_ref[:, rows] = jnp.concatenate(idxs, axis=0)
        gate_ref[:, rows] = jnp.concatenate([e_ / den for e_ in ex], axis=0)
        lrank_ref[:, rows] = jnp.concatenate(lr, axis=0)
        cnt_ref[sb] = _dot_nt(jnp.ones((1, tb), BF16), oh16).astype(I32)


def _attention(x1, kv, norm_xattn, w_q, w_o, norm_moe, w_router, b_router):
    s_len, nb, d = x1.shape
    m = kv.shape[1]
    ne = w_router.shape[-1]
    tq, tb = ATT_ROWS, ROUTE_ROWS
    nq = s_len // tq
    t = nb * s_len
    assert s_len % tq == 0 and tq % tb == 0
    wr_hi = w_router.astype(BF16)
    wr_lo = (w_router - wr_hi.astype(F32)).astype(BF16)
    wr2t = jnp.concatenate([wr_hi, wr_lo], axis=1).T
    tok = lambda b, i: (0, b * nq + i)
    return pl.pallas_call(
        _attn_kernel,
        grid=(nb, nq),
        in_specs=[pl.BlockSpec(memory_space=pl.ANY),
                  pl.BlockSpec((None, m, 2 * d), lambda b, i: (b, 0, 0)),
                  _const_spec((1, d)), _const_spec((d, d)), _const_spec((d, d)),
                  _const_spec((1, d)), _const_spec((2 * ne, d)), _const_spec((ne, 1))],
        out_specs=[pl.BlockSpec((None, tq, d), lambda b, i: (b, i, 0)),
                   pl.BlockSpec((tq, d), lambda b, i: (b * nq + i, 0)),
                   pl.BlockSpec((TOP_K, tq), tok),
                   pl.BlockSpec((TOP_K, tq), tok),
                   pl.BlockSpec((TOP_K, tq), tok),
                   pl.BlockSpec((tq // tb, 1, ne), lambda b, i: (b * nq + i, 0, 0))],
        out_shape=[jax.ShapeDtypeStruct((nb, s_len, d), F32),
                   jax.ShapeDtypeStruct((t, d), BF16),
                   jax.ShapeDtypeStruct((TOP_K, t), I32),
                   jax.ShapeDtypeStruct((TOP_K, t), F32),
                   jax.ShapeDtypeStruct((TOP_K, t), I32),
                   jax.ShapeDtypeStruct((t // tb, 1, ne), I32)],
        scratch_shapes=[pltpu.VMEM((2, tq, d), F32), pltpu.SemaphoreType.DMA((2,)),
                        pltpu.VMEM((tq, d), BF16), pltpu.VMEM((tq, d), BF16), pltpu.VMEM((tq, d), BF16)],
        compiler_params=pltpu.CompilerParams(dimension_semantics=("arbitrary", "arbitrary"),
                                             vmem_limit_bytes=VMEM_LIMIT),
        name="attention",
    )(x1, kv, norm_xattn.reshape(1, d), w_q.astype(BF16), w_o.astype(BF16),
      norm_moe.reshape(1, d), wr2t, b_router.reshape(ne, 1))


def _pow2_chunks(limit):
    sizes = []
    c = RUN_ALIGN
    while c <= limit:
        sizes.append(c)
        c *= 2
    return sizes[::-1]


def _for_each_chunk(n, limit, fn):
    for size in _pow2_chunks(limit):
        @pl.when((n & size) != 0)
        def _(size=size):
            fn(pl.multiple_of(n & ~(2 * size - 1), RUN_ALIGN), size)


def _wait_rows(n, limit, src, dst, sem):
    _for_each_chunk(n, limit, lambda off, size: pltpu.make_async_copy(
        src.at[pl.ds(0, size)], dst.at[pl.ds(0, size)], sem).wait())


def _dispatch_kernel(n8_ref, loc_ref, run_ref, tot_ref, tailn_ref, tails_ref,
                     xn_ref, idx_ref, lrank_ref, locv_ref, xs_hbm, pos_ref, buf, zbuf, sems, zsem):
    b = pl.program_id(0)
    nblk = pl.num_programs(0)
    tb = xn_ref.shape[0]
    r_loc = buf.shape[1]
    ne = tailn_ref.shape[0] - 1
    nbuf = buf.shape[0]
    slot = b % nbuf
    zrows = zbuf.shape[0]

    def send(blk, s, live):
        for e in range(ne):
            n = jnp.where(live, n8_ref[blk * ne + e], 0)
            src0 = loc_ref[blk * ne + e]
            dst0 = run_ref[blk * ne + e]
            _for_each_chunk(n, tb, lambda off, size, src0=src0, dst0=dst0: pltpu.make_async_copy(
                buf.at[s, pl.ds(pl.multiple_of(src0 + off, RUN_ALIGN), size)],
                xs_hbm.at[pl.ds(pl.multiple_of(dst0 + off, RUN_ALIGN), size)], sems.at[s]).start())

    def sent(blk, s, live):
        _wait_rows(jnp.where(live, tot_ref[blk], 0), r_loc, buf.at[s], xs_hbm, sems.at[s])

    def zero_rest(i, carry):
        dst = pl.multiple_of(tails_ref[ne] + i * zrows, zrows)
        pltpu.make_async_copy(zbuf, xs_hbm.at[pl.ds(dst, zrows)], zsem).start()
        return carry

    def wait_rest(i, carry):
        pltpu.make_async_copy(zbuf, xs_hbm.at[pl.ds(0, zrows)], zsem).wait()
        return carry

    @pl.when(b == 0)
    def _():
        zbuf[...] = jnp.zeros_like(zbuf)
        for e in range(ne):
            _for_each_chunk(tailn_ref[e], zrows, lambda off, size, e=e: pltpu.make_async_copy(
                zbuf.at[pl.ds(0, size)],
                xs_hbm.at[pl.ds(pl.multiple_of(tails_ref[e] + off, RUN_ALIGN), size)], zsem).start())
        lax.fori_loop(0, tailn_ref[ne], zero_rest, 0)

    sent(jnp.maximum(b - nbuf, 0), slot, b >= nbuf)
    send(jnp.maximum(b - 1, 0), (b + nbuf - 1) % nbuf, b >= 1)

    iota_e = lax.broadcasted_iota(I32, (ne, tb), 0)
    loc_col = locv_ref[...].astype(F32)
    for k in range(TOP_K):
        run0 = jnp.sum(jnp.where(iota_e == idx_ref[k:k + 1, :], loc_col, 0.0), axis=0, keepdims=True)
        pos_ref[k:k + 1, :] = run0.astype(I32) + lrank_ref[k:k + 1, :]

    iota_r = lax.broadcasted_iota(I32, (r_loc, tb), 0)
    hit = iota_r == pos_ref[0:1, :]
    for k in range(1, TOP_K):
        hit = jnp.logical_or(hit, iota_r == pos_ref[k:k + 1, :])
    buf[slot] = _dot(hit.astype(BF16), xn_ref[...])

    @pl.when(b == nblk - 1)
    def _():
        send(b, slot, True)
        for back in range(nbuf - 1, -1, -1):
            sent(jnp.maximum(b - back, 0), (b + nbuf - back) % nbuf, b >= back)
        for e in range(ne):
            _wait_rows(tailn_ref[e], zrows, zbuf, xs_hbm, zsem)
        lax.fori_loop(0, tailn_ref[ne], wait_rest, 0)


def _dispatch(xn, idx_t, lrank_t, loc, tables, p_rows, r_loc):
    t, d = xn.shape
    tb = ROUTE_ROWS
    nblk, ne = loc.shape
    tok = pl.BlockSpec((TOP_K, tb), lambda b, *_: (0, b))
    grid_spec = pltpu.PrefetchScalarGridSpec(
        num_scalar_prefetch=6,
        grid=(t // tb,),
        in_specs=[pl.BlockSpec((tb, d), lambda b, *_: (b, 0)), tok, tok,
                  pl.BlockSpec((None, ne, 1), lambda b, *_: (b, 0, 0))],
        out_specs=[pl.BlockSpec(memory_space=pl.ANY), tok],
        scratch_shapes=[pltpu.VMEM((3, r_loc, d), F32), pltpu.VMEM((EXPERT_ROWS // 2, d), F32),
                        pltpu.SemaphoreType.DMA((3,)), pltpu.SemaphoreType.DMA],
    )
    return pl.pallas_call(
        _dispatch_kernel,
        grid_spec=grid_spec,
        out_shape=[jax.ShapeDtypeStruct((p_rows, d), F32), jax.ShapeDtypeStruct((TOP_K, t), I32)],
        compiler_params=pltpu.CompilerParams(dimension_semantics=("arbitrary",),
                                             vmem_limit_bytes=VMEM_LIMIT),
        name="dispatch",
    )(*tables, xn, idx_t, lrank_t, loc.reshape(nblk, ne, 1))


def _expert_kernel(blk0_ref, nblk_ref, nu_ref, xs_hbm, wgu_ref, bgu_ref, wdn_ref, bdn_ref, y_hbm,
                   xbuf, ybuf, zbuf, in_sem, out_sem, zsem, wgu_bf, wdn_bf):
    e = pl.program_id(0)
    bm = xbuf.shape[1]
    zrows = zbuf.shape[0]
    de = wdn_ref.shape[0]
    n_used = nu_ref[0]

    def x_copy(g, slot):
        return pltpu.make_async_copy(xs_hbm.at[pl.ds(pl.multiple_of(g * bm, bm), bm)], xbuf.at[slot],
                                     in_sem.at[slot])

    def y_copy(g, slot):
        return pltpu.make_async_copy(ybuf.at[slot], y_hbm.at[pl.ds(pl.multiple_of(g * bm, bm), bm)],
                                     out_sem.at[slot])

    @pl.when(jnp.logical_and(e == 0, n_used > 0))
    def _():
        x_copy(0, 0).start()

    wgu_bf[...] = wgu_ref[...].astype(BF16)
    wdn_bf[...] = wdn_ref[...].astype(BF16)

    def block(j, carry):
        g = blk0_ref[e] + j
        slot = g % 2

        @pl.when(g + 1 < n_used)
        def _():
            x_copy(g + 1, 1 - slot).start()

        x_copy(g, slot).wait()

        @pl.when(g >= 2)
        def _():
            y_copy(g - 2, slot).wait()

        hgu = _dot(xbuf[slot].astype(BF16), wgu_bf[...]) + bgu_ref[...]
        gl = jnp.minimum(hgu[:, :de], SWIGLU_LIMIT)
        up = jnp.clip(hgu[:, de:], -SWIGLU_LIMIT, SWIGLU_LIMIT)
        act = (up + 1.0) * (gl * jax.nn.sigmoid(SWIGLU_ALPHA * gl))
        ybuf[slot] = _dot(act.astype(BF16), wdn_bf[...]) + bdn_ref[...]
        y_copy(g, slot).start()
        return carry

    lax.fori_loop(0, nblk_ref[e], block, 0)

    @pl.when(e == pl.num_programs(0) - 1)
    def _():
        for back in (2, 1):
            @pl.when(n_used >= back)
            def _(back=back):
                y_copy(n_used - back, (n_used - back) % 2).wait()
        n_zero = (y_hbm.shape[0] - n_used * bm) // zrows
        zbuf[...] = jnp.zeros_like(zbuf)

        def z_copy(i):
            return pltpu.make_async_copy(
                zbuf, y_hbm.at[pl.ds(pl.multiple_of(n_used * bm + i * zrows, zrows), zrows)], zsem)

        lax.fori_loop(0, n_zero, lambda i, c: (z_copy(i).start(), c)[1], 0)
        lax.fori_loop(0, n_zero, lambda i, c: (z_copy(i).wait(), c)[1], 0)


def _experts(xs, blk0, nblk, n_used, w_gate_up, b_gate_up, w_down, b_down):
    p, d = xs.shape
    ne, _, de2 = w_gate_up.shape
    de = de2 // 2
    bm = EXPERT_ROWS
    per_expert = lambda shape: pl.BlockSpec((None,) + shape, lambda e, *_: (e, 0, 0))
    grid_spec = pltpu.PrefetchScalarGridSpec(
        num_scalar_prefetch=3,
        grid=(ne,),
        in_specs=[pl.BlockSpec(memory_space=pl.ANY),
                  per_expert((d, de2)), per_expert((1, de2)), per_expert((de, d)), per_expert((1, d))],
        out_specs=pl.BlockSpec(memory_space=pl.ANY),
        scratch_shapes=[pltpu.VMEM((2, bm, d), F32), pltpu.VMEM((2, bm, d), F32),
                        pltpu.VMEM((bm // 4, d), F32),
                        pltpu.SemaphoreType.DMA((2,)), pltpu.SemaphoreType.DMA((2,)), pltpu.SemaphoreType.DMA,
                        pltpu.VMEM((d, de2), BF16), pltpu.VMEM((de, d), BF16)],
    )
    return pl.pallas_call(
        _expert_kernel,
        grid_spec=grid_spec,
        out_shape=jax.ShapeDtypeStruct((p, d), F32),
        compiler_params=pltpu.CompilerParams(dimension_semantics=("arbitrary",),
                                             vmem_limit_bytes=VMEM_LIMIT),
        name="experts",
    )(blk0, nblk, n_used, xs, w_gate_up, b_gate_up.reshape(ne, 1, de2), w_down, b_down.reshape(ne, 1, d))


def _combine_kernel(n8_ref, loc_ref, run_ref, tot_ref,
                    pos_ref, gate_ref, x2_ref, gf_ref, y_hbm, o_ref, buf, sems):
    b = pl.program_id(0)
    nblk = pl.num_programs(0)
    tb = x2_ref.shape[0]
    r_loc = buf.shape[1]
    ne = n8_ref.shape[0] // tot_ref.shape[0]
    nbuf = buf.shape[0]
    slot = b % nbuf

    def fetch(blk, s, live):
        for e in range(ne):
            n = jnp.where(live, n8_ref[blk * ne + e], 0)
            dst0 = loc_ref[blk * ne + e]
            src0 = run_ref[blk * ne + e]
            _for_each_chunk(n, tb, lambda off, size, src0=src0, dst0=dst0: pltpu.make_async_copy(
                y_hbm.at[pl.ds(pl.multiple_of(src0 + off, RUN_ALIGN), size)],
                buf.at[s, pl.ds(pl.multiple_of(dst0 + off, RUN_ALIGN), size)], sems.at[s]).start())

    @pl.when(b == 0)
    def _():
        buf[...] = jnp.zeros_like(buf)
        for ahead in range(nbuf - 1):
            fetch(jnp.minimum(ahead, nblk - 1), ahead, ahead < nblk)

    _wait_rows(tot_ref[b], r_loc, y_hbm, buf.at[slot], sems.at[slot])
    ahead = b + nbuf - 1
    fetch(jnp.minimum(ahead, nblk - 1), ahead % nbuf, ahead < nblk)

    iota_c = lax.broadcasted_iota(I32, (tb, r_loc), 1)
    w = jnp.zeros((tb, r_loc), F32)
    for k in range(TOP_K):
        w = w + jnp.where(iota_c == pos_ref[:, k:k + 1], gate_ref[:, k:k + 1], 0.0)
    w_hi = w.astype(BF16)
    w_lo = (w - w_hi.astype(F32)).astype(BF16)
    y = buf[slot].astype(BF16)
    o_ref[...] = _rms(x2_ref[...] + _dot(w_hi, y) + _dot(w_lo, y), gf_ref[...])


def _combine(pos, gates, x2, y, norm_final, tables, r_loc):
    t, d = x2.shape
    tb = ROUTE_ROWS
    grid_spec = pltpu.PrefetchScalarGridSpec(
        num_scalar_prefetch=4,
        grid=(t // tb,),
        in_specs=[pl.BlockSpec((tb, TOP_K), lambda b, *_: (b, 0)),
                  pl.BlockSpec((tb, TOP_K), lambda b, *_: (b, 0)),
                  pl.BlockSpec((tb, d), lambda b, *_: (b, 0)),
                  pl.BlockSpec((1, d), lambda b, *_: (0, 0)),
                  pl.BlockSpec(memory_space=pl.ANY)],
        out_specs=pl.BlockSpec((tb, d), lambda b, *_: (b, 0)),
        scratch_shapes=[pltpu.VMEM((3, r_loc, d), F32), pltpu.SemaphoreType.DMA((3,))],
    )
    return pl.pallas_call(
        _combine_kernel,
        grid_spec=grid_spec,
        out_shape=jax.ShapeDtypeStruct((t, d), F32),
        compiler_params=pltpu.CompilerParams(dimension_semantics=("arbitrary",),
                                             vmem_limit_bytes=VMEM_LIMIT),
        name="combine",
    )(*tables, pos, gates, x2, norm_final.reshape(1, d), y)


def _excl_cumsum(a, axis):
    n = a.shape[axis]
    a = jnp.moveaxis(a, axis, -1)
    earlier = jnp.arange(n)[None, :] < jnp.arange(n)[:, None]
    out = jnp.sum(jnp.where(earlier, a[..., None, :], 0), axis=-1)
    return jnp.moveaxis(out, -1, axis)


def _layout(cnt):
    nblk, ne = cnt.shape
    tb, bm = ROUTE_ROWS, EXPERT_ROWS
    n8 = (cnt + RUN_ALIGN - 1) // RUN_ALIGN * RUN_ALIGN
    loc = _excl_cumsum(n8, 1)
    tot = jnp.sum(n8, axis=1)
    size = jnp.sum(n8, axis=0)
    padded = (size + bm - 1) // bm * bm
    pstart = _excl_cumsum(padded, 0)
    pend = pstart + padded
    run = pstart[None, :] + _excl_cumsum(n8, 0)
    p_blocks = -(-(nblk * tb * TOP_K + nblk * ne * (RUN_ALIGN - 1) + ne * (bm - RUN_ALIGN)) // bm)
    n_used = (pend[-1] // bm).astype(I32)
    flat = lambda a: a.reshape(-1).astype(I32)
    tables = (flat(n8), flat(loc), flat(run), flat(tot))
    half = bm // 2
    tails = (flat(jnp.concatenate([padded - size, (p_blocks * bm - pend[-1:]) // half])),
             flat(jnp.concatenate([pstart + size, pend[-1:]])))
    r_loc = -(-(tb * TOP_K + ne * (RUN_ALIGN - 1)) // 256) * 256
    blocks = ((pstart // bm).astype(I32), (padded // bm).astype(I32), n_used.reshape(1))
    return tables, tails, loc.astype(I32), blocks, p_blocks * bm, r_loc


def kernel(x, mem, norm_mix, w_in, w_pool_group, pool_scale, w_pool_proj, conv_w, conv_b, lru_w_a, lru_b_a, lru_w_x, lru_b_x, lru_lambda, w_lru_proj, w_mix_out, norm_xattn, norm_mem, w_q, w_kv, w_o, norm_moe, w_router, b_router, w_gate_up, b_gate_up, w_down, b_down, norm_final):
    nb, s_len, d = x.shape
    m_len = mem.shape[1]
    assert norm_mix.shape[0] == 1, "single-layer stack"
    l = 0
    x1 = _mixer(x, norm_mix[l], w_in[l], w_pool_group[l], pool_scale[l], w_pool_proj[l],
                conv_w[l], conv_b[l], lru_w_a[l], lru_b_a[l], lru_w_x[l], lru_b_x[l],
                lru_lambda[l], w_lru_proj[l], w_mix_out[l])
    kv = _kv_proj(mem.reshape(nb * m_len, d), norm_mem[l], w_kv[l]).reshape(nb, m_len, 2 * d)
    x2, xn, idx_t, gate_t, lrank_t, cnt = _attention(
        x1.reshape(s_len, nb, d), kv, norm_xattn[l], w_q[l], w_o[l],
        norm_moe[l], w_router[l], b_router[l])
    tables, tails, loc, blocks, p_rows, r_loc = _layout(cnt.reshape(cnt.shape[0], -1))
    xs, pos_t = _dispatch(xn, idx_t, lrank_t, loc, tables + tails, p_rows, r_loc)
    y = _experts(xs, *blocks, w_gate_up[l], b_gate_up[l], w_down[l], b_down[l])
    out = _combine(pos_t.T, gate_t.T, x2.reshape(nb * s_len, d), y, norm_final, tables, r_loc)
    return out.reshape(nb, s_len, d)
```

```python
import functools

import jax
import jax.numpy as jnp
from jax import lax
from jax.experimental import pallas as pl
from jax.experimental.pallas import tpu as pltpu

POOL_WINDOWS = (2, 4, 8, 16)
N_GROUPS = 4
CONV_WIDTH = 4
RG_C = 8.0
N_EXPERTS = 32
TOP_K = 4
SWIGLU_LIMIT = 7.0
SWIGLU_ALPHA = 1.702
RMS_EPS = 1e-6

MIX_STEPS = 32
KV_ROWS = 512
ATT_ROWS = 512
ROUTE_ROWS = 256
ROUTE_UNROLL = 4
EXPERT_ROWS = 512
RUN_ALIGN = 8
VMEM_LIMIT = 52 * 1024 * 1024

BF16 = jnp.bfloat16
F32 = jnp.float32
I32 = jnp.int32


def _const_spec(shape):
    nd = len(shape)
    return pl.BlockSpec(shape, lambda *_: (0,) * nd, pipeline_mode=pl.Buffered(1))


def _rms(x, g):
    return x * lax.rsqrt(jnp.mean(x * x, axis=-1, keepdims=True) + RMS_EPS) * g


def _dot(a, b):
    return jnp.dot(a, b, preferred_element_type=F32)


def _dot_nt(a, b):
    return lax.dot_general(a, b, (((1,), (1,)), ((), ())), preferred_element_type=F32)


def _mixer_kernel(x_hbm, nm_ref, win_ref, wpg_ref, psc_ref, wpp_ref, cw_ref, cb_ref,
                  wa_ref, ba_ref, wx_ref, bx_ref, lam_ref, wlp_ref, wmo_ref,
                  o_ref,
                  xbuf, xsem,
                  h_ref, up_ref, ul_ref, a_ref, b_ref, pm_ref, m_ref, mb_ref, t_ref, gl_ref, gb_ref, hc_ref,
                  *, nb, ts):
    rows, d = o_ref.shape
    gw = d // N_GROUPS
    halo_p = (POOL_WINDOWS[-1]) * nb
    halo_c = (CONV_WIDTH - 1) * nb
    c = pl.program_id(0)

    @pl.when(c == 0)
    def _():
        up_ref[0:halo_p, :] = jnp.zeros((halo_p, d), F32)
        ul_ref[0:halo_c, :] = jnp.zeros((halo_c, d), F32)
        hc_ref[...] = jnp.zeros_like(hc_ref)

    def x_copy(step, t, slot):
        return pltpu.make_async_copy(x_hbm.at[:, step * ts + t, :], xbuf.at[slot, t], xsem.at[slot])

    def fetch(step, slot):
        lax.fori_loop(0, ts, lambda t, carry: (x_copy(step, t, slot).start(), carry)[1], 0)

    slot = c % 2

    @pl.when(c == 0)
    def _():
        fetch(0, 0)

    @pl.when(c + 1 < pl.num_programs(0))
    def _():
        fetch(c + 1, 1 - slot)

    lax.fori_loop(0, ts, lambda t, carry: (x_copy(c, t, slot).wait(), carry)[1], 0)

    h_ref[...] = _rms(xbuf[slot].reshape(rows, d), nm_ref[...]).astype(BF16)

    t_glob = c * ts + lax.broadcasted_iota(I32, (rows, 1), 0) // nb

    up_ref[halo_p:halo_p + rows, :] = _dot(h_ref[...], win_ref[:, 0:d])
    ul_ref[halo_c:halo_c + rows, :] = _dot(h_ref[...], win_ref[:, d:2 * d])

    for g, w in enumerate(POOL_WINDOWS):
        cols = slice(g * gw, (g + 1) * gw)
        u = up_ref[halo_p:halo_p + rows, cols]
        acc = u
        for j in range(1, w):
            acc = acc + up_ref[halo_p - j * nb:halo_p - j * nb + rows, cols]
        cnt = jnp.minimum(t_glob + 1, w).astype(F32)
        p = acc / cnt - u
        pg = _dot(p.astype(BF16), wpg_ref[g]) * psc_ref[:, cols]
        pm_ref[:, cols] = pg.astype(BF16)
    t_ref[...] = _dot(h_ref[...], win_ref[:, 3 * d:4 * d])
    m_ref[...] = _dot(pm_ref[...], wpp_ref[...])

    for g in range(N_GROUPS):
        cols = slice(g * gw, (g + 1) * gw)
        xr = cb_ref[:, cols]
        for k in range(CONV_WIDTH):
            off = halo_c - (CONV_WIDTH - 1 - k) * nb
            xr = xr + ul_ref[off:off + rows, cols] * cw_ref[k:k + 1, cols]
        xrb = xr.astype(BF16)
        r = jax.nn.sigmoid(_dot(xrb, wa_ref[g]) + ba_ref[:, cols])
        i = jax.nn.sigmoid(_dot(xrb, wx_ref[g]) + bx_ref[:, cols])
        gl_ref[:, cols] = _dot(h_ref[...], win_ref[:, 2 * d + g * gw:2 * d + (g + 1) * gw])
        gb_ref[:, cols] = _dot(h_ref[...], win_ref[:, 4 * d + g * gw:4 * d + (g + 1) * gw])
        lam = lam_ref[:, cols]
        log_sig = jnp.minimum(lam, 0.0) - jnp.log(1.0 + jnp.exp(-jnp.abs(lam)))
        a = jnp.exp((RG_C * r) * log_sig)
        mult = jnp.sqrt(jnp.maximum(1.0 - a * a, 0.0))
        mult = jnp.where(t_glob == 0, 1.0, mult)
        a_ref[:, cols] = a
        b_ref[:, cols] = mult * i * xr
    m_ref[...] = jax.nn.sigmoid(t_ref[...]) * m_ref[...]

    def scan_step(t, hprev):
        sl = pl.ds(pl.multiple_of(t * nb, nb), nb)
        hn = a_ref[sl, :] * hprev + b_ref[sl, :]
        b_ref[sl, :] = hn
        return hn

    hc_ref[...] = lax.fori_loop(0, ts, scan_step, hc_ref[...], unroll=4)

    half = rows // 2
    for hs in (slice(0, half), slice(half, rows)):
        pm_ref[hs, :] = (b_ref[hs, :] * jax.nn.gelu(gl_ref[hs, :], approximate=True)).astype(BF16)
    for hs in (slice(0, half), slice(half, rows)):
        yb = _dot(pm_ref[hs, :], wlp_ref[...])
        mb_ref[hs, :] = (m_ref[hs, :] + jax.nn.sigmoid(gb_ref[hs, :]) * yb).astype(BF16)
    for i, hs in enumerate((slice(0, half), slice(half, rows))):
        x_half = xbuf[slot, i * (ts // 2):(i + 1) * (ts // 2)].reshape(half, d)
        o_ref[hs, :] = x_half + _dot(mb_ref[hs, :], wmo_ref[...])

    up_ref[0:halo_p, :] = up_ref[rows:rows + halo_p, :]
    ul_ref[0:halo_c, :] = ul_ref[rows:rows + halo_c, :]


def _mixer(x, norm_mix, w_in, w_pool_group, pool_scale, w_pool_proj, conv_w, conv_b,
           lru_w_a, lru_b_a, lru_w_x, lru_b_x, lru_lambda, w_lru_proj, w_mix_out):
    nb, s_len, d = x.shape
    ts = MIX_STEPS
    rows = ts * nb
    n_rows = s_len * nb
    assert s_len % ts == 0 and ts % 2 == 0 and ts >= POOL_WINDOWS[-1] and nb % 8 == 0
    row2 = lambda v: v.reshape(1, -1)
    args = (x, row2(norm_mix), w_in.astype(BF16), w_pool_group.astype(BF16), row2(pool_scale),
            w_pool_proj.astype(BF16), conv_w, row2(conv_b), lru_w_a.astype(BF16), row2(lru_b_a),
            lru_w_x.astype(BF16), row2(lru_b_x), row2(lru_lambda), w_lru_proj.astype(BF16),
            w_mix_out.astype(BF16))
    in_specs = [pl.BlockSpec(memory_space=pl.ANY)] + [_const_spec(a.shape) for a in args[1:]]
    halo_p = POOL_WINDOWS[-1] * nb
    halo_c = (CONV_WIDTH - 1) * nb
    return pl.pallas_call(
        functools.partial(_mixer_kernel, nb=nb, ts=ts),
        grid=(n_rows // rows,),
        in_specs=in_specs,
        out_specs=pl.BlockSpec((rows, d), lambda c: (c, 0)),
        out_shape=jax.ShapeDtypeStruct((n_rows, d), F32),
        scratch_shapes=[
            pltpu.VMEM((2, ts, nb, d), F32),
            pltpu.SemaphoreType.DMA((2,)),
            pltpu.VMEM((rows, d), BF16),
            pltpu.VMEM((halo_p + rows, d), F32),
            pltpu.VMEM((halo_c + rows, d), F32),
            pltpu.VMEM((rows, d), F32),
            pltpu.VMEM((rows, d), F32),
            pltpu.VMEM((rows, d), BF16),
            pltpu.VMEM((rows, d), F32),
            pltpu.VMEM((rows, d), BF16),
            pltpu.VMEM((rows, d), F32),
            pltpu.VMEM((rows, d), F32),
            pltpu.VMEM((rows, d), F32),
            pltpu.VMEM((nb, d), F32),
        ],
        compiler_params=pltpu.CompilerParams(dimension_semantics=("arbitrary",),
                                             vmem_limit_bytes=VMEM_LIMIT),
        name="mixer",
    )(*args)


def _kv_kernel(m_ref, g_ref, w_ref, o_ref):
    o_ref[...] = _dot(_rms(m_ref[...], g_ref[...]).astype(BF16), w_ref[...]).astype(BF16)


def _kv_proj(mem2d, norm_mem, w_kv):
    n, d = mem2d.shape
    assert n % KV_ROWS == 0
    return pl.pallas_call(
        _kv_kernel,
        grid=(n // KV_ROWS,),
        in_specs=[pl.BlockSpec((KV_ROWS, d), lambda i: (i, 0)),
                  _const_spec((1, d)), _const_spec((d, 2 * d))],
        out_specs=pl.BlockSpec((KV_ROWS, 2 * d), lambda i: (i, 0)),
        out_shape=jax.ShapeDtypeStruct((n, 2 * d), BF16),
        compiler_params=pltpu.CompilerParams(dimension_semantics=("arbitrary",),
                                             vmem_limit_bytes=VMEM_LIMIT),
        name="kv_proj",
    )(mem2d, norm_mem.reshape(1, d), w_kv.astype(BF16))


def _attn_kernel(x_hbm, kv_ref, gx_ref, wq_ref, wo_ref, gm_ref, wr_ref, br_ref,
                 x2_ref, xn_ref, idx_ref, gate_ref, lrank_ref, cnt_ref,
                 xbuf, xsem, o_scr, xh_scr, xl_scr):
    tq, d = x2_ref.shape
    hd = d // N_GROUPS
    ne = br_ref.shape[0]
    tb = ROUTE_ROWS

    b, i = pl.program_id(0), pl.program_id(1)
    nq = pl.num_programs(1)
    step = b * nq + i
    slot = step % 2

    def x_copy(bb, ii, s):
        return pltpu.make_async_copy(x_hbm.at[pl.ds(ii * tq, tq), bb, :], xbuf.at[s], xsem.at[s])

    @pl.when(step == 0)
    def _():
        x_copy(0, 0, 0).start()

    @pl.when(step + 1 < pl.num_programs(0) * nq)
    def _():
        wrap = i + 1 == nq
        x_copy(jnp.where(wrap, b + 1, b), jnp.where(wrap, 0, i + 1), 1 - slot).start()

    x_copy(b, i, slot).wait()
    x = xbuf[slot]
    q = _dot(_rms(x, gx_ref[...]).astype(BF16), wq_ref[...]).astype(BF16)
    for h in range(N_GROUPS):
        k = kv_ref[:, h * hd:(h + 1) * hd]
        v = kv_ref[:, d + h * hd:d + (h + 1) * hd]
        s = _dot_nt(q[:, h * hd:(h + 1) * hd], k) * (hd ** -0.5)
        e = jnp.exp(s - jnp.max(s, axis=-1, keepdims=True))
        p = e / jnp.sum(e, axis=-1, keepdims=True)
        o_scr[:, h * hd:(h + 1) * hd] = _dot(p.astype(BF16), v).astype(BF16)
    x2 = x + _dot(o_scr[...], wo_ref[...])
    x2_ref[...] = x2
    xn = _rms(x2, gm_ref[...])
    xh = xn.astype(BF16)
    xn_ref[...] = xh
    xh_scr[...] = xh
    xl_scr[...] = (xn - xh.astype(F32)).astype(BF16)

    iota_f = lax.broadcasted_iota(I32, (ne, tb), 0).astype(F32)
    before = (lax.broadcasted_iota(I32, (tb, tb), 0) < lax.broadcasted_iota(I32, (tb, tb), 1)
              ).astype(BF16)
    for sb in range(tq // tb):
        rows = slice(sb * tb, (sb + 1) * tb)
        ph = _dot_nt(wr_ref[...], xh_scr[rows, :])
        pl_ = _dot_nt(wr_ref[0:ne, :], xl_scr[rows, :])
        work = ph[0:ne] + ph[ne:2 * ne] + pl_ + br_ref[...]
        vals, idxs, sels = [], [], []
        for _ in range(TOP_K):
            m = jnp.max(work, axis=0, keepdims=True)
            idx = jnp.min(jnp.where(work == m, iota_f, float(ne)), axis=0, keepdims=True)
            sel = iota_f == idx
            vals.append(m)
            idxs.append(idx.astype(I32))
            sels.append(sel)
            work = jnp.where(sel, -jnp.inf, work)
        ex = [jnp.exp(v - vals[0]) for v in vals]
        den = ex[0] + ex[1] + ex[2] + ex[3]
        onehot = jnp.zeros((ne, tb), F32)
        for sel in sels:
            onehot = onehot + sel.astype(F32)
        oh16 = onehot.astype(BF16)
        prefix = _dot(oh16, before)
        lr = [jnp.sum(jnp.where(sel, prefix, 0.0), axis=0, keepdims=True).astype(I32) for sel in sels]
        idx_ref[:, rows] = jnp.concatenate(idxs, axis=0)
        gate_ref[:, rows] = jnp.concatenate([e_ / den for e_ in ex], axis=0)
        lrank_ref[:, rows] = jnp.concatenate(lr, axis=0)
        cnt_ref[sb] = _dot_nt(jnp.ones((1, tb), BF16), oh16).astype(I32)


def _attention(x1, kv, norm_xattn, w_q, w_o, norm_moe, w_router, b_router):
    s_len, nb, d = x1.shape
    m = kv.shape[1]
    ne = w_router.shape[-1]
    tq, tb = ATT_ROWS, ROUTE_ROWS
    nq = s_len // tq
    t = nb * s_len
    assert s_len % tq == 0 and tq % tb == 0
    wr_hi = w_router.astype(BF16)
    wr_lo = (w_router - wr_hi.astype(F32)).astype(BF16)
    wr2t = jnp.concatenate([wr_hi, wr_lo], axis=1).T
    tok = lambda b, i: (0, b * nq + i)
    return pl.pallas_call(
        _attn_kernel,
        grid=(nb, nq),
        in_specs=[pl.BlockSpec(memory_space=pl.ANY),
                  pl.BlockSpec((None, m, 2 * d), lambda b, i: (b, 0, 0)),
                  _const_spec((1, d)), _const_spec((d, d)), _const_spec((d, d)),
                  _const_spec((1, d)), _const_spec((2 * ne, d)), _const_spec((ne, 1))],
        out_specs=[pl.BlockSpec((None, tq, d), lambda b, i: (b, i, 0)),
                   pl.BlockSpec((tq, d), lambda b, i: (b * nq + i, 0)),
                   pl.BlockSpec((TOP_K, tq), tok),
                   pl.BlockSpec((TOP_K, tq), tok),
                   pl.BlockSpec((TOP_K, tq), tok),
                   pl.BlockSpec((tq // tb, 1, ne), lambda b, i: (b * nq + i, 0, 0))],
        out_shape=[jax.ShapeDtypeStruct((nb, s_len, d), F32),
                   jax.ShapeDtypeStruct((t, d), BF16),
                   jax.ShapeDtypeStruct((TOP_K, t), I32),
                   jax.ShapeDtypeStruct((TOP_K, t), F32),
                   jax.ShapeDtypeStruct((TOP_K, t), I32),
                   jax.ShapeDtypeStruct((t // tb, 1, ne), I32)],
        scratch_shapes=[pltpu.VMEM((2, tq, d), F32), pltpu.SemaphoreType.DMA((2,)),
                        pltpu.VMEM((tq, d), BF16), pltpu.VMEM((tq, d), BF16), pltpu.VMEM((tq, d), BF16)],
        compiler_params=pltpu.CompilerParams(dimension_semantics=("arbitrary", "arbitrary"),
                                             vmem_limit_bytes=VMEM_LIMIT),
        name="attention",
    )(x1, kv, norm_xattn.reshape(1, d), w_q.astype(BF16), w_o.astype(BF16),
      norm_moe.reshape(1, d), wr2t, b_router.reshape(ne, 1))


def _pow2_chunks(limit):
    sizes = []
    c = RUN_ALIGN
    while c <= limit:
        sizes.append(c)
        c *= 2
    return sizes[::-1]


def _for_each_chunk(n, limit, fn):
    for size in _pow2_chunks(limit):
        @pl.when((n & size) != 0)
        def _(size=size):
            fn(pl.multiple_of(n & ~(2 * size - 1), RUN_ALIGN), size)


def _wait_rows(n, limit, src, dst, sem):
    _for_each_chunk(n, limit, lambda off, size: pltpu.make_async_copy(
        src.at[pl.ds(0, size)], dst.at[pl.ds(0, size)], sem).wait())


def _dispatch_kernel(n8_ref, loc_ref, run_ref, tot_ref, tailn_ref, tails_ref,
                     xn_ref, idx_ref, lrank_ref, locv_ref, xs_hbm, pos_ref, *scratch):
    *bufs, zbuf, sems, zsem = scratch
    g = pl.program_id(0)
    nbuf = len(bufs)
    r_loc = bufs[0].shape[0]
    tb = xn_ref.shape[0] // nbuf
    ne = tailn_ref.shape[0] - 1
    zrows = zbuf.shape[0]
    last_blk = pl.num_programs(0) * nbuf - 1

    def send(blk, s, live):
        for e in range(ne):
            n = jnp.where(live, n8_ref[blk * ne + e], 0)
            src0 = loc_ref[blk * ne + e]
            dst0 = run_ref[blk * ne + e]
            _for_each_chunk(n, tb, lambda off, size, src0=src0, dst0=dst0: pltpu.make_async_copy(
                bufs[s].at[pl.ds(pl.multiple_of(src0 + off, RUN_ALIGN), size)],
                xs_hbm.at[pl.ds(pl.multiple_of(dst0 + off, RUN_ALIGN), size)], sems.at[s]).start())

    def sent(blk, s, live):
        _wait_rows(jnp.where(live, tot_ref[blk], 0), r_loc, bufs[s], xs_hbm, sems.at[s])

    def zero_rest(i, carry):
        dst = pl.multiple_of(tails_ref[ne] + i * zrows, zrows)
        pltpu.make_async_copy(zbuf, xs_hbm.at[pl.ds(dst, zrows)], zsem).start()
        return carry

    def wait_rest(i, carry):
        pltpu.make_async_copy(zbuf, xs_hbm.at[pl.ds(0, zrows)], zsem).wait()
        return carry

    @pl.when(g == 0)
    def _():
        zbuf[...] = jnp.zeros_like(zbuf)
        for e in range(ne):
            _for_each_chunk(tailn_ref[e], zrows, lambda off, size, e=e: pltpu.make_async_copy(
                zbuf.at[pl.ds(0, size)],
                xs_hbm.at[pl.ds(pl.multiple_of(tails_ref[e] + off, RUN_ALIGN), size)], zsem).start())
        lax.fori_loop(0, tailn_ref[ne], zero_rest, 0)

    iota_e = lax.broadcasted_iota(I32, (ne, tb), 0)
    iota_r = lax.broadcasted_iota(I32, (r_loc, tb), 0)
    for u in range(nbuf):
        blk = g * nbuf + u
        cols = slice(u * tb, (u + 1) * tb)
        sent(jnp.maximum(blk - nbuf, 0), u, blk >= nbuf)
        send(jnp.maximum(blk - 1, 0), (u - 1) % nbuf, blk >= 1)

        loc_col = locv_ref[u].astype(F32)
        pos = []
        for k in range(TOP_K):
            run0 = jnp.sum(jnp.where(iota_e == idx_ref[k:k + 1, cols], loc_col, 0.0), axis=0, keepdims=True)
            pos.append(run0.astype(I32) + lrank_ref[k:k + 1, cols])
            pos_ref[k:k + 1, cols] = pos[k]

        hit = iota_r == pos[0]
        for k in range(1, TOP_K):
            hit = jnp.logical_or(hit, iota_r == pos[k])
        bufs[u][...] = _dot(hit.astype(BF16), xn_ref[cols, :])

    @pl.when(g == pl.num_programs(0) - 1)
    def _():
        send(last_blk, nbuf - 1, True)
        for s_ in range(nbuf):
            sent(last_blk - (nbuf - 1 - s_), s_, True)
        for e in range(ne):
            _wait_rows(tailn_ref[e], zrows, zbuf, xs_hbm, zsem)
        lax.fori_loop(0, tailn_ref[ne], wait_rest, 0)


def _dispatch(xn, idx_t, lrank_t, loc, tables, p_rows, r_loc):
    t, d = xn.shape
    tb, nu = ROUTE_ROWS, ROUTE_UNROLL
    nblk, ne = loc.shape
    assert nblk % nu == 0
    tok = pl.BlockSpec((TOP_K, nu * tb), lambda g, *_: (0, g))
    grid_spec = pltpu.PrefetchScalarGridSpec(
        num_scalar_prefetch=6,
        grid=(nblk // nu,),
        in_specs=[pl.BlockSpec((nu * tb, d), lambda g, *_: (g, 0)), tok, tok,
                  pl.BlockSpec((nu, ne, 1), lambda g, *_: (g, 0, 0))],
        out_specs=[pl.BlockSpec(memory_space=pl.ANY), tok],
        scratch_shapes=[pltpu.VMEM((r_loc, d), F32)] * nu + [
            pltpu.VMEM((EXPERT_ROWS // 2, d), F32), pltpu.SemaphoreType.DMA((nu,)), pltpu.SemaphoreType.DMA],
    )
    return pl.pallas_call(
        _dispatch_kernel,
        grid_spec=grid_spec,
        out_shape=[jax.ShapeDtypeStruct((p_rows, d), F32), jax.ShapeDtypeStruct((TOP_K, t), I32)],
        compiler_params=pltpu.CompilerParams(dimension_semantics=("arbitrary",),
                                             vmem_limit_bytes=VMEM_LIMIT),
        name="dispatch",
    )(*tables, xn, idx_t, lrank_t, loc.reshape(nblk, ne, 1))


def _expert_kernel(blk0_ref, nblk_ref, nu_ref, xs_hbm, wgu_ref, bgu_ref, wdn_ref, bdn_ref, y_hbm,
                   xbuf, ybuf, zbuf, in_sem, out_sem, zsem, wgu_bf, wdn_bf):
    e = pl.program_id(0)
    bm = xbuf.shape[1]
    zrows = zbuf.shape[0]
    de = wdn_ref.shape[0]
    n_used = nu_ref[0]

    def x_copy(g, slot):
        return pltpu.make_async_copy(xs_hbm.at[pl.ds(pl.multiple_of(g * bm, bm), bm)], xbuf.at[slot],
                                     in_sem.at[slot])

    def y_copy(g, slot):
        return pltpu.make_async_copy(ybuf.at[slot], y_hbm.at[pl.ds(pl.multiple_of(g * bm, bm), bm)],
                                     out_sem.at[slot])

    @pl.when(jnp.logical_and(e == 0, n_used > 0))
    def _():
        x_copy(0, 0).start()

    wgu_bf[...] = wgu_ref[...].astype(BF16)
    wdn_bf[...] = wdn_ref[...].astype(BF16)

    def block(j, carry):
        g = blk0_ref[e] + j
        slot = g % 2

        @pl.when(g + 1 < n_used)
        def _():
            x_copy(g + 1, 1 - slot).start()

        x_copy(g, slot).wait()

        @pl.when(g >= 2)
        def _():
            y_copy(g - 2, slot).wait()

        hgu = _dot(xbuf[slot].astype(BF16), wgu_bf[...]) + bgu_ref[...]
        gl = jnp.minimum(hgu[:, :de], SWIGLU_LIMIT)
        up = jnp.clip(hgu[:, de:], -SWIGLU_LIMIT, SWIGLU_LIMIT)
        act = (up + 1.0) * (gl * jax.nn.sigmoid(SWIGLU_ALPHA * gl))
        ybuf[slot] = _dot(act.astype(BF16), wdn_bf[...]) + bdn_ref[...]
        y_copy(g, slot).start()
        return carry

    lax.fori_loop(0, nblk_ref[e], block, 0)

    @pl.when(e == pl.num_programs(0) - 1)
    def _():
        for back in (2, 1):
            @pl.when(n_used >= back)
            def _(back=back):
                y_copy(n_used - back, (n_used - back) % 2).wait()
        n_zero = (y_hbm.shape[0] - n_used * bm) // zrows
        zbuf[...] = jnp.zeros_like(zbuf)

        def z_copy(i):
            return pltpu.make_async_copy(
                zbuf, y_hbm.at[pl.ds(pl.multiple_of(n_used * bm + i * zrows, zrows), zrows)], zsem)

        lax.fori_loop(0, n_zero, lambda i, c: (z_copy(i).start(), c)[1], 0)
        lax.fori_loop(0, n_zero, lambda i, c: (z_copy(i).wait(), c)[1], 0)


def _experts(xs, blk0, nblk, n_used, w_gate_up, b_gate_up, w_down, b_down):
    p, d = xs.shape
    ne, _, de2 = w_gate_up.shape
    de = de2 // 2
    bm = EXPERT_ROWS
    per_expert = lambda shape: pl.BlockSpec((None,) + shape, lambda e, *_: (e, 0, 0))
    grid_spec = pltpu.PrefetchScalarGridSpec(
        num_scalar_prefetch=3,
        grid=(ne,),
        in_specs=[pl.BlockSpec(memory_space=pl.ANY),
                  per_expert((d, de2)), per_expert((1, de2)), per_expert((de, d)), per_expert((1, d))],
        out_specs=pl.BlockSpec(memory_space=pl.ANY),
        scratch_shapes=[pltpu.VMEM((2, bm, d), F32), pltpu.VMEM((2, bm, d), F32),
                        pltpu.VMEM((bm // 4, d), F32),
                        pltpu.SemaphoreType.DMA((2,)), pltpu.SemaphoreType.DMA((2,)), pltpu.SemaphoreType.DMA,
                        pltpu.VMEM((d, de2), BF16), pltpu.VMEM((de, d), BF16)],
    )
    return pl.pallas_call(
        _expert_kernel,
        grid_spec=grid_spec,
        out_shape=jax.ShapeDtypeStruct((p, d), F32),
        compiler_params=pltpu.CompilerParams(dimension_semantics=("arbitrary",),
                                             vmem_limit_bytes=VMEM_LIMIT),
        name="experts",
    )(blk0, nblk, n_used, xs, w_gate_up, b_gate_up.reshape(ne, 1, de2), w_down, b_down.reshape(ne, 1, d))


def _combine_kernel(n8_ref, loc_ref, run_ref, tot_ref,
                    pos_ref, gate_ref, x2_ref, gf_ref, y_hbm, o_ref, *scratch):
    *bufs, sems = scratch
    g = pl.program_id(0)
    nbuf = len(bufs)
    r_loc = bufs[0].shape[0]
    tb = x2_ref.shape[0] // nbuf
    nblk = pl.num_programs(0) * nbuf
    ne = n8_ref.shape[0] // tot_ref.shape[0]

    def fetch(blk, s, live):
        for e in range(ne):
            n = jnp.where(live, n8_ref[blk * ne + e], 0)
            dst0 = loc_ref[blk * ne + e]
            src0 = run_ref[blk * ne + e]
            _for_each_chunk(n, tb, lambda off, size, src0=src0, dst0=dst0: pltpu.make_async_copy(
                y_hbm.at[pl.ds(pl.multiple_of(src0 + off, RUN_ALIGN), size)],
                bufs[s].at[pl.ds(pl.multiple_of(dst0 + off, RUN_ALIGN), size)], sems.at[s]).start())

    @pl.when(g == 0)
    def _():
        for buf in bufs:
            buf[...] = jnp.zeros_like(buf)
        for ahead in range(nbuf - 2):
            fetch(ahead, ahead, True)

    iota_c = lax.broadcasted_iota(I32, (tb, r_loc), 1)
    for u in range(nbuf):
        blk = g * nbuf + u
        rows = slice(u * tb, (u + 1) * tb)
        _wait_rows(tot_ref[blk], r_loc, y_hbm, bufs[u], sems.at[u])
        ahead = blk + nbuf - 2
        fetch(jnp.minimum(ahead, nblk - 1), (u - 2) % nbuf, ahead < nblk)

        w = jnp.zeros((tb, r_loc), F32)
        for k in range(TOP_K):
            w = w + jnp.where(iota_c == pos_ref[rows, k:k + 1], gate_ref[rows, k:k + 1], 0.0)
        w_hi = w.astype(BF16)
        w_lo = (w - w_hi.astype(F32)).astype(BF16)
        y = bufs[u][...].astype(BF16)
        o_ref[rows, :] = _rms(x2_ref[rows, :] + _dot(w_hi, y) + _dot(w_lo, y), gf_ref[...])


def _combine(pos, gates, x2, y, norm_final, tables, r_loc):
    t, d = x2.shape
    rows = ROUTE_ROWS * ROUTE_UNROLL
    assert t % rows == 0
    grid_spec = pltpu.PrefetchScalarGridSpec(
        num_scalar_prefetch=4,
        grid=(t // rows,),
        in_specs=[pl.BlockSpec((rows, TOP_K), lambda g, *_: (g, 0)),
                  pl.BlockSpec((rows, TOP_K), lambda g, *_: (g, 0)),
                  pl.BlockSpec((rows, d), lambda g, *_: (g, 0)),
                  pl.BlockSpec((1, d), lambda g, *_: (0, 0)),
                  pl.BlockSpec(memory_space=pl.ANY)],
        out_specs=pl.BlockSpec((rows, d), lambda g, *_: (g, 0)),
        scratch_shapes=[pltpu.VMEM((r_loc, d), F32)] * ROUTE_UNROLL + [pltpu.SemaphoreType.DMA((ROUTE_UNROLL,))],
    )
    return pl.pallas_call(
        _combine_kernel,
        grid_spec=grid_spec,
        out_shape=jax.ShapeDtypeStruct((t, d), F32),
        compiler_params=pltpu.CompilerParams(dimension_semantics=("arbitrary",),
                                             vmem_limit_bytes=VMEM_LIMIT),
        name="combine",
    )(*tables, pos, gates, x2, norm_final.reshape(1, d), y)


def _excl_cumsum(a, axis):
    n = a.shape[axis]
    a = jnp.moveaxis(a, axis, -1)
    earlier = jnp.arange(n)[None, :] < jnp.arange(n)[:, None]
    out = jnp.sum(jnp.where(earlier, a[..., None, :], 0), axis=-1)
    return jnp.moveaxis(out, -1, axis)


def _layout(cnt):
    nblk, ne = cnt.shape
    tb, bm = ROUTE_ROWS, EXPERT_ROWS
    n8 = (cnt + RUN_ALIGN - 1) // RUN_ALIGN * RUN_ALIGN
    loc = _excl_cumsum(n8, 1)
    tot = jnp.sum(n8, axis=1)
    size = jnp.sum(n8, axis=0)
    padded = (size + bm - 1) // bm * bm
    pstart = _excl_cumsum(padded, 0)
    pend = pstart + padded
    run = pstart[None, :] + _excl_cumsum(n8, 0)
    p_blocks = -(-(nblk * tb * TOP_K + nblk * ne * (RUN_ALIGN - 1) + ne * (bm - RUN_ALIGN)) // bm)
    n_used = (pend[-1] // bm).astype(I32)
    flat = lambda a: a.reshape(-1).astype(I32)
    tables = (flat(n8), flat(loc), flat(run), flat(tot))
    half = bm // 2
    tails = (flat(jnp.concatenate([padded - size, (p_blocks * bm - pend[-1:]) // half])),
             flat(jnp.concatenate([pstart + size, pend[-1:]])))
    r_loc = -(-(tb * TOP_K + ne * (RUN_ALIGN - 1)) // 256) * 256
    blocks = ((pstart // bm).astype(I32), (padded // bm).astype(I32), n_used.reshape(1))
    return tables, tails, loc.astype(I32), blocks, p_blocks * bm, r_loc


def kernel(x, mem, norm_mix, w_in, w_pool_group, pool_scale, w_pool_proj, conv_w, conv_b, lru_w_a, lru_b_a, lru_w_x, lru_b_x, lru_lambda, w_lru_proj, w_mix_out, norm_xattn, norm_mem, w_q, w_kv, w_o, norm_moe, w_router, b_router, w_gate_up, b_gate_up, w_down, b_down, norm_final):
    nb, s_len, d = x.shape
    m_len = mem.shape[1]
    assert norm_mix.shape[0] == 1, "single-layer stack"
    l = 0
    x1 = _mixer(x, norm_mix[l], w_in[l], w_pool_group[l], pool_scale[l], w_pool_proj[l],
                conv_w[l], conv_b[l], lru_w_a[l], lru_b_a[l], lru_w_x[l], lru_b_x[l],
                lru_lambda[l], w_lru_proj[l], w_mix_out[l])
    kv = _kv_proj(mem.reshape(nb * m_len, d), norm_mem[l], w_kv[l]).reshape(nb, m_len, 2 * d)
    x2, xn, idx_t, gate_t, lrank_t, cnt = _attention(
        x1.reshape(s_len, nb, d), kv, norm_xattn[l], w_q[l], w_o[l],
        norm_moe[l], w_router[l], b_router[l])
    tables, tails, loc, blocks, p_rows, r_loc = _layout(cnt.reshape(cnt.shape[0], -1))
    xs, pos_t = _dispatch(xn, idx_t, lrank_t, loc, tables + tails, p_rows, r_loc)
    y = _experts(xs, *blocks, w_gate_up[l], b_gate_up[l], w_down[l], b_down[l])
    out = _combine(pos_t.T, gate_t.T, x2.reshape(nb * s_len, d), y, norm_final, tables, r_loc)
    return out.reshape(nb, s_len, d)
```

```python
import functools

import jax
import jax.numpy as jnp
from jax import lax
from jax.experimental import pallas as pl
from jax.experimental.pallas import tpu as pltpu

POOL_WINDOWS = (2, 4, 8, 16)
N_GROUPS = 4
CONV_WIDTH = 4
RG_C = 8.0
N_EXPERTS = 32
TOP_K = 4
SWIGLU_LIMIT = 7.0
SWIGLU_ALPHA = 1.702
RMS_EPS = 1e-6

MIX_STEPS = 32
KV_ROWS = 512
ATT_ROWS = 512
ROUTE_ROWS = 256
ROUTE_UNROLL = 4
EXPERT_ROWS = 512
RUN_ALIGN = 8
VMEM_LIMIT = 52 * 1024 * 1024

BF16 = jnp.bfloat16
F32 = jnp.float32
I32 = jnp.int32


def _const_spec(shape):
    nd = len(shape)
    return pl.BlockSpec(shape, lambda *_: (0,) * nd, pipeline_mode=pl.Buffered(1))


def _rms(x, g):
    return x * lax.rsqrt(jnp.mean(x * x, axis=-1, keepdims=True) + RMS_EPS) * g


def _dot(a, b):
    return jnp.dot(a, b, preferred_element_type=F32)


def _dot_nt(a, b):
    return lax.dot_general(a, b, (((1,), (1,)), ((), ())), preferred_element_type=F32)


def _mixer_kernel(x_hbm, nm_ref, win_ref, wpg_ref, psc_ref, wpp_ref, cw_ref, cb_ref,
                  wa_ref, ba_ref, wx_ref, bx_ref, lam_ref, wlp_ref, wmo_ref,
                  o_ref,
                  xbuf, xsem,
                  h_ref, up_ref, ul_ref, a_ref, b_ref, pm_ref, m_ref, mb_ref, t_ref, gl_ref, gb_ref, hc_ref,
                  *, nb, ts):
    rows, d = o_ref.shape
    gw = d // N_GROUPS
    halo_p = (POOL_WINDOWS[-1]) * nb
    halo_c = (CONV_WIDTH - 1) * nb
    c = pl.program_id(0)

    @pl.when(c == 0)
    def _():
        up_ref[0:halo_p, :] = jnp.zeros((halo_p, d), F32)
        ul_ref[0:halo_c, :] = jnp.zeros((halo_c, d), F32)
        hc_ref[...] = jnp.zeros_like(hc_ref)

    def x_copy(step, t, slot):
        return pltpu.make_async_copy(x_hbm.at[:, step * ts + t, :], xbuf.at[slot, t], xsem.at[slot])

    def fetch(step, slot):
        lax.fori_loop(0, ts, lambda t, carry: (x_copy(step, t, slot).start(), carry)[1], 0)

    slot = c % 2

    @pl.when(c == 0)
    def _():
        fetch(0, 0)

    @pl.when(c + 1 < pl.num_programs(0))
    def _():
        fetch(c + 1, 1 - slot)

    lax.fori_loop(0, ts, lambda t, carry: (x_copy(c, t, slot).wait(), carry)[1], 0)

    h_ref[...] = _rms(xbuf[slot].reshape(rows, d), nm_ref[...]).astype(BF16)

    t_glob = c * ts + lax.broadcasted_iota(I32, (rows, 1), 0) // nb

    up_ref[halo_p:halo_p + rows, :] = _dot(h_ref[...], win_ref[:, 0:d])
    ul_ref[halo_c:halo_c + rows, :] = _dot(h_ref[...], win_ref[:, d:2 * d])

    for g, w in enumerate(POOL_WINDOWS):
        cols = slice(g * gw, (g + 1) * gw)
        u = up_ref[halo_p:halo_p + rows, cols]
        acc = u
        for j in range(1, w):
            acc = acc + up_ref[halo_p - j * nb:halo_p - j * nb + rows, cols]
        cnt = jnp.minimum(t_glob + 1, w).astype(F32)
        p = acc / cnt - u
        pg = _dot(p.astype(BF16), wpg_ref[g]) * psc_ref[:, cols]
        pm_ref[:, cols] = pg.astype(BF16)
    t_ref[...] = _dot(h_ref[...], win_ref[:, 3 * d:4 * d])
    m_ref[...] = _dot(pm_ref[...], wpp_ref[...])

    for g in range(N_GROUPS):
        cols = slice(g * gw, (g + 1) * gw)
        xr = cb_ref[:, cols]
        for k in range(CONV_WIDTH):
            off = halo_c - (CONV_WIDTH - 1 - k) * nb
            xr = xr + ul_ref[off:off + rows, cols] * cw_ref[k:k + 1, cols]
        xrb = xr.astype(BF16)
        r = jax.nn.sigmoid(_dot(xrb, wa_ref[g]) + ba_ref[:, cols])
        i = jax.nn.sigmoid(_dot(xrb, wx_ref[g]) + bx_ref[:, cols])
        gl_ref[:, cols] = _dot(h_ref[...], win_ref[:, 2 * d + g * gw:2 * d + (g + 1) * gw])
        gb_ref[:, cols] = _dot(h_ref[...], win_ref[:, 4 * d + g * gw:4 * d + (g + 1) * gw])
        lam = lam_ref[:, cols]
        log_sig = jnp.minimum(lam, 0.0) - jnp.log(1.0 + jnp.exp(-jnp.abs(lam)))
        a = jnp.exp((RG_C * r) * log_sig)
        mult = jnp.sqrt(jnp.maximum(1.0 - a * a, 0.0))
        mult = jnp.where(t_glob == 0, 1.0, mult)
        a_ref[:, cols] = a
        b_ref[:, cols] = mult * i * xr
    m_ref[...] = jax.nn.sigmoid(t_ref[...]) * m_ref[...]

    def scan_step(t, hprev):
        sl = pl.ds(pl.multiple_of(t * nb, nb), nb)
        hn = a_ref[sl, :] * hprev + b_ref[sl, :]
        b_ref[sl, :] = hn
        return hn

    hc_ref[...] = lax.fori_loop(0, ts, scan_step, hc_ref[...], unroll=4)

    half = rows // 2
    for hs in (slice(0, half), slice(half, rows)):
        pm_ref[hs, :] = (b_ref[hs, :] * jax.nn.gelu(gl_ref[hs, :], approximate=True)).astype(BF16)
    for hs in (slice(0, half), slice(half, rows)):
        yb = _dot(pm_ref[hs, :], wlp_ref[...])
        mb_ref[hs, :] = (m_ref[hs, :] + jax.nn.sigmoid(gb_ref[hs, :]) * yb).astype(BF16)
    for i, hs in enumerate((slice(0, half), slice(half, rows))):
        x_half = xbuf[slot, i * (ts // 2):(i + 1) * (ts // 2)].reshape(half, d)
        o_ref[hs, :] = x_half + _dot(mb_ref[hs, :], wmo_ref[...])

    up_ref[0:halo_p, :] = up_ref[rows:rows + halo_p, :]
    ul_ref[0:halo_c, :] = ul_ref[rows:rows + halo_c, :]


def _mixer(x, norm_mix, w_in, w_pool_group, pool_scale, w_pool_proj, conv_w, conv_b,
           lru_w_a, lru_b_a, lru_w_x, lru_b_x, lru_lambda, w_lru_proj, w_mix_out):
    nb, s_len, d = x.shape
    ts = MIX_STEPS
    rows = ts * nb
    n_rows = s_len * nb
    assert s_len % ts == 0 and ts % 2 == 0 and ts >= POOL_WINDOWS[-1] and nb % 8 == 0
    row2 = lambda v: v.reshape(1, -1)
    args = (x, row2(norm_mix), w_in.astype(BF16), w_pool_group.astype(BF16), row2(pool_scale),
            w_pool_proj.astype(BF16), conv_w, row2(conv_b), lru_w_a.astype(BF16), row2(lru_b_a),
            lru_w_x.astype(BF16), row2(lru_b_x), row2(lru_lambda), w_lru_proj.astype(BF16),
            w_mix_out.astype(BF16))
    in_specs = [pl.BlockSpec(memory_space=pl.ANY)] + [_const_spec(a.shape) for a in args[1:]]
    halo_p = POOL_WINDOWS[-1] * nb
    halo_c = (CONV_WIDTH - 1) * nb
    return pl.pallas_call(
        functools.partial(_mixer_kernel, nb=nb, ts=ts),
        grid=(n_rows // rows,),
        in_specs=in_specs,
        out_specs=pl.BlockSpec((rows, d), lambda c: (c, 0)),
        out_shape=jax.ShapeDtypeStruct((n_rows, d), F32),
        scratch_shapes=[
            pltpu.VMEM((2, ts, nb, d), F32),
            pltpu.SemaphoreType.DMA((2,)),
            pltpu.VMEM((rows, d), BF16),
            pltpu.VMEM((halo_p + rows, d), F32),
            pltpu.VMEM((halo_c + rows, d), F32),
            pltpu.VMEM((rows, d), F32),
            pltpu.VMEM((rows, d), F32),
            pltpu.VMEM((rows, d), BF16),
            pltpu.VMEM((rows, d), F32),
            pltpu.VMEM((rows, d), BF16),
            pltpu.VMEM((rows, d), F32),
            pltpu.VMEM((rows, d), F32),
            pltpu.VMEM((rows, d), F32),
            pltpu.VMEM((nb, d), F32),
        ],
        compiler_params=pltpu.CompilerParams(dimension_semantics=("arbitrary",),
                                             vmem_limit_bytes=VMEM_LIMIT),
        name="mixer",
    )(*args)


def _kv_kernel(m_ref, g_ref, w_ref, o_ref):
    o_ref[...] = _dot(_rms(m_ref[...], g_ref[...]).astype(BF16), w_ref[...]).astype(BF16)


def _kv_proj(mem2d, norm_mem, w_kv):
    n, d = mem2d.shape
    assert n % KV_ROWS == 0
    return pl.pallas_call(
        _kv_kernel,
        grid=(n // KV_ROWS,),
        in_specs=[pl.BlockSpec((KV_ROWS, d), lambda i: (i, 0)),
                  _const_spec((1, d)), _const_spec((d, 2 * d))],
        out_specs=pl.BlockSpec((KV_ROWS, 2 * d), lambda i: (i, 0)),
        out_shape=jax.ShapeDtypeStruct((n, 2 * d), BF16),
        compiler_params=pltpu.CompilerParams(dimension_semantics=("arbitrary",),
                                             vmem_limit_bytes=VMEM_LIMIT),
        name="kv_proj",
    )(mem2d, norm_mem.reshape(1, d), w_kv.astype(BF16))


def _attn_kernel(x_hbm, kv_ref, gx_ref, wq_ref, wo_ref, gm_ref, wr_ref, br_ref,
                 x2_ref, xn_ref, idx_ref, gate_ref, lrank_ref, cnt_ref,
                 xbuf, xsem, o_scr, xh_scr, xl_scr, *, nq):
    tq, d = x2_ref.shape
    hd = d // N_GROUPS
    ne = br_ref.shape[0]
    tb = ROUTE_ROWS
    n = pl.program_id(0)
    n_blocks = pl.num_programs(0) - 1
    cur = jnp.minimum(n, n_blocks - 1)
    slot = cur % 2
    prev = (n + 1) % 2

    def x_copy(blk, s):
        return pltpu.make_async_copy(x_hbm.at[pl.ds((blk % nq) * tq, tq), blk // nq, :], xbuf.at[s], xsem.at[s])

    @pl.when(n == 0)
    def _():
        x_copy(0, 0).start()
        xh_scr[...] = jnp.zeros_like(xh_scr)
        xl_scr[...] = jnp.zeros_like(xl_scr)

    @pl.when(n + 1 < n_blocks)
    def _():
        x_copy(n + 1, 1 - slot).start()

    @pl.when(n < n_blocks)
    def _():
        x_copy(n, slot).wait()

    x = xbuf[slot]

    works = []
    for sb in range(tq // tb):
        rows = slice(sb * tb, (sb + 1) * tb)
        ph = _dot_nt(wr_ref[...], xh_scr[prev, rows, :])
        pl_ = _dot_nt(wr_ref[0:ne, :], xl_scr[prev, rows, :])
        works.append(ph[0:ne] + ph[ne:2 * ne] + pl_ + br_ref[...])

    q = _dot(_rms(x, gx_ref[...]).astype(BF16), wq_ref[...]).astype(BF16)

    iota_f = lax.broadcasted_iota(I32, (ne, tb), 0).astype(F32)
    picked = []
    for work in works:
        vals, idxs, sels = [], [], []
        for _ in range(TOP_K):
            m = jnp.max(work, axis=0, keepdims=True)
            idx = jnp.min(jnp.where(work == m, iota_f, float(ne)), axis=0, keepdims=True)
            sel = iota_f == idx
            vals.append(m)
            idxs.append(idx.astype(I32))
            sels.append(sel)
            work = jnp.where(sel, -jnp.inf, work)
        onehot = jnp.zeros((ne, tb), F32)
        for sel in sels:
            onehot = onehot + sel.astype(F32)
        picked.append((vals, idxs, sels, onehot.astype(BF16)))

    scores = [_dot_nt(q[:, h * hd:(h + 1) * hd], kv_ref[:, h * hd:(h + 1) * hd]) * (hd ** -0.5)
              for h in range(N_GROUPS)]
    for h, s in enumerate(scores):
        v = kv_ref[:, d + h * hd:d + (h + 1) * hd]
        e = jnp.exp(s - jnp.max(s, axis=-1, keepdims=True))
        p = e / jnp.sum(e, axis=-1, keepdims=True)
        o_scr[:, h * hd:(h + 1) * hd] = _dot(p.astype(BF16), v).astype(BF16)

    before = (lax.broadcasted_iota(I32, (tb, tb), 0) < lax.broadcasted_iota(I32, (tb, tb), 1)
              ).astype(BF16)
    for sb, (vals, idxs, sels, oh16) in enumerate(picked):
        rows = slice(sb * tb, (sb + 1) * tb)
        ex = [jnp.exp(v - vals[0]) for v in vals]
        den = ex[0] + ex[1] + ex[2] + ex[3]
        prefix = _dot(oh16, before)
        lr = [jnp.sum(jnp.where(sel, prefix, 0.0), axis=0, keepdims=True).astype(I32) for sel in sels]
        idx_ref[:, rows] = jnp.concatenate(idxs, axis=0)
        gate_ref[:, rows] = jnp.concatenate([e_ / den for e_ in ex], axis=0)
        lrank_ref[:, rows] = jnp.concatenate(lr, axis=0)
        cnt_ref[sb] = _dot_nt(jnp.ones((1, tb), BF16), oh16).astype(I32)

    x2 = x + _dot(o_scr[...], wo_ref[...])
    x2_ref[...] = x2
    xn = _rms(x2, gm_ref[...])
    xh = xn.astype(BF16)
    xn_ref[...] = xh
    xh_scr[n % 2] = xh
    xl_scr[n % 2] = (xn - xh.astype(F32)).astype(BF16)


def _attention(x1, kv, norm_xattn, w_q, w_o, norm_moe, w_router, b_router):
    s_len, nb, d = x1.shape
    m = kv.shape[1]
    ne = w_router.shape[-1]
    tq, tb = ATT_ROWS, ROUTE_ROWS
    nq = s_len // tq
    n_blocks = nb * nq
    t = nb * s_len
    assert s_len % tq == 0 and tq % tb == 0
    wr_hi = w_router.astype(BF16)
    wr_lo = (w_router - wr_hi.astype(F32)).astype(BF16)
    wr2t = jnp.concatenate([wr_hi, wr_lo], axis=1).T
    cur = lambda n: jnp.minimum(n, n_blocks - 1)
    routed = lambda n: (0, jnp.maximum(n - 1, 0))
    return pl.pallas_call(
        functools.partial(_attn_kernel, nq=nq),
        grid=(n_blocks + 1,),
        in_specs=[pl.BlockSpec(memory_space=pl.ANY),
                  pl.BlockSpec((None, m, 2 * d), lambda n: (cur(n) // nq, 0, 0)),
                  _const_spec((1, d)), _const_spec((d, d)), _const_spec((d, d)),
                  _const_spec((1, d)), _const_spec((2 * ne, d)), _const_spec((ne, 1))],
        out_specs=[pl.BlockSpec((None, tq, d), lambda n: (cur(n) // nq, cur(n) % nq, 0)),
                   pl.BlockSpec((tq, d), lambda n: (cur(n), 0)),
                   pl.BlockSpec((TOP_K, tq), routed),
                   pl.BlockSpec((TOP_K, tq), routed),
                   pl.BlockSpec((TOP_K, tq), routed),
                   pl.BlockSpec((tq // tb, 1, ne), lambda n: (jnp.maximum(n - 1, 0), 0, 0))],
        out_shape=[jax.ShapeDtypeStruct((nb, s_len, d), F32),
                   jax.ShapeDtypeStruct((t, d), BF16),
                   jax.ShapeDtypeStruct((TOP_K, t), I32),
                   jax.ShapeDtypeStruct((TOP_K, t), F32),
                   jax.ShapeDtypeStruct((TOP_K, t), I32),
                   jax.ShapeDtypeStruct((t // tb, 1, ne), I32)],
        scratch_shapes=[pltpu.VMEM((2, tq, d), F32), pltpu.SemaphoreType.DMA((2,)),
                        pltpu.VMEM((tq, d), BF16), pltpu.VMEM((2, tq, d), BF16), pltpu.VMEM((2, tq, d), BF16)],
        compiler_params=pltpu.CompilerParams(dimension_semantics=("arbitrary",),
                                             vmem_limit_bytes=VMEM_LIMIT),
        name="attention",
    )(x1, kv, norm_xattn.reshape(1, d), w_q.astype(BF16), w_o.astype(BF16),
      norm_moe.reshape(1, d), wr2t, b_router.reshape(ne, 1))


def _pow2_chunks(limit):
    sizes = []
    c = RUN_ALIGN
    while c <= limit:
        sizes.append(c)
        c *= 2
    return sizes[::-1]


def _for_each_chunk(n, limit, fn):
    for size in _pow2_chunks(limit):
        @pl.when((n & size) != 0)
        def _(size=size):
            fn(pl.multiple_of(n & ~(2 * size - 1), RUN_ALIGN), size)


def _wait_rows(n, limit, src, dst, sem):
    _for_each_chunk(n, limit, lambda off, size: pltpu.make_async_copy(
        src.at[pl.ds(0, size)], dst.at[pl.ds(0, size)], sem).wait())


def _dispatch_kernel(n8_ref, loc_ref, run_ref, tot_ref, tailn_ref, tails_ref,
                     xn_ref, idx_ref, lrank_ref, locv_ref, xs_hbm, pos_ref, *scratch):
    *bufs, zbuf, sems, zsem = scratch
    g = pl.program_id(0)
    nbuf = len(bufs)
    r_loc = bufs[0].shape[0]
    tb = xn_ref.shape[0] // nbuf
    ne = tailn_ref.shape[0] - 1
    zrows = zbuf.shape[0]
    last_blk = pl.num_programs(0) * nbuf - 1

    def send(blk, s, live):
        for e in range(ne):
            n = jnp.where(live, n8_ref[blk * ne + e], 0)
            src0 = loc_ref[blk * ne + e]
            dst0 = run_ref[blk * ne + e]
            _for_each_chunk(n, tb, lambda off, size, src0=src0, dst0=dst0: pltpu.make_async_copy(
                bufs[s].at[pl.ds(pl.multiple_of(src0 + off, RUN_ALIGN), size)],
                xs_hbm.at[pl.ds(pl.multiple_of(dst0 + off, RUN_ALIGN), size)], sems.at[s]).start())

    def sent(blk, s, live):
        _wait_rows(jnp.where(live, tot_ref[blk], 0), r_loc, bufs[s], xs_hbm, sems.at[s])

    def zero_rest(i, carry):
        dst = pl.multiple_of(tails_ref[ne] + i * zrows, zrows)
        pltpu.make_async_copy(zbuf, xs_hbm.at[pl.ds(dst, zrows)], zsem).start()
        return carry

    def wait_rest(i, carry):
        pltpu.make_async_copy(zbuf, xs_hbm.at[pl.ds(0, zrows)], zsem).wait()
        return carry

    @pl.when(g == 0)
    def _():
        zbuf[...] = jnp.zeros_like(zbuf)
        for e in range(ne):
            _for_each_chunk(tailn_ref[e], zrows, lambda off, size, e=e: pltpu.make_async_copy(
                zbuf.at[pl.ds(0, size)],
                xs_hbm.at[pl.ds(pl.multiple_of(tails_ref[e] + off, RUN_ALIGN), size)], zsem).start())
        lax.fori_loop(0, tailn_ref[ne], zero_rest, 0)

    iota_e = lax.broadcasted_iota(I32, (ne, tb), 0)
    iota_r = lax.broadcasted_iota(I32, (r_loc, tb), 0)
    for u in range(nbuf):
        blk = g * nbuf + u
        cols = slice(u * tb, (u + 1) * tb)
        sent(jnp.maximum(blk - nbuf, 0), u, blk >= nbuf)
        send(jnp.maximum(blk - 1, 0), (u - 1) % nbuf, blk >= 1)

        loc_col = locv_ref[u].astype(F32)
        pos = []
        for k in range(TOP_K):
            run0 = jnp.sum(jnp.where(iota_e == idx_ref[k:k + 1, cols], loc_col, 0.0), axis=0, keepdims=True)
            pos.append(run0.astype(I32) + lrank_ref[k:k + 1, cols])
            pos_ref[k:k + 1, cols] = pos[k]

        hit = iota_r == pos[0]
        for k in range(1, TOP_K):
            hit = jnp.logical_or(hit, iota_r == pos[k])
        bufs[u][...] = _dot(hit.astype(BF16), xn_ref[cols, :])

    @pl.when(g == pl.num_programs(0) - 1)
    def _():
        send(last_blk, nbuf - 1, True)
        for s_ in range(nbuf):
            sent(last_blk - (nbuf - 1 - s_), s_, True)
        for e in range(ne):
            _wait_rows(tailn_ref[e], zrows, zbuf, xs_hbm, zsem)
        lax.fori_loop(0, tailn_ref[ne], wait_rest, 0)


def _dispatch(xn, idx_t, lrank_t, loc, tables, p_rows, r_loc):
    t, d = xn.shape
    tb, nu = ROUTE_ROWS, ROUTE_UNROLL
    nblk, ne = loc.shape
    assert nblk % nu == 0
    tok = pl.BlockSpec((TOP_K, nu * tb), lambda g, *_: (0, g))
    grid_spec = pltpu.PrefetchScalarGridSpec(
        num_scalar_prefetch=6,
        grid=(nblk // nu,),
        in_specs=[pl.BlockSpec((nu * tb, d), lambda g, *_: (g, 0)), tok, tok,
                  pl.BlockSpec((nu, ne, 1), lambda g, *_: (g, 0, 0))],
        out_specs=[pl.BlockSpec(memory_space=pl.ANY), tok],
        scratch_shapes=[pltpu.VMEM((r_loc, d), F32)] * nu + [
            pltpu.VMEM((EXPERT_ROWS // 2, d), F32), pltpu.SemaphoreType.DMA((nu,)), pltpu.SemaphoreType.DMA],
    )
    return pl.pallas_call(
        _dispatch_kernel,
        grid_spec=grid_spec,
        out_shape=[jax.ShapeDtypeStruct((p_rows, d), F32), jax.ShapeDtypeStruct((TOP_K, t), I32)],
        compiler_params=pltpu.CompilerParams(dimension_semantics=("arbitrary",),
                                             vmem_limit_bytes=VMEM_LIMIT),
        name="dispatch",
    )(*tables, xn, idx_t, lrank_t, loc.reshape(nblk, ne, 1))


def _expert_kernel(blk0_ref, nblk_ref, nu_ref, xs_hbm, wgu_ref, bgu_ref, wdn_ref, bdn_ref, y_hbm,
                   xbuf, ybuf, zbuf, in_sem, out_sem, zsem, wgu_bf, wdn_bf):
    e = pl.program_id(0)
    bm = xbuf.shape[1]
    zrows = zbuf.shape[0]
    de = wdn_ref.shape[0]
    n_used = nu_ref[0]

    def x_copy(g, slot):
        return pltpu.make_async_copy(xs_hbm.at[pl.ds(pl.multiple_of(g * bm, bm), bm)], xbuf.at[slot],
                                     in_sem.at[slot])

    def y_copy(g, slot):
        return pltpu.make_async_copy(ybuf.at[slot], y_hbm.at[pl.ds(pl.multiple_of(g * bm, bm), bm)],
                                     out_sem.at[slot])

    @pl.when(jnp.logical_and(e == 0, n_used > 0))
    def _():
        x_copy(0, 0).start()

    wgu_bf[...] = wgu_ref[...].astype(BF16)
    wdn_bf[...] = wdn_ref[...].astype(BF16)

    def block(j, carry):
        g = blk0_ref[e] + j
        slot = g % 2

        @pl.when(g + 1 < n_used)
        def _():
            x_copy(g + 1, 1 - slot).start()

        x_copy(g, slot).wait()

        @pl.when(g >= 2)
        def _():
            y_copy(g - 2, slot).wait()

        hgu = _dot(xbuf[slot].astype(BF16), wgu_bf[...]) + bgu_ref[...]
        gl = jnp.minimum(hgu[:, :de], SWIGLU_LIMIT)
        up = jnp.clip(hgu[:, de:], -SWIGLU_LIMIT, SWIGLU_LIMIT)
        act = (up + 1.0) * (gl * jax.nn.sigmoid(SWIGLU_ALPHA * gl))
        ybuf[slot] = _dot(act.astype(BF16), wdn_bf[...]) + bdn_ref[...]
        y_copy(g, slot).start()
        return carry

    lax.fori_loop(0, nblk_ref[e], block, 0)

    @pl.when(e == pl.num_programs(0) - 1)
    def _():
        for back in (2, 1):
            @pl.when(n_used >= back)
            def _(back=back):
                y_copy(n_used - back, (n_used - back) % 2).wait()
        n_zero = (y_hbm.shape[0] - n_used * bm) // zrows
        zbuf[...] = jnp.zeros_like(zbuf)

        def z_copy(i):
            return pltpu.make_async_copy(
                zbuf, y_hbm.at[pl.ds(pl.multiple_of(n_used * bm + i * zrows, zrows), zrows)], zsem)

        lax.fori_loop(0, n_zero, lambda i, c: (z_copy(i).start(), c)[1], 0)
        lax.fori_loop(0, n_zero, lambda i, c: (z_copy(i).wait(), c)[1], 0)


def _experts(xs, blk0, nblk, n_used, w_gate_up, b_gate_up, w_down, b_down):
    p, d = xs.shape
    ne, _, de2 = w_gate_up.shape
    de = de2 // 2
    bm = EXPERT_ROWS
    per_expert = lambda shape: pl.BlockSpec((None,) + shape, lambda e, *_: (e, 0, 0))
    grid_spec = pltpu.PrefetchScalarGridSpec(
        num_scalar_prefetch=3,
        grid=(ne,),
        in_specs=[pl.BlockSpec(memory_space=pl.ANY),
                  per_expert((d, de2)), per_expert((1, de2)), per_expert((de, d)), per_expert((1, d))],
        out_specs=pl.BlockSpec(memory_space=pl.ANY),
        scratch_shapes=[pltpu.VMEM((2, bm, d), F32), pltpu.VMEM((2, bm, d), F32),
                        pltpu.VMEM((bm // 4, d), F32),
                        pltpu.SemaphoreType.DMA((2,)), pltpu.SemaphoreType.DMA((2,)), pltpu.SemaphoreType.DMA,
                        pltpu.VMEM((d, de2), BF16), pltpu.VMEM((de, d), BF16)],
    )
    return pl.pallas_call(
        _expert_kernel,
        grid_spec=grid_spec,
        out_shape=jax.ShapeDtypeStruct((p, d), F32),
        compiler_params=pltpu.CompilerParams(dimension_semantics=("arbitrary",),
                                             vmem_limit_bytes=VMEM_LIMIT),
        name="experts",
    )(blk0, nblk, n_used, xs, w_gate_up, b_gate_up.reshape(ne, 1, de2), w_down, b_down.reshape(ne, 1, d))


def _combine_kernel(n8_ref, loc_ref, run_ref, tot_ref,
                    pos_ref, gate_ref, x2_ref, gf_ref, y_hbm, o_ref, *scratch):
    *bufs, sems = scratch
    g = pl.program_id(0)
    nbuf = len(bufs)
    r_loc = bufs[0].shape[0]
    tb = x2_ref.shape[0] // nbuf
    nblk = pl.num_programs(0) * nbuf
    ne = n8_ref.shape[0] // tot_ref.shape[0]

    def fetch(blk, s, live):
        for e in range(ne):
            n = jnp.where(live, n8_ref[blk * ne + e], 0)
            dst0 = loc_ref[blk * ne + e]
            src0 = run_ref[blk * ne + e]
            _for_each_chunk(n, tb, lambda off, size, src0=src0, dst0=dst0: pltpu.make_async_copy(
                y_hbm.at[pl.ds(pl.multiple_of(src0 + off, RUN_ALIGN), size)],
                bufs[s].at[pl.ds(pl.multiple_of(dst0 + off, RUN_ALIGN), size)], sems.at[s]).start())

    @pl.when(g == 0)
    def _():
        for buf in bufs:
            buf[...] = jnp.zeros_like(buf)
        for ahead in range(nbuf - 2):
            fetch(ahead, ahead, True)

    iota_c = lax.broadcasted_iota(I32, (tb, r_loc), 1)
    for u in range(nbuf):
        blk = g * nbuf + u
        rows = slice(u * tb, (u + 1) * tb)
        _wait_rows(tot_ref[blk], r_loc, y_hbm, bufs[u], sems.at[u])
        ahead = blk + nbuf - 2
        fetch(jnp.minimum(ahead, nblk - 1), (u - 2) % nbuf, ahead < nblk)

        w = jnp.zeros((tb, r_loc), F32)
        for k in range(TOP_K):
            w = w + jnp.where(iota_c == pos_ref[rows, k:k + 1], gate_ref[rows, k:k + 1], 0.0)
        w_hi = w.astype(BF16)
        w_lo = (w - w_hi.astype(F32)).astype(BF16)
        y = bufs[u][...].astype(BF16)
        o_ref[rows, :] = _rms(x2_ref[rows, :] + _dot(w_hi, y) + _dot(w_lo, y), gf_ref[...])


def _combine(pos, gates, x2, y, norm_final, tables, r_loc):
    t, d = x2.shape
    rows = ROUTE_ROWS * ROUTE_UNROLL
    assert t % rows == 0
    grid_spec = pltpu.PrefetchScalarGridSpec(
        num_scalar_prefetch=4,
        grid=(t // rows,),
        in_specs=[pl.BlockSpec((rows, TOP_K), lambda g, *_: (g, 0)),
                  pl.BlockSpec((rows, TOP_K), lambda g, *_: (g, 0)),
                  pl.BlockSpec((rows, d), lambda g, *_: (g, 0)),
                  pl.BlockSpec((1, d), lambda g, *_: (0, 0)),
                  pl.BlockSpec(memory_space=pl.ANY)],
        out_specs=pl.BlockSpec((rows, d), lambda g, *_: (g, 0)),
        scratch_shapes=[pltpu.VMEM((r_loc, d), F32)] * ROUTE_UNROLL + [pltpu.SemaphoreType.DMA((ROUTE_UNROLL,))],
    )
    return pl.pallas_call(
        _combine_kernel,
        grid_spec=grid_spec,
        out_shape=jax.ShapeDtypeStruct((t, d), F32),
        compiler_params=pltpu.CompilerParams(dimension_semantics=("arbitrary",),
                                             vmem_limit_bytes=VMEM_LIMIT),
        name="combine",
    )(*tables, pos, gates, x2, norm_final.reshape(1, d), y)


def _excl_cumsum(a, axis):
    n = a.shape[axis]
    a = jnp.moveaxis(a, axis, -1)
    earlier = jnp.arange(n)[None, :] < jnp.arange(n)[:, None]
    out = jnp.sum(jnp.where(earlier, a[..., None, :], 0), axis=-1)
    return jnp.moveaxis(out, -1, axis)


def _layout(cnt):
    nblk, ne = cnt.shape
    tb, bm = ROUTE_ROWS, EXPERT_ROWS
    n8 = (cnt + RUN_ALIGN - 1) // RUN_ALIGN * RUN_ALIGN
    loc = _excl_cumsum(n8, 1)
    tot = jnp.sum(n8, axis=1)
    size = jnp.sum(n8, axis=0)
    padded = (size + bm - 1) // bm * bm
    pstart = _excl_cumsum(padded, 0)
    pend = pstart + padded
    run = pstart[None, :] + _excl_cumsum(n8, 0)
    p_blocks = -(-(nblk * tb * TOP_K + nblk * ne * (RUN_ALIGN - 1) + ne * (bm - RUN_ALIGN)) // bm)
    n_used = (pend[-1] // bm).astype(I32)
    flat = lambda a: a.reshape(-1).astype(I32)
    tables = (flat(n8), flat(loc), flat(run), flat(tot))
    half = bm // 2
    tails = (flat(jnp.concatenate([padded - size, (p_blocks * bm - pend[-1:]) // half])),
             flat(jnp.concatenate([pstart + size, pend[-1:]])))
    r_loc = -(-(tb * TOP_K + ne * (RUN_ALIGN - 1)) // 256) * 256
    blocks = ((pstart // bm).astype(I32), (padded // bm).astype(I32), n_used.reshape(1))
    return tables, tails, loc.astype(I32), blocks, p_blocks * bm, r_loc


def kernel(x, mem, norm_mix, w_in, w_pool_group, pool_scale, w_pool_proj, conv_w, conv_b, lru_w_a, lru_b_a, lru_w_x, lru_b_x, lru_lambda, w_lru_proj, w_mix_out, norm_xattn, norm_mem, w_q, w_kv, w_o, norm_moe, w_router, b_router, w_gate_up, b_gate_up, w_down, b_down, norm_final):
    nb, s_len, d = x.shape
    m_len = mem.shape[1]
    assert norm_mix.shape[0] == 1, "single-layer stack"
    l = 0
    x1 = _mixer(x, norm_mix[l], w_in[l], w_pool_group[l], pool_scale[l], w_pool_proj[l],
                conv_w[l], conv_b[l], lru_w_a[l], lru_b_a[l], lru_w_x[l], lru_b_x[l],
                lru_lambda[l], w_lru_proj[l], w_mix_out[l])
    kv = _kv_proj(mem.reshape(nb * m_len, d), norm_mem[l], w_kv[l]).reshape(nb, m_len, 2 * d)
    x2, xn, idx_t, gate_t, lrank_t, cnt = _attention(
        x1.reshape(s_len, nb, d), kv, norm_xattn[l], w_q[l], w_o[l],
        norm_moe[l], w_router[l], b_router[l])
    tables, tails, loc, blocks, p_rows, r_loc = _layout(cnt.reshape(cnt.shape[0], -1))
    xs, pos_t = _dispatch(xn, idx_t, lrank_t, loc, tables + tails, p_rows, r_loc)
    y = _experts(xs, *blocks, w_gate_up[l], b_gate_up[l], w_down[l], b_down[l])
    out = _combine(pos_t.T, gate_t.T, x2.reshape(nb * s_len, d), y, norm_final, tables, r_loc)
    return out.reshape(nb, s_len, d)
```

```python
import functools

import jax
import jax.numpy as jnp
from jax import lax
from jax.experimental import pallas as pl
from jax.experimental.pallas import tpu as pltpu

POOL_WINDOWS = (2, 4, 8, 16)
N_GROUPS = 4
CONV_WIDTH = 4
RG_C = 8.0
N_EXPERTS = 32
TOP_K = 4
SWIGLU_LIMIT = 7.0
SWIGLU_ALPHA = 1.702
RMS_EPS = 1e-6

MIX_STEPS = 32
KV_ROWS = 512
ATT_ROWS = 512
ROUTE_ROWS = 256
ROUTE_UNROLL = 4
EXPERT_ROWS = 512
RUN_ALIGN = 8
VMEM_LIMIT = 52 * 1024 * 1024

BF16 = jnp.bfloat16
F32 = jnp.float32
I32 = jnp.int32
U32 = jnp.uint32


def _const_spec(shape):
    nd = len(shape)
    return pl.BlockSpec(shape, lambda *_: (0,) * nd, pipeline_mode=pl.Buffered(1))


def _rms(x, g):
    return x * lax.rsqrt(jnp.mean(x * x, axis=-1, keepdims=True) + RMS_EPS) * g


def _dot(a, b):
    return jnp.dot(a, b, preferred_element_type=F32)


def _dot_nt(a, b):
    return lax.dot_general(a, b, (((1,), (1,)), ((), ())), preferred_element_type=F32)


def _pack_halves(x):
    c = x.shape[1] // 2
    lo = lax.bitcast_convert_type(x[:, :c], U32)
    hi = lax.bitcast_convert_type(x[:, c:], U32)
    return (hi & jnp.uint32(0xFFFF0000)) | (lo >> 16)


def _unpack_halves(u):
    lo = lax.bitcast_convert_type(u << 16, F32).astype(BF16)
    hi = lax.bitcast_convert_type(u & jnp.uint32(0xFFFF0000), F32).astype(BF16)
    return lo, hi


def _mixer_kernel(x_hbm, nm_ref, win_ref, wpg_ref, psc_ref, wpp_ref, cw_ref, cb_ref,
                  wa_ref, ba_ref, wx_ref, bx_ref, lam_ref, wlp_ref, wmo_ref,
                  o_ref,
                  xbuf, xsem,
                  h_ref, up_ref, ul_ref, a_ref, b_ref, pm_ref, m_ref, mb_ref, t_ref, gl_ref, gb_ref, hc_ref,
                  *, nb, ts):
    rows, d = o_ref.shape
    gw = d // N_GROUPS
    halo_p = (POOL_WINDOWS[-1]) * nb
    halo_c = (CONV_WIDTH - 1) * nb
    c = pl.program_id(0)

    @pl.when(c == 0)
    def _():
        up_ref[0:halo_p, :] = jnp.zeros((halo_p, d), F32)
        ul_ref[0:halo_c, :] = jnp.zeros((halo_c, d), F32)
        hc_ref[...] = jnp.zeros_like(hc_ref)

    def x_copy(step, t, slot):
        return pltpu.make_async_copy(x_hbm.at[:, step * ts + t, :], xbuf.at[slot, t], xsem.at[slot])

    def fetch(step, slot):
        lax.fori_loop(0, ts, lambda t, carry: (x_copy(step, t, slot).start(), carry)[1], 0)

    slot = c % 2

    @pl.when(c == 0)
    def _():
        fetch(0, 0)

    @pl.when(c + 1 < pl.num_programs(0))
    def _():
        fetch(c + 1, 1 - slot)

    lax.fori_loop(0, ts, lambda t, carry: (x_copy(c, t, slot).wait(), carry)[1], 0)

    h_ref[...] = _rms(xbuf[slot].reshape(rows, d), nm_ref[...]).astype(BF16)

    t_glob = c * ts + lax.broadcasted_iota(I32, (rows, 1), 0) // nb

    up_ref[halo_p:halo_p + rows, :] = _dot(h_ref[...], win_ref[:, 0:d])
    ul_ref[halo_c:halo_c + rows, :] = _dot(h_ref[...], win_ref[:, d:2 * d])

    for g, w in enumerate(POOL_WINDOWS):
        cols = slice(g * gw, (g + 1) * gw)
        u = up_ref[halo_p:halo_p + rows, cols]
        acc = u
        for j in range(1, w):
            acc = acc + up_ref[halo_p - j * nb:halo_p - j * nb + rows, cols]
        cnt = jnp.minimum(t_glob + 1, w).astype(F32)
        p = acc / cnt - u
        pg = _dot(p.astype(BF16), wpg_ref[g]) * psc_ref[:, cols]
        pm_ref[:, cols] = pg.astype(BF16)
    t_ref[...] = _dot(h_ref[...], win_ref[:, 3 * d:4 * d])
    m_ref[...] = _dot(pm_ref[...], wpp_ref[...])

    for g in range(N_GROUPS):
        cols = slice(g * gw, (g + 1) * gw)
        xr = cb_ref[:, cols]
        for k in range(CONV_WIDTH):
            off = halo_c - (CONV_WIDTH - 1 - k) * nb
            xr = xr + ul_ref[off:off + rows, cols] * cw_ref[k:k + 1, cols]
        xrb = xr.astype(BF16)
        r = jax.nn.sigmoid(_dot(xrb, wa_ref[g]) + ba_ref[:, cols])
        i = jax.nn.sigmoid(_dot(xrb, wx_ref[g]) + bx_ref[:, cols])
        gl_ref[:, cols] = _dot(h_ref[...], win_ref[:, 2 * d + g * gw:2 * d + (g + 1) * gw])
        gb_ref[:, cols] = _dot(h_ref[...], win_ref[:, 4 * d + g * gw:4 * d + (g + 1) * gw])
        lam = lam_ref[:, cols]
        log_sig = jnp.minimum(lam, 0.0) - jnp.log(1.0 + jnp.exp(-jnp.abs(lam)))
        a = jnp.exp((RG_C * r) * log_sig)
        mult = jnp.sqrt(jnp.maximum(1.0 - a * a, 0.0))
        mult = jnp.where(t_glob == 0, 1.0, mult)
        a_ref[:, cols] = a
        b_ref[:, cols] = mult * i * xr
    m_ref[...] = jax.nn.sigmoid(t_ref[...]) * m_ref[...]

    def scan_step(t, hprev):
        sl = pl.ds(pl.multiple_of(t * nb, nb), nb)
        hn = a_ref[sl, :] * hprev + b_ref[sl, :]
        b_ref[sl, :] = hn
        return hn

    hc_ref[...] = lax.fori_loop(0, ts, scan_step, hc_ref[...], unroll=4)

    half = rows // 2
    for hs in (slice(0, half), slice(half, rows)):
        pm_ref[hs, :] = (b_ref[hs, :] * jax.nn.gelu(gl_ref[hs, :], approximate=True)).astype(BF16)
    for hs in (slice(0, half), slice(half, rows)):
        yb = _dot(pm_ref[hs, :], wlp_ref[...])
        mb_ref[hs, :] = (m_ref[hs, :] + jax.nn.sigmoid(gb_ref[hs, :]) * yb).astype(BF16)
    for i, hs in enumerate((slice(0, half), slice(half, rows))):
        x_half = xbuf[slot, i * (ts // 2):(i + 1) * (ts // 2)].reshape(half, d)
        o_ref[hs, :] = x_half + _dot(mb_ref[hs, :], wmo_ref[...])

    up_ref[0:halo_p, :] = up_ref[rows:rows + halo_p, :]
    ul_ref[0:halo_c, :] = ul_ref[rows:rows + halo_c, :]


def _mixer(x, norm_mix, w_in, w_pool_group, pool_scale, w_pool_proj, conv_w, conv_b,
           lru_w_a, lru_b_a, lru_w_x, lru_b_x, lru_lambda, w_lru_proj, w_mix_out):
    nb, s_len, d = x.shape
    ts = MIX_STEPS
    rows = ts * nb
    n_rows = s_len * nb
    assert s_len % ts == 0 and ts % 2 == 0 and ts >= POOL_WINDOWS[-1] and nb % 8 == 0
    row2 = lambda v: v.reshape(1, -1)
    args = (x, row2(norm_mix), w_in.astype(BF16), w_pool_group.astype(BF16), row2(pool_scale),
            w_pool_proj.astype(BF16), conv_w, row2(conv_b), lru_w_a.astype(BF16), row2(lru_b_a),
            lru_w_x.astype(BF16), row2(lru_b_x), row2(lru_lambda), w_lru_proj.astype(BF16),
            w_mix_out.astype(BF16))
    in_specs = [pl.BlockSpec(memory_space=pl.ANY)] + [_const_spec(a.shape) for a in args[1:]]
    halo_p = POOL_WINDOWS[-1] * nb
    halo_c = (CONV_WIDTH - 1) * nb
    return pl.pallas_call(
        functools.partial(_mixer_kernel, nb=nb, ts=ts),
        grid=(n_rows // rows,),
        in_specs=in_specs,
        out_specs=pl.BlockSpec((rows, d), lambda c: (c, 0)),
        out_shape=jax.ShapeDtypeStruct((n_rows, d), F32),
        scratch_shapes=[
            pltpu.VMEM((2, ts, nb, d), F32),
            pltpu.SemaphoreType.DMA((2,)),
            pltpu.VMEM((rows, d), BF16),
            pltpu.VMEM((halo_p + rows, d), F32),
            pltpu.VMEM((halo_c + rows, d), F32),
            pltpu.VMEM((rows, d), F32),
            pltpu.VMEM((rows, d), F32),
            pltpu.VMEM((rows, d), BF16),
            pltpu.VMEM((rows, d), F32),
            pltpu.VMEM((rows, d), BF16),
            pltpu.VMEM((rows, d), F32),
            pltpu.VMEM((rows, d), F32),
            pltpu.VMEM((rows, d), F32),
            pltpu.VMEM((nb, d), F32),
        ],
        compiler_params=pltpu.CompilerParams(dimension_semantics=("arbitrary",),
                                             vmem_limit_bytes=VMEM_LIMIT),
        name="mixer",
    )(*args)


def _kv_kernel(m_ref, g_ref, w_ref, o_ref):
    o_ref[...] = _dot(_rms(m_ref[...], g_ref[...]).astype(BF16), w_ref[...]).astype(BF16)


def _kv_proj(mem2d, norm_mem, w_kv):
    n, d = mem2d.shape
    assert n % KV_ROWS == 0
    return pl.pallas_call(
        _kv_kernel,
        grid=(n // KV_ROWS,),
        in_specs=[pl.BlockSpec((KV_ROWS, d), lambda i: (i, 0)),
                  _const_spec((1, d)), _const_spec((d, 2 * d))],
        out_specs=pl.BlockSpec((KV_ROWS, 2 * d), lambda i: (i, 0)),
        out_shape=jax.ShapeDtypeStruct((n, 2 * d), BF16),
        compiler_params=pltpu.CompilerParams(dimension_semantics=("arbitrary",),
                                             vmem_limit_bytes=VMEM_LIMIT),
        name="kv_proj",
    )(mem2d, norm_mem.reshape(1, d), w_kv.astype(BF16))


def _attn_kernel(x_hbm, kv_ref, gx_ref, wq_ref, wo_ref, gm_ref, wr_ref, br_ref,
                 x2_ref, xn_ref, idx_ref, gate_ref, lrank_ref, cnt_ref,
                 xbuf, xsem, o_scr, xh_scr, xl_scr, *, nq):
    tq, d = x2_ref.shape
    hd = d // N_GROUPS
    ne = br_ref.shape[0]
    tb = ROUTE_ROWS
    n = pl.program_id(0)
    n_blocks = pl.num_programs(0) - 1
    cur = jnp.minimum(n, n_blocks - 1)
    slot = cur % 2
    prev = (n + 1) % 2

    def x_copy(blk, s):
        return pltpu.make_async_copy(x_hbm.at[pl.ds((blk % nq) * tq, tq), blk // nq, :], xbuf.at[s], xsem.at[s])

    @pl.when(n == 0)
    def _():
        x_copy(0, 0).start()
        xh_scr[...] = jnp.zeros_like(xh_scr)
        xl_scr[...] = jnp.zeros_like(xl_scr)

    @pl.when(n + 1 < n_blocks)
    def _():
        x_copy(n + 1, 1 - slot).start()

    @pl.when(n < n_blocks)
    def _():
        x_copy(n, slot).wait()

    x = xbuf[slot]

    works = []
    for sb in range(tq // tb):
        rows = slice(sb * tb, (sb + 1) * tb)
        ph = _dot_nt(wr_ref[...], xh_scr[prev, rows, :])
        pl_ = _dot_nt(wr_ref[0:ne, :], xl_scr[prev, rows, :])
        works.append(ph[0:ne] + ph[ne:2 * ne] + pl_ + br_ref[...])

    q = _dot(_rms(x, gx_ref[...]).astype(BF16), wq_ref[...]).astype(BF16)

    iota_f = lax.broadcasted_iota(I32, (ne, tb), 0).astype(F32)
    picked = []
    for work in works:
        vals, idxs, sels = [], [], []
        for _ in range(TOP_K):
            m = jnp.max(work, axis=0, keepdims=True)
            idx = jnp.min(jnp.where(work == m, iota_f, float(ne)), axis=0, keepdims=True)
            sel = iota_f == idx
            vals.append(m)
            idxs.append(idx.astype(I32))
            sels.append(sel)
            work = jnp.where(sel, -jnp.inf, work)
        onehot = jnp.zeros((ne, tb), F32)
        for sel in sels:
            onehot = onehot + sel.astype(F32)
        picked.append((vals, idxs, sels, onehot.astype(BF16)))

    scores = [_dot_nt(q[:, h * hd:(h + 1) * hd], kv_ref[:, h * hd:(h + 1) * hd]) * (hd ** -0.5)
              for h in range(N_GROUPS)]
    for h, s in enumerate(scores):
        v = kv_ref[:, d + h * hd:d + (h + 1) * hd]
        e = jnp.exp(s - jnp.max(s, axis=-1, keepdims=True))
        p = e / jnp.sum(e, axis=-1, keepdims=True)
        o_scr[:, h * hd:(h + 1) * hd] = _dot(p.astype(BF16), v).astype(BF16)

    before = (lax.broadcasted_iota(I32, (tb, tb), 0) < lax.broadcasted_iota(I32, (tb, tb), 1)
              ).astype(BF16)
    for sb, (vals, idxs, sels, oh16) in enumerate(picked):
        rows = slice(sb * tb, (sb + 1) * tb)
        ex = [jnp.exp(v - vals[0]) for v in vals]
        den = ex[0] + ex[1] + ex[2] + ex[3]
        prefix = _dot(oh16, before)
        lr = [jnp.sum(jnp.where(sel, prefix, 0.0), axis=0, keepdims=True).astype(I32) for sel in sels]
        idx_ref[:, rows] = jnp.concatenate(idxs, axis=0)
        gate_ref[:, rows] = jnp.concatenate([e_ / den for e_ in ex], axis=0)
        lrank_ref[:, rows] = jnp.concatenate(lr, axis=0)
        cnt_ref[sb] = _dot_nt(jnp.ones((1, tb), BF16), oh16).astype(I32)

    x2 = x + _dot(o_scr[...], wo_ref[...])
    x2_ref[...] = x2
    xn = _rms(x2, gm_ref[...])
    xh = xn.astype(BF16)
    xn_ref[...] = xh
    xh_scr[n % 2] = xh
    xl_scr[n % 2] = (xn - xh.astype(F32)).astype(BF16)


def _attention(x1, kv, norm_xattn, w_q, w_o, norm_moe, w_router, b_router):
    s_len, nb, d = x1.shape
    m = kv.shape[1]
    ne = w_router.shape[-1]
    tq, tb = ATT_ROWS, ROUTE_ROWS
    nq = s_len // tq
    n_blocks = nb * nq
    t = nb * s_len
    assert s_len % tq == 0 and tq % tb == 0
    wr_hi = w_router.astype(BF16)
    wr_lo = (w_router - wr_hi.astype(F32)).astype(BF16)
    wr2t = jnp.concatenate([wr_hi, wr_lo], axis=1).T
    cur = lambda n: jnp.minimum(n, n_blocks - 1)
    routed = lambda n: (0, jnp.maximum(n - 1, 0))
    return pl.pallas_call(
        functools.partial(_attn_kernel, nq=nq),
        grid=(n_blocks + 1,),
        in_specs=[pl.BlockSpec(memory_space=pl.ANY),
                  pl.BlockSpec((None, m, 2 * d), lambda n: (cur(n) // nq, 0, 0)),
                  _const_spec((1, d)), _const_spec((d, d)), _const_spec((d, d)),
                  _const_spec((1, d)), _const_spec((2 * ne, d)), _const_spec((ne, 1))],
        out_specs=[pl.BlockSpec((None, tq, d), lambda n: (cur(n) // nq, cur(n) % nq, 0)),
                   pl.BlockSpec((tq, d), lambda n: (cur(n), 0)),
                   pl.BlockSpec((TOP_K, tq), routed),
                   pl.BlockSpec((TOP_K, tq), routed),
                   pl.BlockSpec((TOP_K, tq), routed),
                   pl.BlockSpec((tq // tb, 1, ne), lambda n: (jnp.maximum(n - 1, 0), 0, 0))],
        out_shape=[jax.ShapeDtypeStruct((nb, s_len, d), F32),
                   jax.ShapeDtypeStruct((t, d), BF16),
                   jax.ShapeDtypeStruct((TOP_K, t), I32),
                   jax.ShapeDtypeStruct((TOP_K, t), F32),
                   jax.ShapeDtypeStruct((TOP_K, t), I32),
                   jax.ShapeDtypeStruct((t // tb, 1, ne), I32)],
        scratch_shapes=[pltpu.VMEM((2, tq, d), F32), pltpu.SemaphoreType.DMA((2,)),
                        pltpu.VMEM((tq, d), BF16), pltpu.VMEM((2, tq, d), BF16), pltpu.VMEM((2, tq, d), BF16)],
        compiler_params=pltpu.CompilerParams(dimension_semantics=("arbitrary",),
                                             vmem_limit_bytes=VMEM_LIMIT),
        name="attention",
    )(x1, kv, norm_xattn.reshape(1, d), w_q.astype(BF16), w_o.astype(BF16),
      norm_moe.reshape(1, d), wr2t, b_router.reshape(ne, 1))


def _pow2_chunks(limit):
    sizes = []
    c = RUN_ALIGN
    while c <= limit:
        sizes.append(c)
        c *= 2
    return sizes[::-1]


def _for_each_chunk(n, limit, fn):
    for size in _pow2_chunks(limit):
        @pl.when((n & size) != 0)
        def _(size=size):
            fn(pl.multiple_of(n & ~(2 * size - 1), RUN_ALIGN), size)


def _wait_rows(n, limit, src, dst, sem):
    _for_each_chunk(n, limit, lambda off, size: pltpu.make_async_copy(
        src.at[pl.ds(0, size)], dst.at[pl.ds(0, size)], sem).wait())


def _dispatch_kernel(n8_ref, loc_ref, run_ref, tot_ref, tailn_ref, tails_ref,
                     xn_ref, idx_ref, lrank_ref, locv_ref, xs_hbm, pos_ref, *scratch):
    *bufs, zbuf, sems, zsem = scratch
    g = pl.program_id(0)
    nbuf = len(bufs)
    r_loc = bufs[0].shape[0]
    tb = xn_ref.shape[0] // nbuf
    ne = tailn_ref.shape[0] - 1
    zrows = zbuf.shape[0]
    last_blk = pl.num_programs(0) * nbuf - 1

    def send(blk, s, live):
        for e in range(ne):
            n = jnp.where(live, n8_ref[blk * ne + e], 0)
            src0 = loc_ref[blk * ne + e]
            dst0 = run_ref[blk * ne + e]
            _for_each_chunk(n, tb, lambda off, size, src0=src0, dst0=dst0: pltpu.make_async_copy(
                bufs[s].at[pl.ds(pl.multiple_of(src0 + off, RUN_ALIGN), size)],
                xs_hbm.at[pl.ds(pl.multiple_of(dst0 + off, RUN_ALIGN), size)], sems.at[s]).start())

    def sent(blk, s, live):
        _wait_rows(jnp.where(live, tot_ref[blk], 0), r_loc, bufs[s], xs_hbm, sems.at[s])

    def zero_rest(i, carry):
        dst = pl.multiple_of(tails_ref[ne] + i * zrows, zrows)
        pltpu.make_async_copy(zbuf, xs_hbm.at[pl.ds(dst, zrows)], zsem).start()
        return carry

    def wait_rest(i, carry):
        pltpu.make_async_copy(zbuf, xs_hbm.at[pl.ds(0, zrows)], zsem).wait()
        return carry

    @pl.when(g == 0)
    def _():
        zbuf[...] = jnp.zeros_like(zbuf)
        for e in range(ne):
            _for_each_chunk(tailn_ref[e], zrows, lambda off, size, e=e: pltpu.make_async_copy(
                zbuf.at[pl.ds(0, size)],
                xs_hbm.at[pl.ds(pl.multiple_of(tails_ref[e] + off, RUN_ALIGN), size)], zsem).start())
        lax.fori_loop(0, tailn_ref[ne], zero_rest, 0)

    iota_e = lax.broadcasted_iota(I32, (ne, tb), 0)
    iota_r = lax.broadcasted_iota(I32, (r_loc, tb), 0)
    for u in range(nbuf):
        blk = g * nbuf + u
        cols = slice(u * tb, (u + 1) * tb)
        sent(jnp.maximum(blk - nbuf, 0), u, blk >= nbuf)
        send(jnp.maximum(blk - 1, 0), (u - 1) % nbuf, blk >= 1)

        loc_col = locv_ref[u].astype(F32)
        pos = []
        for k in range(TOP_K):
            run0 = jnp.sum(jnp.where(iota_e == idx_ref[k:k + 1, cols], loc_col, 0.0), axis=0, keepdims=True)
            pos.append(run0.astype(I32) + lrank_ref[k:k + 1, cols])
            pos_ref[k:k + 1, cols] = pos[k]

        hit = iota_r == pos[0]
        for k in range(1, TOP_K):
            hit = jnp.logical_or(hit, iota_r == pos[k])
        bufs[u][...] = _pack_halves(_dot(hit.astype(BF16), xn_ref[cols, :]))

    @pl.when(g == pl.num_programs(0) - 1)
    def _():
        send(last_blk, nbuf - 1, True)
        for s_ in range(nbuf):
            sent(last_blk - (nbuf - 1 - s_), s_, True)
        for e in range(ne):
            _wait_rows(tailn_ref[e], zrows, zbuf, xs_hbm, zsem)
        lax.fori_loop(0, tailn_ref[ne], wait_rest, 0)


def _dispatch(xn, idx_t, lrank_t, loc, tables, p_rows, r_loc):
    t, d = xn.shape
    tb, nu = ROUTE_ROWS, ROUTE_UNROLL
    nblk, ne = loc.shape
    assert nblk % nu == 0
    tok = pl.BlockSpec((TOP_K, nu * tb), lambda g, *_: (0, g))
    grid_spec = pltpu.PrefetchScalarGridSpec(
        num_scalar_prefetch=6,
        grid=(nblk // nu,),
        in_specs=[pl.BlockSpec((nu * tb, d), lambda g, *_: (g, 0)), tok, tok,
                  pl.BlockSpec((nu, ne, 1), lambda g, *_: (g, 0, 0))],
        out_specs=[pl.BlockSpec(memory_space=pl.ANY), tok],
        scratch_shapes=[pltpu.VMEM((r_loc, d // 2), U32)] * nu + [
            pltpu.VMEM((EXPERT_ROWS // 2, d // 2), U32), pltpu.SemaphoreType.DMA((nu,)), pltpu.SemaphoreType.DMA],
    )
    return pl.pallas_call(
        _dispatch_kernel,
        grid_spec=grid_spec,
        out_shape=[jax.ShapeDtypeStruct((p_rows, d // 2), U32), jax.ShapeDtypeStruct((TOP_K, t), I32)],
        compiler_params=pltpu.CompilerParams(dimension_semantics=("arbitrary",),
                                             vmem_limit_bytes=VMEM_LIMIT),
        name="dispatch",
    )(*tables, xn, idx_t, lrank_t, loc.reshape(nblk, ne, 1))


def _expert_kernel(blk0_ref, nblk_ref, nu_ref, xs_hbm, wgu_ref, bgu_ref, wdn_ref, bdn_ref, y_hbm,
                   xbuf, ybuf, zbuf, in_sem, out_sem, zsem, wgu_bf, wdn_bf):
    e = pl.program_id(0)
    bm, dh = xbuf.shape[1:]
    zrows = zbuf.shape[0]
    de = wdn_ref.shape[0]
    n_used = nu_ref[0]

    def x_copy(g, slot):
        return pltpu.make_async_copy(xs_hbm.at[pl.ds(pl.multiple_of(g * bm, bm), bm)], xbuf.at[slot],
                                     in_sem.at[slot])

    def y_copy(g, slot):
        return pltpu.make_async_copy(ybuf.at[slot], y_hbm.at[pl.ds(pl.multiple_of(g * bm, bm), bm)],
                                     out_sem.at[slot])

    @pl.when(jnp.logical_and(e == 0, n_used > 0))
    def _():
        x_copy(0, 0).start()

    wgu_bf[...] = wgu_ref[...].astype(BF16)
    wdn_bf[...] = wdn_ref[...].astype(BF16)

    def block(j, carry):
        g = blk0_ref[e] + j
        slot = g % 2

        @pl.when(g + 1 < n_used)
        def _():
            x_copy(g + 1, 1 - slot).start()

        x_copy(g, slot).wait()

        @pl.when(g >= 2)
        def _():
            y_copy(g - 2, slot).wait()

        x_lo, x_hi = _unpack_halves(xbuf[slot])
        hgu = _dot(x_lo, wgu_bf[0:dh, :]) + _dot(x_hi, wgu_bf[dh:2 * dh, :]) + bgu_ref[...]
        gl = jnp.minimum(hgu[:, :de], SWIGLU_LIMIT)
        up = jnp.clip(hgu[:, de:], -SWIGLU_LIMIT, SWIGLU_LIMIT)
        act = (up + 1.0) * (gl * jax.nn.sigmoid(SWIGLU_ALPHA * gl))
        y = _dot(act.astype(BF16), wdn_bf[...]) + bdn_ref[...]
        ybuf[slot] = _pack_halves(y.astype(BF16).astype(F32))
        y_copy(g, slot).start()
        return carry

    lax.fori_loop(0, nblk_ref[e], block, 0)

    @pl.when(e == pl.num_programs(0) - 1)
    def _():
        for back in (2, 1):
            @pl.when(n_used >= back)
            def _(back=back):
                y_copy(n_used - back, (n_used - back) % 2).wait()
        n_zero = (y_hbm.shape[0] - n_used * bm) // zrows
        zbuf[...] = jnp.zeros_like(zbuf)

        def z_copy(i):
            return pltpu.make_async_copy(
                zbuf, y_hbm.at[pl.ds(pl.multiple_of(n_used * bm + i * zrows, zrows), zrows)], zsem)

        lax.fori_loop(0, n_zero, lambda i, c: (z_copy(i).start(), c)[1], 0)
        lax.fori_loop(0, n_zero, lambda i, c: (z_copy(i).wait(), c)[1], 0)


def _experts(xs, blk0, nblk, n_used, w_gate_up, b_gate_up, w_down, b_down):
    p, dh = xs.shape
    ne, d, de2 = w_gate_up.shape
    de = de2 // 2
    bm = EXPERT_ROWS
    per_expert = lambda shape: pl.BlockSpec((None,) + shape, lambda e, *_: (e, 0, 0))
    grid_spec = pltpu.PrefetchScalarGridSpec(
        num_scalar_prefetch=3,
        grid=(ne,),
        in_specs=[pl.BlockSpec(memory_space=pl.ANY),
                  per_expert((d, de2)), per_expert((1, de2)), per_expert((de, d)), per_expert((1, d))],
        out_specs=pl.BlockSpec(memory_space=pl.ANY),
        scratch_shapes=[pltpu.VMEM((2, bm, dh), U32), pltpu.VMEM((2, bm, dh), U32),
                        pltpu.VMEM((bm // 4, dh), U32),
                        pltpu.SemaphoreType.DMA((2,)), pltpu.SemaphoreType.DMA((2,)), pltpu.SemaphoreType.DMA,
                        pltpu.VMEM((d, de2), BF16), pltpu.VMEM((de, d), BF16)],
    )
    return pl.pallas_call(
        _expert_kernel,
        grid_spec=grid_spec,
        out_shape=jax.ShapeDtypeStruct((p, dh), U32),
        compiler_params=pltpu.CompilerParams(dimension_semantics=("arbitrary",),
                                             vmem_limit_bytes=VMEM_LIMIT),
        name="experts",
    )(blk0, nblk, n_used, xs, w_gate_up, b_gate_up.reshape(ne, 1, de2), w_down, b_down.reshape(ne, 1, d))


def _combine_kernel(n8_ref, loc_ref, run_ref, tot_ref,
                    pos_ref, gate_ref, x2_ref, gf_ref, y_hbm, o_ref, *scratch):
    *bufs, sems = scratch
    g = pl.program_id(0)
    nbuf = len(bufs)
    r_loc = bufs[0].shape[0]
    tb = x2_ref.shape[0] // nbuf
    nblk = pl.num_programs(0) * nbuf
    ne = n8_ref.shape[0] // tot_ref.shape[0]

    def fetch(blk, s, live):
        for e in range(ne):
            n = jnp.where(live, n8_ref[blk * ne + e], 0)
            dst0 = loc_ref[blk * ne + e]
            src0 = run_ref[blk * ne + e]
            _for_each_chunk(n, tb, lambda off, size, src0=src0, dst0=dst0: pltpu.make_async_copy(
                y_hbm.at[pl.ds(pl.multiple_of(src0 + off, RUN_ALIGN), size)],
                bufs[s].at[pl.ds(pl.multiple_of(dst0 + off, RUN_ALIGN), size)], sems.at[s]).start())

    @pl.when(g == 0)
    def _():
        for buf in bufs:
            buf[...] = jnp.zeros_like(buf)
        for ahead in range(nbuf - 2):
            fetch(ahead, ahead, True)

    iota_c = lax.broadcasted_iota(I32, (tb, r_loc), 1)
    for u in range(nbuf):
        blk = g * nbuf + u
        rows = slice(u * tb, (u + 1) * tb)
        _wait_rows(tot_ref[blk], r_loc, y_hbm, bufs[u], sems.at[u])
        ahead = blk + nbuf - 2
        fetch(jnp.minimum(ahead, nblk - 1), (u - 2) % nbuf, ahead < nblk)

        w = jnp.zeros((tb, r_loc), F32)
        for k in range(TOP_K):
            w = w + jnp.where(iota_c == pos_ref[rows, k:k + 1], gate_ref[rows, k:k + 1], 0.0)
        w_hi = w.astype(BF16)
        w_lo = (w - w_hi.astype(F32)).astype(BF16)
        y_lo, y_hi = _unpack_halves(bufs[u][...])
        moe = jnp.concatenate([_dot(w_hi, y_lo) + _dot(w_lo, y_lo), _dot(w_hi, y_hi) + _dot(w_lo, y_hi)], axis=1)
        o_ref[rows, :] = _rms(x2_ref[rows, :] + moe, gf_ref[...])


def _combine(pos, gates, x2, y, norm_final, tables, r_loc):
    t, d = x2.shape
    rows = ROUTE_ROWS * ROUTE_UNROLL
    assert t % rows == 0
    grid_spec = pltpu.PrefetchScalarGridSpec(
        num_scalar_prefetch=4,
        grid=(t // rows,),
        in_specs=[pl.BlockSpec((rows, TOP_K), lambda g, *_: (g, 0)),
                  pl.BlockSpec((rows, TOP_K), lambda g, *_: (g, 0)),
                  pl.BlockSpec((rows, d), lambda g, *_: (g, 0)),
                  pl.BlockSpec((1, d), lambda g, *_: (0, 0)),
                  pl.BlockSpec(memory_space=pl.ANY)],
        out_specs=pl.BlockSpec((rows, d), lambda g, *_: (g, 0)),
        scratch_shapes=[pltpu.VMEM((r_loc, d // 2), U32)] * ROUTE_UNROLL + [pltpu.SemaphoreType.DMA((ROUTE_UNROLL,))],
    )
    return pl.pallas_call(
        _combine_kernel,
        grid_spec=grid_spec,
        out_shape=jax.ShapeDtypeStruct((t, d), F32),
        compiler_params=pltpu.CompilerParams(dimension_semantics=("arbitrary",),
                                             vmem_limit_bytes=VMEM_LIMIT),
        name="combine",
    )(*tables, pos, gates, x2, norm_final.reshape(1, d), y)


def _excl_cumsum(a, axis):
    n = a.shape[axis]
    a = jnp.moveaxis(a, axis, -1)
    earlier = jnp.arange(n)[None, :] < jnp.arange(n)[:, None]
    out = jnp.sum(jnp.where(earlier, a[..., None, :], 0), axis=-1)
    return jnp.moveaxis(out, -1, axis)


def _layout(cnt):
    nblk, ne = cnt.shape
    tb, bm = ROUTE_ROWS, EXPERT_ROWS
    n8 = (cnt + RUN_ALIGN - 1) // RUN_ALIGN * RUN_ALIGN
    loc = _excl_cumsum(n8, 1)
    tot = jnp.sum(n8, axis=1)
    size = jnp.sum(n8, axis=0)
    padded = (size + bm - 1) // bm * bm
    pstart = _excl_cumsum(padded, 0)
    pend = pstart + padded
    run = pstart[None, :] + _excl_cumsum(n8, 0)
    p_blocks = -(-(nblk * tb * TOP_K + nblk * ne * (RUN_ALIGN - 1) + ne * (bm - RUN_ALIGN)) // bm)
    n_used = (pend[-1] // bm).astype(I32)
    flat = lambda a: a.reshape(-1).astype(I32)
    tables = (flat(n8), flat(loc), flat(run), flat(tot))
    half = bm // 2
    tails = (flat(jnp.concatenate([padded - size, (p_blocks * bm - pend[-1:]) // half])),
             flat(jnp.concatenate([pstart + size, pend[-1:]])))
    r_loc = -(-(tb * TOP_K + ne * (RUN_ALIGN - 1)) // 256) * 256
    blocks = ((pstart // bm).astype(I32), (padded // bm).astype(I32), n_used.reshape(1))
    return tables, tails, loc.astype(I32), blocks, p_blocks * bm, r_loc


def kernel(x, mem, norm_mix, w_in, w_pool_group, pool_scale, w_pool_proj, conv_w, conv_b, lru_w_a, lru_b_a, lru_w_x, lru_b_x, lru_lambda, w_lru_proj, w_mix_out, norm_xattn, norm_mem, w_q, w_kv, w_o, norm_moe, w_router, b_router, w_gate_up, b_gate_up, w_down, b_down, norm_final):
    nb, s_len, d = x.shape
    m_len = mem.shape[1]
    assert norm_mix.shape[0] == 1, "single-layer stack"
    l = 0
    x1 = _mixer(x, norm_mix[l], w_in[l], w_pool_group[l], pool_scale[l], w_pool_proj[l],
                conv_w[l], conv_b[l], lru_w_a[l], lru_b_a[l], lru_w_x[l], lru_b_x[l],
                lru_lambda[l], w_lru_proj[l], w_mix_out[l])
    kv = _kv_proj(mem.reshape(nb * m_len, d), norm_mem[l], w_kv[l]).reshape(nb, m_len, 2 * d)
    x2, xn, idx_t, gate_t, lrank_t, cnt = _attention(
        x1.reshape(s_len, nb, d), kv, norm_xattn[l], w_q[l], w_o[l],
        norm_moe[l], w_router[l], b_router[l])
    tables, tails, loc, blocks, p_rows, r_loc = _layout(cnt.reshape(cnt.shape[0], -1))
    xs, pos_t = _dispatch(xn, idx_t, lrank_t, loc, tables + tails, p_rows, r_loc)
    y = _experts(xs, *blocks, w_gate_up[l], b_gate_up[l], w_down[l], b_down[l])
    out = _combine(pos_t.T, gate_t.T, x2.reshape(nb * s_len, d), y, norm_final, tables, r_loc)
    return out.reshape(nb, s_len, d)
```

```python
import functools

import jax
import jax.numpy as jnp
from jax import lax
from jax.experimental import pallas as pl
from jax.experimental.pallas import tpu as pltpu

POOL_WINDOWS = (2, 4, 8, 16)
N_GROUPS = 4
CONV_WIDTH = 4
RG_C = 8.0
N_EXPERTS = 32
TOP_K = 4
SWIGLU_LIMIT = 7.0
SWIGLU_ALPHA = 1.702
RMS_EPS = 1e-6

MIX_STEPS = 32
KV_ROWS = 512
ATT_ROWS = 512
ROUTE_ROWS = 256
ROUTE_UNROLL = 4
EXPERT_ROWS = 512
RUN_ALIGN = 8
VMEM_LIMIT = 52 * 1024 * 1024

BF16 = jnp.bfloat16
F32 = jnp.float32
I32 = jnp.int32
U32 = jnp.uint32


def _const_spec(shape):
    nd = len(shape)
    return pl.BlockSpec(shape, lambda *_: (0,) * nd, pipeline_mode=pl.Buffered(1))


def _rms(x, g):
    return x * lax.rsqrt(jnp.mean(x * x, axis=-1, keepdims=True) + RMS_EPS) * g


def _dot(a, b):
    return jnp.dot(a, b, preferred_element_type=F32)


def _dot_nt(a, b):
    return lax.dot_general(a, b, (((1,), (1,)), ((), ())), preferred_element_type=F32)


def _pack_halves(x):
    c = x.shape[1] // 2
    lo = lax.bitcast_convert_type(x[:, :c], U32)
    hi = lax.bitcast_convert_type(x[:, c:], U32)
    return (hi & jnp.uint32(0xFFFF0000)) | (lo >> 16)


def _unpack_halves(u):
    lo = lax.bitcast_convert_type(u << 16, F32).astype(BF16)
    hi = lax.bitcast_convert_type(u & jnp.uint32(0xFFFF0000), F32).astype(BF16)
    return lo, hi


def _mixer_kernel(x_hbm, nm_ref, win_ref, wpg_ref, psc_ref, wpp_ref, cw_ref, cb_ref,
                  wa_ref, ba_ref, wx_ref, bx_ref, lam_ref, wlp_ref, wmo_ref,
                  o_ref,
                  xbuf, xsem,
                  h_ref, up_ref, ul_ref, a_ref, b_ref, pm_ref, m_ref, mb_ref, t_ref, gl_ref, gb_ref, hc_ref,
                  *, nb, ts):
    rows, d = o_ref.shape
    gw = d // N_GROUPS
    halo_p = (POOL_WINDOWS[-1]) * nb
    halo_c = (CONV_WIDTH - 1) * nb
    c = pl.program_id(0)

    @pl.when(c == 0)
    def _():
        up_ref[0:halo_p, :] = jnp.zeros((halo_p, d), F32)
        ul_ref[0:halo_c, :] = jnp.zeros((halo_c, d), F32)
        hc_ref[...] = jnp.zeros_like(hc_ref)

    def x_copy(step, t, slot):
        return pltpu.make_async_copy(x_hbm.at[:, step * ts + t, :], xbuf.at[slot, t], xsem.at[slot])

    def fetch(step, slot):
        lax.fori_loop(0, ts, lambda t, carry: (x_copy(step, t, slot).start(), carry)[1], 0)

    slot = c % 2

    @pl.when(c == 0)
    def _():
        fetch(0, 0)

    @pl.when(c + 1 < pl.num_programs(0))
    def _():
        fetch(c + 1, 1 - slot)

    lax.fori_loop(0, ts, lambda t, carry: (x_copy(c, t, slot).wait(), carry)[1], 0)

    h_ref[...] = _rms(xbuf[slot].reshape(rows, d), nm_ref[...]).astype(BF16)

    t_glob = c * ts + lax.broadcasted_iota(I32, (rows, 1), 0) // nb

    up_ref[halo_p:halo_p + rows, :] = _dot(h_ref[...], win_ref[:, 0:d])
    ul_ref[halo_c:halo_c + rows, :] = _dot(h_ref[...], win_ref[:, d:2 * d])

    for g, w in enumerate(POOL_WINDOWS):
        cols = slice(g * gw, (g + 1) * gw)
        u = up_ref[halo_p:halo_p + rows, cols]
        acc = u
        for j in range(1, w):
            acc = acc + up_ref[halo_p - j * nb:halo_p - j * nb + rows, cols]
        cnt = jnp.minimum(t_glob + 1, w).astype(F32)
        p = acc / cnt - u
        pg = _dot(p.astype(BF16), wpg_ref[g]) * psc_ref[:, cols]
        pm_ref[:, cols] = pg.astype(BF16)
    t_ref[...] = _dot(h_ref[...], win_ref[:, 3 * d:4 * d])
    m_ref[...] = _dot(pm_ref[...], wpp_ref[...])

    for g in range(N_GROUPS):
        cols = slice(g * gw, (g + 1) * gw)
        xr = cb_ref[:, cols]
        for k in range(CONV_WIDTH):
            off = halo_c - (CONV_WIDTH - 1 - k) * nb
            xr = xr + ul_ref[off:off + rows, cols] * cw_ref[k:k + 1, cols]
        xrb = xr.astype(BF16)
        r = jax.nn.sigmoid(_dot(xrb, wa_ref[g]) + ba_ref[:, cols])
        i = jax.nn.sigmoid(_dot(xrb, wx_ref[g]) + bx_ref[:, cols])
        gl_ref[:, cols] = _dot(h_ref[...], win_ref[:, 2 * d + g * gw:2 * d + (g + 1) * gw])
        gb_ref[:, cols] = _dot(h_ref[...], win_ref[:, 4 * d + g * gw:4 * d + (g + 1) * gw])
        lam = lam_ref[:, cols]
        log_sig = jnp.minimum(lam, 0.0) - jnp.log(1.0 + jnp.exp(-jnp.abs(lam)))
        a = jnp.exp((RG_C * r) * log_sig)
        mult = jnp.sqrt(jnp.maximum(1.0 - a * a, 0.0))
        mult = jnp.where(t_glob == 0, 1.0, mult)
        a_ref[:, cols] = a
        b_ref[:, cols] = mult * i * xr
    m_ref[...] = jax.nn.sigmoid(t_ref[...]) * m_ref[...]

    def scan_step(t, hprev):
        sl = pl.ds(pl.multiple_of(t * nb, nb), nb)
        hn = a_ref[sl, :] * hprev + b_ref[sl, :]
        b_ref[sl, :] = hn
        return hn

    hc_ref[...] = lax.fori_loop(0, ts, scan_step, hc_ref[...], unroll=4)

    half = rows // 2
    for hs in (slice(0, half), slice(half, rows)):
        pm_ref[hs, :] = (b_ref[hs, :] * jax.nn.gelu(gl_ref[hs, :], approximate=True)).astype(BF16)
    for hs in (slice(0, half), slice(half, rows)):
        yb = _dot(pm_ref[hs, :], wlp_ref[...])
        mb_ref[hs, :] = (m_ref[hs, :] + jax.nn.sigmoid(gb_ref[hs, :]) * yb).astype(BF16)
    for i, hs in enumerate((slice(0, half), slice(half, rows))):
        x_half = xbuf[slot, i * (ts // 2):(i + 1) * (ts // 2)].reshape(half, d)
        o_ref[hs, :] = x_half + _dot(mb_ref[hs, :], wmo_ref[...])

    up_ref[0:halo_p, :] = up_ref[rows:rows + halo_p, :]
    ul_ref[0:halo_c, :] = ul_ref[rows:rows + halo_c, :]


def _mixer(x, norm_mix, w_in, w_pool_group, pool_scale, w_pool_proj, conv_w, conv_b,
           lru_w_a, lru_b_a, lru_w_x, lru_b_x, lru_lambda, w_lru_proj, w_mix_out):
    nb, s_len, d = x.shape
    ts = MIX_STEPS
    rows = ts * nb
    n_rows = s_len * nb
    assert s_len % ts == 0 and ts % 2 == 0 and ts >= POOL_WINDOWS[-1] and nb % 8 == 0
    row2 = lambda v: v.reshape(1, -1)
    args = (x, row2(norm_mix), w_in.astype(BF16), w_pool_group.astype(BF16), row2(pool_scale),
            w_pool_proj.astype(BF16), conv_w, row2(conv_b), lru_w_a.astype(BF16), row2(lru_b_a),
            lru_w_x.astype(BF16), row2(lru_b_x), row2(lru_lambda), w_lru_proj.astype(BF16),
            w_mix_out.astype(BF16))
    in_specs = [pl.BlockSpec(memory_space=pl.ANY)] + [_const_spec(a.shape) for a in args[1:]]
    halo_p = POOL_WINDOWS[-1] * nb
    halo_c = (CONV_WIDTH - 1) * nb
    return pl.pallas_call(
        functools.partial(_mixer_kernel, nb=nb, ts=ts),
        grid=(n_rows // rows,),
        in_specs=in_specs,
        out_specs=pl.BlockSpec((rows, d), lambda c: (c, 0)),
        out_shape=jax.ShapeDtypeStruct((n_rows, d), F32),
        scratch_shapes=[
            pltpu.VMEM((2, ts, nb, d), F32),
            pltpu.SemaphoreType.DMA((2,)),
            pltpu.VMEM((rows, d), BF16),
            pltpu.VMEM((halo_p + rows, d), F32),
            pltpu.VMEM((halo_c + rows, d), F32),
            pltpu.VMEM((rows, d), F32),
            pltpu.VMEM((rows, d), F32),
            pltpu.VMEM((rows, d), BF16),
            pltpu.VMEM((rows, d), F32),
            pltpu.VMEM((rows, d), BF16),
            pltpu.VMEM((rows, d), F32),
            pltpu.VMEM((rows, d), F32),
            pltpu.VMEM((rows, d), F32),
            pltpu.VMEM((nb, d), F32),
        ],
        compiler_params=pltpu.CompilerParams(dimension_semantics=("arbitrary",),
                                             vmem_limit_bytes=VMEM_LIMIT),
        name="mixer",
    )(*args)


def _kv_kernel(m_ref, g_ref, w_ref, o_ref):
    o_ref[...] = _dot(_rms(m_ref[...], g_ref[...]).astype(BF16), w_ref[...]).astype(BF16)


def _kv_proj(mem2d, norm_mem, w_kv):
    n, d = mem2d.shape
    assert n % KV_ROWS == 0
    return pl.pallas_call(
        _kv_kernel,
        grid=(n // KV_ROWS,),
        in_specs=[pl.BlockSpec((KV_ROWS, d), lambda i: (i, 0)),
                  _const_spec((1, d)), _const_spec((d, 2 * d))],
        out_specs=pl.BlockSpec((KV_ROWS, 2 * d), lambda i: (i, 0)),
        out_shape=jax.ShapeDtypeStruct((n, 2 * d), BF16),
        compiler_params=pltpu.CompilerParams(dimension_semantics=("arbitrary",),
                                             vmem_limit_bytes=VMEM_LIMIT),
        name="kv_proj",
    )(mem2d, norm_mem.reshape(1, d), w_kv.astype(BF16))


def _attn_kernel(x_hbm, kv_ref, gx_ref, wq_ref, wo_ref, gm_ref, wr_ref, br_ref,
                 x2_ref, xn_ref, idx_ref, gate_ref, lrank_ref, cnt_ref,
                 xbuf, xsem, o_scr, xh_scr, xl_scr, *, nq):
    tq, d = x2_ref.shape
    hd = d // N_GROUPS
    ne = br_ref.shape[0]
    tb = ROUTE_ROWS
    n = pl.program_id(0)
    n_blocks = pl.num_programs(0) - 1
    cur = jnp.minimum(n, n_blocks - 1)
    slot = cur % 2
    prev = (n + 1) % 2

    def x_copy(blk, s):
        return pltpu.make_async_copy(x_hbm.at[pl.ds((blk % nq) * tq, tq), blk // nq, :], xbuf.at[s], xsem.at[s])

    @pl.when(n == 0)
    def _():
        x_copy(0, 0).start()
        xh_scr[...] = jnp.zeros_like(xh_scr)
        xl_scr[...] = jnp.zeros_like(xl_scr)

    @pl.when(n + 1 < n_blocks)
    def _():
        x_copy(n + 1, 1 - slot).start()

    @pl.when(n < n_blocks)
    def _():
        x_copy(n, slot).wait()

    x = xbuf[slot]

    works = []
    for sb in range(tq // tb):
        rows = slice(sb * tb, (sb + 1) * tb)
        ph = _dot_nt(wr_ref[...], xh_scr[prev, rows, :])
        pl_ = _dot_nt(wr_ref[0:ne, :], xl_scr[prev, rows, :])
        works.append(ph[0:ne] + ph[ne:2 * ne] + pl_ + br_ref[...])

    q = _dot(_rms(x, gx_ref[...]).astype(BF16), wq_ref[...]).astype(BF16)

    iota_f = lax.broadcasted_iota(I32, (ne, tb), 0).astype(F32)
    picked = []
    for work in works:
        vals, idxs, sels = [], [], []
        for _ in range(TOP_K):
            m = jnp.max(work, axis=0, keepdims=True)
            idx = jnp.min(jnp.where(work == m, iota_f, float(ne)), axis=0, keepdims=True)
            sel = iota_f == idx
            vals.append(m)
            idxs.append(idx.astype(I32))
            sels.append(sel)
            work = jnp.where(sel, -jnp.inf, work)
        onehot = jnp.zeros((ne, tb), F32)
        for sel in sels:
            onehot = onehot + sel.astype(F32)
        picked.append((vals, idxs, sels, onehot.astype(BF16)))

    scores = [_dot_nt(q[:, h * hd:(h + 1) * hd], kv_ref[:, h * hd:(h + 1) * hd]) * (hd ** -0.5)
              for h in range(N_GROUPS)]
    for h, s in enumerate(scores):
        v = kv_ref[:, d + h * hd:d + (h + 1) * hd]
        e = jnp.exp(s - jnp.max(s, axis=-1, keepdims=True))
        p = e / jnp.sum(e, axis=-1, keepdims=True)
        o_scr[:, h * hd:(h + 1) * hd] = _dot(p.astype(BF16), v).astype(BF16)

    before = (lax.broadcasted_iota(I32, (tb, tb), 0) < lax.broadcasted_iota(I32, (tb, tb), 1)
              ).astype(BF16)
    for sb, (vals, idxs, sels, oh16) in enumerate(picked):
        rows = slice(sb * tb, (sb + 1) * tb)
        ex = [jnp.exp(v - vals[0]) for v in vals]
        den = ex[0] + ex[1] + ex[2] + ex[3]
        prefix = _dot(oh16, before)
        lr = [jnp.sum(jnp.where(sel, prefix, 0.0), axis=0, keepdims=True).astype(I32) for sel in sels]
        idx_ref[:, rows] = jnp.concatenate(idxs, axis=0)
        gate_ref[:, rows] = jnp.concatenate([e_ / den for e_ in ex], axis=0)
        lrank_ref[:, rows] = jnp.concatenate(lr, axis=0)
        cnt_ref[sb] = _dot_nt(jnp.ones((1, tb), BF16), oh16).astype(I32)

    x2 = x + _dot(o_scr[...], wo_ref[...])
    x2_ref[...] = x2
    xn = _rms(x2, gm_ref[...])
    xh = xn.astype(BF16)
    xn_ref[...] = xh
    xh_scr[n % 2] = xh
    xl_scr[n % 2] = (xn - xh.astype(F32)).astype(BF16)


def _attention(x1, kv, norm_xattn, w_q, w_o, norm_moe, w_router, b_router):
    s_len, nb, d = x1.shape
    m = kv.shape[1]
    ne = w_router.shape[-1]
    tq, tb = ATT_ROWS, ROUTE_ROWS
    nq = s_len // tq
    n_blocks = nb * nq
    t = nb * s_len
    assert s_len % tq == 0 and tq % tb == 0
    wr_hi = w_router.astype(BF16)
    wr_lo = (w_router - wr_hi.astype(F32)).astype(BF16)
    wr2t = jnp.concatenate([wr_hi, wr_lo], axis=1).T
    cur = lambda n: jnp.minimum(n, n_blocks - 1)
    routed = lambda n: (0, jnp.maximum(n - 1, 0))
    return pl.pallas_call(
        functools.partial(_attn_kernel, nq=nq),
        grid=(n_blocks + 1,),
        in_specs=[pl.BlockSpec(memory_space=pl.ANY),
                  pl.BlockSpec((None, m, 2 * d), lambda n: (cur(n) // nq, 0, 0)),
                  _const_spec((1, d)), _const_spec((d, d)), _const_spec((d, d)),
                  _const_spec((1, d)), _const_spec((2 * ne, d)), _const_spec((ne, 1))],
        out_specs=[pl.BlockSpec((None, tq, d), lambda n: (cur(n) // nq, cur(n) % nq, 0)),
                   pl.BlockSpec((tq, d), lambda n: (cur(n), 0)),
                   pl.BlockSpec((TOP_K, tq), routed),
                   pl.BlockSpec((TOP_K, tq), routed),
                   pl.BlockSpec((TOP_K, tq), routed),
                   pl.BlockSpec((tq // tb, 1, ne), lambda n: (jnp.maximum(n - 1, 0), 0, 0))],
        out_shape=[jax.ShapeDtypeStruct((nb, s_len, d), F32),
                   jax.ShapeDtypeStruct((t, d), BF16),
                   jax.ShapeDtypeStruct((TOP_K, t), I32),
                   jax.ShapeDtypeStruct((TOP_K, t), F32),
                   jax.ShapeDtypeStruct((TOP_K, t), I32),
                   jax.ShapeDtypeStruct((t // tb, 1, ne), I32)],
        scratch_shapes=[pltpu.VMEM((2, tq, d), F32), pltpu.SemaphoreType.DMA((2,)),
                        pltpu.VMEM((tq, d), BF16), pltpu.VMEM((2, tq, d), BF16), pltpu.VMEM((2, tq, d), BF16)],
        compiler_params=pltpu.CompilerParams(dimension_semantics=("arbitrary",),
                                             vmem_limit_bytes=VMEM_LIMIT),
        name="attention",
    )(x1, kv, norm_xattn.reshape(1, d), w_q.astype(BF16), w_o.astype(BF16),
      norm_moe.reshape(1, d), wr2t, b_router.reshape(ne, 1))


def _pow2_chunks(limit):
    sizes = []
    c = RUN_ALIGN
    while c <= limit:
        sizes.append(c)
        c *= 2
    return sizes[::-1]


def _for_each_chunk(n, limit, fn):
    for size in _pow2_chunks(limit):
        @pl.when((n & size) != 0)
        def _(size=size):
            fn(pl.multiple_of(n & ~(2 * size - 1), RUN_ALIGN), size)


def _wait_rows(n, limit, src, dst, sem):
    _for_each_chunk(n, limit, lambda off, size: pltpu.make_async_copy(
        src.at[pl.ds(0, size)], dst.at[pl.ds(0, size)], sem).wait())


def _dispatch_kernel(n8_ref, loc_ref, run_ref, tot_ref, tailn_ref, tails_ref,
                     xn_ref, idx_ref, lrank_ref, locv_ref, xs_hbm, pos_ref, *scratch):
    *bufs, zbuf, sems, zsem = scratch
    g = pl.program_id(0)
    nbuf = len(bufs)
    r_loc = bufs[0].shape[0]
    tb = xn_ref.shape[0] // nbuf
    ne = tailn_ref.shape[0] - 1
    zrows = zbuf.shape[0]
    last_blk = pl.num_programs(0) * nbuf - 1

    def send(blk, s, live):
        for e in range(ne):
            n = jnp.where(live, n8_ref[blk * ne + e], 0)
            src0 = loc_ref[blk * ne + e]
            dst0 = run_ref[blk * ne + e]
            _for_each_chunk(n, tb, lambda off, size, src0=src0, dst0=dst0: pltpu.make_async_copy(
                bufs[s].at[pl.ds(pl.multiple_of(src0 + off, RUN_ALIGN), size)],
                xs_hbm.at[pl.ds(pl.multiple_of(dst0 + off, RUN_ALIGN), size)], sems.at[s]).start())

    def sent(blk, s, live):
        _wait_rows(jnp.where(live, tot_ref[blk], 0), r_loc, bufs[s], xs_hbm, sems.at[s])

    def zero_rest(i, carry):
        dst = pl.multiple_of(tails_ref[ne] + i * zrows, zrows)
        pltpu.make_async_copy(zbuf, xs_hbm.at[pl.ds(dst, zrows)], zsem).start()
        return carry

    def wait_rest(i, carry):
        pltpu.make_async_copy(zbuf, xs_hbm.at[pl.ds(0, zrows)], zsem).wait()
        return carry

    @pl.when(g == 0)
    def _():
        zbuf[...] = jnp.zeros_like(zbuf)
        for e in range(ne):
            _for_each_chunk(tailn_ref[e], zrows, lambda off, size, e=e: pltpu.make_async_copy(
                zbuf.at[pl.ds(0, size)],
                xs_hbm.at[pl.ds(pl.multiple_of(tails_ref[e] + off, RUN_ALIGN), size)], zsem).start())
        lax.fori_loop(0, tailn_ref[ne], zero_rest, 0)

    iota_e = lax.broadcasted_iota(I32, (ne, tb), 0)
    iota_r = lax.broadcasted_iota(I32, (r_loc, tb), 0)
    for u in range(nbuf):
        blk = g * nbuf + u
        cols = slice(u * tb, (u + 1) * tb)
        sent(jnp.maximum(blk - nbuf, 0), u, blk >= nbuf)
        send(jnp.maximum(blk - 1, 0), (u - 1) % nbuf, blk >= 1)

        loc_col = locv_ref[u].astype(F32)
        pos = []
        for k in range(TOP_K):
            run0 = jnp.sum(jnp.where(iota_e == idx_ref[k:k + 1, cols], loc_col, 0.0), axis=0, keepdims=True)
            pos.append(run0.astype(I32) + lrank_ref[k:k + 1, cols])
            pos_ref[k:k + 1, cols] = pos[k]

        hit = iota_r == pos[0]
        for k in range(1, TOP_K):
            hit = jnp.logical_or(hit, iota_r == pos[k])
        bufs[u][...] = _pack_halves(_dot(hit.astype(BF16), xn_ref[cols, :]))

    @pl.when(g == pl.num_programs(0) - 1)
    def _():
        send(last_blk, nbuf - 1, True)
        for s_ in range(nbuf):
            sent(last_blk - (nbuf - 1 - s_), s_, True)
        for e in range(ne):
            _wait_rows(tailn_ref[e], zrows, zbuf, xs_hbm, zsem)
        lax.fori_loop(0, tailn_ref[ne], wait_rest, 0)


def _dispatch(xn, idx_t, lrank_t, loc, tables, p_rows, r_loc):
    t, d = xn.shape
    tb, nu = ROUTE_ROWS, ROUTE_UNROLL
    nblk, ne = loc.shape
    assert nblk % nu == 0
    tok = pl.BlockSpec((TOP_K, nu * tb), lambda g, *_: (0, g))
    grid_spec = pltpu.PrefetchScalarGridSpec(
        num_scalar_prefetch=6,
        grid=(nblk // nu,),
        in_specs=[pl.BlockSpec((nu * tb, d), lambda g, *_: (g, 0)), tok, tok,
                  pl.BlockSpec((nu, ne, 1), lambda g, *_: (g, 0, 0))],
        out_specs=[pl.BlockSpec(memory_space=pl.ANY), tok],
        scratch_shapes=[pltpu.VMEM((r_loc, d // 2), U32)] * nu + [
            pltpu.VMEM((EXPERT_ROWS // 2, d // 2), U32), pltpu.SemaphoreType.DMA((nu,)), pltpu.SemaphoreType.DMA],
    )
    return pl.pallas_call(
        _dispatch_kernel,
        grid_spec=grid_spec,
        out_shape=[jax.ShapeDtypeStruct((p_rows, d // 2), U32), jax.ShapeDtypeStruct((TOP_K, t), I32)],
        compiler_params=pltpu.CompilerParams(dimension_semantics=("arbitrary",),
                                             vmem_limit_bytes=VMEM_LIMIT),
        name="dispatch",
    )(*tables, xn, idx_t, lrank_t, loc.reshape(nblk, ne, 1))


def _expert_kernel(blk0_ref, nblk_ref, nu_ref, xs_hbm, wgu_ref, bgu_ref, wdn_ref, bdn_ref, y_hbm,
                   xbuf, ybuf, zbuf, in_sem, out_sem, zsem, wgu_bf, wdn_bf):
    e = pl.program_id(0)
    bm, dh = xbuf.shape[1:]
    zrows = zbuf.shape[0]
    de = wdn_ref.shape[0]
    n_used = nu_ref[0]

    def x_copy(g, slot):
        return pltpu.make_async_copy(xs_hbm.at[pl.ds(pl.multiple_of(g * bm, bm), bm)], xbuf.at[slot],
                                     in_sem.at[slot])

    def y_copy(g, slot):
        return pltpu.make_async_copy(ybuf.at[slot], y_hbm.at[pl.ds(pl.multiple_of(g * bm, bm), bm)],
                                     out_sem.at[slot])

    @pl.when(jnp.logical_and(e == 0, n_used > 0))
    def _():
        x_copy(0, 0).start()

    wgu_bf[...] = wgu_ref[...].astype(BF16)
    wdn_bf[...] = wdn_ref[...].astype(BF16)

    def block(j, carry):
        g = blk0_ref[e] + j
        slot = g % 2

        @pl.when(g + 1 < n_used)
        def _():
            x_copy(g + 1, 1 - slot).start()

        x_copy(g, slot).wait()

        @pl.when(g >= 2)
        def _():
            y_copy(g - 2, slot).wait()

        x_lo, x_hi = _unpack_halves(xbuf[slot])

        def proj(c0, c1):
            return (_dot(x_lo, wgu_bf[0:dh, c0:c1]) + _dot(x_hi, wgu_bf[dh:2 * dh, c0:c1])
                    + bgu_ref[:, c0:c1])

        nch = 4
        ch = de // nch
        acts = []
        for c in range(nch):
            gl = jnp.minimum(proj(c * ch, (c + 1) * ch), SWIGLU_LIMIT)
            up = jnp.clip(proj(de + c * ch, de + (c + 1) * ch), -SWIGLU_LIMIT, SWIGLU_LIMIT)
            acts.append(((up + 1.0) * (gl * jax.nn.sigmoid(SWIGLU_ALPHA * gl))).astype(BF16))
        y = bdn_ref[...]
        for c in range(nch):
            y = y + _dot(acts[c], wdn_bf[c * ch:(c + 1) * ch, :])
        ybuf[slot] = _pack_halves(y.astype(BF16).astype(F32))
        y_copy(g, slot).start()
        return carry

    lax.fori_loop(0, nblk_ref[e], block, 0)

    @pl.when(e == pl.num_programs(0) - 1)
    def _():
        for back in (2, 1):
            @pl.when(n_used >= back)
            def _(back=back):
                y_copy(n_used - back, (n_used - back) % 2).wait()
        n_zero = (y_hbm.shape[0] - n_used * bm) // zrows
        zbuf[...] = jnp.zeros_like(zbuf)

        def z_copy(i):
            return pltpu.make_async_copy(
                zbuf, y_hbm.at[pl.ds(pl.multiple_of(n_used * bm + i * zrows, zrows), zrows)], zsem)

        lax.fori_loop(0, n_zero, lambda i, c: (z_copy(i).start(), c)[1], 0)
        lax.fori_loop(0, n_zero, lambda i, c: (z_copy(i).wait(), c)[1], 0)


def _experts(xs, blk0, nblk, n_used, w_gate_up, b_gate_up, w_down, b_down):
    p, dh = xs.shape
    ne, d, de2 = w_gate_up.shape
    de = de2 // 2
    bm = EXPERT_ROWS
    per_expert = lambda shape: pl.BlockSpec((None,) + shape, lambda e, *_: (e, 0, 0))
    grid_spec = pltpu.PrefetchScalarGridSpec(
        num_scalar_prefetch=3,
        grid=(ne,),
        in_specs=[pl.BlockSpec(memory_space=pl.ANY),
                  per_expert((d, de2)), per_expert((1, de2)), per_expert((de, d)), per_expert((1, d))],
        out_specs=pl.BlockSpec(memory_space=pl.ANY),
        scratch_shapes=[pltpu.VMEM((2, bm, dh), U32), pltpu.VMEM((2, bm, dh), U32),
                        pltpu.VMEM((bm // 4, dh), U32),
                        pltpu.SemaphoreType.DMA((2,)), pltpu.SemaphoreType.DMA((2,)), pltpu.SemaphoreType.DMA,
                        pltpu.VMEM((d, de2), BF16), pltpu.VMEM((de, d), BF16)],
    )
    return pl.pallas_call(
        _expert_kernel,
        grid_spec=grid_spec,
        out_shape=jax.ShapeDtypeStruct((p, dh), U32),
        compiler_params=pltpu.CompilerParams(dimension_semantics=("arbitrary",),
                                             vmem_limit_bytes=VMEM_LIMIT),
        name="experts",
    )(blk0, nblk, n_used, xs, w_gate_up, b_gate_up.reshape(ne, 1, de2), w_down, b_down.reshape(ne, 1, d))


def _combine_kernel(n8_ref, loc_ref, run_ref, tot_ref,
                    pos_ref, gate_ref, x2_ref, gf_ref, y_hbm, o_ref, *scratch):
    *bufs, sems = scratch
    g = pl.program_id(0)
    nbuf = len(bufs)
    r_loc = bufs[0].shape[0]
    tb = x2_ref.shape[0] // nbuf
    nblk = pl.num_programs(0) * nbuf
    ne = n8_ref.shape[0] // tot_ref.shape[0]

    def fetch(blk, s, live):
        for e in range(ne):
            n = jnp.where(live, n8_ref[blk * ne + e], 0)
            dst0 = loc_ref[blk * ne + e]
            src0 = run_ref[blk * ne + e]
            _for_each_chunk(n, tb, lambda off, size, src0=src0, dst0=dst0: pltpu.make_async_copy(
                y_hbm.at[pl.ds(pl.multiple_of(src0 + off, RUN_ALIGN), size)],
                bufs[s].at[pl.ds(pl.multiple_of(dst0 + off, RUN_ALIGN), size)], sems.at[s]).start())

    @pl.when(g == 0)
    def _():
        for buf in bufs:
            buf[...] = jnp.zeros_like(buf)
        for ahead in range(nbuf - 2):
            fetch(ahead, ahead, True)

    iota_c = lax.broadcasted_iota(I32, (tb, r_loc), 1)
    for u in range(nbuf):
        blk = g * nbuf + u
        rows = slice(u * tb, (u + 1) * tb)
        _wait_rows(tot_ref[blk], r_loc, y_hbm, bufs[u], sems.at[u])
        ahead = blk + nbuf - 2
        fetch(jnp.minimum(ahead, nblk - 1), (u - 2) % nbuf, ahead < nblk)

        w = jnp.zeros((tb, r_loc), F32)
        for k in range(TOP_K):
            w = w + jnp.where(iota_c == pos_ref[rows, k:k + 1], gate_ref[rows, k:k + 1], 0.0)
        w_hi = w.astype(BF16)
        w_lo = (w - w_hi.astype(F32)).astype(BF16)
        y_lo, y_hi = _unpack_halves(bufs[u][...])
        moe = jnp.concatenate([_dot(w_hi, y_lo) + _dot(w_lo, y_lo), _dot(w_hi, y_hi) + _dot(w_lo, y_hi)], axis=1)
        o_ref[rows, :] = _rms(x2_ref[rows, :] + moe, gf_ref[...])


def _combine(pos, gates, x2, y, norm_final, tables, r_loc):
    t, d = x2.shape
    rows = ROUTE_ROWS * ROUTE_UNROLL
    assert t % rows == 0
    grid_spec = pltpu.PrefetchScalarGridSpec(
        num_scalar_prefetch=4,
        grid=(t // rows,),
        in_specs=[pl.BlockSpec((rows, TOP_K), lambda g, *_: (g, 0)),
                  pl.BlockSpec((rows, TOP_K), lambda g, *_: (g, 0)),
                  pl.BlockSpec((rows, d), lambda g, *_: (g, 0)),
                  pl.BlockSpec((1, d), lambda g, *_: (0, 0)),
                  pl.BlockSpec(memory_space=pl.ANY)],
        out_specs=pl.BlockSpec((rows, d), lambda g, *_: (g, 0)),
        scratch_shapes=[pltpu.VMEM((r_loc, d // 2), U32)] * ROUTE_UNROLL + [pltpu.SemaphoreType.DMA((ROUTE_UNROLL,))],
    )
    return pl.pallas_call(
        _combine_kernel,
        grid_spec=grid_spec,
        out_shape=jax.ShapeDtypeStruct((t, d), F32),
        compiler_params=pltpu.CompilerParams(dimension_semantics=("arbitrary",),
                                             vmem_limit_bytes=VMEM_LIMIT),
        name="combine",
    )(*tables, pos, gates, x2, norm_final.reshape(1, d), y)


def _excl_cumsum(a, axis):
    n = a.shape[axis]
    a = jnp.moveaxis(a, axis, -1)
    earlier = jnp.arange(n)[None, :] < jnp.arange(n)[:, None]
    out = jnp.sum(jnp.where(earlier, a[..., None, :], 0), axis=-1)
    return jnp.moveaxis(out, -1, axis)


def _layout(cnt):
    nblk, ne = cnt.shape
    tb, bm = ROUTE_ROWS, EXPERT_ROWS
    n8 = (cnt + RUN_ALIGN - 1) // RUN_ALIGN * RUN_ALIGN
    loc = _excl_cumsum(n8, 1)
    tot = jnp.sum(n8, axis=1)
    size = jnp.sum(n8, axis=0)
    padded = (size + bm - 1) // bm * bm
    pstart = _excl_cumsum(padded, 0)
    pend = pstart + padded
    run = pstart[None, :] + _excl_cumsum(n8, 0)
    p_blocks = -(-(nblk * tb * TOP_K + nblk * ne * (RUN_ALIGN - 1) + ne * (bm - RUN_ALIGN)) // bm)
    n_used = (pend[-1] // bm).astype(I32)
    flat = lambda a: a.reshape(-1).astype(I32)
    tables = (flat(n8), flat(loc), flat(run), flat(tot))
    half = bm // 2
    tails = (flat(jnp.concatenate([padded - size, (p_blocks * bm - pend[-1:]) // half])),
             flat(jnp.concatenate([pstart + size, pend[-1:]])))
    r_loc = -(-(tb * TOP_K + ne * (RUN_ALIGN - 1)) // 256) * 256
    blocks = ((pstart // bm).astype(I32), (padded // bm).astype(I32), n_used.reshape(1))
    return tables, tails, loc.astype(I32), blocks, p_blocks * bm, r_loc


def kernel(x, mem, norm_mix, w_in, w_pool_group, pool_scale, w_pool_proj, conv_w, conv_b, lru_w_a, lru_b_a, lru_w_x, lru_b_x, lru_lambda, w_lru_proj, w_mix_out, norm_xattn, norm_mem, w_q, w_kv, w_o, norm_moe, w_router, b_router, w_gate_up, b_gate_up, w_down, b_down, norm_final):
    nb, s_len, d = x.shape
    m_len = mem.shape[1]
    assert norm_mix.shape[0] == 1, "single-layer stack"
    l = 0
    x1 = _mixer(x, norm_mix[l], w_in[l], w_pool_group[l], pool_scale[l], w_pool_proj[l],
                conv_w[l], conv_b[l], lru_w_a[l], lru_b_a[l], lru_w_x[l], lru_b_x[l],
                lru_lambda[l], w_lru_proj[l], w_mix_out[l])
    kv = _kv_proj(mem.reshape(nb * m_len, d), norm_mem[l], w_kv[l]).reshape(nb, m_len, 2 * d)
    x2, xn, idx_t, gate_t, lrank_t, cnt = _attention(
        x1.reshape(s_len, nb, d), kv, norm_xattn[l], w_q[l], w_o[l],
        norm_moe[l], w_router[l], b_router[l])
    tables, tails, loc, blocks, p_rows, r_loc = _layout(cnt.reshape(cnt.shape[0], -1))
    xs, pos_t = _dispatch(xn, idx_t, lrank_t, loc, tables + tails, p_rows, r_loc)
    y = _experts(xs, *blocks, w_gate_up[l], b_gate_up[l], w_down[l], b_down[l])
    out = _combine(pos_t.T, gate_t.T, x2.reshape(nb * s_len, d), y, norm_final, tables, r_loc)
    return out.reshape(nb, s_len, d)
```

```python
import functools

import jax
import jax.numpy as jnp
from jax import lax
from jax.experimental import pallas as pl
from jax.experimental.pallas import tpu as pltpu

POOL_WINDOWS = (2, 4, 8, 16)
N_GROUPS = 4
CONV_WIDTH = 4
RG_C = 8.0
N_EXPERTS = 32
TOP_K = 4
SWIGLU_LIMIT = 7.0
SWIGLU_ALPHA = 1.702
RMS_EPS = 1e-6

MIX_STEPS = 32
KV_ROWS = 512
ATT_ROWS = 1024
ROUTE_ROWS = 256
ROUTE_UNROLL = 4
EXPERT_ROWS = 256
RUN_ALIGN = 8
VMEM_LIMIT = 52 * 1024 * 1024

BF16 = jnp.bfloat16
F32 = jnp.float32
I32 = jnp.int32
U32 = jnp.uint32


def _const_spec(shape):
    nd = len(shape)
    return pl.BlockSpec(shape, lambda *_: (0,) * nd, pipeline_mode=pl.Buffered(1))


def _rms(x, g):
    return x * lax.rsqrt(jnp.mean(x * x, axis=-1, keepdims=True) + RMS_EPS) * g


def _dot(a, b):
    return jnp.dot(a, b, preferred_element_type=F32)


def _dot_nt(a, b):
    return lax.dot_general(a, b, (((1,), (1,)), ((), ())), preferred_element_type=F32)


def _pack_halves(x):
    c = x.shape[1] // 2
    lo = lax.bitcast_convert_type(x[:, :c], U32)
    hi = lax.bitcast_convert_type(x[:, c:], U32)
    return (hi & jnp.uint32(0xFFFF0000)) | (lo >> 16)


def _unpack_halves(u):
    lo = lax.bitcast_convert_type(u << 16, F32).astype(BF16)
    hi = lax.bitcast_convert_type(u & jnp.uint32(0xFFFF0000), F32).astype(BF16)
    return lo, hi


def _mixer_kernel(x_hbm, nm_ref, win_ref, wpg_ref, psc_ref, wpp_ref, cw_ref, cb_ref,
                  wa_ref, ba_ref, wx_ref, bx_ref, lam_ref, wlp_ref, wmo_ref,
                  o_ref,
                  xbuf, xsem,
                  h_ref, up_ref, ul_ref, a_ref, b_ref, pm_ref, m_ref, mb_ref, t_ref, gl_ref, gb_ref, hc_ref,
                  *, nb, ts):
    rows, d = o_ref.shape
    gw = d // N_GROUPS
    halo_p = (POOL_WINDOWS[-1]) * nb
    halo_c = (CONV_WIDTH - 1) * nb
    c = pl.program_id(0)

    @pl.when(c == 0)
    def _():
        up_ref[0:halo_p, :] = jnp.zeros((halo_p, d), F32)
        ul_ref[0:halo_c, :] = jnp.zeros((halo_c, d), F32)
        hc_ref[...] = jnp.zeros_like(hc_ref)

    def x_copy(step, t, slot):
        return pltpu.make_async_copy(x_hbm.at[:, step * ts + t, :], xbuf.at[slot, t], xsem.at[slot])

    def fetch(step, slot):
        lax.fori_loop(0, ts, lambda t, carry: (x_copy(step, t, slot).start(), carry)[1], 0)

    slot = c % 2

    @pl.when(c == 0)
    def _():
        fetch(0, 0)

    @pl.when(c + 1 < pl.num_programs(0))
    def _():
        fetch(c + 1, 1 - slot)

    lax.fori_loop(0, ts, lambda t, carry: (x_copy(c, t, slot).wait(), carry)[1], 0)

    h_ref[...] = _rms(xbuf[slot].reshape(rows, d), nm_ref[...]).astype(BF16)

    t_glob = c * ts + lax.broadcasted_iota(I32, (rows, 1), 0) // nb

    up_ref[halo_p:halo_p + rows, :] = _dot(h_ref[...], win_ref[:, 0:d])
    ul_ref[halo_c:halo_c + rows, :] = _dot(h_ref[...], win_ref[:, d:2 * d])

    for g, w in enumerate(POOL_WINDOWS):
        cols = slice(g * gw, (g + 1) * gw)
        u = up_ref[halo_p:halo_p + rows, cols]
        acc = u
        for j in range(1, w):
            acc = acc + up_ref[halo_p - j * nb:halo_p - j * nb + rows, cols]
        cnt = jnp.minimum(t_glob + 1, w).astype(F32)
        p = acc / cnt - u
        pg = _dot(p.astype(BF16), wpg_ref[g]) * psc_ref[:, cols]
        pm_ref[:, cols] = pg.astype(BF16)
    t_ref[...] = _dot(h_ref[...], win_ref[:, 3 * d:4 * d])
    m_ref[...] = _dot(pm_ref[...], wpp_ref[...])

    for g in range(N_GROUPS):
        cols = slice(g * gw, (g + 1) * gw)
        xr = cb_ref[:, cols]
        for k in range(CONV_WIDTH):
            off = halo_c - (CONV_WIDTH - 1 - k) * nb
            xr = xr + ul_ref[off:off + rows, cols] * cw_ref[k:k + 1, cols]
        xrb = xr.astype(BF16)
        r = jax.nn.sigmoid(_dot(xrb, wa_ref[g]) + ba_ref[:, cols])
        i = jax.nn.sigmoid(_dot(xrb, wx_ref[g]) + bx_ref[:, cols])
        gl_ref[:, cols] = _dot(h_ref[...], win_ref[:, 2 * d + g * gw:2 * d + (g + 1) * gw])
        gb_ref[:, cols] = _dot(h_ref[...], win_ref[:, 4 * d + g * gw:4 * d + (g + 1) * gw])
        lam = lam_ref[:, cols]
        log_sig = jnp.minimum(lam, 0.0) - jnp.log(1.0 + jnp.exp(-jnp.abs(lam)))
        a = jnp.exp((RG_C * r) * log_sig)
        mult = jnp.sqrt(jnp.maximum(1.0 - a * a, 0.0))
        mult = jnp.where(t_glob == 0, 1.0, mult)
        a_ref[:, cols] = a
        b_ref[:, cols] = mult * i * xr
    m_ref[...] = jax.nn.sigmoid(t_ref[...]) * m_ref[...]

    def scan_step(t, hprev):
        sl = pl.ds(pl.multiple_of(t * nb, nb), nb)
        hn = a_ref[sl, :] * hprev + b_ref[sl, :]
        b_ref[sl, :] = hn
        return hn

    hc_ref[...] = lax.fori_loop(0, ts, scan_step, hc_ref[...], unroll=4)

    half = rows // 2
    for hs in (slice(0, half), slice(half, rows)):
        pm_ref[hs, :] = (b_ref[hs, :] * jax.nn.gelu(gl_ref[hs, :], approximate=True)).astype(BF16)
    for hs in (slice(0, half), slice(half, rows)):
        yb = _dot(pm_ref[hs, :], wlp_ref[...])
        mb_ref[hs, :] = (m_ref[hs, :] + jax.nn.sigmoid(gb_ref[hs, :]) * yb).astype(BF16)
    for i, hs in enumerate((slice(0, half), slice(half, rows))):
        x_half = xbuf[slot, i * (ts // 2):(i + 1) * (ts // 2)].reshape(half, d)
        o_ref[hs, :] = x_half + _dot(mb_ref[hs, :], wmo_ref[...])

    up_ref[0:halo_p, :] = up_ref[rows:rows + halo_p, :]
    ul_ref[0:halo_c, :] = ul_ref[rows:rows + halo_c, :]


def _mixer(x, norm_mix, w_in, w_pool_group, pool_scale, w_pool_proj, conv_w, conv_b,
           lru_w_a, lru_b_a, lru_w_x, lru_b_x, lru_lambda, w_lru_proj, w_mix_out):
    nb, s_len, d = x.shape
    ts = MIX_STEPS
    rows = ts * nb
    n_rows = s_len * nb
    assert s_len % ts == 0 and ts % 2 == 0 and ts >= POOL_WINDOWS[-1] and nb % 8 == 0
    row2 = lambda v: v.reshape(1, -1)
    args = (x, row2(norm_mix), w_in.astype(BF16), w_pool_group.astype(BF16), row2(pool_scale),
            w_pool_proj.astype(BF16), conv_w, row2(conv_b), lru_w_a.astype(BF16), row2(lru_b_a),
            lru_w_x.astype(BF16), row2(lru_b_x), row2(lru_lambda), w_lru_proj.astype(BF16),
            w_mix_out.astype(BF16))
    in_specs = [pl.BlockSpec(memory_space=pl.ANY)] + [_const_spec(a.shape) for a in args[1:]]
    halo_p = POOL_WINDOWS[-1] * nb
    halo_c = (CONV_WIDTH - 1) * nb
    return pl.pallas_call(
        functools.partial(_mixer_kernel, nb=nb, ts=ts),
        grid=(n_rows // rows,),
        in_specs=in_specs,
        out_specs=pl.BlockSpec((rows, d), lambda c: (c, 0)),
        out_shape=jax.ShapeDtypeStruct((n_rows, d), F32),
        scratch_shapes=[
            pltpu.VMEM((2, ts, nb, d), F32),
            pltpu.SemaphoreType.DMA((2,)),
            pltpu.VMEM((rows, d), BF16),
            pltpu.VMEM((halo_p + rows, d), F32),
            pltpu.VMEM((halo_c + rows, d), F32),
            pltpu.VMEM((rows, d), F32),
            pltpu.VMEM((rows, d), F32),
            pltpu.VMEM((rows, d), BF16),
            pltpu.VMEM((rows, d), F32),
            pltpu.VMEM((rows, d), BF16),
            pltpu.VMEM((rows, d), F32),
            pltpu.VMEM((rows, d), F32),
            pltpu.VMEM((rows, d), F32),
            pltpu.VMEM((nb, d), F32),
        ],
        compiler_params=pltpu.CompilerParams(dimension_semantics=("arbitrary",),
                                             vmem_limit_bytes=VMEM_LIMIT),
        name="mixer",
    )(*args)


def _kv_kernel(m_ref, g_ref, w_ref, o_ref):
    o_ref[...] = _dot(_rms(m_ref[...], g_ref[...]).astype(BF16), w_ref[...]).astype(BF16)


def _kv_proj(mem2d, norm_mem, w_kv):
    n, d = mem2d.shape
    assert n % KV_ROWS == 0
    return pl.pallas_call(
        _kv_kernel,
        grid=(n // KV_ROWS,),
        in_specs=[pl.BlockSpec((KV_ROWS, d), lambda i: (i, 0)),
                  _const_spec((1, d)), _const_spec((d, 2 * d))],
        out_specs=pl.BlockSpec((KV_ROWS, 2 * d), lambda i: (i, 0)),
        out_shape=jax.ShapeDtypeStruct((n, 2 * d), BF16),
        compiler_params=pltpu.CompilerParams(dimension_semantics=("arbitrary",),
                                             vmem_limit_bytes=VMEM_LIMIT),
        name="kv_proj",
    )(mem2d, norm_mem.reshape(1, d), w_kv.astype(BF16))


def _attn_kernel(x_hbm, kv_ref, gx_ref, wq_ref, wo_ref, gm_ref, wr_ref, br_ref,
                 x2_ref, xn_ref, idx_ref, gate_ref, lrank_ref, cnt_ref,
                 xbuf, xsem, o_scr, xh_scr, xl_scr, *, nq):
    tq, d = x2_ref.shape
    hd = d // N_GROUPS
    ne = br_ref.shape[0]
    tb = ROUTE_ROWS
    n = pl.program_id(0)
    n_blocks = pl.num_programs(0) - 1
    cur = jnp.minimum(n, n_blocks - 1)
    slot = cur % 2
    prev = (n + 1) % 2

    def x_copy(blk, s):
        return pltpu.make_async_copy(x_hbm.at[pl.ds((blk % nq) * tq, tq), blk // nq, :], xbuf.at[s], xsem.at[s])

    @pl.when(n == 0)
    def _():
        x_copy(0, 0).start()
        xh_scr[...] = jnp.zeros_like(xh_scr)
        xl_scr[...] = jnp.zeros_like(xl_scr)

    @pl.when(n + 1 < n_blocks)
    def _():
        x_copy(n + 1, 1 - slot).start()

    @pl.when(n < n_blocks)
    def _():
        x_copy(n, slot).wait()

    x = xbuf[slot]

    works = []
    for sb in range(tq // tb):
        rows = slice(sb * tb, (sb + 1) * tb)
        ph = _dot_nt(wr_ref[...], xh_scr[prev, rows, :])
        pl_ = _dot_nt(wr_ref[0:ne, :], xl_scr[prev, rows, :])
        works.append(ph[0:ne] + ph[ne:2 * ne] + pl_ + br_ref[...])

    q = _dot(_rms(x, gx_ref[...]).astype(BF16), wq_ref[...]).astype(BF16)

    iota_f = lax.broadcasted_iota(I32, (ne, tb), 0).astype(F32)
    picked = []
    for work in works:
        vals, idxs, sels = [], [], []
        for _ in range(TOP_K):
            m = jnp.max(work, axis=0, keepdims=True)
            idx = jnp.min(jnp.where(work == m, iota_f, float(ne)), axis=0, keepdims=True)
            sel = iota_f == idx
            vals.append(m)
            idxs.append(idx.astype(I32))
            sels.append(sel)
            work = jnp.where(sel, -jnp.inf, work)
        onehot = jnp.zeros((ne, tb), F32)
        for sel in sels:
            onehot = onehot + sel.astype(F32)
        picked.append((vals, idxs, sels, onehot.astype(BF16)))

    scores = [_dot_nt(q[:, h * hd:(h + 1) * hd], kv_ref[:, h * hd:(h + 1) * hd]) * (hd ** -0.5)
              for h in range(N_GROUPS)]
    for h, s in enumerate(scores):
        v = kv_ref[:, d + h * hd:d + (h + 1) * hd]
        e = jnp.exp(s - jnp.max(s, axis=-1, keepdims=True))
        p = e / jnp.sum(e, axis=-1, keepdims=True)
        o_scr[:, h * hd:(h + 1) * hd] = _dot(p.astype(BF16), v).astype(BF16)

    before = (lax.broadcasted_iota(I32, (tb, tb), 0) < lax.broadcasted_iota(I32, (tb, tb), 1)
              ).astype(BF16)
    for sb, (vals, idxs, sels, oh16) in enumerate(picked):
        rows = slice(sb * tb, (sb + 1) * tb)
        ex = [jnp.exp(v - vals[0]) for v in vals]
        den = ex[0] + ex[1] + ex[2] + ex[3]
        prefix = _dot(oh16, before)
        lr = [jnp.sum(jnp.where(sel, prefix, 0.0), axis=0, keepdims=True).astype(I32) for sel in sels]
        idx_ref[:, rows] = jnp.concatenate(idxs, axis=0)
        gate_ref[:, rows] = jnp.concatenate([e_ / den for e_ in ex], axis=0)
        lrank_ref[:, rows] = jnp.concatenate(lr, axis=0)
        cnt_ref[sb] = _dot_nt(jnp.ones((1, tb), BF16), oh16).astype(I32)

    x2 = x + _dot(o_scr[...], wo_ref[...])
    x2_ref[...] = x2
    xn = _rms(x2, gm_ref[...])
    xh = xn.astype(BF16)
    xn_ref[...] = xh
    xh_scr[n % 2] = xh
    xl_scr[n % 2] = (xn - xh.astype(F32)).astype(BF16)


def _attention(x1, kv, norm_xattn, w_q, w_o, norm_moe, w_router, b_router):
    s_len, nb, d = x1.shape
    m = kv.shape[1]
    ne = w_router.shape[-1]
    tq, tb = ATT_ROWS, ROUTE_ROWS
    nq = s_len // tq
    n_blocks = nb * nq
    t = nb * s_len
    assert s_len % tq == 0 and tq % tb == 0
    wr_hi = w_router.astype(BF16)
    wr_lo = (w_router - wr_hi.astype(F32)).astype(BF16)
    wr2t = jnp.concatenate([wr_hi, wr_lo], axis=1).T
    cur = lambda n: jnp.minimum(n, n_blocks - 1)
    routed = lambda n: (0, jnp.maximum(n - 1, 0))
    return pl.pallas_call(
        functools.partial(_attn_kernel, nq=nq),
        grid=(n_blocks + 1,),
        in_specs=[pl.BlockSpec(memory_space=pl.ANY),
                  pl.BlockSpec((None, m, 2 * d), lambda n: (cur(n) // nq, 0, 0)),
                  _const_spec((1, d)), _const_spec((d, d)), _const_spec((d, d)),
                  _const_spec((1, d)), _const_spec((2 * ne, d)), _const_spec((ne, 1))],
        out_specs=[pl.BlockSpec((None, tq, d), lambda n: (cur(n) // nq, cur(n) % nq, 0)),
                   pl.BlockSpec((tq, d), lambda n: (cur(n), 0)),
                   pl.BlockSpec((TOP_K, tq), routed),
                   pl.BlockSpec((TOP_K, tq), routed),
                   pl.BlockSpec((TOP_K, tq), routed),
                   pl.BlockSpec((tq // tb, 1, ne), lambda n: (jnp.maximum(n - 1, 0), 0, 0))],
        out_shape=[jax.ShapeDtypeStruct((nb, s_len, d), F32),
                   jax.ShapeDtypeStruct((t, d), BF16),
                   jax.ShapeDtypeStruct((TOP_K, t), I32),
                   jax.ShapeDtypeStruct((TOP_K, t), F32),
                   jax.ShapeDtypeStruct((TOP_K, t), I32),
                   jax.ShapeDtypeStruct((t // tb, 1, ne), I32)],
        scratch_shapes=[pltpu.VMEM((2, tq, d), F32), pltpu.SemaphoreType.DMA((2,)),
                        pltpu.VMEM((tq, d), BF16), pltpu.VMEM((2, tq, d), BF16), pltpu.VMEM((2, tq, d), BF16)],
        compiler_params=pltpu.CompilerParams(dimension_semantics=("arbitrary",),
                                             vmem_limit_bytes=VMEM_LIMIT),
        name="attention",
    )(x1, kv, norm_xattn.reshape(1, d), w_q.astype(BF16), w_o.astype(BF16),
      norm_moe.reshape(1, d), wr2t, b_router.reshape(ne, 1))


def _pow2_chunks(limit):
    sizes = []
    c = RUN_ALIGN
    while c <= limit:
        sizes.append(c)
        c *= 2
    return sizes[::-1]


def _for_each_chunk(n, limit, fn):
    for size in _pow2_chunks(limit):
        @pl.when((n & size) != 0)
        def _(size=size):
            fn(pl.multiple_of(n & ~(2 * size - 1), RUN_ALIGN), size)


def _wait_rows(n, limit, src, dst, sem):
    _for_each_chunk(n, limit, lambda off, size: pltpu.make_async_copy(
        src.at[pl.ds(0, size)], dst.at[pl.ds(0, size)], sem).wait())


def _dispatch_kernel(n8_ref, loc_ref, run_ref, tot_ref, tailn_ref, tails_ref,
                     xn_ref, idx_ref, lrank_ref, locv_ref, xs_hbm, pos_ref, *scratch):
    *bufs, zbuf, sems, zsem = scratch
    g = pl.program_id(0)
    nbuf = len(bufs)
    r_loc = bufs[0].shape[0]
    tb = xn_ref.shape[0] // nbuf
    ne = tailn_ref.shape[0] - 1
    zrows = zbuf.shape[0]
    last_blk = pl.num_programs(0) * nbuf - 1

    def send(blk, s, live):
        for e in range(ne):
            n = jnp.where(live, n8_ref[blk * ne + e], 0)
            src0 = loc_ref[blk * ne + e]
            dst0 = run_ref[blk * ne + e]
            _for_each_chunk(n, tb, lambda off, size, src0=src0, dst0=dst0: pltpu.make_async_copy(
                bufs[s].at[pl.ds(pl.multiple_of(src0 + off, RUN_ALIGN), size)],
                xs_hbm.at[pl.ds(pl.multiple_of(dst0 + off, RUN_ALIGN), size)], sems.at[s]).start())

    def sent(blk, s, live):
        _wait_rows(jnp.where(live, tot_ref[blk], 0), r_loc, bufs[s], xs_hbm, sems.at[s])

    def zero_rest(i, carry):
        dst = pl.multiple_of(tails_ref[ne] + i * zrows, zrows)
        pltpu.make_async_copy(zbuf, xs_hbm.at[pl.ds(dst, zrows)], zsem).start()
        return carry

    def wait_rest(i, carry):
        pltpu.make_async_copy(zbuf, xs_hbm.at[pl.ds(0, zrows)], zsem).wait()
        return carry

    @pl.when(g == 0)
    def _():
        zbuf[...] = jnp.zeros_like(zbuf)
        for e in range(ne):
            _for_each_chunk(tailn_ref[e], zrows, lambda off, size, e=e: pltpu.make_async_copy(
                zbuf.at[pl.ds(0, size)],
                xs_hbm.at[pl.ds(pl.multiple_of(tails_ref[e] + off, RUN_ALIGN), size)], zsem).start())
        lax.fori_loop(0, tailn_ref[ne], zero_rest, 0)

    iota_e = lax.broadcasted_iota(I32, (ne, tb), 0)
    iota_r = lax.broadcasted_iota(I32, (r_loc, tb), 0)
    for u in range(nbuf):
        blk = g * nbuf + u
        cols = slice(u * tb, (u + 1) * tb)
        sent(jnp.maximum(blk - nbuf, 0), u, blk >= nbuf)
        send(jnp.maximum(blk - 1, 0), (u - 1) % nbuf, blk >= 1)

        loc_col = locv_ref[u].astype(F32)
        pos = []
        for k in range(TOP_K):
            run0 = jnp.sum(jnp.where(iota_e == idx_ref[k:k + 1, cols], loc_col, 0.0), axis=0, keepdims=True)
            pos.append(run0.astype(I32) + lrank_ref[k:k + 1, cols])
            pos_ref[k:k + 1, cols] = pos[k]

        hit = iota_r == pos[0]
        for k in range(1, TOP_K):
            hit = jnp.logical_or(hit, iota_r == pos[k])
        bufs[u][...] = _pack_halves(_dot(hit.astype(BF16), xn_ref[cols, :]))

    @pl.when(g == pl.num_programs(0) - 1)
    def _():
        send(last_blk, nbuf - 1, True)
        for s_ in range(nbuf):
            sent(last_blk - (nbuf - 1 - s_), s_, True)
        for e in range(ne):
            _wait_rows(tailn_ref[e], zrows, zbuf, xs_hbm, zsem)
        lax.fori_loop(0, tailn_ref[ne], wait_rest, 0)


def _dispatch(xn, idx_t, lrank_t, loc, tables, p_rows, r_loc):
    t, d = xn.shape
    tb, nu = ROUTE_ROWS, ROUTE_UNROLL
    nblk, ne = loc.shape
    assert nblk % nu == 0
    tok = pl.BlockSpec((TOP_K, nu * tb), lambda g, *_: (0, g))
    grid_spec = pltpu.PrefetchScalarGridSpec(
        num_scalar_prefetch=6,
        grid=(nblk // nu,),
        in_specs=[pl.BlockSpec((nu * tb, d), lambda g, *_: (g, 0)), tok, tok,
                  pl.BlockSpec((nu, ne, 1), lambda g, *_: (g, 0, 0))],
        out_specs=[pl.BlockSpec(memory_space=pl.ANY), tok],
        scratch_shapes=[pltpu.VMEM((r_loc, d // 2), U32)] * nu + [
            pltpu.VMEM((EXPERT_ROWS // 2, d // 2), U32), pltpu.SemaphoreType.DMA((nu,)), pltpu.SemaphoreType.DMA],
    )
    return pl.pallas_call(
        _dispatch_kernel,
        grid_spec=grid_spec,
        out_shape=[jax.ShapeDtypeStruct((p_rows, d // 2), U32), jax.ShapeDtypeStruct((TOP_K, t), I32)],
        compiler_params=pltpu.CompilerParams(dimension_semantics=("arbitrary",),
                                             vmem_limit_bytes=VMEM_LIMIT),
        name="dispatch",
    )(*tables, xn, idx_t, lrank_t, loc.reshape(nblk, ne, 1))


def _expert_kernel(blk0_ref, nblk_ref, nu_ref, xs_hbm, wgu_ref, bgu_ref, wdn_ref, bdn_ref, y_hbm,
                   xbuf, ybuf, zbuf, in_sem, out_sem, zsem, wgu_bf, wdn_bf):
    e = pl.program_id(0)
    bm, dh = xbuf.shape[1:]
    zrows = zbuf.shape[0]
    de = wdn_ref.shape[0]
    n_used = nu_ref[0]

    def x_copy(g, slot):
        return pltpu.make_async_copy(xs_hbm.at[pl.ds(pl.multiple_of(g * bm, bm), bm)], xbuf.at[slot],
                                     in_sem.at[slot])

    def y_copy(g, slot):
        return pltpu.make_async_copy(ybuf.at[slot], y_hbm.at[pl.ds(pl.multiple_of(g * bm, bm), bm)],
                                     out_sem.at[slot])

    @pl.when(jnp.logical_and(e == 0, n_used > 0))
    def _():
        x_copy(0, 0).start()

    wgu_bf[...] = wgu_ref[...].astype(BF16)
    wdn_bf[...] = wdn_ref[...].astype(BF16)

    def block(j, carry):
        g = blk0_ref[e] + j
        slot = g % 2

        @pl.when(g + 1 < n_used)
        def _():
            x_copy(g + 1, 1 - slot).start()

        x_copy(g, slot).wait()

        @pl.when(g >= 2)
        def _():
            y_copy(g - 2, slot).wait()

        x_lo, x_hi = _unpack_halves(xbuf[slot])

        def proj(c0, c1):
            return (_dot(x_lo, wgu_bf[0:dh, c0:c1]) + _dot(x_hi, wgu_bf[dh:2 * dh, c0:c1])
                    + bgu_ref[:, c0:c1])

        nch = 4
        ch = de // nch
        acts = []
        for c in range(nch):
            gl = jnp.minimum(proj(c * ch, (c + 1) * ch), SWIGLU_LIMIT)
            up = jnp.clip(proj(de + c * ch, de + (c + 1) * ch), -SWIGLU_LIMIT, SWIGLU_LIMIT)
            acts.append(((up + 1.0) * (gl * jax.nn.sigmoid(SWIGLU_ALPHA * gl))).astype(BF16))
        y = bdn_ref[...]
        for c in range(nch):
            y = y + _dot(acts[c], wdn_bf[c * ch:(c + 1) * ch, :])
        ybuf[slot] = _pack_halves(y.astype(BF16).astype(F32))
        y_copy(g, slot).start()
        return carry

    lax.fori_loop(0, nblk_ref[e], block, 0)

    @pl.when(e == pl.num_programs(0) - 1)
    def _():
        for back in (2, 1):
            @pl.when(n_used >= back)
            def _(back=back):
                y_copy(n_used - back, (n_used - back) % 2).wait()
        n_zero = (y_hbm.shape[0] - n_used * bm) // zrows
        zbuf[...] = jnp.zeros_like(zbuf)

        def z_copy(i):
            return pltpu.make_async_copy(
                zbuf, y_hbm.at[pl.ds(pl.multiple_of(n_used * bm + i * zrows, zrows), zrows)], zsem)

        lax.fori_loop(0, n_zero, lambda i, c: (z_copy(i).start(), c)[1], 0)
        lax.fori_loop(0, n_zero, lambda i, c: (z_copy(i).wait(), c)[1], 0)


def _experts(xs, blk0, nblk, n_used, w_gate_up, b_gate_up, w_down, b_down):
    p, dh = xs.shape
    ne, d, de2 = w_gate_up.shape
    de = de2 // 2
    bm = EXPERT_ROWS
    per_expert = lambda shape: pl.BlockSpec((None,) + shape, lambda e, *_: (e, 0, 0))
    grid_spec = pltpu.PrefetchScalarGridSpec(
        num_scalar_prefetch=3,
        grid=(ne,),
        in_specs=[pl.BlockSpec(memory_space=pl.ANY),
                  per_expert((d, de2)), per_expert((1, de2)), per_expert((de, d)), per_expert((1, d))],
        out_specs=pl.BlockSpec(memory_space=pl.ANY),
        scratch_shapes=[pltpu.VMEM((2, bm, dh), U32), pltpu.VMEM((2, bm, dh), U32),
                        pltpu.VMEM((bm // 4, dh), U32),
                        pltpu.SemaphoreType.DMA((2,)), pltpu.SemaphoreType.DMA((2,)), pltpu.SemaphoreType.DMA,
                        pltpu.VMEM((d, de2), BF16), pltpu.VMEM((de, d), BF16)],
    )
    return pl.pallas_call(
        _expert_kernel,
        grid_spec=grid_spec,
        out_shape=jax.ShapeDtypeStruct((p, dh), U32),
        compiler_params=pltpu.CompilerParams(dimension_semantics=("arbitrary",),
                                             vmem_limit_bytes=VMEM_LIMIT),
        name="experts",
    )(blk0, nblk, n_used, xs, w_gate_up, b_gate_up.reshape(ne, 1, de2), w_down, b_down.reshape(ne, 1, d))


def _combine_kernel(n8_ref, loc_ref, run_ref, tot_ref,
                    pos_ref, gate_ref, x2_ref, gf_ref, y_hbm, o_ref, *scratch):
    *bufs, sems = scratch
    g = pl.program_id(0)
    nbuf = len(bufs)
    r_loc = bufs[0].shape[0]
    tb = x2_ref.shape[0] // nbuf
    nblk = pl.num_programs(0) * nbuf
    ne = n8_ref.shape[0] // tot_ref.shape[0]

    def fetch(blk, s, live):
        for e in range(ne):
            n = jnp.where(live, n8_ref[blk * ne + e], 0)
            dst0 = loc_ref[blk * ne + e]
            src0 = run_ref[blk * ne + e]
            _for_each_chunk(n, tb, lambda off, size, src0=src0, dst0=dst0: pltpu.make_async_copy(
                y_hbm.at[pl.ds(pl.multiple_of(src0 + off, RUN_ALIGN), size)],
                bufs[s].at[pl.ds(pl.multiple_of(dst0 + off, RUN_ALIGN), size)], sems.at[s]).start())

    @pl.when(g == 0)
    def _():
        for buf in bufs:
            buf[...] = jnp.zeros_like(buf)
        for ahead in range(nbuf - 2):
            fetch(ahead, ahead, True)

    iota_c = lax.broadcasted_iota(I32, (tb, r_loc), 1)
    for u in range(nbuf):
        blk = g * nbuf + u
        rows = slice(u * tb, (u + 1) * tb)
        _wait_rows(tot_ref[blk], r_loc, y_hbm, bufs[u], sems.at[u])
        ahead = blk + nbuf - 2
        fetch(jnp.minimum(ahead, nblk - 1), (u - 2) % nbuf, ahead < nblk)

        w = jnp.zeros((tb, r_loc), F32)
        for k in range(TOP_K):
            w = w + jnp.where(iota_c == pos_ref[rows, k:k + 1], gate_ref[rows, k:k + 1], 0.0)
        w_hi = w.astype(BF16)
        w_lo = (w - w_hi.astype(F32)).astype(BF16)
        y_lo, y_hi = _unpack_halves(bufs[u][...])
        moe = jnp.concatenate([_dot(w_hi, y_lo) + _dot(w_lo, y_lo), _dot(w_hi, y_hi) + _dot(w_lo, y_hi)], axis=1)
        o_ref[rows, :] = _rms(x2_ref[rows, :] + moe, gf_ref[...])


def _combine(pos, gates, x2, y, norm_final, tables, r_loc):
    t, d = x2.shape
    rows = ROUTE_ROWS * ROUTE_UNROLL
    assert t % rows == 0
    grid_spec = pltpu.PrefetchScalarGridSpec(
        num_scalar_prefetch=4,
        grid=(t // rows,),
        in_specs=[pl.BlockSpec((rows, TOP_K), lambda g, *_: (g, 0)),
                  pl.BlockSpec((rows, TOP_K), lambda g, *_: (g, 0)),
                  pl.BlockSpec((rows, d), lambda g, *_: (g, 0)),
                  pl.BlockSpec((1, d), lambda g, *_: (0, 0)),
                  pl.BlockSpec(memory_space=pl.ANY)],
        out_specs=pl.BlockSpec((rows, d), lambda g, *_: (g, 0)),
        scratch_shapes=[pltpu.VMEM((r_loc, d // 2), U32)] * ROUTE_UNROLL + [pltpu.SemaphoreType.DMA((ROUTE_UNROLL,))],
    )
    return pl.pallas_call(
        _combine_kernel,
        grid_spec=grid_spec,
        out_shape=jax.ShapeDtypeStruct((t, d), F32),
        compiler_params=pltpu.CompilerParams(dimension_semantics=("arbitrary",),
                                             vmem_limit_bytes=VMEM_LIMIT),
        name="combine",
    )(*tables, pos, gates, x2, norm_final.reshape(1, d), y)


def _excl_cumsum(a, axis):
    n = a.shape[axis]
    a = jnp.moveaxis(a, axis, -1)
    earlier = jnp.arange(n)[None, :] < jnp.arange(n)[:, None]
    out = jnp.sum(jnp.where(earlier, a[..., None, :], 0), axis=-1)
    return jnp.moveaxis(out, -1, axis)


def _layout(cnt):
    nblk, ne = cnt.shape
    tb, bm = ROUTE_ROWS, EXPERT_ROWS
    n8 = (cnt + RUN_ALIGN - 1) // RUN_ALIGN * RUN_ALIGN
    loc = _excl_cumsum(n8, 1)
    tot = jnp.sum(n8, axis=1)
    size = jnp.sum(n8, axis=0)
    padded = (size + bm - 1) // bm * bm
    pstart = _excl_cumsum(padded, 0)
    pend = pstart + padded
    run = pstart[None, :] + _excl_cumsum(n8, 0)
    p_blocks = -(-(nblk * tb * TOP_K + nblk * ne * (RUN_ALIGN - 1) + ne * (bm - RUN_ALIGN)) // bm)
    n_used = (pend[-1] // bm).astype(I32)
    flat = lambda a: a.reshape(-1).astype(I32)
    tables = (flat(n8), flat(loc), flat(run), flat(tot))
    half = bm // 2
    tails = (flat(jnp.concatenate([padded - size, (p_blocks * bm - pend[-1:]) // half])),
             flat(jnp.concatenate([pstart + size, pend[-1:]])))
    r_loc = -(-(tb * TOP_K + ne * (RUN_ALIGN - 1)) // 256) * 256
    blocks = ((pstart // bm).astype(I32), (padded // bm).astype(I32), n_used.reshape(1))
    return tables, tails, loc.astype(I32), blocks, p_blocks * bm, r_loc


def kernel(x, mem, norm_mix, w_in, w_pool_group, pool_scale, w_pool_proj, conv_w, conv_b, lru_w_a, lru_b_a, lru_w_x, lru_b_x, lru_lambda, w_lru_proj, w_mix_out, norm_xattn, norm_mem, w_q, w_kv, w_o, norm_moe, w_router, b_router, w_gate_up, b_gate_up, w_down, b_down, norm_final):
    nb, s_len, d = x.shape
    m_len = mem.shape[1]
    assert norm_mix.shape[0] == 1, "single-layer stack"
    l = 0
    x1 = _mixer(x, norm_mix[l], w_in[l], w_pool_group[l], pool_scale[l], w_pool_proj[l],
                conv_w[l], conv_b[l], lru_w_a[l], lru_b_a[l], lru_w_x[l], lru_b_x[l],
                lru_lambda[l], w_lru_proj[l], w_mix_out[l])
    kv = _kv_proj(mem.reshape(nb * m_len, d), norm_mem[l], w_kv[l]).reshape(nb, m_len, 2 * d)
    x2, xn, idx_t, gate_t, lrank_t, cnt = _attention(
        x1.reshape(s_len, nb, d), kv, norm_xattn[l], w_q[l], w_o[l],
        norm_moe[l], w_router[l], b_router[l])
    tables, tails, loc, blocks, p_rows, r_loc = _layout(cnt.reshape(cnt.shape[0], -1))
    xs, pos_t = _dispatch(xn, idx_t, lrank_t, loc, tables + tails, p_rows, r_loc)
    y = _experts(xs, *blocks, w_gate_up[l], b_gate_up[l], w_down[l], b_down[l])
    out = _combine(pos_t.T, gate_t.T, x2.reshape(nb * s_len, d), y, norm_final, tables, r_loc)
    return out.reshape(nb, s_len, d)
```

```python
import functools

import jax
import jax.numpy as jnp
from jax import lax
from jax.experimental import pallas as pl
from jax.experimental.pallas import tpu as pltpu

POOL_WINDOWS = (2, 4, 8, 16)
N_GROUPS = 4
CONV_WIDTH = 4
RG_C = 8.0
N_EXPERTS = 32
TOP_K = 4
SWIGLU_LIMIT = 7.0
SWIGLU_ALPHA = 1.702
RMS_EPS = 1e-6

MIX_STEPS = 32
KV_ROWS = 512
ATT_ROWS = 512
ROUTE_ROWS = 256
ROUTE_UNROLL = 4
EXPERT_ROWS = 512
RUN_ALIGN = 8
VMEM_LIMIT = 52 * 1024 * 1024

BF16 = jnp.bfloat16
F32 = jnp.float32
I32 = jnp.int32
U32 = jnp.uint32


def _const_spec(shape):
    nd = len(shape)
    return pl.BlockSpec(shape, lambda *_: (0,) * nd, pipeline_mode=pl.Buffered(1))


def _rms(x, g):
    return x * lax.rsqrt(jnp.mean(x * x, axis=-1, keepdims=True) + RMS_EPS) * g


def _dot(a, b):
    return jnp.dot(a, b, preferred_element_type=F32)


def _dot_nt(a, b):
    return lax.dot_general(a, b, (((1,), (1,)), ((), ())), preferred_element_type=F32)


def _pack_halves(x):
    c = x.shape[1] // 2
    lo = lax.bitcast_convert_type(x[:, :c], U32)
    hi = lax.bitcast_convert_type(x[:, c:], U32)
    return (hi & jnp.uint32(0xFFFF0000)) | (lo >> 16)


def _unpack_halves(u):
    lo = lax.bitcast_convert_type(u << 16, F32).astype(BF16)
    hi = lax.bitcast_convert_type(u & jnp.uint32(0xFFFF0000), F32).astype(BF16)
    return lo, hi


def _mixer_kernel(x_hbm, nm_ref, win_ref, wpg_ref, psc_ref, wpp_ref, cw_ref, cb_ref,
                  wa_ref, ba_ref, wx_ref, bx_ref, lam_ref, wlp_ref, wmo_ref,
                  o_ref,
                  xbuf, xsem,
                  h_ref, up_ref, ul_ref, a_ref, b_ref, pm_ref, m_ref, mb_ref, t_ref, gl_ref, gb_ref, hc_ref,
                  *, nb, ts):
    rows, d = o_ref.shape
    gw = d // N_GROUPS
    halo_p = (POOL_WINDOWS[-1]) * nb
    halo_c = (CONV_WIDTH - 1) * nb
    c = pl.program_id(0)

    @pl.when(c == 0)
    def _():
        up_ref[0:halo_p, :] = jnp.zeros((halo_p, d), F32)
        ul_ref[0:halo_c, :] = jnp.zeros((halo_c, d), F32)
        hc_ref[...] = jnp.zeros_like(hc_ref)

    def x_copy(step, t, slot):
        return pltpu.make_async_copy(x_hbm.at[:, step * ts + t, :], xbuf.at[slot, t], xsem.at[slot])

    def fetch(step, slot):
        lax.fori_loop(0, ts, lambda t, carry: (x_copy(step, t, slot).start(), carry)[1], 0)

    slot = c % 2

    @pl.when(c == 0)
    def _():
        fetch(0, 0)

    @pl.when(c + 1 < pl.num_programs(0))
    def _():
        fetch(c + 1, 1 - slot)

    lax.fori_loop(0, ts, lambda t, carry: (x_copy(c, t, slot).wait(), carry)[1], 0)

    h_ref[...] = _rms(xbuf[slot].reshape(rows, d), nm_ref[...]).astype(BF16)

    t_glob = c * ts + lax.broadcasted_iota(I32, (rows, 1), 0) // nb

    up_ref[halo_p:halo_p + rows, :] = _dot(h_ref[...], win_ref[:, 0:d])
    ul_ref[halo_c:halo_c + rows, :] = _dot(h_ref[...], win_ref[:, d:2 * d])

    for g, w in enumerate(POOL_WINDOWS):
        cols = slice(g * gw, (g + 1) * gw)
        u = up_ref[halo_p:halo_p + rows, cols]
        acc = u
        for j in range(1, w):
            acc = acc + up_ref[halo_p - j * nb:halo_p - j * nb + rows, cols]
        cnt = jnp.minimum(t_glob + 1, w).astype(F32)
        p = acc / cnt - u
        pg = _dot(p.astype(BF16), wpg_ref[g]) * psc_ref[:, cols]
        pm_ref[:, cols] = pg.astype(BF16)
    t_ref[...] = _dot(h_ref[...], win_ref[:, 3 * d:4 * d])
    m_ref[...] = _dot(pm_ref[...], wpp_ref[...])

    for g in range(N_GROUPS):
        cols = slice(g * gw, (g + 1) * gw)
        xr = cb_ref[:, cols]
        for k in range(CONV_WIDTH):
            off = halo_c - (CONV_WIDTH - 1 - k) * nb
            xr = xr + ul_ref[off:off + rows, cols] * cw_ref[k:k + 1, cols]
        xrb = xr.astype(BF16)
        r = jax.nn.sigmoid(_dot(xrb, wa_ref[g]) + ba_ref[:, cols])
        i = jax.nn.sigmoid(_dot(xrb, wx_ref[g]) + bx_ref[:, cols])
        gl_ref[:, cols] = _dot(h_ref[...], win_ref[:, 2 * d + g * gw:2 * d + (g + 1) * gw])
        gb_ref[:, cols] = _dot(h_ref[...], win_ref[:, 4 * d + g * gw:4 * d + (g + 1) * gw])
        lam = lam_ref[:, cols]
        log_sig = jnp.minimum(lam, 0.0) - jnp.log(1.0 + jnp.exp(-jnp.abs(lam)))
        a = jnp.exp((RG_C * r) * log_sig)
        mult = jnp.sqrt(jnp.maximum(1.0 - a * a, 0.0))
        mult = jnp.where(t_glob == 0, 1.0, mult)
        a_ref[:, cols] = a
        b_ref[:, cols] = mult * i * xr
    m_ref[...] = jax.nn.sigmoid(t_ref[...]) * m_ref[...]

    def scan_step(t, hprev):
        sl = pl.ds(pl.multiple_of(t * nb, nb), nb)
        hn = a_ref[sl, :] * hprev + b_ref[sl, :]
        b_ref[sl, :] = hn
        return hn

    hc_ref[...] = lax.fori_loop(0, ts, scan_step, hc_ref[...], unroll=4)

    half = rows // 2
    for hs in (slice(0, half), slice(half, rows)):
        pm_ref[hs, :] = (b_ref[hs, :] * jax.nn.gelu(gl_ref[hs, :], approximate=True)).astype(BF16)
    for hs in (slice(0, half), slice(half, rows)):
        yb = _dot(pm_ref[hs, :], wlp_ref[...])
        mb_ref[hs, :] = (m_ref[hs, :] + jax.nn.sigmoid(gb_ref[hs, :]) * yb).astype(BF16)
    for i, hs in enumerate((slice(0, half), slice(half, rows))):
        x_half = xbuf[slot, i * (ts // 2):(i + 1) * (ts // 2)].reshape(half, d)
        o_ref[hs, :] = x_half + _dot(mb_ref[hs, :], wmo_ref[...])

    up_ref[0:halo_p, :] = up_ref[rows:rows + halo_p, :]
    ul_ref[0:halo_c, :] = ul_ref[rows:rows + halo_c, :]


def _mixer(x, norm_mix, w_in, w_pool_group, pool_scale, w_pool_proj, conv_w, conv_b,
           lru_w_a, lru_b_a, lru_w_x, lru_b_x, lru_lambda, w_lru_proj, w_mix_out):
    nb, s_len, d = x.shape
    ts = MIX_STEPS
    rows = ts * nb
    n_rows = s_len * nb
    assert s_len % ts == 0 and ts % 2 == 0 and ts >= POOL_WINDOWS[-1] and nb % 8 == 0
    row2 = lambda v: v.reshape(1, -1)
    args = (x, row2(norm_mix), w_in.astype(BF16), w_pool_group.astype(BF16), row2(pool_scale),
            w_pool_proj.astype(BF16), conv_w, row2(conv_b), lru_w_a.astype(BF16), row2(lru_b_a),
            lru_w_x.astype(BF16), row2(lru_b_x), row2(lru_lambda), w_lru_proj.astype(BF16),
            w_mix_out.astype(BF16))
    in_specs = [pl.BlockSpec(memory_space=pl.ANY)] + [_const_spec(a.shape) for a in args[1:]]
    halo_p = POOL_WINDOWS[-1] * nb
    halo_c = (CONV_WIDTH - 1) * nb
    return pl.pallas_call(
        functools.partial(_mixer_kernel, nb=nb, ts=ts),
        grid=(n_rows // rows,),
        in_specs=in_specs,
        out_specs=pl.BlockSpec((rows, d), lambda c: (c, 0)),
        out_shape=jax.ShapeDtypeStruct((n_rows, d), F32),
        scratch_shapes=[
            pltpu.VMEM((2, ts, nb, d), F32),
            pltpu.SemaphoreType.DMA((2,)),
            pltpu.VMEM((rows, d), BF16),
            pltpu.VMEM((halo_p + rows, d), F32),
            pltpu.VMEM((halo_c + rows, d), F32),
            pltpu.VMEM((rows, d), F32),
            pltpu.VMEM((rows, d), F32),
            pltpu.VMEM((rows, d), BF16),
            pltpu.VMEM((rows, d), F32),
            pltpu.VMEM((rows, d), BF16),
            pltpu.VMEM((rows, d), F32),
            pltpu.VMEM((rows, d), F32),
            pltpu.VMEM((rows, d), F32),
            pltpu.VMEM((nb, d), F32),
        ],
        compiler_params=pltpu.CompilerParams(dimension_semantics=("arbitrary",),
                                             vmem_limit_bytes=VMEM_LIMIT),
        name="mixer",
    )(*args)


def _kv_kernel(m_ref, g_ref, w_ref, o_ref):
    o_ref[...] = _dot(_rms(m_ref[...], g_ref[...]).astype(BF16), w_ref[...]).astype(BF16)


def _kv_proj(mem2d, norm_mem, w_kv):
    n, d = mem2d.shape
    assert n % KV_ROWS == 0
    return pl.pallas_call(
        _kv_kernel,
        grid=(n // KV_ROWS,),
        in_specs=[pl.BlockSpec((KV_ROWS, d), lambda i: (i, 0)),
                  _const_spec((1, d)), _const_spec((d, 2 * d))],
        out_specs=pl.BlockSpec((KV_ROWS, 2 * d), lambda i: (i, 0)),
        out_shape=jax.ShapeDtypeStruct((n, 2 * d), BF16),
        compiler_params=pltpu.CompilerParams(dimension_semantics=("arbitrary",),
                                             vmem_limit_bytes=VMEM_LIMIT),
        name="kv_proj",
    )(mem2d, norm_mem.reshape(1, d), w_kv.astype(BF16))


def _attn_kernel(x_hbm, kv_ref, gx_ref, wq_ref, wo_ref, gm_ref, wr_ref, br_ref,
                 x2_ref, xn_ref, idx_ref, gate_ref, lrank_ref, cnt_ref,
                 xbuf, xsem, o_scr, xh_scr, xl_scr, *, nq):
    tq, d = x2_ref.shape
    hd = d // N_GROUPS
    ne = br_ref.shape[0]
    tb = ROUTE_ROWS
    n = pl.program_id(0)
    n_blocks = pl.num_programs(0) - 1
    cur = jnp.minimum(n, n_blocks - 1)
    slot = cur % 2
    prev = (n + 1) % 2

    def x_copy(blk, s):
        return pltpu.make_async_copy(x_hbm.at[pl.ds((blk % nq) * tq, tq), blk // nq, :], xbuf.at[s], xsem.at[s])

    @pl.when(n == 0)
    def _():
        x_copy(0, 0).start()
        xh_scr[...] = jnp.zeros_like(xh_scr)
        xl_scr[...] = jnp.zeros_like(xl_scr)

    @pl.when(n + 1 < n_blocks)
    def _():
        x_copy(n + 1, 1 - slot).start()

    @pl.when(n < n_blocks)
    def _():
        x_copy(n, slot).wait()

    x = xbuf[slot]

    works = []
    for sb in range(tq // tb):
        rows = slice(sb * tb, (sb + 1) * tb)
        ph = _dot_nt(wr_ref[...], xh_scr[prev, rows, :])
        pl_ = _dot_nt(wr_ref[0:ne, :], xl_scr[prev, rows, :])
        works.append(ph[0:ne] + ph[ne:2 * ne] + pl_ + br_ref[...])

    q = _dot(_rms(x, gx_ref[...]).astype(BF16), wq_ref[...]).astype(BF16)

    iota_f = lax.broadcasted_iota(I32, (ne, tb), 0).astype(F32)
    picked = []
    for work in works:
        vals, idxs, sels = [], [], []
        for _ in range(TOP_K):
            m = jnp.max(work, axis=0, keepdims=True)
            idx = jnp.min(jnp.where(work == m, iota_f, float(ne)), axis=0, keepdims=True)
            sel = iota_f == idx
            vals.append(m)
            idxs.append(idx.astype(I32))
            sels.append(sel)
            work = jnp.where(sel, -jnp.inf, work)
        onehot = jnp.zeros((ne, tb), F32)
        for sel in sels:
            onehot = onehot + sel.astype(F32)
        picked.append((vals, idxs, sels, onehot.astype(BF16)))

    scores = [_dot_nt(q[:, h * hd:(h + 1) * hd], kv_ref[:, h * hd:(h + 1) * hd]) * (hd ** -0.5)
              for h in range(N_GROUPS)]
    for h, s in enumerate(scores):
        v = kv_ref[:, d + h * hd:d + (h + 1) * hd]
        e = jnp.exp(s - jnp.max(s, axis=-1, keepdims=True))
        p = e / jnp.sum(e, axis=-1, keepdims=True)
        o_scr[:, h * hd:(h + 1) * hd] = _dot(p.astype(BF16), v).astype(BF16)

    before = (lax.broadcasted_iota(I32, (tb, tb), 0) < lax.broadcasted_iota(I32, (tb, tb), 1)
              ).astype(BF16)
    for sb, (vals, idxs, sels, oh16) in enumerate(picked):
        rows = slice(sb * tb, (sb + 1) * tb)
        ex = [jnp.exp(v - vals[0]) for v in vals]
        den = ex[0] + ex[1] + ex[2] + ex[3]
        prefix = _dot(oh16, before)
        lr = [jnp.sum(jnp.where(sel, prefix, 0.0), axis=0, keepdims=True).astype(I32) for sel in sels]
        idx_ref[:, rows] = jnp.concatenate(idxs, axis=0)
        gate_ref[:, rows] = jnp.concatenate([e_ / den for e_ in ex], axis=0)
        lrank_ref[:, rows] = jnp.concatenate(lr, axis=0)
        cnt_ref[sb] = _dot_nt(jnp.ones((1, tb), BF16), oh16).astype(I32)

    x2 = x + _dot(o_scr[...], wo_ref[...])
    x2_ref[...] = x2
    xn = _rms(x2, gm_ref[...])
    xh = xn.astype(BF16)
    xn_ref[...] = xh
    xh_scr[n % 2] = xh
    xl_scr[n % 2] = (xn - xh.astype(F32)).astype(BF16)


def _attention(x1, kv, norm_xattn, w_q, w_o, norm_moe, w_router, b_router):
    s_len, nb, d = x1.shape
    m = kv.shape[1]
    ne = w_router.shape[-1]
    tq, tb = ATT_ROWS, ROUTE_ROWS
    nq = s_len // tq
    n_blocks = nb * nq
    t = nb * s_len
    assert s_len % tq == 0 and tq % tb == 0
    wr_hi = w_router.astype(BF16)
    wr_lo = (w_router - wr_hi.astype(F32)).astype(BF16)
    wr2t = jnp.concatenate([wr_hi, wr_lo], axis=1).T
    cur = lambda n: jnp.minimum(n, n_blocks - 1)
    routed = lambda n: (0, jnp.maximum(n - 1, 0))
    return pl.pallas_call(
        functools.partial(_attn_kernel, nq=nq),
        grid=(n_blocks + 1,),
        in_specs=[pl.BlockSpec(memory_space=pl.ANY),
                  pl.BlockSpec((None, m, 2 * d), lambda n: (cur(n) // nq, 0, 0)),
                  _const_spec((1, d)), _const_spec((d, d)), _const_spec((d, d)),
                  _const_spec((1, d)), _const_spec((2 * ne, d)), _const_spec((ne, 1))],
        out_specs=[pl.BlockSpec((None, tq, d), lambda n: (cur(n) // nq, cur(n) % nq, 0)),
                   pl.BlockSpec((tq, d), lambda n: (cur(n), 0)),
                   pl.BlockSpec((TOP_K, tq), routed),
                   pl.BlockSpec((TOP_K, tq), routed),
                   pl.BlockSpec((TOP_K, tq), routed),
                   pl.BlockSpec((tq // tb, 1, ne), lambda n: (jnp.maximum(n - 1, 0), 0, 0))],
        out_shape=[jax.ShapeDtypeStruct((nb, s_len, d), F32),
                   jax.ShapeDtypeStruct((t, d), BF16),
                   jax.ShapeDtypeStruct((TOP_K, t), I32),
                   jax.ShapeDtypeStruct((TOP_K, t), F32),
                   jax.ShapeDtypeStruct((TOP_K, t), I32),
                   jax.ShapeDtypeStruct((t // tb, 1, ne), I32)],
        scratch_shapes=[pltpu.VMEM((2, tq, d), F32), pltpu.SemaphoreType.DMA((2,)),
                        pltpu.VMEM((tq, d), BF16), pltpu.VMEM((2, tq, d), BF16), pltpu.VMEM((2, tq, d), BF16)],
        compiler_params=pltpu.CompilerParams(dimension_semantics=("arbitrary",),
                                             vmem_limit_bytes=VMEM_LIMIT),
        name="attention",
    )(x1, kv, norm_xattn.reshape(1, d), w_q.astype(BF16), w_o.astype(BF16),
      norm_moe.reshape(1, d), wr2t, b_router.reshape(ne, 1))


def _pow2_chunks(limit):
    sizes = []
    c = RUN_ALIGN
    while c <= limit:
        sizes.append(c)
        c *= 2
    return sizes[::-1]


def _for_each_chunk(n, limit, fn):
    for size in _pow2_chunks(limit):
        @pl.when((n & size) != 0)
        def _(size=size):
            fn(pl.multiple_of(n & ~(2 * size - 1), RUN_ALIGN), size)


def _wait_rows(n, limit, src, dst, sem):
    _for_each_chunk(n, limit, lambda off, size: pltpu.make_async_copy(
        src.at[pl.ds(0, size)], dst.at[pl.ds(0, size)], sem).wait())


def _dispatch_kernel(n8_ref, loc_ref, run_ref, tot_ref, tailn_ref, tails_ref,
                     xn_ref, idx_ref, lrank_ref, locv_ref, xs_hbm, pos_ref, *scratch):
    *bufs, zbuf, sems, zsem = scratch
    g = pl.program_id(0)
    nbuf = len(bufs)
    r_loc = bufs[0].shape[0]
    tb = xn_ref.shape[0] // nbuf
    ne = tailn_ref.shape[0] - 1
    zrows = zbuf.shape[0]
    last_blk = pl.num_programs(0) * nbuf - 1

    def send(blk, s, live):
        for e in range(ne):
            n = jnp.where(live, n8_ref[blk * ne + e], 0)
            src0 = loc_ref[blk * ne + e]
            dst0 = run_ref[blk * ne + e]
            _for_each_chunk(n, tb, lambda off, size, src0=src0, dst0=dst0: pltpu.make_async_copy(
                bufs[s].at[pl.ds(pl.multiple_of(src0 + off, RUN_ALIGN), size)],
                xs_hbm.at[pl.ds(pl.multiple_of(dst0 + off, RUN_ALIGN), size)], sems.at[s]).start())

    def sent(blk, s, live):
        _wait_rows(jnp.where(live, tot_ref[blk], 0), r_loc, bufs[s], xs_hbm, sems.at[s])

    def zero_rest(i, carry):
        dst = pl.multiple_of(tails_ref[ne] + i * zrows, zrows)
        pltpu.make_async_copy(zbuf, xs_hbm.at[pl.ds(dst, zrows)], zsem).start()
        return carry

    def wait_rest(i, carry):
        pltpu.make_async_copy(zbuf, xs_hbm.at[pl.ds(0, zrows)], zsem).wait()
        return carry

    @pl.when(g == 0)
    def _():
        zbuf[...] = jnp.zeros_like(zbuf)
        for e in range(ne):
            _for_each_chunk(tailn_ref[e], zrows, lambda off, size, e=e: pltpu.make_async_copy(
                zbuf.at[pl.ds(0, size)],
                xs_hbm.at[pl.ds(pl.multiple_of(tails_ref[e] + off, RUN_ALIGN), size)], zsem).start())
        lax.fori_loop(0, tailn_ref[ne], zero_rest, 0)

    iota_e = lax.broadcasted_iota(I32, (ne, tb), 0)
    iota_r = lax.broadcasted_iota(I32, (r_loc, tb), 0)
    for u in range(nbuf):
        blk = g * nbuf + u
        cols = slice(u * tb, (u + 1) * tb)
        sent(jnp.maximum(blk - nbuf, 0), u, blk >= nbuf)
        send(jnp.maximum(blk - 1, 0), (u - 1) % nbuf, blk >= 1)

        loc_col = locv_ref[u].astype(F32)
        pos = []
        for k in range(TOP_K):
            run0 = jnp.sum(jnp.where(iota_e == idx_ref[k:k + 1, cols], loc_col, 0.0), axis=0, keepdims=True)
            pos.append(run0.astype(I32) + lrank_ref[k:k + 1, cols])
            pos_ref[k:k + 1, cols] = pos[k]

        hit = iota_r == pos[0]
        for k in range(1, TOP_K):
            hit = jnp.logical_or(hit, iota_r == pos[k])
        bufs[u][...] = _pack_halves(_dot(hit.astype(BF16), xn_ref[cols, :]))

    @pl.when(g == pl.num_programs(0) - 1)
    def _():
        send(last_blk, nbuf - 1, True)
        for s_ in range(nbuf):
            sent(last_blk - (nbuf - 1 - s_), s_, True)
        for e in range(ne):
            _wait_rows(tailn_ref[e], zrows, zbuf, xs_hbm, zsem)
        lax.fori_loop(0, tailn_ref[ne], wait_rest, 0)


def _dispatch(xn, idx_t, lrank_t, loc, tables, p_rows, r_loc):
    t, d = xn.shape
    tb, nu = ROUTE_ROWS, ROUTE_UNROLL
    nblk, ne = loc.shape
    assert nblk % nu == 0
    tok = pl.BlockSpec((TOP_K, nu * tb), lambda g, *_: (0, g))
    grid_spec = pltpu.PrefetchScalarGridSpec(
        num_scalar_prefetch=6,
        grid=(nblk // nu,),
        in_specs=[pl.BlockSpec((nu * tb, d), lambda g, *_: (g, 0)), tok, tok,
                  pl.BlockSpec((nu, ne, 1), lambda g, *_: (g, 0, 0))],
        out_specs=[pl.BlockSpec(memory_space=pl.ANY), tok],
        scratch_shapes=[pltpu.VMEM((r_loc, d // 2), U32)] * nu + [
            pltpu.VMEM((EXPERT_ROWS // 2, d // 2), U32), pltpu.SemaphoreType.DMA((nu,)), pltpu.SemaphoreType.DMA],
    )
    return pl.pallas_call(
        _dispatch_kernel,
        grid_spec=grid_spec,
        out_shape=[jax.ShapeDtypeStruct((p_rows, d // 2), U32), jax.ShapeDtypeStruct((TOP_K, t), I32)],
        compiler_params=pltpu.CompilerParams(dimension_semantics=("arbitrary",),
                                             vmem_limit_bytes=VMEM_LIMIT),
        name="dispatch",
    )(*tables, xn, idx_t, lrank_t, loc.reshape(nblk, ne, 1))


def _expert_kernel(blk0_ref, nblk_ref, nu_ref, xs_hbm, wgu_ref, bgu_ref, wdn_ref, bdn_ref, y_hbm,
                   xbuf, ybuf, zbuf, in_sem, out_sem, zsem, wgu_bf, wdn_bf):
    e = pl.program_id(0)
    bm, dh = xbuf.shape[1:]
    zrows = zbuf.shape[0]
    de = wdn_ref.shape[0]
    n_used = nu_ref[0]

    def x_copy(g, slot):
        return pltpu.make_async_copy(xs_hbm.at[pl.ds(pl.multiple_of(g * bm, bm), bm)], xbuf.at[slot],
                                     in_sem.at[slot])

    def y_copy(g, slot):
        return pltpu.make_async_copy(ybuf.at[slot], y_hbm.at[pl.ds(pl.multiple_of(g * bm, bm), bm)],
                                     out_sem.at[slot])

    @pl.when(jnp.logical_and(e == 0, n_used > 0))
    def _():
        x_copy(0, 0).start()

    n_zero = (y_hbm.shape[0] - n_used * bm) // zrows

    def z_copy(i):
        return pltpu.make_async_copy(
            zbuf, y_hbm.at[pl.ds(pl.multiple_of(n_used * bm + i * zrows, zrows), zrows)], zsem)

    @pl.when(e == 0)
    def _():
        zbuf[...] = jnp.zeros_like(zbuf)
        lax.fori_loop(0, n_zero, lambda i, c: (z_copy(i).start(), c)[1], 0)

    wgu_bf[...] = wgu_ref[...].astype(BF16)
    wdn_bf[...] = wdn_ref[...].astype(BF16)

    def block(j, carry):
        g = blk0_ref[e] + j
        slot = g % 2

        @pl.when(g + 1 < n_used)
        def _():
            x_copy(g + 1, 1 - slot).start()

        x_copy(g, slot).wait()

        @pl.when(g >= 2)
        def _():
            y_copy(g - 2, slot).wait()

        x_lo, x_hi = _unpack_halves(xbuf[slot])

        def proj(c0, c1):
            return (_dot(x_lo, wgu_bf[0:dh, c0:c1]) + _dot(x_hi, wgu_bf[dh:2 * dh, c0:c1])
                    + bgu_ref[:, c0:c1])

        nch = 4
        ch = de // nch
        acts = []
        for c in range(nch):
            gl = jnp.minimum(proj(c * ch, (c + 1) * ch), SWIGLU_LIMIT)
            up = jnp.clip(proj(de + c * ch, de + (c + 1) * ch), -SWIGLU_LIMIT, SWIGLU_LIMIT)
            acts.append(((up + 1.0) * (gl * jax.nn.sigmoid(SWIGLU_ALPHA * gl))).astype(BF16))
        y = bdn_ref[...]
        for c in range(nch):
            y = y + _dot(acts[c], wdn_bf[c * ch:(c + 1) * ch, :])
        ybuf[slot] = _pack_halves(y.astype(BF16).astype(F32))
        y_copy(g, slot).start()
        return carry

    lax.fori_loop(0, nblk_ref[e], block, 0)

    @pl.when(e == pl.num_programs(0) - 1)
    def _():
        for back in (2, 1):
            @pl.when(n_used >= back)
            def _(back=back):
                y_copy(n_used - back, (n_used - back) % 2).wait()
        lax.fori_loop(0, n_zero, lambda i, c: (z_copy(i).wait(), c)[1], 0)


def _experts(xs, blk0, nblk, n_used, w_gate_up, b_gate_up, w_down, b_down):
    p, dh = xs.shape
    ne, d, de2 = w_gate_up.shape
    de = de2 // 2
    bm = EXPERT_ROWS
    per_expert = lambda shape: pl.BlockSpec((None,) + shape, lambda e, *_: (e, 0, 0))
    grid_spec = pltpu.PrefetchScalarGridSpec(
        num_scalar_prefetch=3,
        grid=(ne,),
        in_specs=[pl.BlockSpec(memory_space=pl.ANY),
                  per_expert((d, de2)), per_expert((1, de2)), per_expert((de, d)), per_expert((1, d))],
        out_specs=pl.BlockSpec(memory_space=pl.ANY),
        scratch_shapes=[pltpu.VMEM((2, bm, dh), U32), pltpu.VMEM((2, bm, dh), U32),
                        pltpu.VMEM((bm // 4, dh), U32),
                        pltpu.SemaphoreType.DMA((2,)), pltpu.SemaphoreType.DMA((2,)), pltpu.SemaphoreType.DMA,
                        pltpu.VMEM((d, de2), BF16), pltpu.VMEM((de, d), BF16)],
    )
    return pl.pallas_call(
        _expert_kernel,
        grid_spec=grid_spec,
        out_shape=jax.ShapeDtypeStruct((p, dh), U32),
        compiler_params=pltpu.CompilerParams(dimension_semantics=("arbitrary",),
                                             vmem_limit_bytes=VMEM_LIMIT),
        name="experts",
    )(blk0, nblk, n_used, xs, w_gate_up, b_gate_up.reshape(ne, 1, de2), w_down, b_down.reshape(ne, 1, d))


def _combine_kernel(n8_ref, loc_ref, run_ref, tot_ref,
                    pos_ref, gate_ref, x2_ref, gf_ref, y_hbm, o_ref, *scratch):
    *bufs, sems = scratch
    g = pl.program_id(0)
    nbuf = len(bufs)
    r_loc = bufs[0].shape[0]
    tb = x2_ref.shape[0] // nbuf
    nblk = pl.num_programs(0) * nbuf
    ne = n8_ref.shape[0] // tot_ref.shape[0]

    def fetch(blk, s, live):
        for e in range(ne):
            n = jnp.where(live, n8_ref[blk * ne + e], 0)
            dst0 = loc_ref[blk * ne + e]
            src0 = run_ref[blk * ne + e]
            _for_each_chunk(n, tb, lambda off, size, src0=src0, dst0=dst0: pltpu.make_async_copy(
                y_hbm.at[pl.ds(pl.multiple_of(src0 + off, RUN_ALIGN), size)],
                bufs[s].at[pl.ds(pl.multiple_of(dst0 + off, RUN_ALIGN), size)], sems.at[s]).start())

    @pl.when(g == 0)
    def _():
        for buf in bufs:
            buf[...] = jnp.zeros_like(buf)
        for ahead in range(nbuf - 2):
            fetch(ahead, ahead, True)

    iota_c = lax.broadcasted_iota(I32, (tb, r_loc), 1)
    for u in range(nbuf):
        blk = g * nbuf + u
        rows = slice(u * tb, (u + 1) * tb)
        w = jnp.zeros((tb, r_loc), F32)
        for k in range(TOP_K):
            w = w + jnp.where(iota_c == pos_ref[rows, k:k + 1], gate_ref[rows, k:k + 1], 0.0)
        w_hi = w.astype(BF16)
        w_lo = (w - w_hi.astype(F32)).astype(BF16)
        _wait_rows(tot_ref[blk], r_loc, y_hbm, bufs[u], sems.at[u])
        ahead = blk + nbuf - 2
        fetch(jnp.minimum(ahead, nblk - 1), (u - 2) % nbuf, ahead < nblk)
        y_lo, y_hi = _unpack_halves(bufs[u][...])
        moe = jnp.concatenate([_dot(w_hi, y_lo) + _dot(w_lo, y_lo), _dot(w_hi, y_hi) + _dot(w_lo, y_hi)], axis=1)
        o_ref[rows, :] = _rms(x2_ref[rows, :] + moe, gf_ref[...])


def _combine(pos, gates, x2, y, norm_final, tables, r_loc):
    t, d = x2.shape
    rows = ROUTE_ROWS * ROUTE_UNROLL
    assert t % rows == 0
    grid_spec = pltpu.PrefetchScalarGridSpec(
        num_scalar_prefetch=4,
        grid=(t // rows,),
        in_specs=[pl.BlockSpec((rows, TOP_K), lambda g, *_: (g, 0)),
                  pl.BlockSpec((rows, TOP_K), lambda g, *_: (g, 0)),
                  pl.BlockSpec((rows, d), lambda g, *_: (g, 0)),
                  pl.BlockSpec((1, d), lambda g, *_: (0, 0)),
                  pl.BlockSpec(memory_space=pl.ANY)],
        out_specs=pl.BlockSpec((rows, d), lambda g, *_: (g, 0)),
        scratch_shapes=[pltpu.VMEM((r_loc, d // 2), U32)] * ROUTE_UNROLL + [pltpu.SemaphoreType.DMA((ROUTE_UNROLL,))],
    )
    return pl.pallas_call(
        _combine_kernel,
        grid_spec=grid_spec,
        out_shape=jax.ShapeDtypeStruct((t, d), F32),
        compiler_params=pltpu.CompilerParams(dimension_semantics=("arbitrary",),
                                             vmem_limit_bytes=VMEM_LIMIT),
        name="combine",
    )(*tables, pos, gates, x2, norm_final.reshape(1, d), y)


def _excl_cumsum(a, axis):
    n = a.shape[axis]
    a = jnp.moveaxis(a, axis, -1)
    earlier = jnp.arange(n)[None, :] < jnp.arange(n)[:, None]
    out = jnp.sum(jnp.where(earlier, a[..., None, :], 0), axis=-1)
    return jnp.moveaxis(out, -1, axis)


def _layout(cnt):
    nblk, ne = cnt.shape
    tb, bm = ROUTE_ROWS, EXPERT_ROWS
    n8 = (cnt + RUN_ALIGN - 1) // RUN_ALIGN * RUN_ALIGN
    loc = _excl_cumsum(n8, 1)
    tot = jnp.sum(n8, axis=1)
    size = jnp.sum(n8, axis=0)
    padded = (size + bm - 1) // bm * bm
    pstart = _excl_cumsum(padded, 0)
    pend = pstart + padded
    run = pstart[None, :] + _excl_cumsum(n8, 0)
    p_blocks = -(-(nblk * tb * TOP_K + nblk * ne * (RUN_ALIGN - 1) + ne * (bm - RUN_ALIGN)) // bm)
    n_used = (pend[-1] // bm).astype(I32)
    flat = lambda a: a.reshape(-1).astype(I32)
    tables = (flat(n8), flat(loc), flat(run), flat(tot))
    half = bm // 2
    tails = (flat(jnp.concatenate([padded - size, (p_blocks * bm - pend[-1:]) // half])),
             flat(jnp.concatenate([pstart + size, pend[-1:]])))
    r_loc = -(-(tb * TOP_K + ne * (RUN_ALIGN - 1)) // 256) * 256
    blocks = ((pstart // bm).astype(I32), (padded // bm).astype(I32), n_used.reshape(1))
    return tables, tails, loc.astype(I32), blocks, p_blocks * bm, r_loc


def kernel(x, mem, norm_mix, w_in, w_pool_group, pool_scale, w_pool_proj, conv_w, conv_b, lru_w_a, lru_b_a, lru_w_x, lru_b_x, lru_lambda, w_lru_proj, w_mix_out, norm_xattn, norm_mem, w_q, w_kv, w_o, norm_moe, w_router, b_router, w_gate_up, b_gate_up, w_down, b_down, norm_final):
    nb, s_len, d = x.shape
    m_len = mem.shape[1]
    assert norm_mix.shape[0] == 1, "single-layer stack"
    l = 0
    x1 = _mixer(x, norm_mix[l], w_in[l], w_pool_group[l], pool_scale[l], w_pool_proj[l],
                conv_w[l], conv_b[l], lru_w_a[l], lru_b_a[l], lru_w_x[l], lru_b_x[l],
                lru_lambda[l], w_lru_proj[l], w_mix_out[l])
    kv = _kv_proj(mem.reshape(nb * m_len, d), norm_mem[l], w_kv[l]).reshape(nb, m_len, 2 * d)
    x2, xn, idx_t, gate_t, lrank_t, cnt = _attention(
        x1.reshape(s_len, nb, d), kv, norm_xattn[l], w_q[l], w_o[l],
        norm_moe[l], w_router[l], b_router[l])
    tables, tails, loc, blocks, p_rows, r_loc = _layout(cnt.reshape(cnt.shape[0], -1))
    xs, pos_t = _dispatch(xn, idx_t, lrank_t, loc, tables + tails, p_rows, r_loc)
    y = _experts(xs, *blocks, w_gate_up[l], b_gate_up[l], w_down[l], b_down[l])
    out = _combine(pos_t.T, gate_t.T, x2.reshape(nb * s_len, d), y, norm_final, tables, r_loc)
    return out.reshape(nb, s_len, d)
```

```python
import functools

import jax
import jax.numpy as jnp
from jax import lax
from jax.experimental import pallas as pl
from jax.experimental.pallas import tpu as pltpu

POOL_WINDOWS = (2, 4, 8, 16)
N_GROUPS = 4
CONV_WIDTH = 4
RG_C = 8.0
N_EXPERTS = 32
TOP_K = 4
SWIGLU_LIMIT = 7.0
SWIGLU_ALPHA = 1.702
RMS_EPS = 1e-6

MIX_STEPS = 32
KV_ROWS = 512
ATT_ROWS = 512
ROUTE_ROWS = 256
ROUTE_UNROLL = 4
EXPERT_ROWS = 1024
RUN_ALIGN = 8
VMEM_LIMIT = 52 * 1024 * 1024

BF16 = jnp.bfloat16
F32 = jnp.float32
I32 = jnp.int32
U32 = jnp.uint32


def _const_spec(shape):
    nd = len(shape)
    return pl.BlockSpec(shape, lambda *_: (0,) * nd, pipeline_mode=pl.Buffered(1))


def _rms(x, g):
    return x * lax.rsqrt(jnp.mean(x * x, axis=-1, keepdims=True) + RMS_EPS) * g


def _dot(a, b):
    return jnp.dot(a, b, preferred_element_type=F32)


def _dot_nt(a, b):
    return lax.dot_general(a, b, (((1,), (1,)), ((), ())), preferred_element_type=F32)


def _pack_halves(x):
    c = x.shape[1] // 2
    lo = lax.bitcast_convert_type(x[:, :c], U32)
    hi = lax.bitcast_convert_type(x[:, c:], U32)
    return (hi & jnp.uint32(0xFFFF0000)) | (lo >> 16)


def _unpack_halves(u):
    lo = lax.bitcast_convert_type(u << 16, F32).astype(BF16)
    hi = lax.bitcast_convert_type(u & jnp.uint32(0xFFFF0000), F32).astype(BF16)
    return lo, hi


def _mixer_kernel(x_hbm, nm_ref, win_ref, wpg_ref, psc_ref, wpp_ref, cw_ref, cb_ref,
                  wa_ref, ba_ref, wx_ref, bx_ref, lam_ref, wlp_ref, wmo_ref,
                  o_ref,
                  xbuf, xsem,
                  h_ref, up_ref, ul_ref, a_ref, b_ref, pm_ref, m_ref, mb_ref, t_ref, gl_ref, gb_ref, hc_ref,
                  *, nb, ts):
    rows, d = o_ref.shape
    gw = d // N_GROUPS
    halo_p = (POOL_WINDOWS[-1]) * nb
    halo_c = (CONV_WIDTH - 1) * nb
    c = pl.program_id(0)

    @pl.when(c == 0)
    def _():
        up_ref[0:halo_p, :] = jnp.zeros((halo_p, d), F32)
        ul_ref[0:halo_c, :] = jnp.zeros((halo_c, d), F32)
        hc_ref[...] = jnp.zeros_like(hc_ref)

    def x_copy(step, t, slot):
        return pltpu.make_async_copy(x_hbm.at[:, step * ts + t, :], xbuf.at[slot, t], xsem.at[slot])

    def fetch(step, slot):
        lax.fori_loop(0, ts, lambda t, carry: (x_copy(step, t, slot).start(), carry)[1], 0)

    slot = c % 2

    @pl.when(c == 0)
    def _():
        fetch(0, 0)

    @pl.when(c + 1 < pl.num_programs(0))
    def _():
        fetch(c + 1, 1 - slot)

    lax.fori_loop(0, ts, lambda t, carry: (x_copy(c, t, slot).wait(), carry)[1], 0)

    h_ref[...] = _rms(xbuf[slot].reshape(rows, d), nm_ref[...]).astype(BF16)

    t_glob = c * ts + lax.broadcasted_iota(I32, (rows, 1), 0) // nb

    up_ref[halo_p:halo_p + rows, :] = _dot(h_ref[...], win_ref[:, 0:d])
    ul_ref[halo_c:halo_c + rows, :] = _dot(h_ref[...], win_ref[:, d:2 * d])

    for g, w in enumerate(POOL_WINDOWS):
        cols = slice(g * gw, (g + 1) * gw)
        u = up_ref[halo_p:halo_p + rows, cols]
        acc = u
        for j in range(1, w):
            acc = acc + up_ref[halo_p - j * nb:halo_p - j * nb + rows, cols]
        cnt = jnp.minimum(t_glob + 1, w).astype(F32)
        p = acc / cnt - u
        pg = _dot(p.astype(BF16), wpg_ref[g]) * psc_ref[:, cols]
        pm_ref[:, cols] = pg.astype(BF16)
    t_ref[...] = _dot(h_ref[...], win_ref[:, 3 * d:4 * d])
    m_ref[...] = _dot(pm_ref[...], wpp_ref[...])

    for g in range(N_GROUPS):
        cols = slice(g * gw, (g + 1) * gw)
        xr = cb_ref[:, cols]
        for k in range(CONV_WIDTH):
            off = halo_c - (CONV_WIDTH - 1 - k) * nb
            xr = xr + ul_ref[off:off + rows, cols] * cw_ref[k:k + 1, cols]
        xrb = xr.astype(BF16)
        r = jax.nn.sigmoid(_dot(xrb, wa_ref[g]) + ba_ref[:, cols])
        i = jax.nn.sigmoid(_dot(xrb, wx_ref[g]) + bx_ref[:, cols])
        gl_ref[:, cols] = _dot(h_ref[...], win_ref[:, 2 * d + g * gw:2 * d + (g + 1) * gw])
        gb_ref[:, cols] = _dot(h_ref[...], win_ref[:, 4 * d + g * gw:4 * d + (g + 1) * gw])
        lam = lam_ref[:, cols]
        log_sig = jnp.minimum(lam, 0.0) - jnp.log(1.0 + jnp.exp(-jnp.abs(lam)))
        a = jnp.exp((RG_C * r) * log_sig)
        mult = jnp.sqrt(jnp.maximum(1.0 - a * a, 0.0))
        mult = jnp.where(t_glob == 0, 1.0, mult)
        a_ref[:, cols] = a
        b_ref[:, cols] = mult * i * xr
    m_ref[...] = jax.nn.sigmoid(t_ref[...]) * m_ref[...]

    def scan_step(t, hprev):
        sl = pl.ds(pl.multiple_of(t * nb, nb), nb)
        hn = a_ref[sl, :] * hprev + b_ref[sl, :]
        b_ref[sl, :] = hn
        return hn

    hc_ref[...] = lax.fori_loop(0, ts, scan_step, hc_ref[...], unroll=4)

    half = rows // 2
    for hs in (slice(0, half), slice(half, rows)):
        pm_ref[hs, :] = (b_ref[hs, :] * jax.nn.gelu(gl_ref[hs, :], approximate=True)).astype(BF16)
    for hs in (slice(0, half), slice(half, rows)):
        yb = _dot(pm_ref[hs, :], wlp_ref[...])
        mb_ref[hs, :] = (m_ref[hs, :] + jax.nn.sigmoid(gb_ref[hs, :]) * yb).astype(BF16)
    for i, hs in enumerate((slice(0, half), slice(half, rows))):
        x_half = xbuf[slot, i * (ts // 2):(i + 1) * (ts // 2)].reshape(half, d)
        o_ref[hs, :] = x_half + _dot(mb_ref[hs, :], wmo_ref[...])

    up_ref[0:halo_p, :] = up_ref[rows:rows + halo_p, :]
    ul_ref[0:halo_c, :] = ul_ref[rows:rows + halo_c, :]


def _mixer(x, norm_mix, w_in, w_pool_group, pool_scale, w_pool_proj, conv_w, conv_b,
           lru_w_a, lru_b_a, lru_w_x, lru_b_x, lru_lambda, w_lru_proj, w_mix_out):
    nb, s_len, d = x.shape
    ts = MIX_STEPS
    rows = ts * nb
    n_rows = s_len * nb
    assert s_len % ts == 0 and ts % 2 == 0 and ts >= POOL_WINDOWS[-1] and nb % 8 == 0
    row2 = lambda v: v.reshape(1, -1)
    args = (x, row2(norm_mix), w_in.astype(BF16), w_pool_group.astype(BF16), row2(pool_scale),
            w_pool_proj.astype(BF16), conv_w, row2(conv_b), lru_w_a.astype(BF16), row2(lru_b_a),
            lru_w_x.astype(BF16), row2(lru_b_x), row2(lru_lambda), w_lru_proj.astype(BF16),
            w_mix_out.astype(BF16))
    in_specs = [pl.BlockSpec(memory_space=pl.ANY)] + [_const_spec(a.shape) for a in args[1:]]
    halo_p = POOL_WINDOWS[-1] * nb
    halo_c = (CONV_WIDTH - 1) * nb
    return pl.pallas_call(
        functools.partial(_mixer_kernel, nb=nb, ts=ts),
        grid=(n_rows // rows,),
        in_specs=in_specs,
        out_specs=pl.BlockSpec((rows, d), lambda c: (c, 0)),
        out_shape=jax.ShapeDtypeStruct((n_rows, d), F32),
        scratch_shapes=[
            pltpu.VMEM((2, ts, nb, d), F32),
            pltpu.SemaphoreType.DMA((2,)),
            pltpu.VMEM((rows, d), BF16),
            pltpu.VMEM((halo_p + rows, d), F32),
            pltpu.VMEM((halo_c + rows, d), F32),
            pltpu.VMEM((rows, d), F32),
            pltpu.VMEM((rows, d), F32),
            pltpu.VMEM((rows, d), BF16),
            pltpu.VMEM((rows, d), F32),
            pltpu.VMEM((rows, d), BF16),
            pltpu.VMEM((rows, d), F32),
            pltpu.VMEM((rows, d), F32),
            pltpu.VMEM((rows, d), F32),
            pltpu.VMEM((nb, d), F32),
        ],
        compiler_params=pltpu.CompilerParams(dimension_semantics=("arbitrary",),
                                             vmem_limit_bytes=VMEM_LIMIT),
        name="mixer",
    )(*args)


def _kv_kernel(m_ref, g_ref, w_ref, o_ref):
    o_ref[...] = _dot(_rms(m_ref[...], g_ref[...]).astype(BF16), w_ref[...]).astype(BF16)


def _kv_proj(mem2d, norm_mem, w_kv):
    n, d = mem2d.shape
    assert n % KV_ROWS == 0
    return pl.pallas_call(
        _kv_kernel,
        grid=(n // KV_ROWS,),
        in_specs=[pl.BlockSpec((KV_ROWS, d), lambda i: (i, 0)),
                  _const_spec((1, d)), _const_spec((d, 2 * d))],
        out_specs=pl.BlockSpec((KV_ROWS, 2 * d), lambda i: (i, 0)),
        out_shape=jax.ShapeDtypeStruct((n, 2 * d), BF16),
        compiler_params=pltpu.CompilerParams(dimension_semantics=("arbitrary",),
                                             vmem_limit_bytes=VMEM_LIMIT),
        name="kv_proj",
    )(mem2d, norm_mem.reshape(1, d), w_kv.astype(BF16))


def _attn_kernel(x_hbm, kv_ref, gx_ref, wq_ref, wo_ref, gm_ref, wr_ref, br_ref,
                 x2_ref, xn_ref, idx_ref, gate_ref, lrank_ref, cnt_ref,
                 xbuf, xsem, o_scr, xh_scr, xl_scr, *, nq):
    tq, d = x2_ref.shape
    hd = d // N_GROUPS
    ne = br_ref.shape[0]
    tb = ROUTE_ROWS
    n = pl.program_id(0)
    n_blocks = pl.num_programs(0) - 1
    cur = jnp.minimum(n, n_blocks - 1)
    slot = cur % 2
    prev = (n + 1) % 2

    def x_copy(blk, s):
        return pltpu.make_async_copy(x_hbm.at[pl.ds((blk % nq) * tq, tq), blk // nq, :], xbuf.at[s], xsem.at[s])

    @pl.when(n == 0)
    def _():
        x_copy(0, 0).start()
        xh_scr[...] = jnp.zeros_like(xh_scr)
        xl_scr[...] = jnp.zeros_like(xl_scr)

    @pl.when(n + 1 < n_blocks)
    def _():
        x_copy(n + 1, 1 - slot).start()

    @pl.when(n < n_blocks)
    def _():
        x_copy(n, slot).wait()

    x = xbuf[slot]

    works = []
    for sb in range(tq // tb):
        rows = slice(sb * tb, (sb + 1) * tb)
        ph = _dot_nt(wr_ref[...], xh_scr[prev, rows, :])
        pl_ = _dot_nt(wr_ref[0:ne, :], xl_scr[prev, rows, :])
        works.append(ph[0:ne] + ph[ne:2 * ne] + pl_ + br_ref[...])

    q = _dot(_rms(x, gx_ref[...]).astype(BF16), wq_ref[...]).astype(BF16)

    iota_f = lax.broadcasted_iota(I32, (ne, tb), 0).astype(F32)
    picked = []
    for work in works:
        vals, idxs, sels = [], [], []
        for _ in range(TOP_K):
            m = jnp.max(work, axis=0, keepdims=True)
            idx = jnp.min(jnp.where(work == m, iota_f, float(ne)), axis=0, keepdims=True)
            sel = iota_f == idx
            vals.append(m)
            idxs.append(idx.astype(I32))
            sels.append(sel)
            work = jnp.where(sel, -jnp.inf, work)
        onehot = jnp.zeros((ne, tb), F32)
        for sel in sels:
            onehot = onehot + sel.astype(F32)
        picked.append((vals, idxs, sels, onehot.astype(BF16)))

    scores = [_dot_nt(q[:, h * hd:(h + 1) * hd], kv_ref[:, h * hd:(h + 1) * hd]) * (hd ** -0.5)
              for h in range(N_GROUPS)]
    for h, s in enumerate(scores):
        v = kv_ref[:, d + h * hd:d + (h + 1) * hd]
        e = jnp.exp(s - jnp.max(s, axis=-1, keepdims=True))
        p = e / jnp.sum(e, axis=-1, keepdims=True)
        o_scr[:, h * hd:(h + 1) * hd] = _dot(p.astype(BF16), v).astype(BF16)

    before = (lax.broadcasted_iota(I32, (tb, tb), 0) < lax.broadcasted_iota(I32, (tb, tb), 1)
              ).astype(BF16)
    for sb, (vals, idxs, sels, oh16) in enumerate(picked):
        rows = slice(sb * tb, (sb + 1) * tb)
        ex = [jnp.exp(v - vals[0]) for v in vals]
        den = ex[0] + ex[1] + ex[2] + ex[3]
        prefix = _dot(oh16, before)
        lr = [jnp.sum(jnp.where(sel, prefix, 0.0), axis=0, keepdims=True).astype(I32) for sel in sels]
        idx_ref[:, rows] = jnp.concatenate(idxs, axis=0)
        gate_ref[:, rows] = jnp.concatenate([e_ / den for e_ in ex], axis=0)
        lrank_ref[:, rows] = jnp.concatenate(lr, axis=0)
        cnt_ref[sb] = _dot_nt(jnp.ones((1, tb), BF16), oh16).astype(I32)

    x2 = x + _dot(o_scr[...], wo_ref[...])
    x2_ref[...] = x2
    xn = _rms(x2, gm_ref[...])
    xh = xn.astype(BF16)
    xn_ref[...] = xh
    xh_scr[n % 2] = xh
    xl_scr[n % 2] = (xn - xh.astype(F32)).astype(BF16)


def _attention(x1, kv, norm_xattn, w_q, w_o, norm_moe, w_router, b_router):
    s_len, nb, d = x1.shape
    m = kv.shape[1]
    ne = w_router.shape[-1]
    tq, tb = ATT_ROWS, ROUTE_ROWS
    nq = s_len // tq
    n_blocks = nb * nq
    t = nb * s_len
    assert s_len % tq == 0 and tq % tb == 0
    wr_hi = w_router.astype(BF16)
    wr_lo = (w_router - wr_hi.astype(F32)).astype(BF16)
    wr2t = jnp.concatenate([wr_hi, wr_lo], axis=1).T
    cur = lambda n: jnp.minimum(n, n_blocks - 1)
    routed = lambda n: (0, jnp.maximum(n - 1, 0))
    return pl.pallas_call(
        functools.partial(_attn_kernel, nq=nq),
        grid=(n_blocks + 1,),
        in_specs=[pl.BlockSpec(memory_space=pl.ANY),
                  pl.BlockSpec((None, m, 2 * d), lambda n: (cur(n) // nq, 0, 0)),
                  _const_spec((1, d)), _const_spec((d, d)), _const_spec((d, d)),
                  _const_spec((1, d)), _const_spec((2 * ne, d)), _const_spec((ne, 1))],
        out_specs=[pl.BlockSpec((None, tq, d), lambda n: (cur(n) // nq, cur(n) % nq, 0)),
                   pl.BlockSpec((tq, d), lambda n: (cur(n), 0)),
                   pl.BlockSpec((TOP_K, tq), routed),
                   pl.BlockSpec((TOP_K, tq), routed),
                   pl.BlockSpec((TOP_K, tq), routed),
                   pl.BlockSpec((tq // tb, 1, ne), lambda n: (jnp.maximum(n - 1, 0), 0, 0))],
        out_shape=[jax.ShapeDtypeStruct((nb, s_len, d), F32),
                   jax.ShapeDtypeStruct((t, d), BF16),
                   jax.ShapeDtypeStruct((TOP_K, t), I32),
                   jax.ShapeDtypeStruct((TOP_K, t), F32),
                   jax.ShapeDtypeStruct((TOP_K, t), I32),
                   jax.ShapeDtypeStruct((t // tb, 1, ne), I32)],
        scratch_shapes=[pltpu.VMEM((2, tq, d), F32), pltpu.SemaphoreType.DMA((2,)),
                        pltpu.VMEM((tq, d), BF16), pltpu.VMEM((2, tq, d), BF16), pltpu.VMEM((2, tq, d), BF16)],
        compiler_params=pltpu.CompilerParams(dimension_semantics=("arbitrary",),
                                             vmem_limit_bytes=VMEM_LIMIT),
        name="attention",
    )(x1, kv, norm_xattn.reshape(1, d), w_q.astype(BF16), w_o.astype(BF16),
      norm_moe.reshape(1, d), wr2t, b_router.reshape(ne, 1))


def _pow2_chunks(limit):
    sizes = []
    c = RUN_ALIGN
    while c <= limit:
        sizes.append(c)
        c *= 2
    return sizes[::-1]


def _for_each_chunk(n, limit, fn):
    for size in _pow2_chunks(limit):
        @pl.when((n & size) != 0)
        def _(size=size):
            fn(pl.multiple_of(n & ~(2 * size - 1), RUN_ALIGN), size)


def _wait_rows(n, limit, src, dst, sem):
    _for_each_chunk(n, limit, lambda off, size: pltpu.make_async_copy(
        src.at[pl.ds(0, size)], dst.at[pl.ds(0, size)], sem).wait())


def _dispatch_kernel(n8_ref, loc_ref, run_ref, tot_ref, tailn_ref, tails_ref,
                     xn_ref, idx_ref, lrank_ref, locv_ref, xs_hbm, pos_ref, *scratch):
    *bufs, zbuf, sems, zsem = scratch
    g = pl.program_id(0)
    nbuf = len(bufs)
    r_loc = bufs[0].shape[0]
    tb = xn_ref.shape[0] // nbuf
    ne = tailn_ref.shape[0] - 1
    zrows = zbuf.shape[0]
    last_blk = pl.num_programs(0) * nbuf - 1

    def send(blk, s, live):
        for e in range(ne):
            n = jnp.where(live, n8_ref[blk * ne + e], 0)
            src0 = loc_ref[blk * ne + e]
            dst0 = run_ref[blk * ne + e]
            _for_each_chunk(n, tb, lambda off, size, src0=src0, dst0=dst0: pltpu.make_async_copy(
                bufs[s].at[pl.ds(pl.multiple_of(src0 + off, RUN_ALIGN), size)],
                xs_hbm.at[pl.ds(pl.multiple_of(dst0 + off, RUN_ALIGN), size)], sems.at[s]).start())

    def sent(blk, s, live):
        _wait_rows(jnp.where(live, tot_ref[blk], 0), r_loc, bufs[s], xs_hbm, sems.at[s])

    def zero_rest(i, carry):
        dst = pl.multiple_of(tails_ref[ne] + i * zrows, zrows)
        pltpu.make_async_copy(zbuf, xs_hbm.at[pl.ds(dst, zrows)], zsem).start()
        return carry

    def wait_rest(i, carry):
        pltpu.make_async_copy(zbuf, xs_hbm.at[pl.ds(0, zrows)], zsem).wait()
        return carry

    @pl.when(g == 0)
    def _():
        zbuf[...] = jnp.zeros_like(zbuf)
        for e in range(ne):
            _for_each_chunk(tailn_ref[e], zrows, lambda off, size, e=e: pltpu.make_async_copy(
                zbuf.at[pl.ds(0, size)],
                xs_hbm.at[pl.ds(pl.multiple_of(tails_ref[e] + off, RUN_ALIGN), size)], zsem).start())
        lax.fori_loop(0, tailn_ref[ne], zero_rest, 0)

    iota_e = lax.broadcasted_iota(I32, (ne, tb), 0)
    iota_r = lax.broadcasted_iota(I32, (r_loc, tb), 0)
    for u in range(nbuf):
        blk = g * nbuf + u
        cols = slice(u * tb, (u + 1) * tb)
        sent(jnp.maximum(blk - nbuf, 0), u, blk >= nbuf)
        send(jnp.maximum(blk - 1, 0), (u - 1) % nbuf, blk >= 1)

        loc_col = locv_ref[u].astype(F32)
        pos = []
        for k in range(TOP_K):
            run0 = jnp.sum(jnp.where(iota_e == idx_ref[k:k + 1, cols], loc_col, 0.0), axis=0, keepdims=True)
            pos.append(run0.astype(I32) + lrank_ref[k:k + 1, cols])
            pos_ref[k:k + 1, cols] = pos[k]

        hit = iota_r == pos[0]
        for k in range(1, TOP_K):
            hit = jnp.logical_or(hit, iota_r == pos[k])
        bufs[u][...] = _pack_halves(_dot(hit.astype(BF16), xn_ref[cols, :]))

    @pl.when(g == pl.num_programs(0) - 1)
    def _():
        send(last_blk, nbuf - 1, True)
        for s_ in range(nbuf):
            sent(last_blk - (nbuf - 1 - s_), s_, True)
        for e in range(ne):
            _wait_rows(tailn_ref[e], zrows, zbuf, xs_hbm, zsem)
        lax.fori_loop(0, tailn_ref[ne], wait_rest, 0)


def _dispatch(xn, idx_t, lrank_t, loc, tables, p_rows, r_loc):
    t, d = xn.shape
    tb, nu = ROUTE_ROWS, ROUTE_UNROLL
    nblk, ne = loc.shape
    assert nblk % nu == 0
    tok = pl.BlockSpec((TOP_K, nu * tb), lambda g, *_: (0, g))
    grid_spec = pltpu.PrefetchScalarGridSpec(
        num_scalar_prefetch=6,
        grid=(nblk // nu,),
        in_specs=[pl.BlockSpec((nu * tb, d), lambda g, *_: (g, 0)), tok, tok,
                  pl.BlockSpec((nu, ne, 1), lambda g, *_: (g, 0, 0))],
        out_specs=[pl.BlockSpec(memory_space=pl.ANY), tok],
        scratch_shapes=[pltpu.VMEM((r_loc, d // 2), U32)] * nu + [
            pltpu.VMEM((EXPERT_ROWS // 2, d // 2), U32), pltpu.SemaphoreType.DMA((nu,)), pltpu.SemaphoreType.DMA],
    )
    return pl.pallas_call(
        _dispatch_kernel,
        grid_spec=grid_spec,
        out_shape=[jax.ShapeDtypeStruct((p_rows, d // 2), U32), jax.ShapeDtypeStruct((TOP_K, t), I32)],
        compiler_params=pltpu.CompilerParams(dimension_semantics=("arbitrary",),
                                             vmem_limit_bytes=VMEM_LIMIT),
        name="dispatch",
    )(*tables, xn, idx_t, lrank_t, loc.reshape(nblk, ne, 1))


def _expert_kernel(blk0_ref, nblk_ref, nu_ref, xs_hbm, wgu_ref, bgu_ref, wdn_ref, bdn_ref, y_hbm,
                   xbuf, ybuf, zbuf, in_sem, out_sem, zsem, wgu_bf, wdn_bf):
    e = pl.program_id(0)
    bm, dh = xbuf.shape[1:]
    zrows = zbuf.shape[0]
    de = wdn_ref.shape[0]
    n_used = nu_ref[0]

    def x_copy(g, slot):
        return pltpu.make_async_copy(xs_hbm.at[pl.ds(pl.multiple_of(g * bm, bm), bm)], xbuf.at[slot],
                                     in_sem.at[slot])

    def y_copy(g, slot):
        return pltpu.make_async_copy(ybuf.at[slot], y_hbm.at[pl.ds(pl.multiple_of(g * bm, bm), bm)],
                                     out_sem.at[slot])

    @pl.when(jnp.logical_and(e == 0, n_used > 0))
    def _():
        x_copy(0, 0).start()

    n_zero = (y_hbm.shape[0] - n_used * bm) // zrows

    def z_copy(i):
        return pltpu.make_async_copy(
            zbuf, y_hbm.at[pl.ds(pl.multiple_of(n_used * bm + i * zrows, zrows), zrows)], zsem)

    @pl.when(e == 0)
    def _():
        zbuf[...] = jnp.zeros_like(zbuf)
        lax.fori_loop(0, n_zero, lambda i, c: (z_copy(i).start(), c)[1], 0)

    wgu_bf[...] = wgu_ref[...].astype(BF16)
    wdn_bf[...] = wdn_ref[...].astype(BF16)

    def block(j, carry):
        g = blk0_ref[e] + j
        slot = g % 2

        @pl.when(g + 1 < n_used)
        def _():
            x_copy(g + 1, 1 - slot).start()

        x_copy(g, slot).wait()

        @pl.when(g >= 2)
        def _():
            y_copy(g - 2, slot).wait()

        x_lo, x_hi = _unpack_halves(xbuf[slot])

        def proj(c0, c1):
            return (_dot(x_lo, wgu_bf[0:dh, c0:c1]) + _dot(x_hi, wgu_bf[dh:2 * dh, c0:c1])
                    + bgu_ref[:, c0:c1])

        nch = 4
        ch = de // nch
        acts = []
        for c in range(nch):
            gl = jnp.minimum(proj(c * ch, (c + 1) * ch), SWIGLU_LIMIT)
            up = jnp.clip(proj(de + c * ch, de + (c + 1) * ch), -SWIGLU_LIMIT, SWIGLU_LIMIT)
            acts.append(((up + 1.0) * (gl * jax.nn.sigmoid(SWIGLU_ALPHA * gl))).astype(BF16))
        y = bdn_ref[...]
        for c in range(nch):
            y = y + _dot(acts[c], wdn_bf[c * ch:(c + 1) * ch, :])
        ybuf[slot] = _pack_halves(y.astype(BF16).astype(F32))
        y_copy(g, slot).start()
        return carry

    lax.fori_loop(0, nblk_ref[e], block, 0)

    @pl.when(e == pl.num_programs(0) - 1)
    def _():
        for back in (2, 1):
            @pl.when(n_used >= back)
            def _(back=back):
                y_copy(n_used - back, (n_used - back) % 2).wait()
        lax.fori_loop(0, n_zero, lambda i, c: (z_copy(i).wait(), c)[1], 0)


def _experts(xs, blk0, nblk, n_used, w_gate_up, b_gate_up, w_down, b_down):
    p, dh = xs.shape
    ne, d, de2 = w_gate_up.shape
    de = de2 // 2
    bm = EXPERT_ROWS
    per_expert = lambda shape: pl.BlockSpec((None,) + shape, lambda e, *_: (e, 0, 0))
    grid_spec = pltpu.PrefetchScalarGridSpec(
        num_scalar_prefetch=3,
        grid=(ne,),
        in_specs=[pl.BlockSpec(memory_space=pl.ANY),
                  per_expert((d, de2)), per_expert((1, de2)), per_expert((de, d)), per_expert((1, d))],
        out_specs=pl.BlockSpec(memory_space=pl.ANY),
        scratch_shapes=[pltpu.VMEM((2, bm, dh), U32), pltpu.VMEM((2, bm, dh), U32),
                        pltpu.VMEM((bm // 4, dh), U32),
                        pltpu.SemaphoreType.DMA((2,)), pltpu.SemaphoreType.DMA((2,)), pltpu.SemaphoreType.DMA,
                        pltpu.VMEM((d, de2), BF16), pltpu.VMEM((de, d), BF16)],
    )
    return pl.pallas_call(
        _expert_kernel,
        grid_spec=grid_spec,
        out_shape=jax.ShapeDtypeStruct((p, dh), U32),
        compiler_params=pltpu.CompilerParams(dimension_semantics=("arbitrary",),
                                             vmem_limit_bytes=VMEM_LIMIT),
        name="experts",
    )(blk0, nblk, n_used, xs, w_gate_up, b_gate_up.reshape(ne, 1, de2), w_down, b_down.reshape(ne, 1, d))


def _combine_kernel(n8_ref, loc_ref, run_ref, tot_ref,
                    pos_ref, gate_ref, x2_ref, gf_ref, y_hbm, o_ref, *scratch):
    *bufs, sems = scratch
    g = pl.program_id(0)
    nbuf = len(bufs)
    r_loc = bufs[0].shape[0]
    tb = x2_ref.shape[0] // nbuf
    nblk = pl.num_programs(0) * nbuf
    ne = n8_ref.shape[0] // tot_ref.shape[0]

    def fetch(blk, s, live):
        for e in range(ne):
            n = jnp.where(live, n8_ref[blk * ne + e], 0)
            dst0 = loc_ref[blk * ne + e]
            src0 = run_ref[blk * ne + e]
            _for_each_chunk(n, tb, lambda off, size, src0=src0, dst0=dst0: pltpu.make_async_copy(
                y_hbm.at[pl.ds(pl.multiple_of(src0 + off, RUN_ALIGN), size)],
                bufs[s].at[pl.ds(pl.multiple_of(dst0 + off, RUN_ALIGN), size)], sems.at[s]).start())

    @pl.when(g == 0)
    def _():
        for buf in bufs:
            buf[...] = jnp.zeros_like(buf)
        for ahead in range(nbuf - 2):
            fetch(ahead, ahead, True)

    iota_c = lax.broadcasted_iota(I32, (tb, r_loc), 1)
    for u in range(nbuf):
        blk = g * nbuf + u
        rows = slice(u * tb, (u + 1) * tb)
        w = jnp.zeros((tb, r_loc), F32)
        for k in range(TOP_K):
            w = w + jnp.where(iota_c == pos_ref[rows, k:k + 1], gate_ref[rows, k:k + 1], 0.0)
        w_hi = w.astype(BF16)
        w_lo = (w - w_hi.astype(F32)).astype(BF16)
        _wait_rows(tot_ref[blk], r_loc, y_hbm, bufs[u], sems.at[u])
        ahead = blk + nbuf - 2
        fetch(jnp.minimum(ahead, nblk - 1), (u - 2) % nbuf, ahead < nblk)
        y_lo, y_hi = _unpack_halves(bufs[u][...])
        moe = jnp.concatenate([_dot(w_hi, y_lo) + _dot(w_lo, y_lo), _dot(w_hi, y_hi) + _dot(w_lo, y_hi)], axis=1)
        o_ref[rows, :] = _rms(x2_ref[rows, :] + moe, gf_ref[...])


def _combine(pos, gates, x2, y, norm_final, tables, r_loc):
    t, d = x2.shape
    rows = ROUTE_ROWS * ROUTE_UNROLL
    assert t % rows == 0
    grid_spec = pltpu.PrefetchScalarGridSpec(
        num_scalar_prefetch=4,
        grid=(t // rows,),
        in_specs=[pl.BlockSpec((rows, TOP_K), lambda g, *_: (g, 0)),
                  pl.BlockSpec((rows, TOP_K), lambda g, *_: (g, 0)),
                  pl.BlockSpec((rows, d), lambda g, *_: (g, 0)),
                  pl.BlockSpec((1, d), lambda g, *_: (0, 0)),
                  pl.BlockSpec(memory_space=pl.ANY)],
        out_specs=pl.BlockSpec((rows, d), lambda g, *_: (g, 0)),
        scratch_shapes=[pltpu.VMEM((r_loc, d // 2), U32)] * ROUTE_UNROLL + [pltpu.SemaphoreType.DMA((ROUTE_UNROLL,))],
    )
    return pl.pallas_call(
        _combine_kernel,
        grid_spec=grid_spec,
        out_shape=jax.ShapeDtypeStruct((t, d), F32),
        compiler_params=pltpu.CompilerParams(dimension_semantics=("arbitrary",),
                                             vmem_limit_bytes=VMEM_LIMIT),
        name="combine",
    )(*tables, pos, gates, x2, norm_final.reshape(1, d), y)


def _excl_cumsum(a, axis):
    n = a.shape[axis]
    a = jnp.moveaxis(a, axis, -1)
    earlier = jnp.arange(n)[None, :] < jnp.arange(n)[:, None]
    out = jnp.sum(jnp.where(earlier, a[..., None, :], 0), axis=-1)
    return jnp.moveaxis(out, -1, axis)


def _layout(cnt):
    nblk, ne = cnt.shape
    tb, bm = ROUTE_ROWS, EXPERT_ROWS
    n8 = (cnt + RUN_ALIGN - 1) // RUN_ALIGN * RUN_ALIGN
    loc = _excl_cumsum(n8, 1)
    tot = jnp.sum(n8, axis=1)
    size = jnp.sum(n8, axis=0)
    padded = (size + bm - 1) // bm * bm
    pstart = _excl_cumsum(padded, 0)
    pend = pstart + padded
    run = pstart[None, :] + _excl_cumsum(n8, 0)
    p_blocks = -(-(nblk * tb * TOP_K + nblk * ne * (RUN_ALIGN - 1) + ne * (bm - RUN_ALIGN)) // bm)
    n_used = (pend[-1] // bm).astype(I32)
    flat = lambda a: a.reshape(-1).astype(I32)
    tables = (flat(n8), flat(loc), flat(run), flat(tot))
    half = bm // 2
    tails = (flat(jnp.concatenate([padded - size, (p_blocks * bm - pend[-1:]) // half])),
             flat(jnp.concatenate([pstart + size, pend[-1:]])))
    r_loc = -(-(tb * TOP_K + ne * (RUN_ALIGN - 1)) // 256) * 256
    blocks = ((pstart // bm).astype(I32), (padded // bm).astype(I32), n_used.reshape(1))
    return tables, tails, loc.astype(I32), blocks, p_blocks * bm, r_loc


def kernel(x, mem, norm_mix, w_in, w_pool_group, pool_scale, w_pool_proj, conv_w, conv_b, lru_w_a, lru_b_a, lru_w_x, lru_b_x, lru_lambda, w_lru_proj, w_mix_out, norm_xattn, norm_mem, w_q, w_kv, w_o, norm_moe, w_router, b_router, w_gate_up, b_gate_up, w_down, b_down, norm_final):
    nb, s_len, d = x.shape
    m_len = mem.shape[1]
    assert norm_mix.shape[0] == 1, "single-layer stack"
    l = 0
    x1 = _mixer(x, norm_mix[l], w_in[l], w_pool_group[l], pool_scale[l], w_pool_proj[l],
                conv_w[l], conv_b[l], lru_w_a[l], lru_b_a[l], lru_w_x[l], lru_b_x[l],
                lru_lambda[l], w_lru_proj[l], w_mix_out[l])
    kv = _kv_proj(mem.reshape(nb * m_len, d), norm_mem[l], w_kv[l]).reshape(nb, m_len, 2 * d)
    x2, xn, idx_t, gate_t, lrank_t, cnt = _attention(
        x1.reshape(s_len, nb, d), kv, norm_xattn[l], w_q[l], w_o[l],
        norm_moe[l], w_router[l], b_router[l])
    tables, tails, loc, blocks, p_rows, r_loc = _layout(cnt.reshape(cnt.shape[0], -1))
    xs, pos_t = _dispatch(xn, idx_t, lrank_t, loc, tables + tails, p_rows, r_loc)
    y = _experts(xs, *blocks, w_gate_up[l], b_gate_up[l], w_down[l], b_down[l])
    out = _combine(pos_t.T, gate_t.T, x2.reshape(nb * s_len, d), y, norm_final, tables, r_loc)
    return out.reshape(nb, s_len, d)
```

```python
import functools

import jax
import jax.numpy as jnp
from jax import lax
from jax.experimental import pallas as pl
from jax.experimental.pallas import tpu as pltpu

POOL_WINDOWS = (2, 4, 8, 16)
N_GROUPS = 4
CONV_WIDTH = 4
RG_C = 8.0
N_EXPERTS = 32
TOP_K = 4
SWIGLU_LIMIT = 7.0
SWIGLU_ALPHA = 1.702
RMS_EPS = 1e-6

MIX_STEPS = 32
KV_ROWS = 512
ATT_ROWS = 512
ROUTE_ROWS = 256
ROUTE_UNROLL = 4
EXPERT_ROWS = 512
RUN_ALIGN = 8
VMEM_LIMIT = 52 * 1024 * 1024

BF16 = jnp.bfloat16
F32 = jnp.float32
I32 = jnp.int32
U32 = jnp.uint32


def _const_spec(shape):
    nd = len(shape)
    return pl.BlockSpec(shape, lambda *_: (0,) * nd, pipeline_mode=pl.Buffered(1))


def _rms(x, g):
    return x * lax.rsqrt(jnp.mean(x * x, axis=-1, keepdims=True) + RMS_EPS) * g


def _dot(a, b):
    return jnp.dot(a, b, preferred_element_type=F32)


def _dot_nt(a, b):
    return lax.dot_general(a, b, (((1,), (1,)), ((), ())), preferred_element_type=F32)


def _pack_halves(x):
    c = x.shape[1] // 2
    lo = lax.bitcast_convert_type(x[:, :c], U32)
    hi = lax.bitcast_convert_type(x[:, c:], U32)
    return (hi & jnp.uint32(0xFFFF0000)) | (lo >> 16)


def _unpack_halves(u):
    lo = lax.bitcast_convert_type(u << 16, F32).astype(BF16)
    hi = lax.bitcast_convert_type(u & jnp.uint32(0xFFFF0000), F32).astype(BF16)
    return lo, hi


def _mixer_kernel(x_hbm, nm_ref, win_ref, wpg_ref, psc_ref, wpp_ref, cw_ref, cb_ref,
                  wa_ref, ba_ref, wx_ref, bx_ref, lam_ref, wlp_ref, wmo_ref,
                  o_ref,
                  xbuf, xsem,
                  h_ref, up_ref, ul_ref, a_ref, b_ref, pm_ref, m_ref, mb_ref, t_ref, gl_ref, gb_ref, hc_ref,
                  *, nb, ts):
    rows, d = o_ref.shape
    gw = d // N_GROUPS
    halo_p = (POOL_WINDOWS[-1]) * nb
    halo_c = (CONV_WIDTH - 1) * nb
    c = pl.program_id(0)

    @pl.when(c == 0)
    def _():
        up_ref[0:halo_p, :] = jnp.zeros((halo_p, d), F32)
        ul_ref[0:halo_c, :] = jnp.zeros((halo_c, d), F32)
        hc_ref[...] = jnp.zeros_like(hc_ref)

    def x_copy(step, t, slot):
        return pltpu.make_async_copy(x_hbm.at[:, step * ts + t, :], xbuf.at[slot, t], xsem.at[slot])

    def fetch(step, slot, live):
        for t in range(ts):
            @pl.when(live)
            def _(t=t):
                x_copy(step, t, slot).start()

    slot = c % 2
    fetch(0, 0, c == 0)
    fetch(jnp.minimum(c + 1, pl.num_programs(0) - 1), 1 - slot, c + 1 < pl.num_programs(0))
    pltpu.make_async_copy(xbuf.at[1 - slot], xbuf.at[slot], xsem.at[slot]).wait()

    h_ref[...] = _rms(xbuf[slot].reshape(rows, d), nm_ref[...]).astype(BF16)

    t_glob = c * ts + lax.broadcasted_iota(I32, (rows, 1), 0) // nb

    up_ref[halo_p:halo_p + rows, :] = _dot(h_ref[...], win_ref[:, 0:d])
    ul_ref[halo_c:halo_c + rows, :] = _dot(h_ref[...], win_ref[:, d:2 * d])

    for g, w in enumerate(POOL_WINDOWS):
        cols = slice(g * gw, (g + 1) * gw)
        u = up_ref[halo_p:halo_p + rows, cols]
        acc = u
        for j in range(1, w):
            acc = acc + up_ref[halo_p - j * nb:halo_p - j * nb + rows, cols]
        cnt = jnp.minimum(t_glob + 1, w).astype(F32)
        p = acc / cnt - u
        pg = _dot(p.astype(BF16), wpg_ref[g]) * psc_ref[:, cols]
        pm_ref[:, cols] = pg.astype(BF16)
    t_ref[...] = _dot(h_ref[...], win_ref[:, 3 * d:4 * d])
    m_ref[...] = _dot(pm_ref[...], wpp_ref[...])

    for g in range(N_GROUPS):
        cols = slice(g * gw, (g + 1) * gw)
        xr = cb_ref[:, cols]
        for k in range(CONV_WIDTH):
            off = halo_c - (CONV_WIDTH - 1 - k) * nb
            xr = xr + ul_ref[off:off + rows, cols] * cw_ref[k:k + 1, cols]
        xrb = xr.astype(BF16)
        r = jax.nn.sigmoid(_dot(xrb, wa_ref[g]) + ba_ref[:, cols])
        i = jax.nn.sigmoid(_dot(xrb, wx_ref[g]) + bx_ref[:, cols])
        gl_ref[:, cols] = _dot(h_ref[...], win_ref[:, 2 * d + g * gw:2 * d + (g + 1) * gw])
        gb_ref[:, cols] = _dot(h_ref[...], win_ref[:, 4 * d + g * gw:4 * d + (g + 1) * gw])
        lam = lam_ref[:, cols]
        log_sig = jnp.minimum(lam, 0.0) - jnp.log(1.0 + jnp.exp(-jnp.abs(lam)))
        a = jnp.exp((RG_C * r) * log_sig)
        mult = jnp.sqrt(jnp.maximum(1.0 - a * a, 0.0))
        mult = jnp.where(t_glob == 0, 1.0, mult)
        a_ref[:, cols] = a
        b_ref[:, cols] = mult * i * xr
    m_ref[...] = jax.nn.sigmoid(t_ref[...]) * m_ref[...]

    def scan_step(t, hprev):
        sl = pl.ds(pl.multiple_of(t * nb, nb), nb)
        hn = a_ref[sl, :] * hprev + b_ref[sl, :]
        b_ref[sl, :] = hn
        return hn

    hc_ref[...] = lax.fori_loop(0, ts, scan_step, hc_ref[...], unroll=4)

    half = rows // 2
    for hs in (slice(0, half), slice(half, rows)):
        pm_ref[hs, :] = (b_ref[hs, :] * jax.nn.gelu(gl_ref[hs, :], approximate=True)).astype(BF16)
    for hs in (slice(0, half), slice(half, rows)):
        yb = _dot(pm_ref[hs, :], wlp_ref[...])
        mb_ref[hs, :] = (m_ref[hs, :] + jax.nn.sigmoid(gb_ref[hs, :]) * yb).astype(BF16)
    for i, hs in enumerate((slice(0, half), slice(half, rows))):
        x_half = xbuf[slot, i * (ts // 2):(i + 1) * (ts // 2)].reshape(half, d)
        o_ref[hs, :] = x_half + _dot(mb_ref[hs, :], wmo_ref[...])

    up_ref[0:halo_p, :] = up_ref[rows:rows + halo_p, :]
    ul_ref[0:halo_c, :] = ul_ref[rows:rows + halo_c, :]


def _mixer(x, norm_mix, w_in, w_pool_group, pool_scale, w_pool_proj, conv_w, conv_b,
           lru_w_a, lru_b_a, lru_w_x, lru_b_x, lru_lambda, w_lru_proj, w_mix_out):
    nb, s_len, d = x.shape
    ts = MIX_STEPS
    rows = ts * nb
    n_rows = s_len * nb
    assert s_len % ts == 0 and ts % 2 == 0 and ts >= POOL_WINDOWS[-1] and nb % 8 == 0
    row2 = lambda v: v.reshape(1, -1)
    args = (x, row2(norm_mix), w_in.astype(BF16), w_pool_group.astype(BF16), row2(pool_scale),
            w_pool_proj.astype(BF16), conv_w, row2(conv_b), lru_w_a.astype(BF16), row2(lru_b_a),
            lru_w_x.astype(BF16), row2(lru_b_x), row2(lru_lambda), w_lru_proj.astype(BF16),
            w_mix_out.astype(BF16))
    in_specs = [pl.BlockSpec(memory_space=pl.ANY)] + [_const_spec(a.shape) for a in args[1:]]
    halo_p = POOL_WINDOWS[-1] * nb
    halo_c = (CONV_WIDTH - 1) * nb
    return pl.pallas_call(
        functools.partial(_mixer_kernel, nb=nb, ts=ts),
        grid=(n_rows // rows,),
        in_specs=in_specs,
        out_specs=pl.BlockSpec((rows, d), lambda c: (c, 0)),
        out_shape=jax.ShapeDtypeStruct((n_rows, d), F32),
        scratch_shapes=[
            pltpu.VMEM((2, ts, nb, d), F32),
            pltpu.SemaphoreType.DMA((2,)),
            pltpu.VMEM((rows, d), BF16),
            pltpu.VMEM((halo_p + rows, d), F32),
            pltpu.VMEM((halo_c + rows, d), F32),
            pltpu.VMEM((rows, d), F32),
            pltpu.VMEM((rows, d), F32),
            pltpu.VMEM((rows, d), BF16),
            pltpu.VMEM((rows, d), F32),
            pltpu.VMEM((rows, d), BF16),
            pltpu.VMEM((rows, d), F32),
            pltpu.VMEM((rows, d), F32),
            pltpu.VMEM((rows, d), F32),
            pltpu.VMEM((nb, d), F32),
        ],
        compiler_params=pltpu.CompilerParams(dimension_semantics=("arbitrary",),
                                             vmem_limit_bytes=VMEM_LIMIT),
        name="mixer",
    )(*args)


def _kv_kernel(m_ref, g_ref, w_ref, o_ref):
    o_ref[...] = _dot(_rms(m_ref[...], g_ref[...]).astype(BF16), w_ref[...]).astype(BF16)


def _kv_proj(mem2d, norm_mem, w_kv):
    n, d = mem2d.shape
    assert n % KV_ROWS == 0
    return pl.pallas_call(
        _kv_kernel,
        grid=(n // KV_ROWS,),
        in_specs=[pl.BlockSpec((KV_ROWS, d), lambda i: (i, 0)),
                  _const_spec((1, d)), _const_spec((d, 2 * d))],
        out_specs=pl.BlockSpec((KV_ROWS, 2 * d), lambda i: (i, 0)),
        out_shape=jax.ShapeDtypeStruct((n, 2 * d), BF16),
        compiler_params=pltpu.CompilerParams(dimension_semantics=("arbitrary",),
                                             vmem_limit_bytes=VMEM_LIMIT),
        name="kv_proj",
    )(mem2d, norm_mem.reshape(1, d), w_kv.astype(BF16))


def _attn_kernel(x_hbm, kv_ref, gx_ref, wq_ref, wo_ref, gm_ref, wr_ref, br_ref,
                 x2_ref, xn_ref, idx_ref, gate_ref, lrank_ref, cnt_ref,
                 xbuf, xsem, o_scr, xh_scr, xl_scr, *, nq):
    tq, d = x2_ref.shape
    hd = d // N_GROUPS
    ne = br_ref.shape[0]
    tb = ROUTE_ROWS
    n = pl.program_id(0)
    n_blocks = pl.num_programs(0) - 1
    cur = jnp.minimum(n, n_blocks - 1)
    slot = cur % 2
    prev = (n + 1) % 2

    def x_copy(blk, s):
        return pltpu.make_async_copy(x_hbm.at[pl.ds((blk % nq) * tq, tq), blk // nq, :], xbuf.at[s], xsem.at[s])

    @pl.when(n == 0)
    def _():
        x_copy(0, 0).start()
        xh_scr[...] = jnp.zeros_like(xh_scr)
        xl_scr[...] = jnp.zeros_like(xl_scr)

    @pl.when(n + 1 < n_blocks)
    def _():
        x_copy(n + 1, 1 - slot).start()

    @pl.when(n < n_blocks)
    def _():
        x_copy(n, slot).wait()

    x = xbuf[slot]

    works = []
    for sb in range(tq // tb):
        rows = slice(sb * tb, (sb + 1) * tb)
        ph = _dot_nt(wr_ref[...], xh_scr[prev, rows, :])
        pl_ = _dot_nt(wr_ref[0:ne, :], xl_scr[prev, rows, :])
        works.append(ph[0:ne] + ph[ne:2 * ne] + pl_ + br_ref[...])

    q = _dot(_rms(x, gx_ref[...]).astype(BF16), wq_ref[...]).astype(BF16)

    iota_f = lax.broadcasted_iota(I32, (ne, tb), 0).astype(F32)
    picked = []
    for work in works:
        vals, idxs, sels = [], [], []
        for _ in range(TOP_K):
            m = jnp.max(work, axis=0, keepdims=True)
            idx = jnp.min(jnp.where(work == m, iota_f, float(ne)), axis=0, keepdims=True)
            sel = iota_f == idx
            vals.append(m)
            idxs.append(idx.astype(I32))
            sels.append(sel)
            work = jnp.where(sel, -jnp.inf, work)
        onehot = jnp.zeros((ne, tb), F32)
        for sel in sels:
            onehot = onehot + sel.astype(F32)
        picked.append((vals, idxs, sels, onehot.astype(BF16)))

    scores = [_dot_nt(q[:, h * hd:(h + 1) * hd], kv_ref[:, h * hd:(h + 1) * hd]) * (hd ** -0.5)
              for h in range(N_GROUPS)]
    for h, s in enumerate(scores):
        v = kv_ref[:, d + h * hd:d + (h + 1) * hd]
        e = jnp.exp(s - jnp.max(s, axis=-1, keepdims=True))
        p = e / jnp.sum(e, axis=-1, keepdims=True)
        o_scr[:, h * hd:(h + 1) * hd] = _dot(p.astype(BF16), v).astype(BF16)

    before = (lax.broadcasted_iota(I32, (tb, tb), 0) < lax.broadcasted_iota(I32, (tb, tb), 1)
              ).astype(BF16)
    for sb, (vals, idxs, sels, oh16) in enumerate(picked):
        rows = slice(sb * tb, (sb + 1) * tb)
        ex = [jnp.exp(v - vals[0]) for v in vals]
        den = ex[0] + ex[1] + ex[2] + ex[3]
        prefix = _dot(oh16, before)
        lr = [jnp.sum(jnp.where(sel, prefix, 0.0), axis=0, keepdims=True).astype(I32) for sel in sels]
        idx_ref[:, rows] = jnp.concatenate(idxs, axis=0)
        gate_ref[:, rows] = jnp.concatenate([e_ / den for e_ in ex], axis=0)
        lrank_ref[:, rows] = jnp.concatenate(lr, axis=0)
        cnt_ref[sb] = _dot_nt(jnp.ones((1, tb), BF16), oh16).astype(I32)

    x2 = x + _dot(o_scr[...], wo_ref[...])
    x2_ref[...] = x2
    xn = _rms(x2, gm_ref[...])
    xh = xn.astype(BF16)
    xn_ref[...] = xh
    xh_scr[n % 2] = xh
    xl_scr[n % 2] = (xn - xh.astype(F32)).astype(BF16)


def _attention(x1, kv, norm_xattn, w_q, w_o, norm_moe, w_router, b_router):
    s_len, nb, d = x1.shape
    m = kv.shape[1]
    ne = w_router.shape[-1]
    tq, tb = ATT_ROWS, ROUTE_ROWS
    nq = s_len // tq
    n_blocks = nb * nq
    t = nb * s_len
    assert s_len % tq == 0 and tq % tb == 0
    wr_hi = w_router.astype(BF16)
    wr_lo = (w_router - wr_hi.astype(F32)).astype(BF16)
    wr2t = jnp.concatenate([wr_hi, wr_lo], axis=1).T
    cur = lambda n: jnp.minimum(n, n_blocks - 1)
    routed = lambda n: (0, jnp.maximum(n - 1, 0))
    return pl.pallas_call(
        functools.partial(_attn_kernel, nq=nq),
        grid=(n_blocks + 1,),
        in_specs=[pl.BlockSpec(memory_space=pl.ANY),
                  pl.BlockSpec((None, m, 2 * d), lambda n: (cur(n) // nq, 0, 0)),
                  _const_spec((1, d)), _const_spec((d, d)), _const_spec((d, d)),
                  _const_spec((1, d)), _const_spec((2 * ne, d)), _const_spec((ne, 1))],
        out_specs=[pl.BlockSpec((None, tq, d), lambda n: (cur(n) // nq, cur(n) % nq, 0)),
                   pl.BlockSpec((tq, d), lambda n: (cur(n), 0)),
                   pl.BlockSpec((TOP_K, tq), routed),
                   pl.BlockSpec((TOP_K, tq), routed),
                   pl.BlockSpec((TOP_K, tq), routed),
                   pl.BlockSpec((tq // tb, 1, ne), lambda n: (jnp.maximum(n - 1, 0), 0, 0))],
        out_shape=[jax.ShapeDtypeStruct((nb, s_len, d), F32),
                   jax.ShapeDtypeStruct((t, d), BF16),
                   jax.ShapeDtypeStruct((TOP_K, t), I32),
                   jax.ShapeDtypeStruct((TOP_K, t), F32),
                   jax.ShapeDtypeStruct((TOP_K, t), I32),
                   jax.ShapeDtypeStruct((t // tb, 1, ne), I32)],
        scratch_shapes=[pltpu.VMEM((2, tq, d), F32), pltpu.SemaphoreType.DMA((2,)),
                        pltpu.VMEM((tq, d), BF16), pltpu.VMEM((2, tq, d), BF16), pltpu.VMEM((2, tq, d), BF16)],
        compiler_params=pltpu.CompilerParams(dimension_semantics=("arbitrary",),
                                             vmem_limit_bytes=VMEM_LIMIT),
        name="attention",
    )(x1, kv, norm_xattn.reshape(1, d), w_q.astype(BF16), w_o.astype(BF16),
      norm_moe.reshape(1, d), wr2t, b_router.reshape(ne, 1))


def _pow2_chunks(limit):
    sizes = []
    c = RUN_ALIGN
    while c <= limit:
        sizes.append(c)
        c *= 2
    return sizes[::-1]


def _for_each_chunk(n, limit, fn):
    for size in _pow2_chunks(limit):
        @pl.when((n & size) != 0)
        def _(size=size):
            fn(pl.multiple_of(n & ~(2 * size - 1), RUN_ALIGN), size)


def _wait_rows(n, limit, src, dst, sem):
    _for_each_chunk(n, limit, lambda off, size: pltpu.make_async_copy(
        src.at[pl.ds(0, size)], dst.at[pl.ds(0, size)], sem).wait())


def _dispatch_kernel(n8_ref, loc_ref, run_ref, tot_ref, tailn_ref, tails_ref,
                     xn_ref, idx_ref, lrank_ref, locv_ref, xs_hbm, pos_ref, *scratch):
    *bufs, zbuf, sems, zsem = scratch
    g = pl.program_id(0)
    nbuf = len(bufs)
    r_loc = bufs[0].shape[0]
    tb = xn_ref.shape[0] // nbuf
    ne = tailn_ref.shape[0] - 1
    zrows = zbuf.shape[0]
    last_blk = pl.num_programs(0) * nbuf - 1

    def send(blk, s, live):
        for e in range(ne):
            n = jnp.where(live, n8_ref[blk * ne + e], 0)
            src0 = loc_ref[blk * ne + e]
            dst0 = run_ref[blk * ne + e]
            _for_each_chunk(n, tb, lambda off, size, src0=src0, dst0=dst0: pltpu.make_async_copy(
                bufs[s].at[pl.ds(pl.multiple_of(src0 + off, RUN_ALIGN), size)],
                xs_hbm.at[pl.ds(pl.multiple_of(dst0 + off, RUN_ALIGN), size)], sems.at[s]).start())

    def sent(blk, s, live):
        _wait_rows(jnp.where(live, tot_ref[blk], 0), r_loc, bufs[s], xs_hbm, sems.at[s])

    def zero_rest(i, carry):
        dst = pl.multiple_of(tails_ref[ne] + i * zrows, zrows)
        pltpu.make_async_copy(zbuf, xs_hbm.at[pl.ds(dst, zrows)], zsem).start()
        return carry

    def wait_rest(i, carry):
        pltpu.make_async_copy(zbuf, xs_hbm.at[pl.ds(0, zrows)], zsem).wait()
        return carry

    @pl.when(g == 0)
    def _():
        zbuf[...] = jnp.zeros_like(zbuf)
        for e in range(ne):
            _for_each_chunk(tailn_ref[e], zrows, lambda off, size, e=e: pltpu.make_async_copy(
                zbuf.at[pl.ds(0, size)],
                xs_hbm.at[pl.ds(pl.multiple_of(tails_ref[e] + off, RUN_ALIGN), size)], zsem).start())
        lax.fori_loop(0, tailn_ref[ne], zero_rest, 0)

    iota_e = lax.broadcasted_iota(I32, (ne, tb), 0)
    iota_r = lax.broadcasted_iota(I32, (r_loc, tb), 0)
    for u in range(nbuf):
        blk = g * nbuf + u
        cols = slice(u * tb, (u + 1) * tb)
        sent(jnp.maximum(blk - nbuf, 0), u, blk >= nbuf)
        send(jnp.maximum(blk - 1, 0), (u - 1) % nbuf, blk >= 1)

        loc_col = locv_ref[u].astype(F32)
        pos = []
        for k in range(TOP_K):
            run0 = jnp.sum(jnp.where(iota_e == idx_ref[k:k + 1, cols], loc_col, 0.0), axis=0, keepdims=True)
            pos.append(run0.astype(I32) + lrank_ref[k:k + 1, cols])
            pos_ref[k:k + 1, cols] = pos[k]

        hit = iota_r == pos[0]
        for k in range(1, TOP_K):
            hit = jnp.logical_or(hit, iota_r == pos[k])
        bufs[u][...] = _pack_halves(_dot(hit.astype(BF16), xn_ref[cols, :]))

    @pl.when(g == pl.num_programs(0) - 1)
    def _():
        send(last_blk, nbuf - 1, True)
        for s_ in range(nbuf):
            sent(last_blk - (nbuf - 1 - s_), s_, True)
        for e in range(ne):
            _wait_rows(tailn_ref[e], zrows, zbuf, xs_hbm, zsem)
        lax.fori_loop(0, tailn_ref[ne], wait_rest, 0)


def _dispatch(xn, idx_t, lrank_t, loc, tables, p_rows, r_loc):
    t, d = xn.shape
    tb, nu = ROUTE_ROWS, ROUTE_UNROLL
    nblk, ne = loc.shape
    assert nblk % nu == 0
    tok = pl.BlockSpec((TOP_K, nu * tb), lambda g, *_: (0, g))
    grid_spec = pltpu.PrefetchScalarGridSpec(
        num_scalar_prefetch=6,
        grid=(nblk // nu,),
        in_specs=[pl.BlockSpec((nu * tb, d), lambda g, *_: (g, 0)), tok, tok,
                  pl.BlockSpec((nu, ne, 1), lambda g, *_: (g, 0, 0))],
        out_specs=[pl.BlockSpec(memory_space=pl.ANY), tok],
        scratch_shapes=[pltpu.VMEM((r_loc, d // 2), U32)] * nu + [
            pltpu.VMEM((EXPERT_ROWS // 2, d // 2), U32), pltpu.SemaphoreType.DMA((nu,)), pltpu.SemaphoreType.DMA],
    )
    return pl.pallas_call(
        _dispatch_kernel,
        grid_spec=grid_spec,
        out_shape=[jax.ShapeDtypeStruct((p_rows, d // 2), U32), jax.ShapeDtypeStruct((TOP_K, t), I32)],
        compiler_params=pltpu.CompilerParams(dimension_semantics=("arbitrary",),
                                             vmem_limit_bytes=VMEM_LIMIT),
        name="dispatch",
    )(*tables, xn, idx_t, lrank_t, loc.reshape(nblk, ne, 1))


def _expert_kernel(blk0_ref, nblk_ref, nu_ref, xs_hbm, wgu_ref, bgu_ref, wdn_ref, bdn_ref, y_hbm,
                   xbuf, ybuf, zbuf, in_sem, out_sem, zsem, wgu_bf, wdn_bf):
    e = pl.program_id(0)
    bm, dh = xbuf.shape[1:]
    zrows = zbuf.shape[0]
    de = wdn_ref.shape[0]
    n_used = nu_ref[0]

    def x_copy(g, slot):
        return pltpu.make_async_copy(xs_hbm.at[pl.ds(pl.multiple_of(g * bm, bm), bm)], xbuf.at[slot],
                                     in_sem.at[slot])

    def y_copy(g, slot):
        return pltpu.make_async_copy(ybuf.at[slot], y_hbm.at[pl.ds(pl.multiple_of(g * bm, bm), bm)],
                                     out_sem.at[slot])

    @pl.when(jnp.logical_and(e == 0, n_used > 0))
    def _():
        x_copy(0, 0).start()

    n_zero = (y_hbm.shape[0] - n_used * bm) // zrows

    def z_copy(i):
        return pltpu.make_async_copy(
            zbuf, y_hbm.at[pl.ds(pl.multiple_of(n_used * bm + i * zrows, zrows), zrows)], zsem)

    @pl.when(e == 0)
    def _():
        zbuf[...] = jnp.zeros_like(zbuf)
        lax.fori_loop(0, n_zero, lambda i, c: (z_copy(i).start(), c)[1], 0)

    wgu_bf[...] = wgu_ref[...].astype(BF16)
    wdn_bf[...] = wdn_ref[...].astype(BF16)

    def block(j, carry):
        g = blk0_ref[e] + j
        slot = g % 2

        @pl.when(g + 1 < n_used)
        def _():
            x_copy(g + 1, 1 - slot).start()

        x_copy(g, slot).wait()

        @pl.when(g >= 2)
        def _():
            y_copy(g - 2, slot).wait()

        x_lo, x_hi = _unpack_halves(xbuf[slot])

        def proj(c0, c1):
            return (_dot(x_lo, wgu_bf[0:dh, c0:c1]) + _dot(x_hi, wgu_bf[dh:2 * dh, c0:c1])
                    + bgu_ref[:, c0:c1])

        nch = 4
        ch = de // nch
        acts = []
        for c in range(nch):
            gl = jnp.minimum(proj(c * ch, (c + 1) * ch), SWIGLU_LIMIT)
            up = jnp.clip(proj(de + c * ch, de + (c + 1) * ch), -SWIGLU_LIMIT, SWIGLU_LIMIT)
            acts.append(((up + 1.0) * (gl * jax.nn.sigmoid(SWIGLU_ALPHA * gl))).astype(BF16))
        y = bdn_ref[...]
        for c in range(nch):
            y = y + _dot(acts[c], wdn_bf[c * ch:(c + 1) * ch, :])
        ybuf[slot] = _pack_halves(y.astype(BF16).astype(F32))
        y_copy(g, slot).start()
        return carry

    lax.fori_loop(0, nblk_ref[e], block, 0)

    @pl.when(e == pl.num_programs(0) - 1)
    def _():
        for back in (2, 1):
            @pl.when(n_used >= back)
            def _(back=back):
                y_copy(n_used - back, (n_used - back) % 2).wait()
        lax.fori_loop(0, n_zero, lambda i, c: (z_copy(i).wait(), c)[1], 0)


def _experts(xs, blk0, nblk, n_used, w_gate_up, b_gate_up, w_down, b_down):
    p, dh = xs.shape
    ne, d, de2 = w_gate_up.shape
    de = de2 // 2
    bm = EXPERT_ROWS
    per_expert = lambda shape: pl.BlockSpec((None,) + shape, lambda e, *_: (e, 0, 0))
    grid_spec = pltpu.PrefetchScalarGridSpec(
        num_scalar_prefetch=3,
        grid=(ne,),
        in_specs=[pl.BlockSpec(memory_space=pl.ANY),
                  per_expert((d, de2)), per_expert((1, de2)), per_expert((de, d)), per_expert((1, d))],
        out_specs=pl.BlockSpec(memory_space=pl.ANY),
        scratch_shapes=[pltpu.VMEM((2, bm, dh), U32), pltpu.VMEM((2, bm, dh), U32),
                        pltpu.VMEM((bm // 4, dh), U32),
                        pltpu.SemaphoreType.DMA((2,)), pltpu.SemaphoreType.DMA((2,)), pltpu.SemaphoreType.DMA,
                        pltpu.VMEM((d, de2), BF16), pltpu.VMEM((de, d), BF16)],
    )
    return pl.pallas_call(
        _expert_kernel,
        grid_spec=grid_spec,
        out_shape=jax.ShapeDtypeStruct((p, dh), U32),
        compiler_params=pltpu.CompilerParams(dimension_semantics=("arbitrary",),
                                             vmem_limit_bytes=VMEM_LIMIT),
        name="experts",
    )(blk0, nblk, n_used, xs, w_gate_up, b_gate_up.reshape(ne, 1, de2), w_down, b_down.reshape(ne, 1, d))


def _combine_kernel(n8_ref, loc_ref, run_ref, tot_ref,
                    pos_ref, gate_ref, x2_ref, gf_ref, y_hbm, o_ref, *scratch):
    *bufs, sems = scratch
    g = pl.program_id(0)
    nbuf = len(bufs)
    r_loc = bufs[0].shape[0]
    tb = x2_ref.shape[0] // nbuf
    nblk = pl.num_programs(0) * nbuf
    ne = n8_ref.shape[0] // tot_ref.shape[0]

    def fetch(blk, s, live):
        for e in range(ne):
            n = jnp.where(live, n8_ref[blk * ne + e], 0)
            dst0 = loc_ref[blk * ne + e]
            src0 = run_ref[blk * ne + e]
            _for_each_chunk(n, tb, lambda off, size, src0=src0, dst0=dst0: pltpu.make_async_copy(
                y_hbm.at[pl.ds(pl.multiple_of(src0 + off, RUN_ALIGN), size)],
                bufs[s].at[pl.ds(pl.multiple_of(dst0 + off, RUN_ALIGN), size)], sems.at[s]).start())

    @pl.when(g == 0)
    def _():
        for buf in bufs:
            buf[...] = jnp.zeros_like(buf)
        for ahead in range(nbuf - 2):
            fetch(ahead, ahead, True)

    iota_c = lax.broadcasted_iota(I32, (tb, r_loc), 1)
    for u in range(nbuf):
        blk = g * nbuf + u
        rows = slice(u * tb, (u + 1) * tb)
        w = jnp.zeros((tb, r_loc), F32)
        for k in range(TOP_K):
            w = w + jnp.where(iota_c == pos_ref[rows, k:k + 1], gate_ref[rows, k:k + 1], 0.0)
        w_hi = w.astype(BF16)
        w_lo = (w - w_hi.astype(F32)).astype(BF16)
        _wait_rows(tot_ref[blk], r_loc, y_hbm, bufs[u], sems.at[u])
        ahead = blk + nbuf - 2
        fetch(jnp.minimum(ahead, nblk - 1), (u - 2) % nbuf, ahead < nblk)
        y_lo, y_hi = _unpack_halves(bufs[u][...])
        moe = jnp.concatenate([_dot(w_hi, y_lo) + _dot(w_lo, y_lo), _dot(w_hi, y_hi) + _dot(w_lo, y_hi)], axis=1)
        o_ref[rows, :] = _rms(x2_ref[rows, :] + moe, gf_ref[...])


def _combine(pos, gates, x2, y, norm_final, tables, r_loc):
    t, d = x2.shape
    rows = ROUTE_ROWS * ROUTE_UNROLL
    assert t % rows == 0
    grid_spec = pltpu.PrefetchScalarGridSpec(
        num_scalar_prefetch=4,
        grid=(t // rows,),
        in_specs=[pl.BlockSpec((rows, TOP_K), lambda g, *_: (g, 0)),
                  pl.BlockSpec((rows, TOP_K), lambda g, *_: (g, 0)),
                  pl.BlockSpec((rows, d), lambda g, *_: (g, 0)),
                  pl.BlockSpec((1, d), lambda g, *_: (0, 0)),
                  pl.BlockSpec(memory_space=pl.ANY)],
        out_specs=pl.BlockSpec((rows, d), lambda g, *_: (g, 0)),
        scratch_shapes=[pltpu.VMEM((r_loc, d // 2), U32)] * ROUTE_UNROLL + [pltpu.SemaphoreType.DMA((ROUTE_UNROLL,))],
    )
    return pl.pallas_call(
        _combine_kernel,
        grid_spec=grid_spec,
        out_shape=jax.ShapeDtypeStruct((t, d), F32),
        compiler_params=pltpu.CompilerParams(dimension_semantics=("arbitrary",),
                                             vmem_limit_bytes=VMEM_LIMIT),
        name="combine",
    )(*tables, pos, gates, x2, norm_final.reshape(1, d), y)


def _excl_cumsum(a, axis):
    n = a.shape[axis]
    a = jnp.moveaxis(a, axis, -1)
    earlier = jnp.arange(n)[None, :] < jnp.arange(n)[:, None]
    out = jnp.sum(jnp.where(earlier, a[..., None, :], 0), axis=-1)
    return jnp.moveaxis(out, -1, axis)


def _layout(cnt):
    nblk, ne = cnt.shape
    tb, bm = ROUTE_ROWS, EXPERT_ROWS
    n8 = (cnt + RUN_ALIGN - 1) // RUN_ALIGN * RUN_ALIGN
    loc = _excl_cumsum(n8, 1)
    tot = jnp.sum(n8, axis=1)
    size = jnp.sum(n8, axis=0)
    padded = (size + bm - 1) // bm * bm
    pstart = _excl_cumsum(padded, 0)
    pend = pstart + padded
    run = pstart[None, :] + _excl_cumsum(n8, 0)
    p_blocks = -(-(nblk * tb * TOP_K + nblk * ne * (RUN_ALIGN - 1) + ne * (bm - RUN_ALIGN)) // bm)
    n_used = (pend[-1] // bm).astype(I32)
    flat = lambda a: a.reshape(-1).astype(I32)
    tables = (flat(n8), flat(loc), flat(run), flat(tot))
    half = bm // 2
    tails = (flat(jnp.concatenate([padded - size, (p_blocks * bm - pend[-1:]) // half])),
             flat(jnp.concatenate([pstart + size, pend[-1:]])))
    r_loc = -(-(tb * TOP_K + ne * (RUN_ALIGN - 1)) // 256) * 256
    blocks = ((pstart // bm).astype(I32), (padded // bm).astype(I32), n_used.reshape(1))
    return tables, tails, loc.astype(I32), blocks, p_blocks * bm, r_loc


def kernel(x, mem, norm_mix, w_in, w_pool_group, pool_scale, w_pool_proj, conv_w, conv_b, lru_w_a, lru_b_a, lru_w_x, lru_b_x, lru_lambda, w_lru_proj, w_mix_out, norm_xattn, norm_mem, w_q, w_kv, w_o, norm_moe, w_router, b_router, w_gate_up, b_gate_up, w_down, b_down, norm_final):
    nb, s_len, d = x.shape
    m_len = mem.shape[1]
    assert norm_mix.shape[0] == 1, "single-layer stack"
    l = 0
    x1 = _mixer(x, norm_mix[l], w_in[l], w_pool_group[l], pool_scale[l], w_pool_proj[l],
                conv_w[l], conv_b[l], lru_w_a[l], lru_b_a[l], lru_w_x[l], lru_b_x[l],
                lru_lambda[l], w_lru_proj[l], w_mix_out[l])
    kv = _kv_proj(mem.reshape(nb * m_len, d), norm_mem[l], w_kv[l]).reshape(nb, m_len, 2 * d)
    x2, xn, idx_t, gate_t, lrank_t, cnt = _attention(
        x1.reshape(s_len, nb, d), kv, norm_xattn[l], w_q[l], w_o[l],
        norm_moe[l], w_router[l], b_router[l])
    tables, tails, loc, blocks, p_rows, r_loc = _layout(cnt.reshape(cnt.shape[0], -1))
    xs, pos_t = _dispatch(xn, idx_t, lrank_t, loc, tables + tails, p_rows, r_loc)
    y = _experts(xs, *blocks, w_gate_up[l], b_gate_up[l], w_down[l], b_down[l])
    out = _combine(pos_t.T, gate_t.T, x2.reshape(nb * s_len, d), y, norm_final, tables, r_loc)
    return out.reshape(nb, s_len, d)
```

```python
import functools

import jax
import jax.numpy as jnp
from jax import lax
from jax.experimental import pallas as pl
from jax.experimental.pallas import tpu as pltpu

POOL_WINDOWS = (2, 4, 8, 16)
N_GROUPS = 4
CONV_WIDTH = 4
RG_C = 8.0
N_EXPERTS = 32
TOP_K = 4
SWIGLU_LIMIT = 7.0
SWIGLU_ALPHA = 1.702
RMS_EPS = 1e-6

MIX_STEPS = 32
KV_ROWS = 512
ATT_ROWS = 512
ROUTE_ROWS = 256
ROUTE_UNROLL = 4
EXPERT_ROWS = 512
RUN_ALIGN = 8
VMEM_LIMIT = 52 * 1024 * 1024

BF16 = jnp.bfloat16
F32 = jnp.float32
I32 = jnp.int32
U32 = jnp.uint32


def _const_spec(shape):
    nd = len(shape)
    return pl.BlockSpec(shape, lambda *_: (0,) * nd, pipeline_mode=pl.Buffered(1))


def _rms(x, g):
    return x * lax.rsqrt(jnp.mean(x * x, axis=-1, keepdims=True) + RMS_EPS) * g


def _dot(a, b):
    return jnp.dot(a, b, preferred_element_type=F32)


def _dot_nt(a, b):
    return lax.dot_general(a, b, (((1,), (1,)), ((), ())), preferred_element_type=F32)


def _pack_halves(x):
    c = x.shape[1] // 2
    lo = lax.bitcast_convert_type(x[:, :c], U32)
    hi = lax.bitcast_convert_type(x[:, c:], U32)
    return (hi & jnp.uint32(0xFFFF0000)) | (lo >> 16)


def _unpack_halves(u):
    lo = lax.bitcast_convert_type(u << 16, F32).astype(BF16)
    hi = lax.bitcast_convert_type(u & jnp.uint32(0xFFFF0000), F32).astype(BF16)
    return lo, hi


def _mixer_kernel(x_hbm, nm_ref, win_ref, wpg_ref, psc_ref, wpp_ref, cw_ref, cb_ref,
                  wa_ref, ba_ref, wx_ref, bx_ref, lam_ref, wlp_ref, wmo_ref,
                  o_ref,
                  xbuf, xsem,
                  h_ref, up_ref, ul_ref, a_ref, b_ref, pm_ref, m_ref, mb_ref, t_ref, gl_ref, gb_ref, hc_ref,
                  *, nb, ts):
    rows, d = o_ref.shape
    gw = d // N_GROUPS
    halo_p = (POOL_WINDOWS[-1]) * nb
    halo_c = (CONV_WIDTH - 1) * nb
    c = pl.program_id(0)

    @pl.when(c == 0)
    def _():
        up_ref[0:halo_p, :] = jnp.zeros((halo_p, d), F32)
        ul_ref[0:halo_c, :] = jnp.zeros((halo_c, d), F32)
        hc_ref[...] = jnp.zeros_like(hc_ref)

    def x_copy(step, t, slot):
        return pltpu.make_async_copy(x_hbm.at[:, step * ts + t, :], xbuf.at[slot, t], xsem.at[slot])

    def fetch(step, slot, live):
        for t in range(ts):
            @pl.when(live)
            def _(t=t):
                x_copy(step, t, slot).start()

    slot = c % 2
    fetch(0, 0, c == 0)
    fetch(jnp.minimum(c + 1, pl.num_programs(0) - 1), 1 - slot, c + 1 < pl.num_programs(0))
    pltpu.make_async_copy(xbuf.at[1 - slot], xbuf.at[slot], xsem.at[slot]).wait()

    h_ref[...] = _rms(xbuf[slot].reshape(rows, d), nm_ref[...]).astype(BF16)

    t_glob = c * ts + lax.broadcasted_iota(I32, (rows, 1), 0) // nb

    up_ref[halo_p:halo_p + rows, :] = _dot(h_ref[...], win_ref[:, 0:d])
    ul_ref[halo_c:halo_c + rows, :] = _dot(h_ref[...], win_ref[:, d:2 * d])

    for g, w in enumerate(POOL_WINDOWS):
        cols = slice(g * gw, (g + 1) * gw)
        u = up_ref[halo_p:halo_p + rows, cols]
        acc = u
        for j in range(1, w):
            acc = acc + up_ref[halo_p - j * nb:halo_p - j * nb + rows, cols]
        cnt = jnp.minimum(t_glob + 1, w).astype(F32)
        p = acc / cnt - u
        pg = _dot(p.astype(BF16), wpg_ref[g]) * psc_ref[:, cols]
        pm_ref[:, cols] = pg.astype(BF16)
    t_ref[...] = _dot(h_ref[...], win_ref[:, 3 * d:4 * d])
    m_ref[...] = _dot(pm_ref[...], wpp_ref[...])

    for g in range(N_GROUPS):
        cols = slice(g * gw, (g + 1) * gw)
        xr = cb_ref[:, cols]
        for k in range(CONV_WIDTH):
            off = halo_c - (CONV_WIDTH - 1 - k) * nb
            xr = xr + ul_ref[off:off + rows, cols] * cw_ref[k:k + 1, cols]
        xrb = xr.astype(BF16)
        r = jax.nn.sigmoid(_dot(xrb, wa_ref[g]) + ba_ref[:, cols])
        i = jax.nn.sigmoid(_dot(xrb, wx_ref[g]) + bx_ref[:, cols])
        gl_ref[:, cols] = _dot(h_ref[...], win_ref[:, 2 * d + g * gw:2 * d + (g + 1) * gw])
        gb_ref[:, cols] = _dot(h_ref[...], win_ref[:, 4 * d + g * gw:4 * d + (g + 1) * gw])
        lam = lam_ref[:, cols]
        log_sig = jnp.minimum(lam, 0.0) - jnp.log(1.0 + jnp.exp(-jnp.abs(lam)))
        a = jnp.exp((RG_C * r) * log_sig)
        mult = jnp.sqrt(jnp.maximum(1.0 - a * a, 0.0))
        mult = jnp.where(t_glob == 0, 1.0, mult)
        a_ref[:, cols] = a
        b_ref[:, cols] = mult * i * xr
    m_ref[...] = jax.nn.sigmoid(t_ref[...]) * m_ref[...]

    def scan_step(t, hprev):
        sl = pl.ds(pl.multiple_of(t * nb, nb), nb)
        hn = a_ref[sl, :] * hprev + b_ref[sl, :]
        b_ref[sl, :] = hn
        return hn

    hc_ref[...] = lax.fori_loop(0, ts, scan_step, hc_ref[...], unroll=4)

    half = rows // 2
    for hs in (slice(0, half), slice(half, rows)):
        pm_ref[hs, :] = (b_ref[hs, :] * jax.nn.gelu(gl_ref[hs, :], approximate=True)).astype(BF16)
    for hs in (slice(0, half), slice(half, rows)):
        yb = _dot(pm_ref[hs, :], wlp_ref[...])
        mb_ref[hs, :] = (m_ref[hs, :] + jax.nn.sigmoid(gb_ref[hs, :]) * yb).astype(BF16)
    for i, hs in enumerate((slice(0, half), slice(half, rows))):
        x_half = xbuf[slot, i * (ts // 2):(i + 1) * (ts // 2)].reshape(half, d)
        o_ref[hs, :] = x_half + _dot(mb_ref[hs, :], wmo_ref[...])

    up_ref[0:halo_p, :] = up_ref[rows:rows + halo_p, :]
    ul_ref[0:halo_c, :] = ul_ref[rows:rows + halo_c, :]


def _mixer(x, norm_mix, w_in, w_pool_group, pool_scale, w_pool_proj, conv_w, conv_b,
           lru_w_a, lru_b_a, lru_w_x, lru_b_x, lru_lambda, w_lru_proj, w_mix_out):
    nb, s_len, d = x.shape
    ts = MIX_STEPS
    rows = ts * nb
    n_rows = s_len * nb
    assert s_len % ts == 0 and ts % 2 == 0 and ts >= POOL_WINDOWS[-1] and nb % 8 == 0
    row2 = lambda v: v.reshape(1, -1)
    args = (x, row2(norm_mix), w_in.astype(BF16), w_pool_group.astype(BF16), row2(pool_scale),
            w_pool_proj.astype(BF16), conv_w, row2(conv_b), lru_w_a.astype(BF16), row2(lru_b_a),
            lru_w_x.astype(BF16), row2(lru_b_x), row2(lru_lambda), w_lru_proj.astype(BF16),
            w_mix_out.astype(BF16))
    in_specs = [pl.BlockSpec(memory_space=pl.ANY)] + [_const_spec(a.shape) for a in args[1:]]
    halo_p = POOL_WINDOWS[-1] * nb
    halo_c = (CONV_WIDTH - 1) * nb
    return pl.pallas_call(
        functools.partial(_mixer_kernel, nb=nb, ts=ts),
        grid=(n_rows // rows,),
        in_specs=in_specs,
        out_specs=pl.BlockSpec((rows, d), lambda c: (c, 0)),
        out_shape=jax.ShapeDtypeStruct((n_rows, d), F32),
        scratch_shapes=[
            pltpu.VMEM((2, ts, nb, d), F32),
            pltpu.SemaphoreType.DMA((2,)),
            pltpu.VMEM((rows, d), BF16),
            pltpu.VMEM((halo_p + rows, d), F32),
            pltpu.VMEM((halo_c + rows, d), F32),
            pltpu.VMEM((rows, d), F32),
            pltpu.VMEM((rows, d), F32),
            pltpu.VMEM((rows, d), BF16),
            pltpu.VMEM((rows, d), F32),
            pltpu.VMEM((rows, d), BF16),
            pltpu.VMEM((rows, d), F32),
            pltpu.VMEM((rows, d), F32),
            pltpu.VMEM((rows, d), F32),
            pltpu.VMEM((nb, d), F32),
        ],
        compiler_params=pltpu.CompilerParams(dimension_semantics=("arbitrary",),
                                             vmem_limit_bytes=VMEM_LIMIT),
        name="mixer",
    )(*args)


def _kv_kernel(m_ref, g_ref, w_ref, o_ref):
    o_ref[...] = _dot(_rms(m_ref[...], g_ref[...]).astype(BF16), w_ref[...]).astype(BF16)


def _kv_proj(mem2d, norm_mem, w_kv):
    n, d = mem2d.shape
    assert n % KV_ROWS == 0
    return pl.pallas_call(
        _kv_kernel,
        grid=(n // KV_ROWS,),
        in_specs=[pl.BlockSpec((KV_ROWS, d), lambda i: (i, 0)),
                  _const_spec((1, d)), _const_spec((d, 2 * d))],
        out_specs=pl.BlockSpec((KV_ROWS, 2 * d), lambda i: (i, 0)),
        out_shape=jax.ShapeDtypeStruct((n, 2 * d), BF16),
        compiler_params=pltpu.CompilerParams(dimension_semantics=("arbitrary",),
                                             vmem_limit_bytes=VMEM_LIMIT),
        name="kv_proj",
    )(mem2d, norm_mem.reshape(1, d), w_kv.astype(BF16))


def _attn_kernel(x_hbm, kv_ref, gx_ref, wq_ref, wo_ref, gm_ref, wr_ref, br_ref,
                 x2_ref, xn_ref, idx_ref, gate_ref, lrank_ref, cnt_ref,
                 xbuf, xsem, o_scr, xh_scr, xl_scr, *, nq):
    tq, d = x2_ref.shape
    hd = d // N_GROUPS
    ne = br_ref.shape[0]
    tb = ROUTE_ROWS
    n = pl.program_id(0)
    n_blocks = pl.num_programs(0) - 1
    cur = jnp.minimum(n, n_blocks - 1)
    slot = cur % 2
    prev = (n + 1) % 2

    def x_copy(blk, s):
        return pltpu.make_async_copy(x_hbm.at[pl.ds((blk % nq) * tq, tq), blk // nq, :], xbuf.at[s], xsem.at[s])

    @pl.when(n == 0)
    def _():
        x_copy(0, 0).start()
        xh_scr[...] = jnp.zeros_like(xh_scr)
        xl_scr[...] = jnp.zeros_like(xl_scr)

    @pl.when(n + 1 < n_blocks)
    def _():
        x_copy(n + 1, 1 - slot).start()

    @pl.when(n < n_blocks)
    def _():
        x_copy(n, slot).wait()

    x = xbuf[slot]

    works = []
    for sb in range(tq // tb):
        rows = slice(sb * tb, (sb + 1) * tb)
        ph = _dot_nt(wr_ref[...], xh_scr[prev, rows, :])
        pl_ = _dot_nt(wr_ref[0:ne, :], xl_scr[prev, rows, :])
        works.append(ph[0:ne] + ph[ne:2 * ne] + pl_ + br_ref[...])

    q = _dot(_rms(x, gx_ref[...]).astype(BF16), wq_ref[...]).astype(BF16)

    iota_f = lax.broadcasted_iota(I32, (ne, tb), 0).astype(F32)
    picked = []
    for work in works:
        vals, idxs, sels = [], [], []
        for _ in range(TOP_K):
            m = jnp.max(work, axis=0, keepdims=True)
            idx = jnp.min(jnp.where(work == m, iota_f, float(ne)), axis=0, keepdims=True)
            sel = iota_f == idx
            vals.append(m)
            idxs.append(idx.astype(I32))
            sels.append(sel)
            work = jnp.where(sel, -jnp.inf, work)
        onehot = jnp.zeros((ne, tb), F32)
        for sel in sels:
            onehot = onehot + sel.astype(F32)
        picked.append((vals, idxs, sels, onehot.astype(BF16)))

    scores = [_dot_nt(q[:, h * hd:(h + 1) * hd], kv_ref[:, h * hd:(h + 1) * hd]) * (hd ** -0.5)
              for h in range(N_GROUPS)]
    for h, s in enumerate(scores):
        v = kv_ref[:, d + h * hd:d + (h + 1) * hd]
        e = jnp.exp(s - jnp.max(s, axis=-1, keepdims=True))
        p = e / jnp.sum(e, axis=-1, keepdims=True)
        o_scr[:, h * hd:(h + 1) * hd] = _dot(p.astype(BF16), v).astype(BF16)

    before = (lax.broadcasted_iota(I32, (tb, tb), 0) < lax.broadcasted_iota(I32, (tb, tb), 1)
              ).astype(BF16)
    for sb, (vals, idxs, sels, oh16) in enumerate(picked):
        rows = slice(sb * tb, (sb + 1) * tb)
        ex = [jnp.exp(v - vals[0]) for v in vals]
        den = ex[0] + ex[1] + ex[2] + ex[3]
        prefix = _dot(oh16, before)
        lr = [jnp.sum(jnp.where(sel, prefix, 0.0), axis=0, keepdims=True).astype(I32) for sel in sels]
        idx_ref[:, rows] = jnp.concatenate(idxs, axis=0)
        gate_ref[:, rows] = jnp.concatenate([e_ / den for e_ in ex], axis=0)
        lrank_ref[:, rows] = jnp.concatenate(lr, axis=0)
        cnt_ref[sb] = _dot_nt(jnp.ones((1, tb), BF16), oh16).astype(I32)

    x2 = x + _dot(o_scr[...], wo_ref[...])
    x2_ref[...] = x2
    xn = _rms(x2, gm_ref[...])
    xh = xn.astype(BF16)
    xn_ref[...] = xh
    xh_scr[n % 2] = xh
    xl_scr[n % 2] = (xn - xh.astype(F32)).astype(BF16)


def _attention(x1, kv, norm_xattn, w_q, w_o, norm_moe, w_router, b_router):
    s_len, nb, d = x1.shape
    m = kv.shape[1]
    ne = w_router.shape[-1]
    tq, tb = ATT_ROWS, ROUTE_ROWS
    nq = s_len // tq
    n_blocks = nb * nq
    t = nb * s_len
    assert s_len % tq == 0 and tq % tb == 0
    wr_hi = w_router.astype(BF16)
    wr_lo = (w_router - wr_hi.astype(F32)).astype(BF16)
    wr2t = jnp.concatenate([wr_hi, wr_lo], axis=1).T
    cur = lambda n: jnp.minimum(n, n_blocks - 1)
    routed = lambda n: (0, jnp.maximum(n - 1, 0))
    return pl.pallas_call(
        functools.partial(_attn_kernel, nq=nq),
        grid=(n_blocks + 1,),
        in_specs=[pl.BlockSpec(memory_space=pl.ANY),
                  pl.BlockSpec((None, m, 2 * d), lambda n: (cur(n) // nq, 0, 0)),
                  _const_spec((1, d)), _const_spec((d, d)), _const_spec((d, d)),
                  _const_spec((1, d)), _const_spec((2 * ne, d)), _const_spec((ne, 1))],
        out_specs=[pl.BlockSpec((None, tq, d), lambda n: (cur(n) // nq, cur(n) % nq, 0)),
                   pl.BlockSpec((tq, d), lambda n: (cur(n), 0)),
                   pl.BlockSpec((TOP_K, tq), routed),
                   pl.BlockSpec((TOP_K, tq), routed),
                   pl.BlockSpec((TOP_K, tq), routed),
                   pl.BlockSpec((tq // tb, 1, ne), lambda n: (jnp.maximum(n - 1, 0), 0, 0))],
        out_shape=[jax.ShapeDtypeStruct((nb, s_len, d), F32),
                   jax.ShapeDtypeStruct((t, d), BF16),
                   jax.ShapeDtypeStruct((TOP_K, t), I32),
                   jax.ShapeDtypeStruct((TOP_K, t), F32),
                   jax.ShapeDtypeStruct((TOP_K, t), I32),
                   jax.ShapeDtypeStruct((t // tb, 1, ne), I32)],
        scratch_shapes=[pltpu.VMEM((2, tq, d), F32), pltpu.SemaphoreType.DMA((2,)),
                        pltpu.VMEM((tq, d), BF16), pltpu.VMEM((2, tq, d), BF16), pltpu.VMEM((2, tq, d), BF16)],
        compiler_params=pltpu.CompilerParams(dimension_semantics=("arbitrary",),
                                             vmem_limit_bytes=VMEM_LIMIT),
        name="attention",
    )(x1, kv, norm_xattn.reshape(1, d), w_q.astype(BF16), w_o.astype(BF16),
      norm_moe.reshape(1, d), wr2t, b_router.reshape(ne, 1))


def _pow2_chunks(limit):
    sizes = []
    c = RUN_ALIGN
    while c <= limit:
        sizes.append(c)
        c *= 2
    return sizes[::-1]


def _for_each_chunk(n, limit, fn):
    for size in _pow2_chunks(limit):
        @pl.when((n & size) != 0)
        def _(size=size):
            fn(pl.multiple_of(n & ~(2 * size - 1), RUN_ALIGN), size)


def _wait_rows(n, limit, src, dst, sem):
    _for_each_chunk(n, limit, lambda off, size: pltpu.make_async_copy(
        src.at[pl.ds(0, size)], dst.at[pl.ds(0, size)], sem).wait())


def _dispatch_kernel(n8_ref, loc_ref, run_ref, tot_ref, tailn_ref, tails_ref,
                     xn_ref, idx_ref, lrank_ref, locv_ref, xs_hbm, pos_ref, *scratch):
    *bufs, zbuf, sems, zsem = scratch
    g = pl.program_id(0)
    nbuf = len(bufs)
    r_loc = bufs[0].shape[0]
    tb = xn_ref.shape[0] // nbuf
    ne = tailn_ref.shape[0] - 1
    zrows = zbuf.shape[0]
    last_blk = pl.num_programs(0) * nbuf - 1

    def send(blk, s, live):
        for e in range(ne):
            n = jnp.where(live, n8_ref[blk * ne + e], 0)
            src0 = loc_ref[blk * ne + e]
            dst0 = run_ref[blk * ne + e]
            _for_each_chunk(n, tb, lambda off, size, src0=src0, dst0=dst0: pltpu.make_async_copy(
                bufs[s].at[pl.ds(pl.multiple_of(src0 + off, RUN_ALIGN), size)],
                xs_hbm.at[pl.ds(pl.multiple_of(dst0 + off, RUN_ALIGN), size)], sems.at[s]).start())

    def sent(blk, s, live):
        _wait_rows(jnp.where(live, tot_ref[blk], 0), r_loc, bufs[s], xs_hbm, sems.at[s])

    def zero_rest(i, carry):
        dst = pl.multiple_of(tails_ref[ne] + i * zrows, zrows)
        pltpu.make_async_copy(zbuf, xs_hbm.at[pl.ds(dst, zrows)], zsem).start()
        return carry

    def wait_rest(i, carry):
        pltpu.make_async_copy(zbuf, xs_hbm.at[pl.ds(0, zrows)], zsem).wait()
        return carry

    @pl.when(g == 0)
    def _():
        zbuf[...] = jnp.zeros_like(zbuf)
        for e in range(ne):
            _for_each_chunk(tailn_ref[e], zrows, lambda off, size, e=e: pltpu.make_async_copy(
                zbuf.at[pl.ds(0, size)],
                xs_hbm.at[pl.ds(pl.multiple_of(tails_ref[e] + off, RUN_ALIGN), size)], zsem).start())
        lax.fori_loop(0, tailn_ref[ne], zero_rest, 0)

    iota_e = lax.broadcasted_iota(I32, (ne, tb), 0)
    iota_r = lax.broadcasted_iota(I32, (r_loc, tb), 0)
    for u in range(nbuf):
        blk = g * nbuf + u
        cols = slice(u * tb, (u + 1) * tb)
        sent(jnp.maximum(blk - nbuf, 0), u, blk >= nbuf)
        send(jnp.maximum(blk - 1, 0), (u - 1) % nbuf, blk >= 1)

        loc_col = locv_ref[u].astype(F32)
        pos = []
        for k in range(TOP_K):
            run0 = jnp.sum(jnp.where(iota_e == idx_ref[k:k + 1, cols], loc_col, 0.0), axis=0, keepdims=True)
            pos.append(run0.astype(I32) + lrank_ref[k:k + 1, cols])
            pos_ref[k:k + 1, cols] = pos[k]

        hit = iota_r == pos[0]
        for k in range(1, TOP_K):
            hit = jnp.logical_or(hit, iota_r == pos[k])
        bufs[u][...] = _pack_halves(_dot(hit.astype(BF16), xn_ref[cols, :]))

    @pl.when(g == pl.num_programs(0) - 1)
    def _():
        send(last_blk, nbuf - 1, True)
        for s_ in range(nbuf):
            sent(last_blk - (nbuf - 1 - s_), s_, True)
        for e in range(ne):
            _wait_rows(tailn_ref[e], zrows, zbuf, xs_hbm, zsem)
        lax.fori_loop(0, tailn_ref[ne], wait_rest, 0)


def _dispatch(xn, idx_t, lrank_t, loc, tables, p_rows, r_loc):
    t, d = xn.shape
    tb, nu = ROUTE_ROWS, ROUTE_UNROLL
    nblk, ne = loc.shape
    assert nblk % nu == 0
    tok = pl.BlockSpec((TOP_K, nu * tb), lambda g, *_: (0, g))
    grid_spec = pltpu.PrefetchScalarGridSpec(
        num_scalar_prefetch=6,
        grid=(nblk // nu,),
        in_specs=[pl.BlockSpec((nu * tb, d), lambda g, *_: (g, 0)), tok, tok,
                  pl.BlockSpec((nu, ne, 1), lambda g, *_: (g, 0, 0))],
        out_specs=[pl.BlockSpec(memory_space=pl.ANY), tok],
        scratch_shapes=[pltpu.VMEM((r_loc, d // 2), U32)] * nu + [
            pltpu.VMEM((EXPERT_ROWS // 2, d // 2), U32), pltpu.SemaphoreType.DMA((nu,)), pltpu.SemaphoreType.DMA],
    )
    return pl.pallas_call(
        _dispatch_kernel,
        grid_spec=grid_spec,
        out_shape=[jax.ShapeDtypeStruct((p_rows, d // 2), U32), jax.ShapeDtypeStruct((TOP_K, t), I32)],
        compiler_params=pltpu.CompilerParams(dimension_semantics=("arbitrary",),
                                             vmem_limit_bytes=VMEM_LIMIT),
        name="dispatch",
    )(*tables, xn, idx_t, lrank_t, loc.reshape(nblk, ne, 1))


def _expert_kernel(blk0_ref, nblk_ref, nu_ref, xs_hbm, wgu_ref, bgu_ref, wdn_ref, bdn_ref, y_hbm,
                   xbuf, ybuf, zbuf, in_sem, out_sem, zsem, wgu_bf, wdn_bf, next_ref):
    e = pl.program_id(0)
    nbuf, bm, dh = xbuf.shape
    zrows = zbuf.shape[0]
    de = wdn_ref.shape[0]
    n_used = nu_ref[0]

    def x_copy(g):
        return pltpu.make_async_copy(xs_hbm.at[pl.ds(pl.multiple_of(g * bm, bm), bm)], xbuf.at[g % nbuf],
                                     in_sem.at[g % nbuf])

    def y_copy(g):
        return pltpu.make_async_copy(ybuf.at[g % nbuf], y_hbm.at[pl.ds(pl.multiple_of(g * bm, bm), bm)],
                                     out_sem.at[g % nbuf])

    def prefetch_until(limit):
        def start(g):
            x_copy(g).start()
            return g + 1
        next_ref[0] = lax.while_loop(lambda g: g < jnp.minimum(limit, n_used), start, next_ref[0])

    n_zero = (y_hbm.shape[0] - n_used * bm) // zrows

    def z_copy(i):
        return pltpu.make_async_copy(
            zbuf, y_hbm.at[pl.ds(pl.multiple_of(n_used * bm + i * zrows, zrows), zrows)], zsem)

    @pl.when(e == 0)
    def _():
        next_ref[0] = 0
        zbuf[...] = jnp.zeros_like(zbuf)
        lax.fori_loop(0, n_zero, lambda i, c: (z_copy(i).start(), c)[1], 0)

    wgu_bf[...] = wgu_ref[...].astype(BF16)
    wdn_bf[...] = wdn_ref[...].astype(BF16)

    def mlp(xu):
        x_lo, x_hi = _unpack_halves(xu)

        def proj(c0, c1):
            return (_dot(x_lo, wgu_bf[0:dh, c0:c1]) + _dot(x_hi, wgu_bf[dh:2 * dh, c0:c1])
                    + bgu_ref[:, c0:c1])

        nch = 4
        ch = de // nch
        acts = []
        for c in range(nch):
            gl = jnp.minimum(proj(c * ch, (c + 1) * ch), SWIGLU_LIMIT)
            up = jnp.clip(proj(de + c * ch, de + (c + 1) * ch), -SWIGLU_LIMIT, SWIGLU_LIMIT)
            acts.append(((up + 1.0) * (gl * jax.nn.sigmoid(SWIGLU_ALPHA * gl))).astype(BF16))
        y = bdn_ref[...]
        for c in range(nch):
            y = y + _dot(acts[c], wdn_bf[c * ch:(c + 1) * ch, :])
        return _pack_halves(y.astype(BF16).astype(F32))

    def run(g, units):
        prefetch_until(g + units + 2)
        for k in range(units):
            x_copy(g + k).wait()

            @pl.when(g + k >= nbuf)
            def _(k=k):
                y_copy(g + k - nbuf).wait()
        if units == 1:
            ybuf[g % nbuf] = mlp(xbuf[g % nbuf])
        else:
            slots = pl.ds(pl.multiple_of(g % nbuf, 2), 2)
            ybuf[slots] = mlp(xbuf[slots].reshape(2 * bm, dh)).reshape(2, bm, dh)
        for k in range(units):
            y_copy(g + k).start()

    g0 = blk0_ref[e]
    n = nblk_ref[e]
    lead = jnp.logical_and(g0 % 2 == 1, n > 0).astype(I32)
    pairs = (n - lead) // 2

    @pl.when(lead == 1)
    def _():
        run(g0, 1)

    lax.fori_loop(0, pairs, lambda j, c: (run(g0 + lead + 2 * j, 2), c)[1], 0)

    @pl.when((n - lead) % 2 == 1)
    def _():
        run(g0 + lead + 2 * pairs, 1)

    @pl.when(e == pl.num_programs(0) - 1)
    def _():
        for back in range(nbuf, 0, -1):
            @pl.when(n_used >= back)
            def _(back=back):
                y_copy(n_used - back).wait()
        lax.fori_loop(0, n_zero, lambda i, c: (z_copy(i).wait(), c)[1], 0)


def _experts(xs, blk0, nblk, n_used, w_gate_up, b_gate_up, w_down, b_down):
    p, dh = xs.shape
    ne, d, de2 = w_gate_up.shape
    de = de2 // 2
    bm = EXPERT_ROWS
    per_expert = lambda shape: pl.BlockSpec((None,) + shape, lambda e, *_: (e, 0, 0))
    grid_spec = pltpu.PrefetchScalarGridSpec(
        num_scalar_prefetch=3,
        grid=(ne,),
        in_specs=[pl.BlockSpec(memory_space=pl.ANY),
                  per_expert((d, de2)), per_expert((1, de2)), per_expert((de, d)), per_expert((1, d))],
        out_specs=pl.BlockSpec(memory_space=pl.ANY),
        scratch_shapes=[pltpu.VMEM((4, bm, dh), U32), pltpu.VMEM((4, bm, dh), U32),
                        pltpu.VMEM((bm // 4, dh), U32),
                        pltpu.SemaphoreType.DMA((4,)), pltpu.SemaphoreType.DMA((4,)), pltpu.SemaphoreType.DMA,
                        pltpu.VMEM((d, de2), BF16), pltpu.VMEM((de, d), BF16), pltpu.SMEM((1,), I32)],
    )
    return pl.pallas_call(
        _expert_kernel,
        grid_spec=grid_spec,
        out_shape=jax.ShapeDtypeStruct((p, dh), U32),
        compiler_params=pltpu.CompilerParams(dimension_semantics=("arbitrary",),
                                             vmem_limit_bytes=VMEM_LIMIT),
        name="experts",
    )(blk0, nblk, n_used, xs, w_gate_up, b_gate_up.reshape(ne, 1, de2), w_down, b_down.reshape(ne, 1, d))


def _combine_kernel(n8_ref, loc_ref, run_ref, tot_ref,
                    pos_ref, gate_ref, x2_ref, gf_ref, y_hbm, o_ref, *scratch):
    *bufs, sems = scratch
    g = pl.program_id(0)
    nbuf = len(bufs)
    r_loc = bufs[0].shape[0]
    tb = x2_ref.shape[0] // nbuf
    nblk = pl.num_programs(0) * nbuf
    ne = n8_ref.shape[0] // tot_ref.shape[0]

    def fetch(blk, s, live):
        for e in range(ne):
            n = jnp.where(live, n8_ref[blk * ne + e], 0)
            dst0 = loc_ref[blk * ne + e]
            src0 = run_ref[blk * ne + e]
            _for_each_chunk(n, tb, lambda off, size, src0=src0, dst0=dst0: pltpu.make_async_copy(
                y_hbm.at[pl.ds(pl.multiple_of(src0 + off, RUN_ALIGN), size)],
                bufs[s].at[pl.ds(pl.multiple_of(dst0 + off, RUN_ALIGN), size)], sems.at[s]).start())

    @pl.when(g == 0)
    def _():
        for buf in bufs:
            buf[...] = jnp.zeros_like(buf)
        for ahead in range(nbuf - 2):
            fetch(ahead, ahead, True)

    iota_c = lax.broadcasted_iota(I32, (tb, r_loc), 1)
    for u in range(nbuf):
        blk = g * nbuf + u
        rows = slice(u * tb, (u + 1) * tb)
        w = jnp.zeros((tb, r_loc), F32)
        for k in range(TOP_K):
            w = w + jnp.where(iota_c == pos_ref[rows, k:k + 1], gate_ref[rows, k:k + 1], 0.0)
        w_hi = w.astype(BF16)
        w_lo = (w - w_hi.astype(F32)).astype(BF16)
        _wait_rows(tot_ref[blk], r_loc, y_hbm, bufs[u], sems.at[u])
        ahead = blk + nbuf - 2
        fetch(jnp.minimum(ahead, nblk - 1), (u - 2) % nbuf, ahead < nblk)
        y_lo, y_hi = _unpack_halves(bufs[u][...])
        moe = jnp.concatenate([_dot(w_hi, y_lo) + _dot(w_lo, y_lo), _dot(w_hi, y_hi) + _dot(w_lo, y_hi)], axis=1)
        o_ref[rows, :] = _rms(x2_ref[rows, :] + moe, gf_ref[...])


def _combine(pos, gates, x2, y, norm_final, tables, r_loc):
    t, d = x2.shape
    rows = ROUTE_ROWS * ROUTE_UNROLL
    assert t % rows == 0
    grid_spec = pltpu.PrefetchScalarGridSpec(
        num_scalar_prefetch=4,
        grid=(t // rows,),
        in_specs=[pl.BlockSpec((rows, TOP_K), lambda g, *_: (g, 0)),
                  pl.BlockSpec((rows, TOP_K), lambda g, *_: (g, 0)),
                  pl.BlockSpec((rows, d), lambda g, *_: (g, 0)),
                  pl.BlockSpec((1, d), lambda g, *_: (0, 0)),
                  pl.BlockSpec(memory_space=pl.ANY)],
        out_specs=pl.BlockSpec((rows, d), lambda g, *_: (g, 0)),
        scratch_shapes=[pltpu.VMEM((r_loc, d // 2), U32)] * ROUTE_UNROLL + [pltpu.SemaphoreType.DMA((ROUTE_UNROLL,))],
    )
    return pl.pallas_call(
        _combine_kernel,
        grid_spec=grid_spec,
        out_shape=jax.ShapeDtypeStruct((t, d), F32),
        compiler_params=pltpu.CompilerParams(dimension_semantics=("arbitrary",),
                                             vmem_limit_bytes=VMEM_LIMIT),
        name="combine",
    )(*tables, pos, gates, x2, norm_final.reshape(1, d), y)


def _excl_cumsum(a, axis):
    n = a.shape[axis]
    a = jnp.moveaxis(a, axis, -1)
    earlier = jnp.arange(n)[None, :] < jnp.arange(n)[:, None]
    out = jnp.sum(jnp.where(earlier, a[..., None, :], 0), axis=-1)
    return jnp.moveaxis(out, -1, axis)


def _layout(cnt):
    nblk, ne = cnt.shape
    tb, bm = ROUTE_ROWS, EXPERT_ROWS
    n8 = (cnt + RUN_ALIGN - 1) // RUN_ALIGN * RUN_ALIGN
    loc = _excl_cumsum(n8, 1)
    tot = jnp.sum(n8, axis=1)
    size = jnp.sum(n8, axis=0)
    padded = (size + bm - 1) // bm * bm
    pstart = _excl_cumsum(padded, 0)
    pend = pstart + padded
    run = pstart[None, :] + _excl_cumsum(n8, 0)
    p_blocks = -(-(nblk * tb * TOP_K + nblk * ne * (RUN_ALIGN - 1) + ne * (bm - RUN_ALIGN)) // bm)
    n_used = (pend[-1] // bm).astype(I32)
    flat = lambda a: a.reshape(-1).astype(I32)
    tables = (flat(n8), flat(loc), flat(run), flat(tot))
    half = bm // 2
    tails = (flat(jnp.concatenate([padded - size, (p_blocks * bm - pend[-1:]) // half])),
             flat(jnp.concatenate([pstart + size, pend[-1:]])))
    r_loc = -(-(tb * TOP_K + ne * (RUN_ALIGN - 1)) // 256) * 256
    blocks = ((pstart // bm).astype(I32), (padded // bm).astype(I32), n_used.reshape(1))
    return tables, tails, loc.astype(I32), blocks, p_blocks * bm, r_loc


def kernel(x, mem, norm_mix, w_in, w_pool_group, pool_scale, w_pool_proj, conv_w, conv_b, lru_w_a, lru_b_a, lru_w_x, lru_b_x, lru_lambda, w_lru_proj, w_mix_out, norm_xattn, norm_mem, w_q, w_kv, w_o, norm_moe, w_router, b_router, w_gate_up, b_gate_up, w_down, b_down, norm_final):
    nb, s_len, d = x.shape
    m_len = mem.shape[1]
    assert norm_mix.shape[0] == 1, "single-layer stack"
    l = 0
    x1 = _mixer(x, norm_mix[l], w_in[l], w_pool_group[l], pool_scale[l], w_pool_proj[l],
                conv_w[l], conv_b[l], lru_w_a[l], lru_b_a[l], lru_w_x[l], lru_b_x[l],
                lru_lambda[l], w_lru_proj[l], w_mix_out[l])
    kv = _kv_proj(mem.reshape(nb * m_len, d), norm_mem[l], w_kv[l]).reshape(nb, m_len, 2 * d)
    x2, xn, idx_t, gate_t, lrank_t, cnt = _attention(
        x1.reshape(s_len, nb, d), kv, norm_xattn[l], w_q[l], w_o[l],
        norm_moe[l], w_router[l], b_router[l])
    tables, tails, loc, blocks, p_rows, r_loc = _layout(cnt.reshape(cnt.shape[0], -1))
    xs, pos_t = _dispatch(xn, idx_t, lrank_t, loc, tables + tails, p_rows, r_loc)
    y = _experts(xs, *blocks, w_gate_up[l], b_gate_up[l], w_down[l], b_down[l])
    out = _combine(pos_t.T, gate_t.T, x2.reshape(nb * s_len, d), y, norm_final, tables, r_loc)
    return out.reshape(nb, s_len, d)
```

```python
import functools

import jax
import jax.numpy as jnp
from jax import lax
from jax.experimental import pallas as pl
from jax.experimental.pallas import tpu as pltpu

POOL_WINDOWS = (2, 4, 8, 16)
N_GROUPS = 4
CONV_WIDTH = 4
RG_C = 8.0
TOP_K = 4
SWIGLU_LIMIT = 7.0
SWIGLU_ALPHA = 1.702
RMS_EPS = 1e-6

MIX_STEPS = 32
KV_ROWS = 512
ATT_ROWS = 512
ROUTE_ROWS = 256
ROUTE_UNROLL = 4
EXPERT_ROWS = 512
RUN_ALIGN = 8
MXU_TILE = 256
VMEM_LIMIT = 52 * 1024 * 1024

BF16 = jnp.bfloat16
F32 = jnp.float32
I32 = jnp.int32
U32 = jnp.uint32


def _const_spec(shape):
    nd = len(shape)
    return pl.BlockSpec(shape, lambda *_: (0,) * nd, pipeline_mode=pl.Buffered(1))


def _rms(x, g):
    return x * lax.rsqrt(jnp.mean(x * x, axis=-1, keepdims=True) + RMS_EPS) * g


def _dot(a, b):
    return jnp.dot(a, b, preferred_element_type=F32)


def _dot_nt(a, b):
    return lax.dot_general(a, b, (((1,), (1,)), ((), ())), preferred_element_type=F32)


def _pack_halves(x):
    c = x.shape[1] // 2
    lo = lax.bitcast_convert_type(x[:, :c], U32)
    hi = lax.bitcast_convert_type(x[:, c:], U32)
    return (hi & jnp.uint32(0xFFFF0000)) | (lo >> 16)


def _unpack_halves(u):
    lo = lax.bitcast_convert_type(u << 16, F32).astype(BF16)
    hi = lax.bitcast_convert_type(u & jnp.uint32(0xFFFF0000), F32).astype(BF16)
    return lo, hi


def _mixer_kernel(x_hbm, nm_ref, win_ref, wpg_ref, psc_ref, wpp_ref, cw_ref, cb_ref,
                  wa_ref, ba_ref, wx_ref, bx_ref, lam_ref, wlp_ref, wmo_ref,
                  o_ref,
                  xbuf, xsem,
                  h_ref, up_ref, ul_ref, a_ref, b_ref, pm_ref, m_ref, mb_ref, t_ref, gl_ref, gb_ref, hc_ref,
                  *, nb, ts):
    rows, d = o_ref.shape
    gw = d // N_GROUPS
    halo_p = (POOL_WINDOWS[-1]) * nb
    halo_c = (CONV_WIDTH - 1) * nb
    c = pl.program_id(0)

    @pl.when(c == 0)
    def _():
        up_ref[0:halo_p, :] = jnp.zeros((halo_p, d), F32)
        ul_ref[0:halo_c, :] = jnp.zeros((halo_c, d), F32)
        hc_ref[...] = jnp.zeros_like(hc_ref)

    def x_copy(step, t, slot):
        return pltpu.make_async_copy(x_hbm.at[:, step * ts + t, :], xbuf.at[slot, t], xsem.at[slot])

    def fetch(step, slot, live):
        for t in range(ts):
            @pl.when(live)
            def _(t=t):
                x_copy(step, t, slot).start()

    slot = c % 2
    fetch(0, 0, c == 0)
    fetch(jnp.minimum(c + 1, pl.num_programs(0) - 1), 1 - slot, c + 1 < pl.num_programs(0))
    pltpu.make_async_copy(xbuf.at[1 - slot], xbuf.at[slot], xsem.at[slot]).wait()

    h_ref[...] = _rms(xbuf[slot].reshape(rows, d), nm_ref[...]).astype(BF16)

    t_glob = c * ts + lax.broadcasted_iota(I32, (rows, 1), 0) // nb

    up_ref[halo_p:halo_p + rows, :] = _dot(h_ref[...], win_ref[:, 0:d])
    ul_ref[halo_c:halo_c + rows, :] = _dot(h_ref[...], win_ref[:, d:2 * d])

    for g, w in enumerate(POOL_WINDOWS):
        cols = slice(g * gw, (g + 1) * gw)
        u = up_ref[halo_p:halo_p + rows, cols]
        acc = u
        for j in range(1, w):
            acc = acc + up_ref[halo_p - j * nb:halo_p - j * nb + rows, cols]
        cnt = jnp.minimum(t_glob + 1, w).astype(F32)
        p = acc / cnt - u
        pg = _dot(p.astype(BF16), wpg_ref[g]) * psc_ref[:, cols]
        pm_ref[:, cols] = pg.astype(BF16)
    t_ref[...] = _dot(h_ref[...], win_ref[:, 3 * d:4 * d])
    m_ref[...] = _dot(pm_ref[...], wpp_ref[...])

    for g in range(N_GROUPS):
        cols = slice(g * gw, (g + 1) * gw)
        xr = cb_ref[:, cols]
        for k in range(CONV_WIDTH):
            off = halo_c - (CONV_WIDTH - 1 - k) * nb
            xr = xr + ul_ref[off:off + rows, cols] * cw_ref[k:k + 1, cols]
        xrb = xr.astype(BF16)
        r = jax.nn.sigmoid(_dot(xrb, wa_ref[g]) + ba_ref[:, cols])
        i = jax.nn.sigmoid(_dot(xrb, wx_ref[g]) + bx_ref[:, cols])
        gl_ref[:, cols] = _dot(h_ref[...], win_ref[:, 2 * d + g * gw:2 * d + (g + 1) * gw])
        gb_ref[:, cols] = _dot(h_ref[...], win_ref[:, 4 * d + g * gw:4 * d + (g + 1) * gw])
        lam = lam_ref[:, cols]
        log_sig = jnp.minimum(lam, 0.0) - jnp.log(1.0 + jnp.exp(-jnp.abs(lam)))
        a = jnp.exp((RG_C * r) * log_sig)
        mult = jnp.sqrt(jnp.maximum(1.0 - a * a, 0.0))
        mult = jnp.where(t_glob == 0, 1.0, mult)
        a_ref[:, cols] = a
        b_ref[:, cols] = mult * i * xr
    m_ref[...] = jax.nn.sigmoid(t_ref[...]) * m_ref[...]

    def scan_step(t, hprev):
        sl = pl.ds(pl.multiple_of(t * nb, nb), nb)
        hn = a_ref[sl, :] * hprev + b_ref[sl, :]
        b_ref[sl, :] = hn
        return hn

    hc_ref[...] = lax.fori_loop(0, ts, scan_step, hc_ref[...], unroll=4)

    half = rows // 2
    for hs in (slice(0, half), slice(half, rows)):
        pm_ref[hs, :] = (b_ref[hs, :] * jax.nn.gelu(gl_ref[hs, :], approximate=True)).astype(BF16)
    for hs in (slice(0, half), slice(half, rows)):
        yb = _dot(pm_ref[hs, :], wlp_ref[...])
        mb_ref[hs, :] = (m_ref[hs, :] + jax.nn.sigmoid(gb_ref[hs, :]) * yb).astype(BF16)
    for i, hs in enumerate((slice(0, half), slice(half, rows))):
        x_half = xbuf[slot, i * (ts // 2):(i + 1) * (ts // 2)].reshape(half, d)
        o_ref[hs, :] = x_half + _dot(mb_ref[hs, :], wmo_ref[...])

    up_ref[0:halo_p, :] = up_ref[rows:rows + halo_p, :]
    ul_ref[0:halo_c, :] = ul_ref[rows:rows + halo_c, :]


def _mixer(x, norm_mix, w_in, w_pool_group, pool_scale, w_pool_proj, conv_w, conv_b,
           lru_w_a, lru_b_a, lru_w_x, lru_b_x, lru_lambda, w_lru_proj, w_mix_out):
    nb, s_len, d = x.shape
    ts = MIX_STEPS
    rows = ts * nb
    n_rows = s_len * nb
    assert s_len % ts == 0 and ts % 2 == 0 and ts >= POOL_WINDOWS[-1] and nb % 8 == 0
    row2 = lambda v: v.reshape(1, -1)
    args = (x, row2(norm_mix), w_in.astype(BF16), w_pool_group.astype(BF16), row2(pool_scale),
            w_pool_proj.astype(BF16), conv_w, row2(conv_b), lru_w_a.astype(BF16), row2(lru_b_a),
            lru_w_x.astype(BF16), row2(lru_b_x), row2(lru_lambda), w_lru_proj.astype(BF16),
            w_mix_out.astype(BF16))
    in_specs = [pl.BlockSpec(memory_space=pl.ANY)] + [_const_spec(a.shape) for a in args[1:]]
    halo_p = POOL_WINDOWS[-1] * nb
    halo_c = (CONV_WIDTH - 1) * nb
    return pl.pallas_call(
        functools.partial(_mixer_kernel, nb=nb, ts=ts),
        grid=(n_rows // rows,),
        in_specs=in_specs,
        out_specs=pl.BlockSpec((rows, d), lambda c: (c, 0)),
        out_shape=jax.ShapeDtypeStruct((n_rows, d), F32),
        scratch_shapes=[
            pltpu.VMEM((2, ts, nb, d), F32),
            pltpu.SemaphoreType.DMA((2,)),
            pltpu.VMEM((rows, d), BF16),
            pltpu.VMEM((halo_p + rows, d), F32),
            pltpu.VMEM((halo_c + rows, d), F32),
            pltpu.VMEM((rows, d), F32),
            pltpu.VMEM((rows, d), F32),
            pltpu.VMEM((rows, d), BF16),
            pltpu.VMEM((rows, d), F32),
            pltpu.VMEM((rows, d), BF16),
            pltpu.VMEM((rows, d), F32),
            pltpu.VMEM((rows, d), F32),
            pltpu.VMEM((rows, d), F32),
            pltpu.VMEM((nb, d), F32),
        ],
        compiler_params=pltpu.CompilerParams(dimension_semantics=("arbitrary",),
                                             vmem_limit_bytes=VMEM_LIMIT),
        name="mixer",
    )(*args)


def _kv_kernel(m_ref, g_ref, w_ref, o_ref):
    o_ref[...] = _dot(_rms(m_ref[...], g_ref[...]).astype(BF16), w_ref[...]).astype(BF16)


def _kv_proj(mem2d, norm_mem, w_kv):
    n, d = mem2d.shape
    assert n % KV_ROWS == 0
    return pl.pallas_call(
        _kv_kernel,
        grid=(n // KV_ROWS,),
        in_specs=[pl.BlockSpec((KV_ROWS, d), lambda i: (i, 0)),
                  _const_spec((1, d)), _const_spec((d, 2 * d))],
        out_specs=pl.BlockSpec((KV_ROWS, 2 * d), lambda i: (i, 0)),
        out_shape=jax.ShapeDtypeStruct((n, 2 * d), BF16),
        compiler_params=pltpu.CompilerParams(dimension_semantics=("arbitrary",),
                                             vmem_limit_bytes=VMEM_LIMIT),
        name="kv_proj",
    )(mem2d, norm_mem.reshape(1, d), w_kv.astype(BF16))


def _attn_kernel(x_hbm, kv_ref, gx_ref, wq_ref, wo_ref, gm_ref, wr_ref, br_ref,
                 x2_ref, xn_ref, idx_ref, gate_ref, lrank_ref, cnt_ref,
                 xbuf, xsem, o_scr, xh_scr, xl_scr, *, nq):
    tq, d = x2_ref.shape
    hd = d // N_GROUPS
    ne = br_ref.shape[0]
    tb = ROUTE_ROWS
    n = pl.program_id(0)
    n_blocks = pl.num_programs(0) - 1
    cur = jnp.minimum(n, n_blocks - 1)
    slot = cur % 2
    prev = (n + 1) % 2

    def x_copy(blk, s):
        return pltpu.make_async_copy(x_hbm.at[pl.ds((blk % nq) * tq, tq), blk // nq, :], xbuf.at[s], xsem.at[s])

    @pl.when(n == 0)
    def _():
        x_copy(0, 0).start()
        xh_scr[...] = jnp.zeros_like(xh_scr)
        xl_scr[...] = jnp.zeros_like(xl_scr)

    @pl.when(n + 1 < n_blocks)
    def _():
        x_copy(n + 1, 1 - slot).start()

    @pl.when(n < n_blocks)
    def _():
        x_copy(n, slot).wait()

    x = xbuf[slot]

    works = []
    for sb in range(tq // tb):
        rows = slice(sb * tb, (sb + 1) * tb)
        ph = _dot_nt(wr_ref[...], xh_scr[prev, rows, :])
        pl_ = _dot_nt(wr_ref[0:ne, :], xl_scr[prev, rows, :])
        works.append(ph[0:ne] + ph[ne:2 * ne] + pl_ + br_ref[...])

    q = _dot(_rms(x, gx_ref[...]).astype(BF16), wq_ref[...]).astype(BF16)

    iota_f = lax.broadcasted_iota(I32, (ne, tb), 0).astype(F32)
    picked = []
    for work in works:
        vals, idxs, sels = [], [], []
        for _ in range(TOP_K):
            m = jnp.max(work, axis=0, keepdims=True)
            idx = jnp.min(jnp.where(work == m, iota_f, float(ne)), axis=0, keepdims=True)
            sel = iota_f == idx
            vals.append(m)
            idxs.append(idx.astype(I32))
            sels.append(sel)
            work = jnp.where(sel, -jnp.inf, work)
        onehot = jnp.zeros((ne, tb), F32)
        for sel in sels:
            onehot = onehot + sel.astype(F32)
        picked.append((vals, idxs, sels, onehot.astype(BF16)))

    scores = [_dot_nt(q[:, h * hd:(h + 1) * hd], kv_ref[:, h * hd:(h + 1) * hd]) * (hd ** -0.5)
              for h in range(N_GROUPS)]
    for h, s in enumerate(scores):
        v = kv_ref[:, d + h * hd:d + (h + 1) * hd]
        e = jnp.exp(s - jnp.max(s, axis=-1, keepdims=True))
        p = e / jnp.sum(e, axis=-1, keepdims=True)
        o_scr[:, h * hd:(h + 1) * hd] = _dot(p.astype(BF16), v).astype(BF16)

    before = (lax.broadcasted_iota(I32, (tb, tb), 0) < lax.broadcasted_iota(I32, (tb, tb), 1)
              ).astype(BF16)
    for sb, (vals, idxs, sels, oh16) in enumerate(picked):
        rows = slice(sb * tb, (sb + 1) * tb)
        ex = [jnp.exp(v - vals[0]) for v in vals]
        den = ex[0] + ex[1] + ex[2] + ex[3]
        prefix = _dot(oh16, before)
        lr = [jnp.sum(jnp.where(sel, prefix, 0.0), axis=0, keepdims=True).astype(I32) for sel in sels]
        idx_ref[:, rows] = jnp.concatenate(idxs, axis=0)
        gate_ref[:, rows] = jnp.concatenate([e_ / den for e_ in ex], axis=0)
        lrank_ref[:, rows] = jnp.concatenate(lr, axis=0)
        cnt_ref[sb] = _dot_nt(jnp.ones((1, tb), BF16), oh16).astype(I32)

    x2 = x + _dot(o_scr[...], wo_ref[...])
    x2_ref[...] = x2
    xn = _rms(x2, gm_ref[...])
    xh = xn.astype(BF16)
    xn_ref[...] = xh
    xh_scr[n % 2] = xh
    xl_scr[n % 2] = (xn - xh.astype(F32)).astype(BF16)


def _attention(x1, kv, norm_xattn, w_q, w_o, norm_moe, w_router, b_router):
    s_len, nb, d = x1.shape
    m = kv.shape[1]
    ne = w_router.shape[-1]
    tq, tb = ATT_ROWS, ROUTE_ROWS
    nq = s_len // tq
    n_blocks = nb * nq
    t = nb * s_len
    assert s_len % tq == 0 and tq % tb == 0
    wr_hi = w_router.astype(BF16)
    wr_lo = (w_router - wr_hi.astype(F32)).astype(BF16)
    wr2t = jnp.concatenate([wr_hi, wr_lo], axis=1).T
    cur = lambda n: jnp.minimum(n, n_blocks - 1)
    routed = lambda n: (0, jnp.maximum(n - 1, 0))
    return pl.pallas_call(
        functools.partial(_attn_kernel, nq=nq),
        grid=(n_blocks + 1,),
        in_specs=[pl.BlockSpec(memory_space=pl.ANY),
                  pl.BlockSpec((None, m, 2 * d), lambda n: (cur(n) // nq, 0, 0)),
                  _const_spec((1, d)), _const_spec((d, d)), _const_spec((d, d)),
                  _const_spec((1, d)), _const_spec((2 * ne, d)), _const_spec((ne, 1))],
        out_specs=[pl.BlockSpec((None, tq, d), lambda n: (cur(n) // nq, cur(n) % nq, 0)),
                   pl.BlockSpec((tq, d), lambda n: (cur(n), 0)),
                   pl.BlockSpec((TOP_K, tq), routed),
                   pl.BlockSpec((TOP_K, tq), routed),
                   pl.BlockSpec((TOP_K, tq), routed),
                   pl.BlockSpec((tq // tb, 1, ne), lambda n: (jnp.maximum(n - 1, 0), 0, 0))],
        out_shape=[jax.ShapeDtypeStruct((nb, s_len, d), F32),
                   jax.ShapeDtypeStruct((t, d), BF16),
                   jax.ShapeDtypeStruct((TOP_K, t), I32),
                   jax.ShapeDtypeStruct((TOP_K, t), F32),
                   jax.ShapeDtypeStruct((TOP_K, t), I32),
                   jax.ShapeDtypeStruct((t // tb, 1, ne), I32)],
        scratch_shapes=[pltpu.VMEM((2, tq, d), F32), pltpu.SemaphoreType.DMA((2,)),
                        pltpu.VMEM((tq, d), BF16), pltpu.VMEM((2, tq, d), BF16), pltpu.VMEM((2, tq, d), BF16)],
        compiler_params=pltpu.CompilerParams(dimension_semantics=("arbitrary",),
                                             vmem_limit_bytes=VMEM_LIMIT),
        name="attention",
    )(x1, kv, norm_xattn.reshape(1, d), w_q.astype(BF16), w_o.astype(BF16),
      norm_moe.reshape(1, d), wr2t, b_router.reshape(ne, 1))


def _pow2_chunks(limit):
    sizes = []
    c = RUN_ALIGN
    while c <= limit:
        sizes.append(c)
        c *= 2
    return sizes[::-1]


def _for_each_chunk(n, limit, fn):
    for size in _pow2_chunks(limit):
        @pl.when((n & size) != 0)
        def _(size=size):
            fn(pl.multiple_of(n & ~(2 * size - 1), RUN_ALIGN), size)


def _wait_rows(n, limit, src, dst, sem):
    _for_each_chunk(n, limit, lambda off, size: pltpu.make_async_copy(
        src.at[pl.ds(0, size)], dst.at[pl.ds(0, size)], sem).wait())


def _dispatch_kernel(n8_ref, loc_ref, run_ref, tot_ref, tailn_ref, tails_ref,
                     xn_ref, idx_ref, lrank_ref, locv_ref, xs_hbm, pos_ref, *scratch):
    *bufs, zbuf, sems, zsem = scratch
    g = pl.program_id(0)
    nbuf = len(bufs)
    r_loc = bufs[0].shape[0]
    tb = xn_ref.shape[0] // nbuf
    ne = tailn_ref.shape[0] - 1
    zrows = zbuf.shape[0]
    last_blk = pl.num_programs(0) * nbuf - 1

    def send(blk, s, live):
        for e in range(ne):
            n = jnp.where(live, n8_ref[blk * ne + e], 0)
            src0 = loc_ref[blk * ne + e]
            dst0 = run_ref[blk * ne + e]
            _for_each_chunk(n, tb, lambda off, size, src0=src0, dst0=dst0: pltpu.make_async_copy(
                bufs[s].at[pl.ds(pl.multiple_of(src0 + off, RUN_ALIGN), size)],
                xs_hbm.at[pl.ds(pl.multiple_of(dst0 + off, RUN_ALIGN), size)], sems.at[s]).start())

    def sent(blk, s, live):
        _wait_rows(jnp.where(live, tot_ref[blk], 0), r_loc, bufs[s], xs_hbm, sems.at[s])

    def zero_rest(i, carry):
        dst = pl.multiple_of(tails_ref[ne] + i * zrows, zrows)
        pltpu.make_async_copy(zbuf, xs_hbm.at[pl.ds(dst, zrows)], zsem).start()
        return carry

    def wait_rest(i, carry):
        pltpu.make_async_copy(zbuf, xs_hbm.at[pl.ds(0, zrows)], zsem).wait()
        return carry

    @pl.when(g == 0)
    def _():
        zbuf[...] = jnp.zeros_like(zbuf)
        for e in range(ne):
            _for_each_chunk(tailn_ref[e], zrows, lambda off, size, e=e: pltpu.make_async_copy(
                zbuf.at[pl.ds(0, size)],
                xs_hbm.at[pl.ds(pl.multiple_of(tails_ref[e] + off, RUN_ALIGN), size)], zsem).start())
        lax.fori_loop(0, tailn_ref[ne], zero_rest, 0)

    iota_e = lax.broadcasted_iota(I32, (ne, tb), 0)
    iota_r = lax.broadcasted_iota(I32, (r_loc, tb), 0)
    for u in range(nbuf):
        blk = g * nbuf + u
        cols = slice(u * tb, (u + 1) * tb)
        sent(jnp.maximum(blk - nbuf, 0), u, blk >= nbuf)
        send(jnp.maximum(blk - 1, 0), (u - 1) % nbuf, blk >= 1)

        loc_col = locv_ref[u].astype(F32)
        pos = []
        for k in range(TOP_K):
            run0 = jnp.sum(jnp.where(iota_e == idx_ref[k:k + 1, cols], loc_col, 0.0), axis=0, keepdims=True)
            pos.append(run0.astype(I32) + lrank_ref[k:k + 1, cols])
            pos_ref[k:k + 1, cols] = pos[k]

        hit = iota_r == pos[0]
        for k in range(1, TOP_K):
            hit = jnp.logical_or(hit, iota_r == pos[k])
        bufs[u][...] = _pack_halves(_dot(hit.astype(BF16), xn_ref[cols, :]))

    @pl.when(g == pl.num_programs(0) - 1)
    def _():
        send(last_blk, nbuf - 1, True)
        for s_ in range(nbuf):
            sent(last_blk - (nbuf - 1 - s_), s_, True)
        for e in range(ne):
            _wait_rows(tailn_ref[e], zrows, zbuf, xs_hbm, zsem)
        lax.fori_loop(0, tailn_ref[ne], wait_rest, 0)


def _dispatch(xn, idx_t, lrank_t, loc, tables, p_rows, r_loc):
    t, d = xn.shape
    tb, nu = ROUTE_ROWS, ROUTE_UNROLL
    nblk, ne = loc.shape
    assert nblk % nu == 0
    tok = pl.BlockSpec((TOP_K, nu * tb), lambda g, *_: (0, g))
    grid_spec = pltpu.PrefetchScalarGridSpec(
        num_scalar_prefetch=6,
        grid=(nblk // nu,),
        in_specs=[pl.BlockSpec((nu * tb, d), lambda g, *_: (g, 0)), tok, tok,
                  pl.BlockSpec((nu, ne, 1), lambda g, *_: (g, 0, 0))],
        out_specs=[pl.BlockSpec(memory_space=pl.ANY), tok],
        scratch_shapes=[pltpu.VMEM((r_loc, d // 2), U32)] * nu + [
            pltpu.VMEM((EXPERT_ROWS // 2, d // 2), U32), pltpu.SemaphoreType.DMA((nu,)), pltpu.SemaphoreType.DMA],
    )
    return pl.pallas_call(
        _dispatch_kernel,
        grid_spec=grid_spec,
        out_shape=[jax.ShapeDtypeStruct((p_rows, d // 2), U32), jax.ShapeDtypeStruct((TOP_K, t), I32)],
        compiler_params=pltpu.CompilerParams(dimension_semantics=("arbitrary",),
                                             vmem_limit_bytes=VMEM_LIMIT),
        name="dispatch",
    )(*tables, xn, idx_t, lrank_t, loc.reshape(nblk, ne, 1))


def _expert_kernel(blk0_ref, nblk_ref, nu_ref, xs_hbm, wgu_ref, bgu_ref, wdn_ref, bdn_ref, y_hbm,
                   xbuf, ybuf, zbuf, in_sem, out_sem, zsem, wgu_bf, wdn_bf, next_ref):
    e = pl.program_id(0)
    nbuf, bm, dh = xbuf.shape
    zrows = zbuf.shape[0]
    de = wdn_ref.shape[0]
    n_used = nu_ref[0]

    def x_copy(g):
        return pltpu.make_async_copy(xs_hbm.at[pl.ds(pl.multiple_of(g * bm, bm), bm)], xbuf.at[g % nbuf],
                                     in_sem.at[g % nbuf])

    def y_copy(g):
        return pltpu.make_async_copy(ybuf.at[g % nbuf], y_hbm.at[pl.ds(pl.multiple_of(g * bm, bm), bm)],
                                     out_sem.at[g % nbuf])

    def prefetch_until(limit):
        def start(g):
            x_copy(g).start()
            return g + 1
        next_ref[0] = lax.while_loop(lambda g: g < jnp.minimum(limit, n_used), start, next_ref[0])

    n_zero = (y_hbm.shape[0] - n_used * bm) // zrows

    def z_copy(i):
        return pltpu.make_async_copy(
            zbuf, y_hbm.at[pl.ds(pl.multiple_of(n_used * bm + i * zrows, zrows), zrows)], zsem)

    @pl.when(e == 0)
    def _():
        next_ref[0] = 0
        zbuf[...] = jnp.zeros_like(zbuf)
        lax.fori_loop(0, n_zero, lambda i, c: (z_copy(i).start(), c)[1], 0)

    wgu_bf[...] = wgu_ref[...].astype(BF16)
    wdn_bf[...] = wdn_ref[...].astype(BF16)

    def mlp(xu):
        x_lo, x_hi = _unpack_halves(xu)

        def proj(c0, c1):
            return (_dot(x_lo, wgu_bf[0:dh, c0:c1]) + _dot(x_hi, wgu_bf[dh:2 * dh, c0:c1])
                    + bgu_ref[:, c0:c1])

        nch = 4
        ch = de // nch
        acts = []
        for c in range(nch):
            gl = jnp.minimum(proj(c * ch, (c + 1) * ch), SWIGLU_LIMIT)
            up = jnp.clip(proj(de + c * ch, de + (c + 1) * ch), -SWIGLU_LIMIT, SWIGLU_LIMIT)
            acts.append(((up + 1.0) * (gl * jax.nn.sigmoid(SWIGLU_ALPHA * gl))).astype(BF16))
        y = bdn_ref[...]
        for c in range(nch):
            y = y + _dot(acts[c], wdn_bf[c * ch:(c + 1) * ch, :])
        return _pack_halves(y.astype(BF16).astype(F32))

    def run(g, units):
        prefetch_until(g + units + 2)
        for k in range(units):
            x_copy(g + k).wait()

            @pl.when(g + k >= nbuf)
            def _(k=k):
                y_copy(g + k - nbuf).wait()
        if units == 1:
            ybuf[g % nbuf] = mlp(xbuf[g % nbuf])
        else:
            slots = pl.ds(pl.multiple_of(g % nbuf, 2), 2)
            ybuf[slots] = mlp(xbuf[slots].reshape(2 * bm, dh)).reshape(2, bm, dh)
        for k in range(units):
            y_copy(g + k).start()

    g0 = blk0_ref[e]
    n = nblk_ref[e]
    lead = jnp.logical_and(g0 % 2 == 1, n > 0).astype(I32)
    pairs = (n - lead) // 2

    @pl.when(lead == 1)
    def _():
        run(g0, 1)

    lax.fori_loop(0, pairs, lambda j, c: (run(g0 + lead + 2 * j, 2), c)[1], 0)

    @pl.when((n - lead) % 2 == 1)
    def _():
        run(g0 + lead + 2 * pairs, 1)

    @pl.when(e == pl.num_programs(0) - 1)
    def _():
        for back in range(nbuf, 0, -1):
            @pl.when(n_used >= back)
            def _(back=back):
                y_copy(n_used - back).wait()
        lax.fori_loop(0, n_zero, lambda i, c: (z_copy(i).wait(), c)[1], 0)


def _experts(xs, blk0, nblk, n_used, w_gate_up, b_gate_up, w_down, b_down):
    p, dh = xs.shape
    ne, d, de2 = w_gate_up.shape
    de = de2 // 2
    bm = EXPERT_ROWS
    per_expert = lambda shape: pl.BlockSpec((None,) + shape, lambda e, *_: (e, 0, 0))
    grid_spec = pltpu.PrefetchScalarGridSpec(
        num_scalar_prefetch=3,
        grid=(ne,),
        in_specs=[pl.BlockSpec(memory_space=pl.ANY),
                  per_expert((d, de2)), per_expert((1, de2)), per_expert((de, d)), per_expert((1, d))],
        out_specs=pl.BlockSpec(memory_space=pl.ANY),
        scratch_shapes=[pltpu.VMEM((4, bm, dh), U32), pltpu.VMEM((4, bm, dh), U32),
                        pltpu.VMEM((bm // 4, dh), U32),
                        pltpu.SemaphoreType.DMA((4,)), pltpu.SemaphoreType.DMA((4,)), pltpu.SemaphoreType.DMA,
                        pltpu.VMEM((d, de2), BF16), pltpu.VMEM((de, d), BF16), pltpu.SMEM((1,), I32)],
    )
    return pl.pallas_call(
        _expert_kernel,
        grid_spec=grid_spec,
        out_shape=jax.ShapeDtypeStruct((p, dh), U32),
        compiler_params=pltpu.CompilerParams(dimension_semantics=("arbitrary",),
                                             vmem_limit_bytes=VMEM_LIMIT),
        name="experts",
    )(blk0, nblk, n_used, xs, w_gate_up, b_gate_up.reshape(ne, 1, de2), w_down, b_down.reshape(ne, 1, d))


def _combine_kernel(n8_ref, loc_ref, run_ref, tot_ref,
                    pos_ref, gate_ref, x2_ref, gf_ref, y_hbm, o_ref, *scratch):
    *bufs, sems = scratch
    g = pl.program_id(0)
    nbuf = len(bufs)
    r_loc = bufs[0].shape[0]
    tb = x2_ref.shape[0] // nbuf
    nblk = pl.num_programs(0) * nbuf
    ne = n8_ref.shape[0] // tot_ref.shape[0]

    def fetch(blk, s, live):
        for e in range(ne):
            n = jnp.where(live, n8_ref[blk * ne + e], 0)
            dst0 = loc_ref[blk * ne + e]
            src0 = run_ref[blk * ne + e]
            _for_each_chunk(n, tb, lambda off, size, src0=src0, dst0=dst0: pltpu.make_async_copy(
                y_hbm.at[pl.ds(pl.multiple_of(src0 + off, RUN_ALIGN), size)],
                bufs[s].at[pl.ds(pl.multiple_of(dst0 + off, RUN_ALIGN), size)], sems.at[s]).start())

    @pl.when(g == 0)
    def _():
        for buf in bufs:
            buf[...] = jnp.zeros_like(buf)
        for ahead in range(nbuf - 2):
            fetch(ahead, ahead, True)

    iota_c = lax.broadcasted_iota(I32, (tb, r_loc), 1)
    for u in range(nbuf):
        blk = g * nbuf + u
        rows = slice(u * tb, (u + 1) * tb)
        w = jnp.zeros((tb, r_loc), F32)
        for k in range(TOP_K):
            w = w + jnp.where(iota_c == pos_ref[rows, k:k + 1], gate_ref[rows, k:k + 1], 0.0)
        w_hi = w.astype(BF16)
        w_lo = (w - w_hi.astype(F32)).astype(BF16)
        _wait_rows(tot_ref[blk], r_loc, y_hbm, bufs[u], sems.at[u])
        ahead = blk + nbuf - 2
        fetch(jnp.minimum(ahead, nblk - 1), (u - 2) % nbuf, ahead < nblk)
        y_lo, y_hi = _unpack_halves(bufs[u][...])
        moe = jnp.concatenate([_dot(w_hi, y_lo) + _dot(w_lo, y_lo), _dot(w_hi, y_hi) + _dot(w_lo, y_hi)], axis=1)
        o_ref[rows, :] = _rms(x2_ref[rows, :] + moe, gf_ref[...])


def _combine(pos, gates, x2, y, norm_final, tables, r_loc):
    t, d = x2.shape
    rows = ROUTE_ROWS * ROUTE_UNROLL
    assert t % rows == 0
    grid_spec = pltpu.PrefetchScalarGridSpec(
        num_scalar_prefetch=4,
        grid=(t // rows,),
        in_specs=[pl.BlockSpec((rows, TOP_K), lambda g, *_: (g, 0)),
                  pl.BlockSpec((rows, TOP_K), lambda g, *_: (g, 0)),
                  pl.BlockSpec((rows, d), lambda g, *_: (g, 0)),
                  pl.BlockSpec((1, d), lambda g, *_: (0, 0)),
                  pl.BlockSpec(memory_space=pl.ANY)],
        out_specs=pl.BlockSpec((rows, d), lambda g, *_: (g, 0)),
        scratch_shapes=[pltpu.VMEM((r_loc, d // 2), U32)] * ROUTE_UNROLL + [pltpu.SemaphoreType.DMA((ROUTE_UNROLL,))],
    )
    return pl.pallas_call(
        _combine_kernel,
        grid_spec=grid_spec,
        out_shape=jax.ShapeDtypeStruct((t, d), F32),
        compiler_params=pltpu.CompilerParams(dimension_semantics=("arbitrary",),
                                             vmem_limit_bytes=VMEM_LIMIT),
        name="combine",
    )(*tables, pos, gates, x2, norm_final.reshape(1, d), y)


def _excl_cumsum(a, axis):
    n = a.shape[axis]
    a = jnp.moveaxis(a, axis, -1)
    earlier = jnp.arange(n)[None, :] < jnp.arange(n)[:, None]
    out = jnp.sum(jnp.where(earlier, a[..., None, :], 0), axis=-1)
    return jnp.moveaxis(out, -1, axis)


def _layout(cnt):
    nblk, ne = cnt.shape
    tb, bm = ROUTE_ROWS, EXPERT_ROWS
    n8 = (cnt + RUN_ALIGN - 1) // RUN_ALIGN * RUN_ALIGN
    loc = _excl_cumsum(n8, 1)
    tot = jnp.sum(n8, axis=1)
    size = jnp.sum(n8, axis=0)
    padded = (size + bm - 1) // bm * bm
    pstart = _excl_cumsum(padded, 0)
    pend = pstart + padded
    run = pstart[None, :] + _excl_cumsum(n8, 0)
    p_blocks = -(-(nblk * tb * TOP_K + nblk * ne * (RUN_ALIGN - 1) + ne * (bm - RUN_ALIGN)) // bm)
    n_used = (pend[-1] // bm).astype(I32)
    flat = lambda a: a.reshape(-1).astype(I32)
    tables = (flat(n8), flat(loc), flat(run), flat(tot))
    half = bm // 2
    tails = (flat(jnp.concatenate([padded - size, (p_blocks * bm - pend[-1:]) // half])),
             flat(jnp.concatenate([pstart + size, pend[-1:]])))
    r_loc = -(-(tb * TOP_K + ne * (RUN_ALIGN - 1)) // MXU_TILE) * MXU_TILE
    blocks = ((pstart // bm).astype(I32), (padded // bm).astype(I32), n_used.reshape(1))
    return tables, tails, loc.astype(I32), blocks, p_blocks * bm, r_loc


def kernel(x, mem, norm_mix, w_in, w_pool_group, pool_scale, w_pool_proj, conv_w, conv_b, lru_w_a, lru_b_a, lru_w_x, lru_b_x, lru_lambda, w_lru_proj, w_mix_out, norm_xattn, norm_mem, w_q, w_kv, w_o, norm_moe, w_router, b_router, w_gate_up, b_gate_up, w_down, b_down, norm_final):
    nb, s_len, d = x.shape
    m_len = mem.shape[1]
    assert norm_mix.shape[0] == 1, "single-layer stack"
    l = 0
    x1 = _mixer(x, norm_mix[l], w_in[l], w_pool_group[l], pool_scale[l], w_pool_proj[l],
                conv_w[l], conv_b[l], lru_w_a[l], lru_b_a[l], lru_w_x[l], lru_b_x[l],
                lru_lambda[l], w_lru_proj[l], w_mix_out[l])
    kv = _kv_proj(mem.reshape(nb * m_len, d), norm_mem[l], w_kv[l]).reshape(nb, m_len, 2 * d)
    x2, xn, idx_t, gate_t, lrank_t, cnt = _attention(
        x1.reshape(s_len, nb, d), kv, norm_xattn[l], w_q[l], w_o[l],
        norm_moe[l], w_router[l], b_router[l])
    tables, tails, loc, blocks, p_rows, r_loc = _layout(cnt.reshape(cnt.shape[0], -1))
    xs, pos_t = _dispatch(xn, idx_t, lrank_t, loc, tables + tails, p_rows, r_loc)
    y = _experts(xs, *blocks, w_gate_up[l], b_gate_up[l], w_down[l], b_down[l])
    out = _combine(pos_t.T, gate_t.T, x2.reshape(nb * s_len, d), y, norm_final, tables, r_loc)
    return out.reshape(nb, s_len, d)
```

```python
import functools

import jax
import jax.numpy as jnp
from jax import lax
from jax.experimental import pallas as pl
from jax.experimental.pallas import tpu as pltpu

POOL_WINDOWS = (2, 4, 8, 16)
N_GROUPS = 4
CONV_WIDTH = 4
RG_C = 8.0
TOP_K = 4
SWIGLU_LIMIT = 7.0
SWIGLU_ALPHA = 1.702
RMS_EPS = 1e-6

MIX_STEPS = 32
KV_ROWS = 512
ATT_ROWS = 512
ROUTE_ROWS = 256
ROUTE_UNROLL = 4
EXPERT_ROWS = 512
RUN_ALIGN = 8
MXU_TILE = 256
VMEM_LIMIT = 52 * 1024 * 1024

BF16 = jnp.bfloat16
F32 = jnp.float32
I32 = jnp.int32
U32 = jnp.uint32


def _const_spec(shape):
    nd = len(shape)
    return pl.BlockSpec(shape, lambda *_: (0,) * nd, pipeline_mode=pl.Buffered(1))


def _rms(x, g):
    return x * lax.rsqrt(jnp.mean(x * x, axis=-1, keepdims=True) + RMS_EPS) * g


def _dot(a, b):
    return jnp.dot(a, b, preferred_element_type=F32)


def _dot_nt(a, b):
    return lax.dot_general(a, b, (((1,), (1,)), ((), ())), preferred_element_type=F32)


def _pack_halves(x):
    c = x.shape[1] // 2
    lo = lax.bitcast_convert_type(x[:, :c], U32)
    hi = lax.bitcast_convert_type(x[:, c:], U32)
    return (hi & jnp.uint32(0xFFFF0000)) | (lo >> 16)


def _unpack_halves(u):
    lo = lax.bitcast_convert_type(u << 16, F32).astype(BF16)
    hi = lax.bitcast_convert_type(u & jnp.uint32(0xFFFF0000), F32).astype(BF16)
    return lo, hi


def _mixer_kernel(x_hbm, nm_ref, win_ref, wpg_ref, psc_ref, wpp_ref, cw_ref, cb_ref,
                  wa_ref, ba_ref, wx_ref, bx_ref, lam_ref, wlp_ref, wmo_ref,
                  o_ref,
                  xbuf, xsem,
                  h_ref, up_ref, ul_ref, a_ref, b_ref, pm_ref, m_ref, mb_ref, t_ref, gl_ref, gb_ref, hc_ref,
                  *, nb, ts):
    rows, d = o_ref.shape
    gw = d // N_GROUPS
    halo_p = (POOL_WINDOWS[-1]) * nb
    halo_c = (CONV_WIDTH - 1) * nb
    c = pl.program_id(0)

    @pl.when(c == 0)
    def _():
        up_ref[0:halo_p, :] = jnp.zeros((halo_p, d), F32)
        ul_ref[0:halo_c, :] = jnp.zeros((halo_c, d), F32)
        hc_ref[...] = jnp.zeros_like(hc_ref)

    def x_copy(step, t, slot):
        return pltpu.make_async_copy(x_hbm.at[:, step * ts + t, :], xbuf.at[slot, t], xsem.at[slot])

    def fetch(step, slot, live):
        for t in range(ts):
            @pl.when(live)
            def _(t=t):
                x_copy(step, t, slot).start()

    slot = c % 2
    fetch(0, 0, c == 0)
    fetch(jnp.minimum(c + 1, pl.num_programs(0) - 1), 1 - slot, c + 1 < pl.num_programs(0))
    pltpu.make_async_copy(xbuf.at[1 - slot], xbuf.at[slot], xsem.at[slot]).wait()

    h_ref[...] = _rms(xbuf[slot].reshape(rows, d), nm_ref[...]).astype(BF16)

    t_glob = c * ts + lax.broadcasted_iota(I32, (rows, 1), 0) // nb

    up_ref[halo_p:halo_p + rows, :] = _dot(h_ref[...], win_ref[:, 0:d])
    ul_ref[halo_c:halo_c + rows, :] = _dot(h_ref[...], win_ref[:, d:2 * d])

    for g, w in enumerate(POOL_WINDOWS):
        cols = slice(g * gw, (g + 1) * gw)
        u = up_ref[halo_p:halo_p + rows, cols]
        acc = u
        for j in range(1, w):
            acc = acc + up_ref[halo_p - j * nb:halo_p - j * nb + rows, cols]
        cnt = jnp.minimum(t_glob + 1, w).astype(F32)
        p = acc / cnt - u
        pg = _dot(p.astype(BF16), wpg_ref[g]) * psc_ref[:, cols]
        pm_ref[:, cols] = pg.astype(BF16)
    t_ref[...] = _dot(h_ref[...], win_ref[:, 3 * d:4 * d])
    m_ref[...] = _dot(pm_ref[...], wpp_ref[...])

    for g in range(N_GROUPS):
        cols = slice(g * gw, (g + 1) * gw)
        xr = cb_ref[:, cols]
        for k in range(CONV_WIDTH):
            off = halo_c - (CONV_WIDTH - 1 - k) * nb
            xr = xr + ul_ref[off:off + rows, cols] * cw_ref[k:k + 1, cols]
        xrb = xr.astype(BF16)
        r = jax.nn.sigmoid(_dot(xrb, wa_ref[g]) + ba_ref[:, cols])
        i = jax.nn.sigmoid(_dot(xrb, wx_ref[g]) + bx_ref[:, cols])
        gl_ref[:, cols] = _dot(h_ref[...], win_ref[:, 2 * d + g * gw:2 * d + (g + 1) * gw])
        gb_ref[:, cols] = _dot(h_ref[...], win_ref[:, 4 * d + g * gw:4 * d + (g + 1) * gw])
        lam = lam_ref[:, cols]
        log_sig = jnp.minimum(lam, 0.0) - jnp.log(1.0 + jnp.exp(-jnp.abs(lam)))
        a = jnp.exp((RG_C * r) * log_sig)
        mult = jnp.sqrt(jnp.maximum(1.0 - a * a, 0.0))
        mult = jnp.where(t_glob == 0, 1.0, mult)
        a_ref[:, cols] = a
        b_ref[:, cols] = mult * i * xr
    m_ref[...] = jax.nn.sigmoid(t_ref[...]) * m_ref[...]

    def scan_step(t, hprev):
        sl = pl.ds(pl.multiple_of(t * nb, nb), nb)
        hn = a_ref[sl, :] * hprev + b_ref[sl, :]
        b_ref[sl, :] = hn
        return hn

    hc_ref[...] = lax.fori_loop(0, ts, scan_step, hc_ref[...], unroll=4)

    half = rows // 2
    for hs in (slice(0, half), slice(half, rows)):
        pm_ref[hs, :] = (b_ref[hs, :] * jax.nn.gelu(gl_ref[hs, :], approximate=True)).astype(BF16)
    for hs in (slice(0, half), slice(half, rows)):
        yb = _dot(pm_ref[hs, :], wlp_ref[...])
        mb_ref[hs, :] = (m_ref[hs, :] + jax.nn.sigmoid(gb_ref[hs, :]) * yb).astype(BF16)
    for i, hs in enumerate((slice(0, half), slice(half, rows))):
        x_half = xbuf[slot, i * (ts // 2):(i + 1) * (ts // 2)].reshape(half, d)
        o_ref[hs, :] = x_half + _dot(mb_ref[hs, :], wmo_ref[...])

    up_ref[0:halo_p, :] = up_ref[rows:rows + halo_p, :]
    ul_ref[0:halo_c, :] = ul_ref[rows:rows + halo_c, :]


def _mixer(x, norm_mix, w_in, w_pool_group, pool_scale, w_pool_proj, conv_w, conv_b,
           lru_w_a, lru_b_a, lru_w_x, lru_b_x, lru_lambda, w_lru_proj, w_mix_out):
    nb, s_len, d = x.shape
    ts = MIX_STEPS
    rows = ts * nb
    n_rows = s_len * nb
    assert s_len % ts == 0 and ts % 2 == 0 and ts >= POOL_WINDOWS[-1] and nb % 8 == 0
    row2 = lambda v: v.reshape(1, -1)
    args = (x, row2(norm_mix), w_in.astype(BF16), w_pool_group.astype(BF16), row2(pool_scale),
            w_pool_proj.astype(BF16), conv_w, row2(conv_b), lru_w_a.astype(BF16), row2(lru_b_a),
            lru_w_x.astype(BF16), row2(lru_b_x), row2(lru_lambda), w_lru_proj.astype(BF16),
            w_mix_out.astype(BF16))
    in_specs = [pl.BlockSpec(memory_space=pl.ANY)] + [_const_spec(a.shape) for a in args[1:]]
    halo_p = POOL_WINDOWS[-1] * nb
    halo_c = (CONV_WIDTH - 1) * nb
    return pl.pallas_call(
        functools.partial(_mixer_kernel, nb=nb, ts=ts),
        grid=(n_rows // rows,),
        in_specs=in_specs,
        out_specs=pl.BlockSpec((rows, d), lambda c: (c, 0)),
        out_shape=jax.ShapeDtypeStruct((n_rows, d), F32),
        scratch_shapes=[
            pltpu.VMEM((2, ts, nb, d), F32),
            pltpu.SemaphoreType.DMA((2,)),
            pltpu.VMEM((rows, d), BF16),
            pltpu.VMEM((halo_p + rows, d), F32),
            pltpu.VMEM((halo_c + rows, d), F32),
            pltpu.VMEM((rows, d), F32),
            pltpu.VMEM((rows, d), F32),
            pltpu.VMEM((rows, d), BF16),
            pltpu.VMEM((rows, d), F32),
            pltpu.VMEM((rows, d), BF16),
            pltpu.VMEM((rows, d), F32),
            pltpu.VMEM((rows, d), F32),
            pltpu.VMEM((rows, d), F32),
            pltpu.VMEM((nb, d), F32),
        ],
        compiler_params=pltpu.CompilerParams(dimension_semantics=("arbitrary",),
                                             vmem_limit_bytes=VMEM_LIMIT),
        name="mixer",
    )(*args)


def _kv_kernel(m_ref, g_ref, w_ref, o_ref, w_bf):
    @pl.when(pl.program_id(0) == 0)
    def _():
        w_bf[...] = w_ref[...].astype(BF16)

    o_ref[...] = _dot(_rms(m_ref[...], g_ref[...]).astype(BF16), w_bf[...]).astype(BF16)


def _kv_proj(mem2d, norm_mem, w_kv):
    n, d = mem2d.shape
    assert n % KV_ROWS == 0
    return pl.pallas_call(
        _kv_kernel,
        grid=(n // KV_ROWS,),
        in_specs=[pl.BlockSpec((KV_ROWS, d), lambda i: (i, 0)),
                  _const_spec((1, d)), _const_spec((d, 2 * d))],
        out_specs=pl.BlockSpec((KV_ROWS, 2 * d), lambda i: (i, 0)),
        out_shape=jax.ShapeDtypeStruct((n, 2 * d), BF16),
        scratch_shapes=[pltpu.VMEM((d, 2 * d), BF16)],
        compiler_params=pltpu.CompilerParams(dimension_semantics=("arbitrary",),
                                             vmem_limit_bytes=VMEM_LIMIT),
        name="kv_proj",
    )(mem2d, norm_mem.reshape(1, d), w_kv)


def _attn_kernel(x_hbm, kv_ref, gx_ref, wq_ref, wo_ref, gm_ref, wr_ref, br_ref,
                 x2_ref, xn_ref, idx_ref, gate_ref, lrank_ref, cnt_ref,
                 xbuf, xsem, o_scr, xh_scr, xl_scr, wq_bf, wo_bf, *, nq):
    tq, d = x2_ref.shape
    hd = d // N_GROUPS
    ne = br_ref.shape[0]
    tb = ROUTE_ROWS
    n = pl.program_id(0)
    n_blocks = pl.num_programs(0) - 1
    cur = jnp.minimum(n, n_blocks - 1)
    slot = cur % 2
    prev = (n + 1) % 2

    def x_copy(blk, s):
        return pltpu.make_async_copy(x_hbm.at[pl.ds((blk % nq) * tq, tq), blk // nq, :], xbuf.at[s], xsem.at[s])

    @pl.when(n == 0)
    def _():
        x_copy(0, 0).start()
        xh_scr[...] = jnp.zeros_like(xh_scr)
        xl_scr[...] = jnp.zeros_like(xl_scr)
        wq_bf[...] = wq_ref[...].astype(BF16)
        wo_bf[...] = wo_ref[...].astype(BF16)

    @pl.when(n + 1 < n_blocks)
    def _():
        x_copy(n + 1, 1 - slot).start()

    @pl.when(n < n_blocks)
    def _():
        x_copy(n, slot).wait()

    x = xbuf[slot]

    works = []
    for sb in range(tq // tb):
        rows = slice(sb * tb, (sb + 1) * tb)
        ph = _dot_nt(wr_ref[...], xh_scr[prev, rows, :])
        pl_ = _dot_nt(wr_ref[0:ne, :], xl_scr[prev, rows, :])
        works.append(ph[0:ne] + ph[ne:2 * ne] + pl_ + br_ref[...])

    q = _dot(_rms(x, gx_ref[...]).astype(BF16), wq_bf[...]).astype(BF16)

    iota_f = lax.broadcasted_iota(I32, (ne, tb), 0).astype(F32)
    picked = []
    for work in works:
        vals, idxs, sels = [], [], []
        for _ in range(TOP_K):
            m = jnp.max(work, axis=0, keepdims=True)
            idx = jnp.min(jnp.where(work == m, iota_f, float(ne)), axis=0, keepdims=True)
            sel = iota_f == idx
            vals.append(m)
            idxs.append(idx.astype(I32))
            sels.append(sel)
            work = jnp.where(sel, -jnp.inf, work)
        onehot = jnp.zeros((ne, tb), F32)
        for sel in sels:
            onehot = onehot + sel.astype(F32)
        picked.append((vals, idxs, sels, onehot.astype(BF16)))

    scores = [_dot_nt(q[:, h * hd:(h + 1) * hd], kv_ref[:, h * hd:(h + 1) * hd]) * (hd ** -0.5)
              for h in range(N_GROUPS)]
    for h, s in enumerate(scores):
        v = kv_ref[:, d + h * hd:d + (h + 1) * hd]
        e = jnp.exp(s - jnp.max(s, axis=-1, keepdims=True))
        p = e / jnp.sum(e, axis=-1, keepdims=True)
        o_scr[:, h * hd:(h + 1) * hd] = _dot(p.astype(BF16), v).astype(BF16)

    before = (lax.broadcasted_iota(I32, (tb, tb), 0) < lax.broadcasted_iota(I32, (tb, tb), 1)
              ).astype(BF16)
    for sb, (vals, idxs, sels, oh16) in enumerate(picked):
        rows = slice(sb * tb, (sb + 1) * tb)
        ex = [jnp.exp(v - vals[0]) for v in vals]
        den = ex[0] + ex[1] + ex[2] + ex[3]
        prefix = _dot(oh16, before)
        lr = [jnp.sum(jnp.where(sel, prefix, 0.0), axis=0, keepdims=True).astype(I32) for sel in sels]
        idx_ref[:, rows] = jnp.concatenate(idxs, axis=0)
        gate_ref[:, rows] = jnp.concatenate([e_ / den for e_ in ex], axis=0)
        lrank_ref[:, rows] = jnp.concatenate(lr, axis=0)
        cnt_ref[sb] = _dot_nt(jnp.ones((1, tb), BF16), oh16).astype(I32)

    x2 = x + _dot(o_scr[...], wo_bf[...])
    x2_ref[...] = x2
    xn = _rms(x2, gm_ref[...])
    xh = xn.astype(BF16)
    xn_ref[...] = xh
    xh_scr[n % 2] = xh
    xl_scr[n % 2] = (xn - xh.astype(F32)).astype(BF16)


def _attention(x1, kv, norm_xattn, w_q, w_o, norm_moe, w_router, b_router):
    s_len, nb, d = x1.shape
    m = kv.shape[1]
    ne = w_router.shape[-1]
    tq, tb = ATT_ROWS, ROUTE_ROWS
    nq = s_len // tq
    n_blocks = nb * nq
    t = nb * s_len
    assert s_len % tq == 0 and tq % tb == 0
    wr_hi = w_router.astype(BF16)
    wr_lo = (w_router - wr_hi.astype(F32)).astype(BF16)
    wr2t = jnp.concatenate([wr_hi, wr_lo], axis=1).T
    cur = lambda n: jnp.minimum(n, n_blocks - 1)
    routed = lambda n: (0, jnp.maximum(n - 1, 0))
    return pl.pallas_call(
        functools.partial(_attn_kernel, nq=nq),
        grid=(n_blocks + 1,),
        in_specs=[pl.BlockSpec(memory_space=pl.ANY),
                  pl.BlockSpec((None, m, 2 * d), lambda n: (cur(n) // nq, 0, 0)),
                  _const_spec((1, d)), _const_spec((d, d)), _const_spec((d, d)),
                  _const_spec((1, d)), _const_spec((2 * ne, d)), _const_spec((ne, 1))],
        out_specs=[pl.BlockSpec((None, tq, d), lambda n: (cur(n) // nq, cur(n) % nq, 0)),
                   pl.BlockSpec((tq, d), lambda n: (cur(n), 0)),
                   pl.BlockSpec((TOP_K, tq), routed),
                   pl.BlockSpec((TOP_K, tq), routed),
                   pl.BlockSpec((TOP_K, tq), routed),
                   pl.BlockSpec((tq // tb, 1, ne), lambda n: (jnp.maximum(n - 1, 0), 0, 0))],
        out_shape=[jax.ShapeDtypeStruct((nb, s_len, d), F32),
                   jax.ShapeDtypeStruct((t, d), BF16),
                   jax.ShapeDtypeStruct((TOP_K, t), I32),
                   jax.ShapeDtypeStruct((TOP_K, t), F32),
                   jax.ShapeDtypeStruct((TOP_K, t), I32),
                   jax.ShapeDtypeStruct((t // tb, 1, ne), I32)],
        scratch_shapes=[pltpu.VMEM((2, tq, d), F32), pltpu.SemaphoreType.DMA((2,)),
                        pltpu.VMEM((tq, d), BF16), pltpu.VMEM((2, tq, d), BF16), pltpu.VMEM((2, tq, d), BF16),
                        pltpu.VMEM((d, d), BF16), pltpu.VMEM((d, d), BF16)],
        compiler_params=pltpu.CompilerParams(dimension_semantics=("arbitrary",),
                                             vmem_limit_bytes=VMEM_LIMIT),
        name="attention",
    )(x1, kv, norm_xattn.reshape(1, d), w_q, w_o,
      norm_moe.reshape(1, d), wr2t, b_router.reshape(ne, 1))


def _pow2_chunks(limit):
    sizes = []
    c = RUN_ALIGN
    while c <= limit:
        sizes.append(c)
        c *= 2
    return sizes[::-1]


def _for_each_chunk(n, limit, fn):
    for size in _pow2_chunks(limit):
        @pl.when((n & size) != 0)
        def _(size=size):
            fn(pl.multiple_of(n & ~(2 * size - 1), RUN_ALIGN), size)


def _wait_rows(n, limit, src, dst, sem):
    _for_each_chunk(n, limit, lambda off, size: pltpu.make_async_copy(
        src.at[pl.ds(0, size)], dst.at[pl.ds(0, size)], sem).wait())


def _dispatch_kernel(n8_ref, loc_ref, run_ref, tot_ref, tailn_ref, tails_ref,
                     xn_ref, idx_ref, lrank_ref, locv_ref, xs_hbm, pos_ref, *scratch):
    *bufs, zbuf, sems, zsem = scratch
    g = pl.program_id(0)
    nbuf = len(bufs)
    r_loc = bufs[0].shape[0]
    tb = xn_ref.shape[0] // nbuf
    ne = tailn_ref.shape[0] - 1
    zrows = zbuf.shape[0]
    last_blk = pl.num_programs(0) * nbuf - 1

    def send(blk, s, live):
        for e in range(ne):
            n = jnp.where(live, n8_ref[blk * ne + e], 0)
            src0 = loc_ref[blk * ne + e]
            dst0 = run_ref[blk * ne + e]
            _for_each_chunk(n, tb, lambda off, size, src0=src0, dst0=dst0: pltpu.make_async_copy(
                bufs[s].at[pl.ds(pl.multiple_of(src0 + off, RUN_ALIGN), size)],
                xs_hbm.at[pl.ds(pl.multiple_of(dst0 + off, RUN_ALIGN), size)], sems.at[s]).start())

    def sent(blk, s, live):
        _wait_rows(jnp.where(live, tot_ref[blk], 0), r_loc, bufs[s], xs_hbm, sems.at[s])

    def zero_rest(i, carry):
        dst = pl.multiple_of(tails_ref[ne] + i * zrows, zrows)
        pltpu.make_async_copy(zbuf, xs_hbm.at[pl.ds(dst, zrows)], zsem).start()
        return carry

    def wait_rest(i, carry):
        pltpu.make_async_copy(zbuf, xs_hbm.at[pl.ds(0, zrows)], zsem).wait()
        return carry

    @pl.when(g == 0)
    def _():
        zbuf[...] = jnp.zeros_like(zbuf)
        for e in range(ne):
            _for_each_chunk(tailn_ref[e], zrows, lambda off, size, e=e: pltpu.make_async_copy(
                zbuf.at[pl.ds(0, size)],
                xs_hbm.at[pl.ds(pl.multiple_of(tails_ref[e] + off, RUN_ALIGN), size)], zsem).start())
        lax.fori_loop(0, tailn_ref[ne], zero_rest, 0)

    iota_e = lax.broadcasted_iota(I32, (ne, tb), 0)
    iota_r = lax.broadcasted_iota(I32, (r_loc, tb), 0)
    for u in range(nbuf):
        blk = g * nbuf + u
        cols = slice(u * tb, (u + 1) * tb)
        sent(jnp.maximum(blk - nbuf, 0), u, blk >= nbuf)
        send(jnp.maximum(blk - 1, 0), (u - 1) % nbuf, blk >= 1)

        loc_col = locv_ref[u].astype(F32)
        pos = []
        for k in range(TOP_K):
            run0 = jnp.sum(jnp.where(iota_e == idx_ref[k:k + 1, cols], loc_col, 0.0), axis=0, keepdims=True)
            pos.append(run0.astype(I32) + lrank_ref[k:k + 1, cols])
            pos_ref[k:k + 1, cols] = pos[k]

        hit = iota_r == pos[0]
        for k in range(1, TOP_K):
            hit = jnp.logical_or(hit, iota_r == pos[k])
        bufs[u][...] = _pack_halves(_dot(hit.astype(BF16), xn_ref[cols, :]))

    @pl.when(g == pl.num_programs(0) - 1)
    def _():
        send(last_blk, nbuf - 1, True)
        for s_ in range(nbuf):
            sent(last_blk - (nbuf - 1 - s_), s_, True)
        for e in range(ne):
            _wait_rows(tailn_ref[e], zrows, zbuf, xs_hbm, zsem)
        lax.fori_loop(0, tailn_ref[ne], wait_rest, 0)


def _dispatch(xn, idx_t, lrank_t, loc, tables, p_rows, r_loc):
    t, d = xn.shape
    tb, nu = ROUTE_ROWS, ROUTE_UNROLL
    nblk, ne = loc.shape
    assert nblk % nu == 0
    tok = pl.BlockSpec((TOP_K, nu * tb), lambda g, *_: (0, g))
    grid_spec = pltpu.PrefetchScalarGridSpec(
        num_scalar_prefetch=6,
        grid=(nblk // nu,),
        in_specs=[pl.BlockSpec((nu * tb, d), lambda g, *_: (g, 0)), tok, tok,
                  pl.BlockSpec((nu, ne, 1), lambda g, *_: (g, 0, 0))],
        out_specs=[pl.BlockSpec(memory_space=pl.ANY), tok],
        scratch_shapes=[pltpu.VMEM((r_loc, d // 2), U32)] * nu + [
            pltpu.VMEM((EXPERT_ROWS // 2, d // 2), U32), pltpu.SemaphoreType.DMA((nu,)), pltpu.SemaphoreType.DMA],
    )
    return pl.pallas_call(
        _dispatch_kernel,
        grid_spec=grid_spec,
        out_shape=[jax.ShapeDtypeStruct((p_rows, d // 2), U32), jax.ShapeDtypeStruct((TOP_K, t), I32)],
        compiler_params=pltpu.CompilerParams(dimension_semantics=("arbitrary",),
                                             vmem_limit_bytes=VMEM_LIMIT),
        name="dispatch",
    )(*tables, xn, idx_t, lrank_t, loc.reshape(nblk, ne, 1))


def _expert_kernel(blk0_ref, nblk_ref, nu_ref, xs_hbm, wgu_ref, bgu_ref, wdn_ref, bdn_ref, y_hbm,
                   xbuf, ybuf, zbuf, in_sem, out_sem, zsem, wgu_bf, wdn_bf, next_ref):
    e = pl.program_id(0)
    nbuf, bm, dh = xbuf.shape
    zrows = zbuf.shape[0]
    de = wdn_ref.shape[0]
    n_used = nu_ref[0]

    def x_copy(g):
        return pltpu.make_async_copy(xs_hbm.at[pl.ds(pl.multiple_of(g * bm, bm), bm)], xbuf.at[g % nbuf],
                                     in_sem.at[g % nbuf])

    def y_copy(g):
        return pltpu.make_async_copy(ybuf.at[g % nbuf], y_hbm.at[pl.ds(pl.multiple_of(g * bm, bm), bm)],
                                     out_sem.at[g % nbuf])

    def prefetch_until(limit):
        def start(g):
            x_copy(g).start()
            return g + 1
        next_ref[0] = lax.while_loop(lambda g: g < jnp.minimum(limit, n_used), start, next_ref[0])

    n_zero = (y_hbm.shape[0] - n_used * bm) // zrows

    def z_copy(i):
        return pltpu.make_async_copy(
            zbuf, y_hbm.at[pl.ds(pl.multiple_of(n_used * bm + i * zrows, zrows), zrows)], zsem)

    @pl.when(e == 0)
    def _():
        next_ref[0] = 0
        zbuf[...] = jnp.zeros_like(zbuf)
        lax.fori_loop(0, n_zero, lambda i, c: (z_copy(i).start(), c)[1], 0)

    wgu_bf[...] = wgu_ref[...].astype(BF16)
    wdn_bf[...] = wdn_ref[...].astype(BF16)

    def mlp(xu):
        x_lo, x_hi = _unpack_halves(xu)

        def proj(c0, c1):
            return (_dot(x_lo, wgu_bf[0:dh, c0:c1]) + _dot(x_hi, wgu_bf[dh:2 * dh, c0:c1])
                    + bgu_ref[:, c0:c1])

        nch = 4
        ch = de // nch
        acts = []
        for c in range(nch):
            gl = jnp.minimum(proj(c * ch, (c + 1) * ch), SWIGLU_LIMIT)
            up = jnp.clip(proj(de + c * ch, de + (c + 1) * ch), -SWIGLU_LIMIT, SWIGLU_LIMIT)
            acts.append(((up + 1.0) * (gl * jax.nn.sigmoid(SWIGLU_ALPHA * gl))).astype(BF16))
        y = bdn_ref[...]
        for c in range(nch):
            y = y + _dot(acts[c], wdn_bf[c * ch:(c + 1) * ch, :])
        return _pack_halves(y.astype(BF16).astype(F32))

    def run(g, units):
        prefetch_until(g + units + 2)
        for k in range(units):
            x_copy(g + k).wait()

            @pl.when(g + k >= nbuf)
            def _(k=k):
                y_copy(g + k - nbuf).wait()
        if units == 1:
            ybuf[g % nbuf] = mlp(xbuf[g % nbuf])
        else:
            slots = pl.ds(pl.multiple_of(g % nbuf, 2), 2)
            ybuf[slots] = mlp(xbuf[slots].reshape(2 * bm, dh)).reshape(2, bm, dh)
        for k in range(units):
            y_copy(g + k).start()

    g0 = blk0_ref[e]
    n = nblk_ref[e]
    lead = jnp.logical_and(g0 % 2 == 1, n > 0).astype(I32)
    pairs = (n - lead) // 2

    @pl.when(lead == 1)
    def _():
        run(g0, 1)

    lax.fori_loop(0, pairs, lambda j, c: (run(g0 + lead + 2 * j, 2), c)[1], 0)

    @pl.when((n - lead) % 2 == 1)
    def _():
        run(g0 + lead + 2 * pairs, 1)

    @pl.when(e == pl.num_programs(0) - 1)
    def _():
        for back in range(nbuf, 0, -1):
            @pl.when(n_used >= back)
            def _(back=back):
                y_copy(n_used - back).wait()
        lax.fori_loop(0, n_zero, lambda i, c: (z_copy(i).wait(), c)[1], 0)


def _experts(xs, blk0, nblk, n_used, w_gate_up, b_gate_up, w_down, b_down):
    p, dh = xs.shape
    ne, d, de2 = w_gate_up.shape
    de = de2 // 2
    bm = EXPERT_ROWS
    per_expert = lambda shape: pl.BlockSpec((None,) + shape, lambda e, *_: (e, 0, 0))
    grid_spec = pltpu.PrefetchScalarGridSpec(
        num_scalar_prefetch=3,
        grid=(ne,),
        in_specs=[pl.BlockSpec(memory_space=pl.ANY),
                  per_expert((d, de2)), per_expert((1, de2)), per_expert((de, d)), per_expert((1, d))],
        out_specs=pl.BlockSpec(memory_space=pl.ANY),
        scratch_shapes=[pltpu.VMEM((4, bm, dh), U32), pltpu.VMEM((4, bm, dh), U32),
                        pltpu.VMEM((bm // 4, dh), U32),
                        pltpu.SemaphoreType.DMA((4,)), pltpu.SemaphoreType.DMA((4,)), pltpu.SemaphoreType.DMA,
                        pltpu.VMEM((d, de2), BF16), pltpu.VMEM((de, d), BF16), pltpu.SMEM((1,), I32)],
    )
    return pl.pallas_call(
        _expert_kernel,
        grid_spec=grid_spec,
        out_shape=jax.ShapeDtypeStruct((p, dh), U32),
        compiler_params=pltpu.CompilerParams(dimension_semantics=("arbitrary",),
                                             vmem_limit_bytes=VMEM_LIMIT),
        name="experts",
    )(blk0, nblk, n_used, xs, w_gate_up, b_gate_up.reshape(ne, 1, de2), w_down, b_down.reshape(ne, 1, d))


def _combine_kernel(n8_ref, loc_ref, run_ref, tot_ref,
                    pos_ref, gate_ref, x2_ref, gf_ref, y_hbm, o_ref, *scratch):
    *bufs, sems = scratch
    g = pl.program_id(0)
    nbuf = len(bufs)
    r_loc = bufs[0].shape[0]
    tb = x2_ref.shape[0] // nbuf
    nblk = pl.num_programs(0) * nbuf
    ne = n8_ref.shape[0] // tot_ref.shape[0]

    def fetch(blk, s, live):
        for e in range(ne):
            n = jnp.where(live, n8_ref[blk * ne + e], 0)
            dst0 = loc_ref[blk * ne + e]
            src0 = run_ref[blk * ne + e]
            _for_each_chunk(n, tb, lambda off, size, src0=src0, dst0=dst0: pltpu.make_async_copy(
                y_hbm.at[pl.ds(pl.multiple_of(src0 + off, RUN_ALIGN), size)],
                bufs[s].at[pl.ds(pl.multiple_of(dst0 + off, RUN_ALIGN), size)], sems.at[s]).start())

    @pl.when(g == 0)
    def _():
        for buf in bufs:
            buf[...] = jnp.zeros_like(buf)
        for ahead in range(nbuf - 2):
            fetch(ahead, ahead, True)

    iota_c = lax.broadcasted_iota(I32, (tb, r_loc), 1)
    for u in range(nbuf):
        blk = g * nbuf + u
        rows = slice(u * tb, (u + 1) * tb)
        w = jnp.zeros((tb, r_loc), F32)
        for k in range(TOP_K):
            w = w + jnp.where(iota_c == pos_ref[rows, k:k + 1], gate_ref[rows, k:k + 1], 0.0)
        w_hi = w.astype(BF16)
        w_lo = (w - w_hi.astype(F32)).astype(BF16)
        _wait_rows(tot_ref[blk], r_loc, y_hbm, bufs[u], sems.at[u])
        ahead = blk + nbuf - 2
        fetch(jnp.minimum(ahead, nblk - 1), (u - 2) % nbuf, ahead < nblk)
        y_lo, y_hi = _unpack_halves(bufs[u][...])
        moe = jnp.concatenate([_dot(w_hi, y_lo) + _dot(w_lo, y_lo), _dot(w_hi, y_hi) + _dot(w_lo, y_hi)], axis=1)
        o_ref[rows, :] = _rms(x2_ref[rows, :] + moe, gf_ref[...])


def _combine(pos, gates, x2, y, norm_final, tables, r_loc):
    t, d = x2.shape
    rows = ROUTE_ROWS * ROUTE_UNROLL
    assert t % rows == 0
    grid_spec = pltpu.PrefetchScalarGridSpec(
        num_scalar_prefetch=4,
        grid=(t // rows,),
        in_specs=[pl.BlockSpec((rows, TOP_K), lambda g, *_: (g, 0)),
                  pl.BlockSpec((rows, TOP_K), lambda g, *_: (g, 0)),
                  pl.BlockSpec((rows, d), lambda g, *_: (g, 0)),
                  pl.BlockSpec((1, d), lambda g, *_: (0, 0)),
                  pl.BlockSpec(memory_space=pl.ANY)],
        out_specs=pl.BlockSpec((rows, d), lambda g, *_: (g, 0)),
        scratch_shapes=[pltpu.VMEM((r_loc, d // 2), U32)] * ROUTE_UNROLL + [pltpu.SemaphoreType.DMA((ROUTE_UNROLL,))],
    )
    return pl.pallas_call(
        _combine_kernel,
        grid_spec=grid_spec,
        out_shape=jax.ShapeDtypeStruct((t, d), F32),
        compiler_params=pltpu.CompilerParams(dimension_semantics=("arbitrary",),
                                             vmem_limit_bytes=VMEM_LIMIT),
        name="combine",
    )(*tables, pos, gates, x2, norm_final.reshape(1, d), y)


def _excl_cumsum(a, axis):
    n = a.shape[axis]
    a = jnp.moveaxis(a, axis, -1)
    earlier = jnp.arange(n)[None, :] < jnp.arange(n)[:, None]
    out = jnp.sum(jnp.where(earlier, a[..., None, :], 0), axis=-1)
    return jnp.moveaxis(out, -1, axis)


def _layout(cnt):
    nblk, ne = cnt.shape
    tb, bm = ROUTE_ROWS, EXPERT_ROWS
    n8 = (cnt + RUN_ALIGN - 1) // RUN_ALIGN * RUN_ALIGN
    loc = _excl_cumsum(n8, 1)
    tot = jnp.sum(n8, axis=1)
    size = jnp.sum(n8, axis=0)
    padded = (size + bm - 1) // bm * bm
    pstart = _excl_cumsum(padded, 0)
    pend = pstart + padded
    run = pstart[None, :] + _excl_cumsum(n8, 0)
    p_blocks = -(-(nblk * tb * TOP_K + nblk * ne * (RUN_ALIGN - 1) + ne * (bm - RUN_ALIGN)) // bm)
    n_used = (pend[-1] // bm).astype(I32)
    flat = lambda a: a.reshape(-1).astype(I32)
    tables = (flat(n8), flat(loc), flat(run), flat(tot))
    half = bm // 2
    tails = (flat(jnp.concatenate([padded - size, (p_blocks * bm - pend[-1:]) // half])),
             flat(jnp.concatenate([pstart + size, pend[-1:]])))
    r_loc = -(-(tb * TOP_K + ne * (RUN_ALIGN - 1)) // MXU_TILE) * MXU_TILE
    blocks = ((pstart // bm).astype(I32), (padded // bm).astype(I32), n_used.reshape(1))
    return tables, tails, loc.astype(I32), blocks, p_blocks * bm, r_loc


def kernel(x, mem, norm_mix, w_in, w_pool_group, pool_scale, w_pool_proj, conv_w, conv_b, lru_w_a, lru_b_a, lru_w_x, lru_b_x, lru_lambda, w_lru_proj, w_mix_out, norm_xattn, norm_mem, w_q, w_kv, w_o, norm_moe, w_router, b_router, w_gate_up, b_gate_up, w_down, b_down, norm_final):
    nb, s_len, d = x.shape
    m_len = mem.shape[1]
    assert norm_mix.shape[0] == 1, "single-layer stack"
    l = 0
    x1 = _mixer(x, norm_mix[l], w_in[l], w_pool_group[l], pool_scale[l], w_pool_proj[l],
                conv_w[l], conv_b[l], lru_w_a[l], lru_b_a[l], lru_w_x[l], lru_b_x[l],
                lru_lambda[l], w_lru_proj[l], w_mix_out[l])
    kv = _kv_proj(mem.reshape(nb * m_len, d), norm_mem[l], w_kv[l]).reshape(nb, m_len, 2 * d)
    x2, xn, idx_t, gate_t, lrank_t, cnt = _attention(
        x1.reshape(s_len, nb, d), kv, norm_xattn[l], w_q[l], w_o[l],
        norm_moe[l], w_router[l], b_router[l])
    tables, tails, loc, blocks, p_rows, r_loc = _layout(cnt.reshape(cnt.shape[0], -1))
    xs, pos_t = _dispatch(xn, idx_t, lrank_t, loc, tables + tails, p_rows, r_loc)
    y = _experts(xs, *blocks, w_gate_up[l], b_gate_up[l], w_down[l], b_down[l])
    out = _combine(pos_t.T, gate_t.T, x2.reshape(nb * s_len, d), y, norm_final, tables, r_loc)
    return out.reshape(nb, s_len, d)
```

```python
import functools

import jax
import jax.numpy as jnp
from jax import lax
from jax.experimental import pallas as pl
from jax.experimental.pallas import tpu as pltpu

POOL_WINDOWS = (2, 4, 8, 16)
N_GROUPS = 4
CONV_WIDTH = 4
RG_C = 8.0
TOP_K = 4
SWIGLU_LIMIT = 7.0
SWIGLU_ALPHA = 1.702
RMS_EPS = 1e-6

MIX_STEPS = 32
KV_ROWS = 512
ATT_ROWS = 512
ROUTE_ROWS = 256
ROUTE_UNROLL = 4
EXPERT_ROWS = 512
RUN_ALIGN = 8
MXU_TILE = 256
VMEM_LIMIT = 52 * 1024 * 1024

BF16 = jnp.bfloat16
F32 = jnp.float32
I32 = jnp.int32
U32 = jnp.uint32


def _const_spec(shape):
    nd = len(shape)
    return pl.BlockSpec(shape, lambda *_: (0,) * nd, pipeline_mode=pl.Buffered(1))


def _rms(x, g):
    return x * lax.rsqrt(jnp.mean(x * x, axis=-1, keepdims=True) + RMS_EPS) * g


def _dot(a, b):
    return jnp.dot(a, b, preferred_element_type=F32)


def _dot_nt(a, b):
    return lax.dot_general(a, b, (((1,), (1,)), ((), ())), preferred_element_type=F32)


def _pack_halves(x):
    c = x.shape[1] // 2
    lo = lax.bitcast_convert_type(x[:, :c], U32)
    hi = lax.bitcast_convert_type(x[:, c:], U32)
    return (hi & jnp.uint32(0xFFFF0000)) | (lo >> 16)


def _unpack_halves(u):
    lo = lax.bitcast_convert_type(u << 16, F32).astype(BF16)
    hi = lax.bitcast_convert_type(u & jnp.uint32(0xFFFF0000), F32).astype(BF16)
    return lo, hi


def _mixer_kernel(x_hbm, nm_ref, win_ref, wpg_ref, psc_ref, wpp_ref, cw_ref, cb_ref,
                  wa_ref, ba_ref, wx_ref, bx_ref, lam_ref, wlp_ref, wmo_ref,
                  o_ref,
                  xbuf, xsem,
                  h_ref, up_ref, ul_ref, a_ref, b_ref, pm_ref, m_ref, mb_ref, t_ref, gl_ref, gb_ref, hc_ref,
                  *, nb, ts):
    rows, d = o_ref.shape
    gw = d // N_GROUPS
    halo_p = (POOL_WINDOWS[-1]) * nb
    halo_c = (CONV_WIDTH - 1) * nb
    c = pl.program_id(0)

    @pl.when(c == 0)
    def _():
        up_ref[0:halo_p, :] = jnp.zeros((halo_p, d), F32)
        ul_ref[0:halo_c, :] = jnp.zeros((halo_c, d), F32)
        hc_ref[...] = jnp.zeros_like(hc_ref)

    def x_copy(step, t, slot):
        return pltpu.make_async_copy(x_hbm.at[:, step * ts + t, :], xbuf.at[slot, t], xsem.at[slot])

    def fetch(step, slot, live):
        for t in range(ts):
            @pl.when(live)
            def _(t=t):
                x_copy(step, t, slot).start(priority=t % 2)

    slot = c % 2
    fetch(0, 0, c == 0)
    fetch(jnp.minimum(c + 1, pl.num_programs(0) - 1), 1 - slot, c + 1 < pl.num_programs(0))
    pltpu.make_async_copy(xbuf.at[1 - slot], xbuf.at[slot], xsem.at[slot]).wait()

    h_ref[...] = _rms(xbuf[slot].reshape(rows, d), nm_ref[...]).astype(BF16)

    t_glob = c * ts + lax.broadcasted_iota(I32, (rows, 1), 0) // nb

    up_ref[halo_p:halo_p + rows, :] = _dot(h_ref[...], win_ref[:, 0:d])
    ul_ref[halo_c:halo_c + rows, :] = _dot(h_ref[...], win_ref[:, d:2 * d])

    for g, w in enumerate(POOL_WINDOWS):
        cols = slice(g * gw, (g + 1) * gw)
        u = up_ref[halo_p:halo_p + rows, cols]
        acc = u
        for j in range(1, w):
            acc = acc + up_ref[halo_p - j * nb:halo_p - j * nb + rows, cols]
        cnt = jnp.minimum(t_glob + 1, w).astype(F32)
        p = acc / cnt - u
        pg = _dot(p.astype(BF16), wpg_ref[g]) * psc_ref[:, cols]
        pm_ref[:, cols] = pg.astype(BF16)
    t_ref[...] = _dot(h_ref[...], win_ref[:, 3 * d:4 * d])
    m_ref[...] = _dot(pm_ref[...], wpp_ref[...])

    for g in range(N_GROUPS):
        cols = slice(g * gw, (g + 1) * gw)
        xr = cb_ref[:, cols]
        for k in range(CONV_WIDTH):
            off = halo_c - (CONV_WIDTH - 1 - k) * nb
            xr = xr + ul_ref[off:off + rows, cols] * cw_ref[k:k + 1, cols]
        xrb = xr.astype(BF16)
        r = jax.nn.sigmoid(_dot(xrb, wa_ref[g]) + ba_ref[:, cols])
        i = jax.nn.sigmoid(_dot(xrb, wx_ref[g]) + bx_ref[:, cols])
        gl_ref[:, cols] = _dot(h_ref[...], win_ref[:, 2 * d + g * gw:2 * d + (g + 1) * gw])
        gb_ref[:, cols] = _dot(h_ref[...], win_ref[:, 4 * d + g * gw:4 * d + (g + 1) * gw])
        lam = lam_ref[:, cols]
        log_sig = jnp.minimum(lam, 0.0) - jnp.log(1.0 + jnp.exp(-jnp.abs(lam)))
        a = jnp.exp((RG_C * r) * log_sig)
        mult = jnp.sqrt(jnp.maximum(1.0 - a * a, 0.0))
        mult = jnp.where(t_glob == 0, 1.0, mult)
        a_ref[:, cols] = a
        b_ref[:, cols] = mult * i * xr
    m_ref[...] = jax.nn.sigmoid(t_ref[...]) * m_ref[...]

    def scan_step(t, hprev):
        sl = pl.ds(pl.multiple_of(t * nb, nb), nb)
        hn = a_ref[sl, :] * hprev + b_ref[sl, :]
        b_ref[sl, :] = hn
        return hn

    hc_ref[...] = lax.fori_loop(0, ts, scan_step, hc_ref[...], unroll=4)

    half = rows // 2
    for hs in (slice(0, half), slice(half, rows)):
        pm_ref[hs, :] = (b_ref[hs, :] * jax.nn.gelu(gl_ref[hs, :], approximate=True)).astype(BF16)
    for hs in (slice(0, half), slice(half, rows)):
        yb = _dot(pm_ref[hs, :], wlp_ref[...])
        mb_ref[hs, :] = (m_ref[hs, :] + jax.nn.sigmoid(gb_ref[hs, :]) * yb).astype(BF16)
    for i, hs in enumerate((slice(0, half), slice(half, rows))):
        x_half = xbuf[slot, i * (ts // 2):(i + 1) * (ts // 2)].reshape(half, d)
        o_ref[hs, :] = x_half + _dot(mb_ref[hs, :], wmo_ref[...])

    up_ref[0:halo_p, :] = up_ref[rows:rows + halo_p, :]
    ul_ref[0:halo_c, :] = ul_ref[rows:rows + halo_c, :]


def _mixer(x, norm_mix, w_in, w_pool_group, pool_scale, w_pool_proj, conv_w, conv_b,
           lru_w_a, lru_b_a, lru_w_x, lru_b_x, lru_lambda, w_lru_proj, w_mix_out):
    nb, s_len, d = x.shape
    ts = MIX_STEPS
    rows = ts * nb
    n_rows = s_len * nb
    assert s_len % ts == 0 and ts % 2 == 0 and ts >= POOL_WINDOWS[-1] and nb % 8 == 0
    row2 = lambda v: v.reshape(1, -1)
    args = (x, row2(norm_mix), w_in.astype(BF16), w_pool_group.astype(BF16), row2(pool_scale),
            w_pool_proj.astype(BF16), conv_w, row2(conv_b), lru_w_a.astype(BF16), row2(lru_b_a),
            lru_w_x.astype(BF16), row2(lru_b_x), row2(lru_lambda), w_lru_proj.astype(BF16),
            w_mix_out.astype(BF16))
    in_specs = [pl.BlockSpec(memory_space=pl.ANY)] + [_const_spec(a.shape) for a in args[1:]]
    halo_p = POOL_WINDOWS[-1] * nb
    halo_c = (CONV_WIDTH - 1) * nb
    return pl.pallas_call(
        functools.partial(_mixer_kernel, nb=nb, ts=ts),
        grid=(n_rows // rows,),
        in_specs=in_specs,
        out_specs=pl.BlockSpec((rows, d), lambda c: (c, 0)),
        out_shape=jax.ShapeDtypeStruct((n_rows, d), F32),
        scratch_shapes=[
            pltpu.VMEM((2, ts, nb, d), F32),
            pltpu.SemaphoreType.DMA((2,)),
            pltpu.VMEM((rows, d), BF16),
            pltpu.VMEM((halo_p + rows, d), F32),
            pltpu.VMEM((halo_c + rows, d), F32),
            pltpu.VMEM((rows, d), F32),
            pltpu.VMEM((rows, d), F32),
            pltpu.VMEM((rows, d), BF16),
            pltpu.VMEM((rows, d), F32),
            pltpu.VMEM((rows, d), BF16),
            pltpu.VMEM((rows, d), F32),
            pltpu.VMEM((rows, d), F32),
            pltpu.VMEM((rows, d), F32),
            pltpu.VMEM((nb, d), F32),
        ],
        compiler_params=pltpu.CompilerParams(dimension_semantics=("arbitrary",),
                                             vmem_limit_bytes=VMEM_LIMIT),
        name="mixer",
    )(*args)


def _kv_kernel(m_ref, g_ref, w_ref, o_ref, w_bf):
    @pl.when(pl.program_id(0) == 0)
    def _():
        w_bf[...] = w_ref[...].astype(BF16)

    o_ref[...] = _dot(_rms(m_ref[...], g_ref[...]).astype(BF16), w_bf[...]).astype(BF16)


def _kv_proj(mem2d, norm_mem, w_kv):
    n, d = mem2d.shape
    assert n % KV_ROWS == 0
    return pl.pallas_call(
        _kv_kernel,
        grid=(n // KV_ROWS,),
        in_specs=[pl.BlockSpec((KV_ROWS, d), lambda i: (i, 0)),
                  _const_spec((1, d)), _const_spec((d, 2 * d))],
        out_specs=pl.BlockSpec((KV_ROWS, 2 * d), lambda i: (i, 0)),
        out_shape=jax.ShapeDtypeStruct((n, 2 * d), BF16),
        scratch_shapes=[pltpu.VMEM((d, 2 * d), BF16)],
        compiler_params=pltpu.CompilerParams(dimension_semantics=("arbitrary",),
                                             vmem_limit_bytes=VMEM_LIMIT),
        name="kv_proj",
    )(mem2d, norm_mem.reshape(1, d), w_kv)


def _attn_kernel(x_hbm, kv_ref, gx_ref, wq_ref, wo_ref, gm_ref, wr_ref, br_ref,
                 x2_ref, xn_ref, idx_ref, gate_ref, lrank_ref, cnt_ref,
                 xbuf, xsem, o_scr, xh_scr, xl_scr, wq_bf, wo_bf, *, nq):
    tq, d = x2_ref.shape
    hd = d // N_GROUPS
    ne = br_ref.shape[0]
    tb = ROUTE_ROWS
    n = pl.program_id(0)
    n_blocks = pl.num_programs(0) - 1
    cur = jnp.minimum(n, n_blocks - 1)
    slot = cur % 2
    prev = (n + 1) % 2

    def x_copy(blk, s):
        return pltpu.make_async_copy(x_hbm.at[pl.ds((blk % nq) * tq, tq), blk // nq, :], xbuf.at[s], xsem.at[s])

    @pl.when(n == 0)
    def _():
        x_copy(0, 0).start()
        xh_scr[...] = jnp.zeros_like(xh_scr)
        xl_scr[...] = jnp.zeros_like(xl_scr)
        wq_bf[...] = wq_ref[...].astype(BF16)
        wo_bf[...] = wo_ref[...].astype(BF16)

    @pl.when(n + 1 < n_blocks)
    def _():
        x_copy(n + 1, 1 - slot).start()

    @pl.when(n < n_blocks)
    def _():
        x_copy(n, slot).wait()

    x = xbuf[slot]

    works = []
    for sb in range(tq // tb):
        rows = slice(sb * tb, (sb + 1) * tb)
        ph = _dot_nt(wr_ref[...], xh_scr[prev, rows, :])
        pl_ = _dot_nt(wr_ref[0:ne, :], xl_scr[prev, rows, :])
        works.append(ph[0:ne] + ph[ne:2 * ne] + pl_ + br_ref[...])

    q = _dot(_rms(x, gx_ref[...]).astype(BF16), wq_bf[...]).astype(BF16)

    iota_f = lax.broadcasted_iota(I32, (ne, tb), 0).astype(F32)
    picked = []
    for work in works:
        vals, idxs, sels = [], [], []
        for _ in range(TOP_K):
            m = jnp.max(work, axis=0, keepdims=True)
            idx = jnp.min(jnp.where(work == m, iota_f, float(ne)), axis=0, keepdims=True)
            sel = iota_f == idx
            vals.append(m)
            idxs.append(idx.astype(I32))
            sels.append(sel)
            work = jnp.where(sel, -jnp.inf, work)
        onehot = jnp.zeros((ne, tb), F32)
        for sel in sels:
            onehot = onehot + sel.astype(F32)
        picked.append((vals, idxs, sels, onehot.astype(BF16)))

    scores = [_dot_nt(q[:, h * hd:(h + 1) * hd], kv_ref[:, h * hd:(h + 1) * hd]) * (hd ** -0.5)
              for h in range(N_GROUPS)]
    for h, s in enumerate(scores):
        v = kv_ref[:, d + h * hd:d + (h + 1) * hd]
        e = jnp.exp(s - jnp.max(s, axis=-1, keepdims=True))
        p = e / jnp.sum(e, axis=-1, keepdims=True)
        o_scr[:, h * hd:(h + 1) * hd] = _dot(p.astype(BF16), v).astype(BF16)

    before = (lax.broadcasted_iota(I32, (tb, tb), 0) < lax.broadcasted_iota(I32, (tb, tb), 1)
              ).astype(BF16)
    for sb, (vals, idxs, sels, oh16) in enumerate(picked):
        rows = slice(sb * tb, (sb + 1) * tb)
        ex = [jnp.exp(v - vals[0]) for v in vals]
        den = ex[0] + ex[1] + ex[2] + ex[3]
        prefix = _dot(oh16, before)
        lr = [jnp.sum(jnp.where(sel, prefix, 0.0), axis=0, keepdims=True).astype(I32) for sel in sels]
        idx_ref[:, rows] = jnp.concatenate(idxs, axis=0)
        gate_ref[:, rows] = jnp.concatenate([e_ / den for e_ in ex], axis=0)
        lrank_ref[:, rows] = jnp.concatenate(lr, axis=0)
        cnt_ref[sb] = _dot_nt(jnp.ones((1, tb), BF16), oh16).astype(I32)

    x2 = x + _dot(o_scr[...], wo_bf[...])
    x2_ref[...] = x2
    xn = _rms(x2, gm_ref[...])
    xh = xn.astype(BF16)
    xn_ref[...] = xh
    xh_scr[n % 2] = xh
    xl_scr[n % 2] = (xn - xh.astype(F32)).astype(BF16)


def _attention(x1, kv, norm_xattn, w_q, w_o, norm_moe, w_router, b_router):
    s_len, nb, d = x1.shape
    m = kv.shape[1]
    ne = w_router.shape[-1]
    tq, tb = ATT_ROWS, ROUTE_ROWS
    nq = s_len // tq
    n_blocks = nb * nq
    t = nb * s_len
    assert s_len % tq == 0 and tq % tb == 0
    wr_hi = w_router.astype(BF16)
    wr_lo = (w_router - wr_hi.astype(F32)).astype(BF16)
    wr2t = jnp.concatenate([wr_hi, wr_lo], axis=1).T
    cur = lambda n: jnp.minimum(n, n_blocks - 1)
    routed = lambda n: (0, jnp.maximum(n - 1, 0))
    return pl.pallas_call(
        functools.partial(_attn_kernel, nq=nq),
        grid=(n_blocks + 1,),
        in_specs=[pl.BlockSpec(memory_space=pl.ANY),
                  pl.BlockSpec((None, m, 2 * d), lambda n: (cur(n) // nq, 0, 0)),
                  _const_spec((1, d)), _const_spec((d, d)), _const_spec((d, d)),
                  _const_spec((1, d)), _const_spec((2 * ne, d)), _const_spec((ne, 1))],
        out_specs=[pl.BlockSpec((None, tq, d), lambda n: (cur(n) // nq, cur(n) % nq, 0)),
                   pl.BlockSpec((tq, d), lambda n: (cur(n), 0)),
                   pl.BlockSpec((TOP_K, tq), routed),
                   pl.BlockSpec((TOP_K, tq), routed),
                   pl.BlockSpec((TOP_K, tq), routed),
                   pl.BlockSpec((tq // tb, 1, ne), lambda n: (jnp.maximum(n - 1, 0), 0, 0))],
        out_shape=[jax.ShapeDtypeStruct((nb, s_len, d), F32),
                   jax.ShapeDtypeStruct((t, d), BF16),
                   jax.ShapeDtypeStruct((TOP_K, t), I32),
                   jax.ShapeDtypeStruct((TOP_K, t), F32),
                   jax.ShapeDtypeStruct((TOP_K, t), I32),
                   jax.ShapeDtypeStruct((t // tb, 1, ne), I32)],
        scratch_shapes=[pltpu.VMEM((2, tq, d), F32), pltpu.SemaphoreType.DMA((2,)),
                        pltpu.VMEM((tq, d), BF16), pltpu.VMEM((2, tq, d), BF16), pltpu.VMEM((2, tq, d), BF16),
                        pltpu.VMEM((d, d), BF16), pltpu.VMEM((d, d), BF16)],
        compiler_params=pltpu.CompilerParams(dimension_semantics=("arbitrary",),
                                             vmem_limit_bytes=VMEM_LIMIT),
        name="attention",
    )(x1, kv, norm_xattn.reshape(1, d), w_q, w_o,
      norm_moe.reshape(1, d), wr2t, b_router.reshape(ne, 1))


def _pow2_chunks(limit):
    sizes = []
    c = RUN_ALIGN
    while c <= limit:
        sizes.append(c)
        c *= 2
    return sizes[::-1]


def _for_each_chunk(n, limit, fn):
    for size in _pow2_chunks(limit):
        @pl.when((n & size) != 0)
        def _(size=size):
            fn(pl.multiple_of(n & ~(2 * size - 1), RUN_ALIGN), size)


def _wait_rows(n, limit, src, dst, sem):
    _for_each_chunk(n, limit, lambda off, size: pltpu.make_async_copy(
        src.at[pl.ds(0, size)], dst.at[pl.ds(0, size)], sem).wait())


def _dispatch_kernel(n8_ref, loc_ref, run_ref, tot_ref, tailn_ref, tails_ref,
                     xn_ref, idx_ref, lrank_ref, locv_ref, xs_hbm, pos_ref, *scratch):
    *bufs, zbuf, sems, zsem = scratch
    g = pl.program_id(0)
    nbuf = len(bufs)
    r_loc = bufs[0].shape[0]
    tb = xn_ref.shape[0] // nbuf
    ne = tailn_ref.shape[0] - 1
    zrows = zbuf.shape[0]
    last_blk = pl.num_programs(0) * nbuf - 1

    def send(blk, s, live):
        for e in range(ne):
            n = jnp.where(live, n8_ref[blk * ne + e], 0)
            src0 = loc_ref[blk * ne + e]
            dst0 = run_ref[blk * ne + e]
            _for_each_chunk(n, tb, lambda off, size, src0=src0, dst0=dst0, e=e: pltpu.make_async_copy(
                bufs[s].at[pl.ds(pl.multiple_of(src0 + off, RUN_ALIGN), size)],
                xs_hbm.at[pl.ds(pl.multiple_of(dst0 + off, RUN_ALIGN), size)], sems.at[s]).start(priority=e % 2))

    def sent(blk, s, live):
        _wait_rows(jnp.where(live, tot_ref[blk], 0), r_loc, bufs[s], xs_hbm, sems.at[s])

    def zero_rest(i, carry):
        dst = pl.multiple_of(tails_ref[ne] + i * zrows, zrows)
        pltpu.make_async_copy(zbuf, xs_hbm.at[pl.ds(dst, zrows)], zsem).start()
        return carry

    def wait_rest(i, carry):
        pltpu.make_async_copy(zbuf, xs_hbm.at[pl.ds(0, zrows)], zsem).wait()
        return carry

    @pl.when(g == 0)
    def _():
        zbuf[...] = jnp.zeros_like(zbuf)
        for e in range(ne):
            _for_each_chunk(tailn_ref[e], zrows, lambda off, size, e=e: pltpu.make_async_copy(
                zbuf.at[pl.ds(0, size)],
                xs_hbm.at[pl.ds(pl.multiple_of(tails_ref[e] + off, RUN_ALIGN), size)], zsem).start())
        lax.fori_loop(0, tailn_ref[ne], zero_rest, 0)

    iota_e = lax.broadcasted_iota(I32, (ne, tb), 0)
    iota_r = lax.broadcasted_iota(I32, (r_loc, tb), 0)
    for u in range(nbuf):
        blk = g * nbuf + u
        cols = slice(u * tb, (u + 1) * tb)
        sent(jnp.maximum(blk - nbuf, 0), u, blk >= nbuf)
        send(jnp.maximum(blk - 1, 0), (u - 1) % nbuf, blk >= 1)

        loc_col = locv_ref[u].astype(F32)
        pos = []
        for k in range(TOP_K):
            run0 = jnp.sum(jnp.where(iota_e == idx_ref[k:k + 1, cols], loc_col, 0.0), axis=0, keepdims=True)
            pos.append(run0.astype(I32) + lrank_ref[k:k + 1, cols])
            pos_ref[k:k + 1, cols] = pos[k]

        hit = iota_r == pos[0]
        for k in range(1, TOP_K):
            hit = jnp.logical_or(hit, iota_r == pos[k])
        bufs[u][...] = _pack_halves(_dot(hit.astype(BF16), xn_ref[cols, :]))

    @pl.when(g == pl.num_programs(0) - 1)
    def _():
        send(last_blk, nbuf - 1, True)
        for s_ in range(nbuf):
            sent(last_blk - (nbuf - 1 - s_), s_, True)
        for e in range(ne):
            _wait_rows(tailn_ref[e], zrows, zbuf, xs_hbm, zsem)
        lax.fori_loop(0, tailn_ref[ne], wait_rest, 0)


def _dispatch(xn, idx_t, lrank_t, loc, tables, p_rows, r_loc):
    t, d = xn.shape
    tb, nu = ROUTE_ROWS, ROUTE_UNROLL
    nblk, ne = loc.shape
    assert nblk % nu == 0
    tok = pl.BlockSpec((TOP_K, nu * tb), lambda g, *_: (0, g))
    grid_spec = pltpu.PrefetchScalarGridSpec(
        num_scalar_prefetch=6,
        grid=(nblk // nu,),
        in_specs=[pl.BlockSpec((nu * tb, d), lambda g, *_: (g, 0)), tok, tok,
                  pl.BlockSpec((nu, ne, 1), lambda g, *_: (g, 0, 0))],
        out_specs=[pl.BlockSpec(memory_space=pl.ANY), tok],
        scratch_shapes=[pltpu.VMEM((r_loc, d // 2), U32)] * nu + [
            pltpu.VMEM((EXPERT_ROWS // 2, d // 2), U32), pltpu.SemaphoreType.DMA((nu,)), pltpu.SemaphoreType.DMA],
    )
    return pl.pallas_call(
        _dispatch_kernel,
        grid_spec=grid_spec,
        out_shape=[jax.ShapeDtypeStruct((p_rows, d // 2), U32), jax.ShapeDtypeStruct((TOP_K, t), I32)],
        compiler_params=pltpu.CompilerParams(dimension_semantics=("arbitrary",),
                                             vmem_limit_bytes=VMEM_LIMIT),
        name="dispatch",
    )(*tables, xn, idx_t, lrank_t, loc.reshape(nblk, ne, 1))


def _expert_kernel(blk0_ref, nblk_ref, nu_ref, xs_hbm, wgu_ref, bgu_ref, wdn_ref, bdn_ref, y_hbm,
                   xbuf, ybuf, zbuf, in_sem, out_sem, zsem, wgu_bf, wdn_bf, next_ref):
    e = pl.program_id(0)
    nbuf, bm, dh = xbuf.shape
    zrows = zbuf.shape[0]
    de = wdn_ref.shape[0]
    n_used = nu_ref[0]

    def x_copy(g):
        return pltpu.make_async_copy(xs_hbm.at[pl.ds(pl.multiple_of(g * bm, bm), bm)], xbuf.at[g % nbuf],
                                     in_sem.at[g % nbuf])

    def y_copy(g):
        return pltpu.make_async_copy(ybuf.at[g % nbuf], y_hbm.at[pl.ds(pl.multiple_of(g * bm, bm), bm)],
                                     out_sem.at[g % nbuf])

    def prefetch_until(limit):
        def start(g):
            x_copy(g).start()
            return g + 1
        next_ref[0] = lax.while_loop(lambda g: g < jnp.minimum(limit, n_used), start, next_ref[0])

    n_zero = (y_hbm.shape[0] - n_used * bm) // zrows

    def z_copy(i):
        return pltpu.make_async_copy(
            zbuf, y_hbm.at[pl.ds(pl.multiple_of(n_used * bm + i * zrows, zrows), zrows)], zsem)

    @pl.when(e == 0)
    def _():
        next_ref[0] = 0
        zbuf[...] = jnp.zeros_like(zbuf)
        lax.fori_loop(0, n_zero, lambda i, c: (z_copy(i).start(), c)[1], 0)

    wgu_bf[...] = wgu_ref[...].astype(BF16)
    wdn_bf[...] = wdn_ref[...].astype(BF16)

    def mlp(xu):
        x_lo, x_hi = _unpack_halves(xu)

        def proj(c0, c1):
            return (_dot(x_lo, wgu_bf[0:dh, c0:c1]) + _dot(x_hi, wgu_bf[dh:2 * dh, c0:c1])
                    + bgu_ref[:, c0:c1])

        nch = 4
        ch = de // nch
        acts = []
        for c in range(nch):
            gl = jnp.minimum(proj(c * ch, (c + 1) * ch), SWIGLU_LIMIT)
            up = jnp.clip(proj(de + c * ch, de + (c + 1) * ch), -SWIGLU_LIMIT, SWIGLU_LIMIT)
            acts.append(((up + 1.0) * (gl * jax.nn.sigmoid(SWIGLU_ALPHA * gl))).astype(BF16))
        y = bdn_ref[...]
        for c in range(nch):
            y = y + _dot(acts[c], wdn_bf[c * ch:(c + 1) * ch, :])
        return _pack_halves(y.astype(BF16).astype(F32))

    def run(g, units):
        prefetch_until(g + units + 2)
        for k in range(units):
            x_copy(g + k).wait()

            @pl.when(g + k >= nbuf)
            def _(k=k):
                y_copy(g + k - nbuf).wait()
        if units == 1:
            ybuf[g % nbuf] = mlp(xbuf[g % nbuf])
        else:
            slots = pl.ds(pl.multiple_of(g % nbuf, 2), 2)
            ybuf[slots] = mlp(xbuf[slots].reshape(2 * bm, dh)).reshape(2, bm, dh)
        for k in range(units):
            y_copy(g + k).start()

    g0 = blk0_ref[e]
    n = nblk_ref[e]
    lead = jnp.logical_and(g0 % 2 == 1, n > 0).astype(I32)
    pairs = (n - lead) // 2

    @pl.when(lead == 1)
    def _():
        run(g0, 1)

    lax.fori_loop(0, pairs, lambda j, c: (run(g0 + lead + 2 * j, 2), c)[1], 0)

    @pl.when((n - lead) % 2 == 1)
    def _():
        run(g0 + lead + 2 * pairs, 1)

    @pl.when(e == pl.num_programs(0) - 1)
    def _():
        for back in range(nbuf, 0, -1):
            @pl.when(n_used >= back)
            def _(back=back):
                y_copy(n_used - back).wait()
        lax.fori_loop(0, n_zero, lambda i, c: (z_copy(i).wait(), c)[1], 0)


def _experts(xs, blk0, nblk, n_used, w_gate_up, b_gate_up, w_down, b_down):
    p, dh = xs.shape
    ne, d, de2 = w_gate_up.shape
    de = de2 // 2
    bm = EXPERT_ROWS
    per_expert = lambda shape: pl.BlockSpec((None,) + shape, lambda e, *_: (e, 0, 0))
    grid_spec = pltpu.PrefetchScalarGridSpec(
        num_scalar_prefetch=3,
        grid=(ne,),
        in_specs=[pl.BlockSpec(memory_space=pl.ANY),
                  per_expert((d, de2)), per_expert((1, de2)), per_expert((de, d)), per_expert((1, d))],
        out_specs=pl.BlockSpec(memory_space=pl.ANY),
        scratch_shapes=[pltpu.VMEM((4, bm, dh), U32), pltpu.VMEM((4, bm, dh), U32),
                        pltpu.VMEM((bm // 4, dh), U32),
                        pltpu.SemaphoreType.DMA((4,)), pltpu.SemaphoreType.DMA((4,)), pltpu.SemaphoreType.DMA,
                        pltpu.VMEM((d, de2), BF16), pltpu.VMEM((de, d), BF16), pltpu.SMEM((1,), I32)],
    )
    return pl.pallas_call(
        _expert_kernel,
        grid_spec=grid_spec,
        out_shape=jax.ShapeDtypeStruct((p, dh), U32),
        compiler_params=pltpu.CompilerParams(dimension_semantics=("arbitrary",),
                                             vmem_limit_bytes=VMEM_LIMIT),
        name="experts",
    )(blk0, nblk, n_used, xs, w_gate_up, b_gate_up.reshape(ne, 1, de2), w_down, b_down.reshape(ne, 1, d))


def _combine_kernel(n8_ref, loc_ref, run_ref, tot_ref,
                    pos_ref, gate_ref, x2_ref, gf_ref, y_hbm, o_ref, *scratch):
    *bufs, sems = scratch
    g = pl.program_id(0)
    nbuf = len(bufs)
    r_loc = bufs[0].shape[0]
    tb = x2_ref.shape[0] // nbuf
    nblk = pl.num_programs(0) * nbuf
    ne = n8_ref.shape[0] // tot_ref.shape[0]

    def fetch(blk, s, live):
        for e in range(ne):
            n = jnp.where(live, n8_ref[blk * ne + e], 0)
            dst0 = loc_ref[blk * ne + e]
            src0 = run_ref[blk * ne + e]
            _for_each_chunk(n, tb, lambda off, size, src0=src0, dst0=dst0, e=e: pltpu.make_async_copy(
                y_hbm.at[pl.ds(pl.multiple_of(src0 + off, RUN_ALIGN), size)],
                bufs[s].at[pl.ds(pl.multiple_of(dst0 + off, RUN_ALIGN), size)], sems.at[s]).start(priority=e % 2))

    @pl.when(g == 0)
    def _():
        for buf in bufs:
            buf[...] = jnp.zeros_like(buf)
        for ahead in range(nbuf - 2):
            fetch(ahead, ahead, True)

    iota_c = lax.broadcasted_iota(I32, (tb, r_loc), 1)
    for u in range(nbuf):
        blk = g * nbuf + u
        rows = slice(u * tb, (u + 1) * tb)
        w = jnp.zeros((tb, r_loc), F32)
        for k in range(TOP_K):
            w = w + jnp.where(iota_c == pos_ref[rows, k:k + 1], gate_ref[rows, k:k + 1], 0.0)
        w_hi = w.astype(BF16)
        w_lo = (w - w_hi.astype(F32)).astype(BF16)
        _wait_rows(tot_ref[blk], r_loc, y_hbm, bufs[u], sems.at[u])
        ahead = blk + nbuf - 2
        fetch(jnp.minimum(ahead, nblk - 1), (u - 2) % nbuf, ahead < nblk)
        y_lo, y_hi = _unpack_halves(bufs[u][...])
        moe = jnp.concatenate([_dot(w_hi, y_lo) + _dot(w_lo, y_lo), _dot(w_hi, y_hi) + _dot(w_lo, y_hi)], axis=1)
        o_ref[rows, :] = _rms(x2_ref[rows, :] + moe, gf_ref[...])


def _combine(pos, gates, x2, y, norm_final, tables, r_loc):
    t, d = x2.shape
    rows = ROUTE_ROWS * ROUTE_UNROLL
    assert t % rows == 0
    grid_spec = pltpu.PrefetchScalarGridSpec(
        num_scalar_prefetch=4,
        grid=(t // rows,),
        in_specs=[pl.BlockSpec((rows, TOP_K), lambda g, *_: (g, 0)),
                  pl.BlockSpec((rows, TOP_K), lambda g, *_: (g, 0)),
                  pl.BlockSpec((rows, d), lambda g, *_: (g, 0)),
                  pl.BlockSpec((1, d), lambda g, *_: (0, 0)),
                  pl.BlockSpec(memory_space=pl.ANY)],
        out_specs=pl.BlockSpec((rows, d), lambda g, *_: (g, 0)),
        scratch_shapes=[pltpu.VMEM((r_loc, d // 2), U32)] * ROUTE_UNROLL + [pltpu.SemaphoreType.DMA((ROUTE_UNROLL,))],
    )
    return pl.pallas_call(
        _combine_kernel,
        grid_spec=grid_spec,
        out_shape=jax.ShapeDtypeStruct((t, d), F32),
        compiler_params=pltpu.CompilerParams(dimension_semantics=("arbitrary",),
                                             vmem_limit_bytes=VMEM_LIMIT),
        name="combine",
    )(*tables, pos, gates, x2, norm_final.reshape(1, d), y)


def _excl_cumsum(a, axis):
    n = a.shape[axis]
    a = jnp.moveaxis(a, axis, -1)
    earlier = jnp.arange(n)[None, :] < jnp.arange(n)[:, None]
    out = jnp.sum(jnp.where(earlier, a[..., None, :], 0), axis=-1)
    return jnp.moveaxis(out, -1, axis)


def _layout(cnt):
    nblk, ne = cnt.shape
    tb, bm = ROUTE_ROWS, EXPERT_ROWS
    n8 = (cnt + RUN_ALIGN - 1) // RUN_ALIGN * RUN_ALIGN
    loc = _excl_cumsum(n8, 1)
    tot = jnp.sum(n8, axis=1)
    size = jnp.sum(n8, axis=0)
    padded = (size + bm - 1) // bm * bm
    pstart = _excl_cumsum(padded, 0)
    pend = pstart + padded
    run = pstart[None, :] + _excl_cumsum(n8, 0)
    p_blocks = -(-(nblk * tb * TOP_K + nblk * ne * (RUN_ALIGN - 1) + ne * (bm - RUN_ALIGN)) // bm)
    n_used = (pend[-1] // bm).astype(I32)
    flat = lambda a: a.reshape(-1).astype(I32)
    tables = (flat(n8), flat(loc), flat(run), flat(tot))
    half = bm // 2
    tails = (flat(jnp.concatenate([padded - size, (p_blocks * bm - pend[-1:]) // half])),
             flat(jnp.concatenate([pstart + size, pend[-1:]])))
    r_loc = -(-(tb * TOP_K + ne * (RUN_ALIGN - 1)) // MXU_TILE) * MXU_TILE
    blocks = ((pstart // bm).astype(I32), (padded // bm).astype(I32), n_used.reshape(1))
    return tables, tails, loc.astype(I32), blocks, p_blocks * bm, r_loc


def kernel(x, mem, norm_mix, w_in, w_pool_group, pool_scale, w_pool_proj, conv_w, conv_b, lru_w_a, lru_b_a, lru_w_x, lru_b_x, lru_lambda, w_lru_proj, w_mix_out, norm_xattn, norm_mem, w_q, w_kv, w_o, norm_moe, w_router, b_router, w_gate_up, b_gate_up, w_down, b_down, norm_final):
    nb, s_len, d = x.shape
    m_len = mem.shape[1]
    assert norm_mix.shape[0] == 1, "single-layer stack"
    l = 0
    x1 = _mixer(x, norm_mix[l], w_in[l], w_pool_group[l], pool_scale[l], w_pool_proj[l],
                conv_w[l], conv_b[l], lru_w_a[l], lru_b_a[l], lru_w_x[l], lru_b_x[l],
                lru_lambda[l], w_lru_proj[l], w_mix_out[l])
    kv = _kv_proj(mem.reshape(nb * m_len, d), norm_mem[l], w_kv[l]).reshape(nb, m_len, 2 * d)
    x2, xn, idx_t, gate_t, lrank_t, cnt = _attention(
        x1.reshape(s_len, nb, d), kv, norm_xattn[l], w_q[l], w_o[l],
        norm_moe[l], w_router[l], b_router[l])
    tables, tails, loc, blocks, p_rows, r_loc = _layout(cnt.reshape(cnt.shape[0], -1))
    xs, pos_t = _dispatch(xn, idx_t, lrank_t, loc, tables + tails, p_rows, r_loc)
    y = _experts(xs, *blocks, w_gate_up[l], b_gate_up[l], w_down[l], b_down[l])
    out = _combine(pos_t.T, gate_t.T, x2.reshape(nb * s_len, d), y, norm_final, tables, r_loc)
    return out.reshape(nb, s_len, d)
```

```python
import functools

import jax
import jax.numpy as jnp
from jax import lax
from jax.experimental import pallas as pl
from jax.experimental.pallas import tpu as pltpu

POOL_WINDOWS = (2, 4, 8, 16)
N_GROUPS = 4
CONV_WIDTH = 4
RG_C = 8.0
TOP_K = 4
SWIGLU_LIMIT = 7.0
SWIGLU_ALPHA = 1.702
RMS_EPS = 1e-6

MIX_STEPS = 32
KV_ROWS = 512
ATT_ROWS = 512
ROUTE_ROWS = 256
ROUTE_UNROLL = 4
EXPERT_ROWS = 512
RUN_ALIGN = 8
MXU_TILE = 256
VMEM_LIMIT = 52 * 1024 * 1024

BF16 = jnp.bfloat16
F32 = jnp.float32
I32 = jnp.int32
U32 = jnp.uint32


def _const_spec(shape):
    nd = len(shape)
    return pl.BlockSpec(shape, lambda *_: (0,) * nd, pipeline_mode=pl.Buffered(1))


def _rms(x, g):
    return x * lax.rsqrt(jnp.mean(x * x, axis=-1, keepdims=True) + RMS_EPS) * g


def _dot(a, b):
    return jnp.dot(a, b, preferred_element_type=F32)


def _dot_nt(a, b):
    return lax.dot_general(a, b, (((1,), (1,)), ((), ())), preferred_element_type=F32)


def _pack_halves(x):
    c = x.shape[1] // 2
    lo = lax.bitcast_convert_type(x[:, :c], U32)
    hi = lax.bitcast_convert_type(x[:, c:], U32)
    return (hi & jnp.uint32(0xFFFF0000)) | (lo >> 16)


def _unpack_halves(u):
    lo = lax.bitcast_convert_type(u << 16, F32).astype(BF16)
    hi = lax.bitcast_convert_type(u & jnp.uint32(0xFFFF0000), F32).astype(BF16)
    return lo, hi


def _mixer_kernel(x_hbm, nm_ref, win_ref, wpg_ref, psc_ref, wpp_ref, cw_ref, cb_ref,
                  wa_ref, ba_ref, wx_ref, bx_ref, lam_ref, wlp_ref, wmo_ref,
                  o_ref,
                  xbuf, xsem,
                  h_ref, up_ref, ul_ref, a_ref, b_ref, pm_ref, m_ref, mb_ref, t_ref, gl_ref, gb_ref, hc_ref,
                  *, nb, ts):
    rows, d = o_ref.shape
    gw = d // N_GROUPS
    halo_p = (POOL_WINDOWS[-1]) * nb
    halo_c = (CONV_WIDTH - 1) * nb
    c = pl.program_id(0)

    @pl.when(c == 0)
    def _():
        up_ref[0:halo_p, :] = jnp.zeros((halo_p, d), F32)
        ul_ref[0:halo_c, :] = jnp.zeros((halo_c, d), F32)
        hc_ref[...] = jnp.zeros_like(hc_ref)

    def x_copy(step, t, slot):
        return pltpu.make_async_copy(x_hbm.at[:, step * ts + t, :], xbuf.at[slot, t], xsem.at[slot])

    def fetch(step, slot, live):
        for t in range(ts):
            @pl.when(live)
            def _(t=t):
                x_copy(step, t, slot).start()

    slot = c % 2
    fetch(0, 0, c == 0)
    fetch(jnp.minimum(c + 1, pl.num_programs(0) - 1), 1 - slot, c + 1 < pl.num_programs(0))
    pltpu.make_async_copy(xbuf.at[1 - slot], xbuf.at[slot], xsem.at[slot]).wait()

    h_ref[...] = _rms(xbuf[slot].reshape(rows, d), nm_ref[...]).astype(BF16)

    t_glob = c * ts + lax.broadcasted_iota(I32, (rows, 1), 0) // nb

    up_ref[halo_p:halo_p + rows, :] = _dot(h_ref[...], win_ref[:, 0:d])
    ul_ref[halo_c:halo_c + rows, :] = _dot(h_ref[...], win_ref[:, d:2 * d])

    for g, w in enumerate(POOL_WINDOWS):
        cols = slice(g * gw, (g + 1) * gw)
        u = up_ref[halo_p:halo_p + rows, cols]
        acc = u
        for j in range(1, w):
            acc = acc + up_ref[halo_p - j * nb:halo_p - j * nb + rows, cols]
        cnt = jnp.minimum(t_glob + 1, w).astype(F32)
        p = acc / cnt - u
        pg = _dot(p.astype(BF16), wpg_ref[g]) * psc_ref[:, cols]
        pm_ref[:, cols] = pg.astype(BF16)
    t_ref[...] = _dot(h_ref[...], win_ref[:, 3 * d:4 * d])
    m_ref[...] = _dot(pm_ref[...], wpp_ref[...])

    for g in range(N_GROUPS):
        cols = slice(g * gw, (g + 1) * gw)
        xr = cb_ref[:, cols]
        for k in range(CONV_WIDTH):
            off = halo_c - (CONV_WIDTH - 1 - k) * nb
            xr = xr + ul_ref[off:off + rows, cols] * cw_ref[k:k + 1, cols]
        xrb = xr.astype(BF16)
        r = jax.nn.sigmoid(_dot(xrb, wa_ref[g]) + ba_ref[:, cols])
        i = jax.nn.sigmoid(_dot(xrb, wx_ref[g]) + bx_ref[:, cols])
        gl_ref[:, cols] = _dot(h_ref[...], win_ref[:, 2 * d + g * gw:2 * d + (g + 1) * gw])
        gb_ref[:, cols] = _dot(h_ref[...], win_ref[:, 4 * d + g * gw:4 * d + (g + 1) * gw])
        lam = lam_ref[:, cols]
        log_sig = jnp.minimum(lam, 0.0) - jnp.log(1.0 + jnp.exp(-jnp.abs(lam)))
        a = jnp.exp((RG_C * r) * log_sig)
        mult = jnp.sqrt(jnp.maximum(1.0 - a * a, 0.0))
        mult = jnp.where(t_glob == 0, 1.0, mult)
        a_ref[:, cols] = a
        b_ref[:, cols] = mult * i * xr
    m_ref[...] = jax.nn.sigmoid(t_ref[...]) * m_ref[...]

    def scan_step(t, hprev):
        sl = pl.ds(pl.multiple_of(t * nb, nb), nb)
        hn = a_ref[sl, :] * hprev + b_ref[sl, :]
        b_ref[sl, :] = hn
        return hn

    hc_ref[...] = lax.fori_loop(0, ts, scan_step, hc_ref[...], unroll=4)

    half = rows // 2
    for hs in (slice(0, half), slice(half, rows)):
        pm_ref[hs, :] = (b_ref[hs, :] * jax.nn.gelu(gl_ref[hs, :], approximate=True)).astype(BF16)
    for hs in (slice(0, half), slice(half, rows)):
        yb = _dot(pm_ref[hs, :], wlp_ref[...])
        mb_ref[hs, :] = (m_ref[hs, :] + jax.nn.sigmoid(gb_ref[hs, :]) * yb).astype(BF16)
    for i, hs in enumerate((slice(0, half), slice(half, rows))):
        x_half = xbuf[slot, i * (ts // 2):(i + 1) * (ts // 2)].reshape(half, d)
        o_ref[hs, :] = x_half + _dot(mb_ref[hs, :], wmo_ref[...])

    up_ref[0:halo_p, :] = up_ref[rows:rows + halo_p, :]
    ul_ref[0:halo_c, :] = ul_ref[rows:rows + halo_c, :]


def _mixer(x, norm_mix, w_in, w_pool_group, pool_scale, w_pool_proj, conv_w, conv_b,
           lru_w_a, lru_b_a, lru_w_x, lru_b_x, lru_lambda, w_lru_proj, w_mix_out):
    nb, s_len, d = x.shape
    ts = MIX_STEPS
    rows = ts * nb
    n_rows = s_len * nb
    assert s_len % ts == 0 and ts % 2 == 0 and ts >= POOL_WINDOWS[-1] and nb % 8 == 0
    row2 = lambda v: v.reshape(1, -1)
    args = (x, row2(norm_mix), w_in.astype(BF16), w_pool_group.astype(BF16), row2(pool_scale),
            w_pool_proj.astype(BF16), conv_w, row2(conv_b), lru_w_a.astype(BF16), row2(lru_b_a),
            lru_w_x.astype(BF16), row2(lru_b_x), row2(lru_lambda), w_lru_proj.astype(BF16),
            w_mix_out.astype(BF16))
    in_specs = [pl.BlockSpec(memory_space=pl.ANY)] + [_const_spec(a.shape) for a in args[1:]]
    halo_p = POOL_WINDOWS[-1] * nb
    halo_c = (CONV_WIDTH - 1) * nb
    return pl.pallas_call(
        functools.partial(_mixer_kernel, nb=nb, ts=ts),
        grid=(n_rows // rows,),
        in_specs=in_specs,
        out_specs=pl.BlockSpec((rows, d), lambda c: (c, 0)),
        out_shape=jax.ShapeDtypeStruct((n_rows, d), F32),
        scratch_shapes=[
            pltpu.VMEM((2, ts, nb, d), F32),
            pltpu.SemaphoreType.DMA((2,)),
            pltpu.VMEM((rows, d), BF16),
            pltpu.VMEM((halo_p + rows, d), F32),
            pltpu.VMEM((halo_c + rows, d), F32),
            pltpu.VMEM((rows, d), F32),
            pltpu.VMEM((rows, d), F32),
            pltpu.VMEM((rows, d), BF16),
            pltpu.VMEM((rows, d), F32),
            pltpu.VMEM((rows, d), BF16),
            pltpu.VMEM((rows, d), F32),
            pltpu.VMEM((rows, d), F32),
            pltpu.VMEM((rows, d), F32),
            pltpu.VMEM((nb, d), F32),
        ],
        compiler_params=pltpu.CompilerParams(dimension_semantics=("arbitrary",),
                                             vmem_limit_bytes=VMEM_LIMIT),
        name="mixer",
    )(*args)


def _kv_kernel(m_ref, g_ref, w_ref, o_ref, w_bf):
    @pl.when(pl.program_id(0) == 0)
    def _():
        w_bf[...] = w_ref[...].astype(BF16)

    o_ref[...] = _dot(_rms(m_ref[...], g_ref[...]).astype(BF16), w_bf[...]).astype(BF16)


def _kv_proj(mem2d, norm_mem, w_kv):
    n, d = mem2d.shape
    assert n % KV_ROWS == 0
    return pl.pallas_call(
        _kv_kernel,
        grid=(n // KV_ROWS,),
        in_specs=[pl.BlockSpec((KV_ROWS, d), lambda i: (i, 0)),
                  _const_spec((1, d)), _const_spec((d, 2 * d))],
        out_specs=pl.BlockSpec((KV_ROWS, 2 * d), lambda i: (i, 0)),
        out_shape=jax.ShapeDtypeStruct((n, 2 * d), BF16),
        scratch_shapes=[pltpu.VMEM((d, 2 * d), BF16)],
        compiler_params=pltpu.CompilerParams(dimension_semantics=("arbitrary",),
                                             vmem_limit_bytes=VMEM_LIMIT),
        name="kv_proj",
    )(mem2d, norm_mem.reshape(1, d), w_kv)


def _attn_kernel(x_hbm, kv_ref, gx_ref, wq_ref, wo_ref, gm_ref, wr_ref, br_ref,
                 x2_ref, xn_ref, idx_ref, gate_ref, lrank_ref, cnt_ref,
                 xbuf, xsem, o_scr, xh_scr, xl_scr, wq_bf, wo_bf, *, nq):
    tq, d = x2_ref.shape
    hd = d // N_GROUPS
    ne = br_ref.shape[0]
    tb = ROUTE_ROWS
    n = pl.program_id(0)
    n_blocks = pl.num_programs(0) - 1
    cur = jnp.minimum(n, n_blocks - 1)
    slot = cur % 2
    prev = (n + 1) % 2

    def x_copy(blk, s):
        return pltpu.make_async_copy(x_hbm.at[pl.ds((blk % nq) * tq, tq), blk // nq, :], xbuf.at[s], xsem.at[s])

    @pl.when(n == 0)
    def _():
        x_copy(0, 0).start()
        xh_scr[...] = jnp.zeros_like(xh_scr)
        xl_scr[...] = jnp.zeros_like(xl_scr)
        wq_bf[...] = wq_ref[...].astype(BF16)
        wo_bf[...] = wo_ref[...].astype(BF16)

    @pl.when(n + 1 < n_blocks)
    def _():
        x_copy(n + 1, 1 - slot).start()

    @pl.when(n < n_blocks)
    def _():
        x_copy(n, slot).wait()

    x = xbuf[slot]

    works = []
    for sb in range(tq // tb):
        rows = slice(sb * tb, (sb + 1) * tb)
        ph = _dot_nt(wr_ref[...], xh_scr[prev, rows, :])
        pl_ = _dot_nt(wr_ref[0:ne, :], xl_scr[prev, rows, :])
        works.append(ph[0:ne] + ph[ne:2 * ne] + pl_ + br_ref[...])

    q = _dot(_rms(x, gx_ref[...]).astype(BF16), wq_bf[...]).astype(BF16)

    iota_f = lax.broadcasted_iota(I32, (ne, tb), 0).astype(F32)
    picked = []
    for work in works:
        vals, idxs, sels = [], [], []
        for _ in range(TOP_K):
            m = jnp.max(work, axis=0, keepdims=True)
            idx = jnp.min(jnp.where(work == m, iota_f, float(ne)), axis=0, keepdims=True)
            sel = iota_f == idx
            vals.append(m)
            idxs.append(idx.astype(I32))
            sels.append(sel)
            work = jnp.where(sel, -jnp.inf, work)
        onehot = jnp.zeros((ne, tb), F32)
        for sel in sels:
            onehot = onehot + sel.astype(F32)
        picked.append((vals, idxs, sels, onehot.astype(BF16)))

    scores = [_dot_nt(q[:, h * hd:(h + 1) * hd], kv_ref[:, h * hd:(h + 1) * hd]) * (hd ** -0.5)
              for h in range(N_GROUPS)]
    for h, s in enumerate(scores):
        v = kv_ref[:, d + h * hd:d + (h + 1) * hd]
        e = jnp.exp(s - jnp.max(s, axis=-1, keepdims=True))
        p = e / jnp.sum(e, axis=-1, keepdims=True)
        o_scr[:, h * hd:(h + 1) * hd] = _dot(p.astype(BF16), v).astype(BF16)

    before = (lax.broadcasted_iota(I32, (tb, tb), 0) < lax.broadcasted_iota(I32, (tb, tb), 1)
              ).astype(BF16)
    for sb, (vals, idxs, sels, oh16) in enumerate(picked):
        rows = slice(sb * tb, (sb + 1) * tb)
        ex = [jnp.exp(v - vals[0]) for v in vals]
        den = ex[0] + ex[1] + ex[2] + ex[3]
        prefix = _dot(oh16, before)
        lr = [jnp.sum(jnp.where(sel, prefix, 0.0), axis=0, keepdims=True).astype(I32) for sel in sels]
        idx_ref[:, rows] = jnp.concatenate(idxs, axis=0)
        gate_ref[:, rows] = jnp.concatenate([e_ / den for e_ in ex], axis=0)
        lrank_ref[:, rows] = jnp.concatenate(lr, axis=0)
        cnt_ref[sb] = _dot_nt(jnp.ones((1, tb), BF16), oh16).astype(I32)

    x2 = x + _dot(o_scr[...], wo_bf[...])
    x2_ref[...] = x2
    xn = _rms(x2, gm_ref[...])
    xh = xn.astype(BF16)
    xn_ref[...] = xh
    xh_scr[n % 2] = xh
    xl_scr[n % 2] = (xn - xh.astype(F32)).astype(BF16)


def _attention(x1, kv, norm_xattn, w_q, w_o, norm_moe, w_router, b_router):
    s_len, nb, d = x1.shape
    m = kv.shape[1]
    ne = w_router.shape[-1]
    tq, tb = ATT_ROWS, ROUTE_ROWS
    nq = s_len // tq
    n_blocks = nb * nq
    t = nb * s_len
    assert s_len % tq == 0 and tq % tb == 0
    wr_hi = w_router.astype(BF16)
    wr_lo = (w_router - wr_hi.astype(F32)).astype(BF16)
    wr2t = jnp.concatenate([wr_hi, wr_lo], axis=1).T
    cur = lambda n: jnp.minimum(n, n_blocks - 1)
    routed = lambda n: (0, jnp.maximum(n - 1, 0))
    return pl.pallas_call(
        functools.partial(_attn_kernel, nq=nq),
        grid=(n_blocks + 1,),
        in_specs=[pl.BlockSpec(memory_space=pl.ANY),
                  pl.BlockSpec((None, m, 2 * d), lambda n: (cur(n) // nq, 0, 0)),
                  _const_spec((1, d)), _const_spec((d, d)), _const_spec((d, d)),
                  _const_spec((1, d)), _const_spec((2 * ne, d)), _const_spec((ne, 1))],
        out_specs=[pl.BlockSpec((None, tq, d), lambda n: (cur(n) // nq, cur(n) % nq, 0)),
                   pl.BlockSpec((tq, d), lambda n: (cur(n), 0)),
                   pl.BlockSpec((TOP_K, tq), routed),
                   pl.BlockSpec((TOP_K, tq), routed),
                   pl.BlockSpec((TOP_K, tq), routed),
                   pl.BlockSpec((tq // tb, 1, ne), lambda n: (jnp.maximum(n - 1, 0), 0, 0))],
        out_shape=[jax.ShapeDtypeStruct((nb, s_len, d), F32),
                   jax.ShapeDtypeStruct((t, d), BF16),
                   jax.ShapeDtypeStruct((TOP_K, t), I32),
                   jax.ShapeDtypeStruct((TOP_K, t), F32),
                   jax.ShapeDtypeStruct((TOP_K, t), I32),
                   jax.ShapeDtypeStruct((t // tb, 1, ne), I32)],
        scratch_shapes=[pltpu.VMEM((2, tq, d), F32), pltpu.SemaphoreType.DMA((2,)),
                        pltpu.VMEM((tq, d), BF16), pltpu.VMEM((2, tq, d), BF16), pltpu.VMEM((2, tq, d), BF16),
                        pltpu.VMEM((d, d), BF16), pltpu.VMEM((d, d), BF16)],
        compiler_params=pltpu.CompilerParams(dimension_semantics=("arbitrary",),
                                             vmem_limit_bytes=VMEM_LIMIT),
        name="attention",
    )(x1, kv, norm_xattn.reshape(1, d), w_q, w_o,
      norm_moe.reshape(1, d), wr2t, b_router.reshape(ne, 1))


def _pow2_chunks(limit):
    sizes = []
    c = RUN_ALIGN
    while c <= limit:
        sizes.append(c)
        c *= 2
    return sizes[::-1]


def _for_each_chunk(n, limit, fn):
    for size in _pow2_chunks(limit):
        @pl.when((n & size) != 0)
        def _(size=size):
            fn(pl.multiple_of(n & ~(2 * size - 1), RUN_ALIGN), size)


def _wait_rows(n, limit, src, dst, sem):
    _for_each_chunk(n, limit, lambda off, size: pltpu.make_async_copy(
        src.at[pl.ds(0, size)], dst.at[pl.ds(0, size)], sem).wait())


def _dispatch_kernel(n8_ref, loc_ref, run_ref, tot_ref, tailn_ref, tails_ref,
                     xn_ref, idx_ref, lrank_ref, locv_ref, xs_hbm, pos_ref, *scratch):
    *bufs, zbuf, sems, zsem = scratch
    g = pl.program_id(0)
    nbuf = len(bufs)
    r_loc = bufs[0].shape[0]
    tb = xn_ref.shape[0] // nbuf
    ne = tailn_ref.shape[0] - 1
    zrows = zbuf.shape[0]
    last_blk = pl.num_programs(0) * nbuf - 1

    def send(blk, s, live):
        for e in range(ne):
            n = jnp.where(live, n8_ref[blk * ne + e], 0)
            src0 = loc_ref[blk * ne + e]
            dst0 = run_ref[blk * ne + e]
            _for_each_chunk(n, tb, lambda off, size, src0=src0, dst0=dst0: pltpu.make_async_copy(
                bufs[s].at[pl.ds(pl.multiple_of(src0 + off, RUN_ALIGN), size)],
                xs_hbm.at[pl.ds(pl.multiple_of(dst0 + off, RUN_ALIGN), size)], sems.at[s]).start())

    def sent(blk, s, live):
        _wait_rows(jnp.where(live, tot_ref[blk], 0), r_loc, bufs[s], xs_hbm, sems.at[s])

    def zero_rest(i, carry):
        dst = pl.multiple_of(tails_ref[ne] + i * zrows, zrows)
        pltpu.make_async_copy(zbuf, xs_hbm.at[pl.ds(dst, zrows)], zsem).start()
        return carry

    def wait_rest(i, carry):
        pltpu.make_async_copy(zbuf, xs_hbm.at[pl.ds(0, zrows)], zsem).wait()
        return carry

    @pl.when(g == 0)
    def _():
        zbuf[...] = jnp.zeros_like(zbuf)
        for e in range(ne):
            _for_each_chunk(tailn_ref[e], zrows, lambda off, size, e=e: pltpu.make_async_copy(
                zbuf.at[pl.ds(0, size)],
                xs_hbm.at[pl.ds(pl.multiple_of(tails_ref[e] + off, RUN_ALIGN), size)], zsem).start())
        lax.fori_loop(0, tailn_ref[ne], zero_rest, 0)

    iota_e = lax.broadcasted_iota(I32, (ne, tb), 0)
    iota_r = lax.broadcasted_iota(I32, (r_loc, tb), 0)
    for u in range(nbuf):
        blk = g * nbuf + u
        cols = slice(u * tb, (u + 1) * tb)
        sent(jnp.maximum(blk - nbuf, 0), u, blk >= nbuf)
        send(jnp.maximum(blk - 1, 0), (u - 1) % nbuf, blk >= 1)

        loc_col = locv_ref[u].astype(F32)
        pos = []
        for k in range(TOP_K):
            run0 = jnp.sum(jnp.where(iota_e == idx_ref[k:k + 1, cols], loc_col, 0.0), axis=0, keepdims=True)
            pos.append(run0.astype(I32) + lrank_ref[k:k + 1, cols])
            pos_ref[k:k + 1, cols] = pos[k]

        hit = iota_r == pos[0]
        for k in range(1, TOP_K):
            hit = jnp.logical_or(hit, iota_r == pos[k])
        bufs[u][...] = _pack_halves(_dot(hit.astype(BF16), xn_ref[cols, :]))

    @pl.when(g == pl.num_programs(0) - 1)
    def _():
        send(last_blk, nbuf - 1, True)
        for s_ in range(nbuf):
            sent(last_blk - (nbuf - 1 - s_), s_, True)
        for e in range(ne):
            _wait_rows(tailn_ref[e], zrows, zbuf, xs_hbm, zsem)
        lax.fori_loop(0, tailn_ref[ne], wait_rest, 0)


def _dispatch(xn, idx_t, lrank_t, loc, tables, p_rows, r_loc):
    t, d = xn.shape
    tb, nu = ROUTE_ROWS, ROUTE_UNROLL
    nblk, ne = loc.shape
    assert nblk % nu == 0
    tok = pl.BlockSpec((TOP_K, nu * tb), lambda g, *_: (0, g))
    grid_spec = pltpu.PrefetchScalarGridSpec(
        num_scalar_prefetch=6,
        grid=(nblk // nu,),
        in_specs=[pl.BlockSpec((nu * tb, d), lambda g, *_: (g, 0)), tok, tok,
                  pl.BlockSpec((nu, ne, 1), lambda g, *_: (g, 0, 0))],
        out_specs=[pl.BlockSpec(memory_space=pl.ANY), tok],
        scratch_shapes=[pltpu.VMEM((r_loc, d // 2), U32)] * nu + [
            pltpu.VMEM((EXPERT_ROWS // 2, d // 2), U32), pltpu.SemaphoreType.DMA((nu,)), pltpu.SemaphoreType.DMA],
    )
    return pl.pallas_call(
        _dispatch_kernel,
        grid_spec=grid_spec,
        out_shape=[jax.ShapeDtypeStruct((p_rows, d // 2), U32), jax.ShapeDtypeStruct((TOP_K, t), I32)],
        compiler_params=pltpu.CompilerParams(dimension_semantics=("arbitrary",),
                                             vmem_limit_bytes=VMEM_LIMIT),
        name="dispatch",
    )(*tables, xn, idx_t, lrank_t, loc.reshape(nblk, ne, 1))


def _expert_kernel(blk0_ref, nblk_ref, nu_ref, xs_hbm, wgu_ref, bgu_ref, wdn_ref, bdn_ref, y_hbm,
                   xbuf, ybuf, zbuf, in_sem, out_sem, zsem, wgu_bf, wdn_bf, next_ref):
    e = pl.program_id(0)
    nbuf, bm, dh = xbuf.shape
    zrows = zbuf.shape[0]
    de = wdn_ref.shape[0]
    n_used = nu_ref[0]

    def x_copy(g):
        return pltpu.make_async_copy(xs_hbm.at[pl.ds(pl.multiple_of(g * bm, bm), bm)], xbuf.at[g % nbuf],
                                     in_sem.at[g % nbuf])

    def y_copy(g):
        return pltpu.make_async_copy(ybuf.at[g % nbuf], y_hbm.at[pl.ds(pl.multiple_of(g * bm, bm), bm)],
                                     out_sem.at[g % nbuf])

    def prefetch_until(limit):
        def start(g):
            x_copy(g).start()
            return g + 1
        next_ref[0] = lax.while_loop(lambda g: g < jnp.minimum(limit, n_used), start, next_ref[0])

    n_zero = (y_hbm.shape[0] - n_used * bm) // zrows

    def z_copy(i):
        return pltpu.make_async_copy(
            zbuf, y_hbm.at[pl.ds(pl.multiple_of(n_used * bm + i * zrows, zrows), zrows)], zsem)

    @pl.when(e == 0)
    def _():
        next_ref[0] = 0
        zbuf[...] = jnp.zeros_like(zbuf)
        lax.fori_loop(0, n_zero, lambda i, c: (z_copy(i).start(), c)[1], 0)

    wgu_bf[...] = wgu_ref[...].astype(BF16)
    wdn_bf[...] = wdn_ref[...].astype(BF16)

    def mlp(xu):
        x_lo, x_hi = _unpack_halves(xu)

        def proj(c0, c1):
            return (_dot(x_lo, wgu_bf[0:dh, c0:c1]) + _dot(x_hi, wgu_bf[dh:2 * dh, c0:c1])
                    + bgu_ref[:, c0:c1])

        nch = 4
        ch = de // nch
        y = bdn_ref[...]
        prev = None
        for c in range(nch):
            gl = jnp.minimum(proj(c * ch, (c + 1) * ch), SWIGLU_LIMIT)
            up = jnp.clip(proj(de + c * ch, de + (c + 1) * ch), -SWIGLU_LIMIT, SWIGLU_LIMIT)
            if prev is not None:
                y = y + _dot(prev, wdn_bf[(c - 1) * ch:c * ch, :])
            prev = ((up + 1.0) * (gl * jax.nn.sigmoid(SWIGLU_ALPHA * gl))).astype(BF16)
        y = y + _dot(prev, wdn_bf[(nch - 1) * ch:nch * ch, :])
        return _pack_halves(y.astype(BF16).astype(F32))

    def run(g, units):
        prefetch_until(g + units + 2)
        for k in range(units):
            x_copy(g + k).wait()

            @pl.when(g + k >= nbuf)
            def _(k=k):
                y_copy(g + k - nbuf).wait()
        if units == 1:
            ybuf[g % nbuf] = mlp(xbuf[g % nbuf])
        else:
            slots = pl.ds(pl.multiple_of(g % nbuf, 2), 2)
            ybuf[slots] = mlp(xbuf[slots].reshape(2 * bm, dh)).reshape(2, bm, dh)
        for k in range(units):
            y_copy(g + k).start()

    g0 = blk0_ref[e]
    n = nblk_ref[e]
    lead = jnp.logical_and(g0 % 2 == 1, n > 0).astype(I32)
    pairs = (n - lead) // 2

    @pl.when(lead == 1)
    def _():
        run(g0, 1)

    lax.fori_loop(0, pairs, lambda j, c: (run(g0 + lead + 2 * j, 2), c)[1], 0)

    @pl.when((n - lead) % 2 == 1)
    def _():
        run(g0 + lead + 2 * pairs, 1)

    @pl.when(e == pl.num_programs(0) - 1)
    def _():
        for back in range(nbuf, 0, -1):
            @pl.when(n_used >= back)
            def _(back=back):
                y_copy(n_used - back).wait()
        lax.fori_loop(0, n_zero, lambda i, c: (z_copy(i).wait(), c)[1], 0)


def _experts(xs, blk0, nblk, n_used, w_gate_up, b_gate_up, w_down, b_down):
    p, dh = xs.shape
    ne, d, de2 = w_gate_up.shape
    de = de2 // 2
    bm = EXPERT_ROWS
    per_expert = lambda shape: pl.BlockSpec((None,) + shape, lambda e, *_: (e, 0, 0))
    grid_spec = pltpu.PrefetchScalarGridSpec(
        num_scalar_prefetch=3,
        grid=(ne,),
        in_specs=[pl.BlockSpec(memory_space=pl.ANY),
                  per_expert((d, de2)), per_expert((1, de2)), per_expert((de, d)), per_expert((1, d))],
        out_specs=pl.BlockSpec(memory_space=pl.ANY),
        scratch_shapes=[pltpu.VMEM((4, bm, dh), U32), pltpu.VMEM((4, bm, dh), U32),
                        pltpu.VMEM((bm // 4, dh), U32),
                        pltpu.SemaphoreType.DMA((4,)), pltpu.SemaphoreType.DMA((4,)), pltpu.SemaphoreType.DMA,
                        pltpu.VMEM((d, de2), BF16), pltpu.VMEM((de, d), BF16), pltpu.SMEM((1,), I32)],
    )
    return pl.pallas_call(
        _expert_kernel,
        grid_spec=grid_spec,
        out_shape=jax.ShapeDtypeStruct((p, dh), U32),
        compiler_params=pltpu.CompilerParams(dimension_semantics=("arbitrary",),
                                             vmem_limit_bytes=VMEM_LIMIT),
        name="experts",
    )(blk0, nblk, n_used, xs, w_gate_up, b_gate_up.reshape(ne, 1, de2), w_down, b_down.reshape(ne, 1, d))


def _combine_kernel(n8_ref, loc_ref, run_ref, tot_ref,
                    pos_ref, gate_ref, x2_ref, gf_ref, y_hbm, o_ref, *scratch):
    *bufs, sems = scratch
    g = pl.program_id(0)
    nbuf = len(bufs)
    r_loc = bufs[0].shape[0]
    tb = x2_ref.shape[0] // nbuf
    nblk = pl.num_programs(0) * nbuf
    ne = n8_ref.shape[0] // tot_ref.shape[0]

    def fetch(blk, s, live):
        for e in range(ne):
            n = jnp.where(live, n8_ref[blk * ne + e], 0)
            dst0 = loc_ref[blk * ne + e]
            src0 = run_ref[blk * ne + e]
            _for_each_chunk(n, tb, lambda off, size, src0=src0, dst0=dst0: pltpu.make_async_copy(
                y_hbm.at[pl.ds(pl.multiple_of(src0 + off, RUN_ALIGN), size)],
                bufs[s].at[pl.ds(pl.multiple_of(dst0 + off, RUN_ALIGN), size)], sems.at[s]).start())

    @pl.when(g == 0)
    def _():
        for buf in bufs:
            buf[...] = jnp.zeros_like(buf)
        for ahead in range(nbuf - 2):
            fetch(ahead, ahead, True)

    iota_c = lax.broadcasted_iota(I32, (tb, r_loc), 1)
    for u in range(nbuf):
        blk = g * nbuf + u
        rows = slice(u * tb, (u + 1) * tb)
        w = jnp.zeros((tb, r_loc), F32)
        for k in range(TOP_K):
            w = w + jnp.where(iota_c == pos_ref[rows, k:k + 1], gate_ref[rows, k:k + 1], 0.0)
        w_hi = w.astype(BF16)
        w_lo = (w - w_hi.astype(F32)).astype(BF16)
        _wait_rows(tot_ref[blk], r_loc, y_hbm, bufs[u], sems.at[u])
        ahead = blk + nbuf - 2
        fetch(jnp.minimum(ahead, nblk - 1), (u - 2) % nbuf, ahead < nblk)
        y_lo, y_hi = _unpack_halves(bufs[u][...])
        moe = jnp.concatenate([_dot(w_hi, y_lo) + _dot(w_lo, y_lo), _dot(w_hi, y_hi) + _dot(w_lo, y_hi)], axis=1)
        o_ref[rows, :] = _rms(x2_ref[rows, :] + moe, gf_ref[...])


def _combine(pos, gates, x2, y, norm_final, tables, r_loc):
    t, d = x2.shape
    rows = ROUTE_ROWS * ROUTE_UNROLL
    assert t % rows == 0
    grid_spec = pltpu.PrefetchScalarGridSpec(
        num_scalar_prefetch=4,
        grid=(t // rows,),
        in_specs=[pl.BlockSpec((rows, TOP_K), lambda g, *_: (g, 0)),
                  pl.BlockSpec((rows, TOP_K), lambda g, *_: (g, 0)),
                  pl.BlockSpec((rows, d), lambda g, *_: (g, 0)),
                  pl.BlockSpec((1, d), lambda g, *_: (0, 0)),
                  pl.BlockSpec(memory_space=pl.ANY)],
        out_specs=pl.BlockSpec((rows, d), lambda g, *_: (g, 0)),
        scratch_shapes=[pltpu.VMEM((r_loc, d // 2), U32)] * ROUTE_UNROLL + [pltpu.SemaphoreType.DMA((ROUTE_UNROLL,))],
    )
    return pl.pallas_call(
        _combine_kernel,
        grid_spec=grid_spec,
        out_shape=jax.ShapeDtypeStruct((t, d), F32),
        compiler_params=pltpu.CompilerParams(dimension_semantics=("arbitrary",),
                                             vmem_limit_bytes=VMEM_LIMIT),
        name="combine",
    )(*tables, pos, gates, x2, norm_final.reshape(1, d), y)


def _excl_cumsum(a, axis):
    n = a.shape[axis]
    a = jnp.moveaxis(a, axis, -1)
    earlier = jnp.arange(n)[None, :] < jnp.arange(n)[:, None]
    out = jnp.sum(jnp.where(earlier, a[..., None, :], 0), axis=-1)
    return jnp.moveaxis(out, -1, axis)


def _layout(cnt):
    nblk, ne = cnt.shape
    tb, bm = ROUTE_ROWS, EXPERT_ROWS
    n8 = (cnt + RUN_ALIGN - 1) // RUN_ALIGN * RUN_ALIGN
    loc = _excl_cumsum(n8, 1)
    tot = jnp.sum(n8, axis=1)
    size = jnp.sum(n8, axis=0)
    padded = (size + bm - 1) // bm * bm
    pstart = _excl_cumsum(padded, 0)
    pend = pstart + padded
    run = pstart[None, :] + _excl_cumsum(n8, 0)
    p_blocks = -(-(nblk * tb * TOP_K + nblk * ne * (RUN_ALIGN - 1) + ne * (bm - RUN_ALIGN)) // bm)
    n_used = (pend[-1] // bm).astype(I32)
    flat = lambda a: a.reshape(-1).astype(I32)
    tables = (flat(n8), flat(loc), flat(run), flat(tot))
    half = bm // 2
    tails = (flat(jnp.concatenate([padded - size, (p_blocks * bm - pend[-1:]) // half])),
             flat(jnp.concatenate([pstart + size, pend[-1:]])))
    r_loc = -(-(tb * TOP_K + ne * (RUN_ALIGN - 1)) // MXU_TILE) * MXU_TILE
    blocks = ((pstart // bm).astype(I32), (padded // bm).astype(I32), n_used.reshape(1))
    return tables, tails, loc.astype(I32), blocks, p_blocks * bm, r_loc


def kernel(x, mem, norm_mix, w_in, w_pool_group, pool_scale, w_pool_proj, conv_w, conv_b, lru_w_a, lru_b_a, lru_w_x, lru_b_x, lru_lambda, w_lru_proj, w_mix_out, norm_xattn, norm_mem, w_q, w_kv, w_o, norm_moe, w_router, b_router, w_gate_up, b_gate_up, w_down, b_down, norm_final):
    nb, s_len, d = x.shape
    m_len = mem.shape[1]
    assert norm_mix.shape[0] == 1, "single-layer stack"
    l = 0
    x1 = _mixer(x, norm_mix[l], w_in[l], w_pool_group[l], pool_scale[l], w_pool_proj[l],
                conv_w[l], conv_b[l], lru_w_a[l], lru_b_a[l], lru_w_x[l], lru_b_x[l],
                lru_lambda[l], w_lru_proj[l], w_mix_out[l])
    kv = _kv_proj(mem.reshape(nb * m_len, d), norm_mem[l], w_kv[l]).reshape(nb, m_len, 2 * d)
    x2, xn, idx_t, gate_t, lrank_t, cnt = _attention(
        x1.reshape(s_len, nb, d), kv, norm_xattn[l], w_q[l], w_o[l],
        norm_moe[l], w_router[l], b_router[l])
    tables, tails, loc, blocks, p_rows, r_loc = _layout(cnt.reshape(cnt.shape[0], -1))
    xs, pos_t = _dispatch(xn, idx_t, lrank_t, loc, tables + tails, p_rows, r_loc)
    y = _experts(xs, *blocks, w_gate_up[l], b_gate_up[l], w_down[l], b_down[l])
    out = _combine(pos_t.T, gate_t.T, x2.reshape(nb * s_len, d), y, norm_final, tables, r_loc)
    return out.reshape(nb, s_len, d)
```
